```python
import math
import jax, jax.numpy as jnp
from jax import lax
import numpy as np

D_MODEL = 1024
BATCH = 8
SEQ = 2048
DEPTH = 2
DEC_BATCH = 128
DEC_SEQ = 4
PAST_LEN = 16384
PAGE_SIZE = 128

N_META = 16
N_A = (DEPTH + 1) // 2
N_B = DEPTH // 2
NORM_EPS = 1e-5
RW_HEAD = 64
RW_HEADS = D_MODEL // RW_HEAD
DECAY_LORA = 64
AAA_LORA = 64
GATE_LORA = 160
RW_LN_EPS = 64e-5
M_EXPAND = 2
M_D_INNER = M_EXPAND * D_MODEL
M_HEADDIM = 64
M_HEADS = M_D_INNER // M_HEADDIM
M_GROUPS = 4
M_HPG = M_HEADS // M_GROUPS
M_STATE = 128
M_CONV = 4
M_CONV_DIM = M_D_INNER + 2 * M_GROUPS * M_STATE
M_IN_DIM = 2 * M_D_INNER + 2 * M_GROUPS * M_STATE + M_HEADS
M_CHUNK = 128
P_HEADS = 8
P_NKEYS = 128
P_EXPERTS = P_NKEYS ** 2
P_QDIM = 256
P_TOPK = 16
P_BLOCK = 128

kernel_name = 'hybrid_rwkv7_mamba2_peer_step'

F32 = jnp.float32


def rms_norm(x, g):
    xf = x.astype(F32)
    y = xf * lax.rsqrt(jnp.mean(xf * xf, axis=-1, keepdims=True) + NORM_EPS)
    return (y * g.astype(F32)).astype(x.dtype)


def rwkv7_time_mix(xn, shift0, wkv0, mix, w_rkv, w0, w1, w2, a0, a1, a2, g1, g2,
                   k_k, k_a, r_k, ln_w, ln_b, w_o):
    bsz, L, D = xn.shape
    H, N = RW_HEADS, RW_HEAD
    prev = jnp.concatenate([shift0[:, None, :].astype(xn.dtype), xn[:, :-1]], axis=1)
    dx = prev - xn
    xr, xw, xk, xv, xa, xg = [xn + dx * mix[j] for j in range(6)]
    r = xr @ w_rkv[0]
    k = xk @ w_rkv[1]
    v = xv @ w_rkv[2]
    w_log = -jax.nn.softplus(-(w0 + jnp.tanh(xw @ w1) @ w2)) - 0.5
    decay = jnp.exp(-jnp.exp(w_log.astype(F32)))
    a = jax.nn.sigmoid(a0 + (xa @ a1) @ a2)
    g = jax.nn.sigmoid(xg @ g1) @ g2
    kk = (k * k_k).reshape(bsz, L, H, N).astype(F32)
    kk = kk / jnp.maximum(jnp.linalg.norm(kk, axis=-1, keepdims=True), 1e-12)
    k = k * (1 + (a - 1) * k_a)

    def heads(t):
        return t.reshape(bsz, L, H, N).astype(F32)

    r_h, k_h, v_h, a_h, w_h = heads(r), heads(k), heads(v), heads(a), heads(decay)

    def step(S, inp):
        r_t, w_t, k_t, v_t, kk_t, a_t = inp
        s_kk = jnp.einsum('bhvk,bhk->bhv', S, -kk_t)
        S = (S * w_t[:, :, None, :]
             + s_kk[..., None] * (kk_t * a_t)[:, :, None, :]
             + v_t[..., None] * k_t[:, :, None, :])
        return S, jnp.einsum('bhvk,bhk->bhv', S, r_t)

    seq = tuple(jnp.swapaxes(t, 0, 1) for t in (r_h, w_h, k_h, v_h, kk, a_h))
    S_fin, o = lax.scan(step, wkv0.astype(F32), seq)
    o = jnp.swapaxes(o, 0, 1)
    mu = jnp.mean(o, axis=-1, keepdims=True)
    var = jnp.mean(jnp.square(o - mu), axis=-1, keepdims=True)
    o = ((o - mu) * lax.rsqrt(var + RW_LN_EPS)).reshape(bsz, L, D) * ln_w + ln_b
    bonus = jnp.sum(r_h * k_h * r_k, axis=-1, keepdims=True) * v_h
    o = (o + bonus.reshape(bsz, L, D)).astype(xn.dtype)
    out = (o * g) @ w_o
    return out, xn[:, -1], S_fin.astype(xn.dtype)


def segsum(x):
    T = x.shape[-1]
    xe = jnp.broadcast_to(x[..., :, None], x.shape + (T,))
    low = jnp.tril(jnp.ones((T, T), dtype=bool), -1)
    cs = jnp.cumsum(jnp.where(low, xe, 0.0), axis=-2)
    return jnp.where(jnp.tril(jnp.ones((T, T), dtype=bool)), cs, -jnp.inf)


def ssd(x, dt, A, Bm, Cm, h0, lead_pad, chunk):
    b, L = x.shape[:2]
    padt = lambda t: jnp.pad(t, [(0, 0), (lead_pad, 0)] + [(0, 0)] * (t.ndim - 2))
    x, dt, Bm, Cm = padt(x), padt(dt), padt(Bm), padt(Cm)
    Lp = L + lead_pad
    nc = Lp // chunk
    x = x.reshape(b, nc, chunk, M_GROUPS, M_HPG, M_HEADDIM)
    dt = dt.reshape(b, nc, chunk, M_GROUPS, M_HPG)
    Bm = Bm.reshape(b, nc, chunk, M_GROUPS, M_STATE)
    Cm = Cm.reshape(b, nc, chunk, M_GROUPS, M_STATE)
    xdt = x * dt[..., None]
    adt = jnp.moveaxis(dt * A, 2, -1)
    a_cs = jnp.cumsum(adt, axis=-1)
    Lmat = jnp.exp(segsum(adt))
    cb = jnp.einsum('bclgn,bcsgn->bcgls', Cm, Bm)
    y_diag = jnp.einsum('bcgls,bcgrls,bcsgrp->bclgrp', cb, Lmat, xdt)
    decay_st = jnp.exp(a_cs[..., -1:] - a_cs)
    st = jnp.einsum('bclgn,bcgrl,bclgrp->bcgrpn', Bm, decay_st, xdt)
    chunk_decay = jnp.exp(a_cs[..., -1])

    def step(h, inp):
        s_c, d_c = inp
        return h * d_c[..., None, None] + s_c, h

    h_fin, h_in = lax.scan(step, h0, (jnp.moveaxis(st, 1, 0), jnp.moveaxis(chunk_decay, 1, 0)))
    h_in = jnp.moveaxis(h_in, 0, 1)
    y_off = jnp.einsum('bclgn,bcgrpn,bcgrl->bclgrp', Cm, h_in, jnp.exp(a_cs))
    y = (y_diag + y_off).reshape(b, Lp, M_GROUPS, M_HPG, M_HEADDIM)[:, lead_pad:]
    return y, h_fin


def mamba2_mix(xn, conv0, ssm0, in_proj, conv_w, conv_b, dt_bias, a_log, d_skip, norm_w, out_proj,
               lead_pad, chunk):
    bsz, L, _ = xn.shape
    zxbcdt = xn @ in_proj
    z = zxbcdt[..., :M_D_INNER]
    xbc = zxbcdt[..., M_D_INNER:M_D_INNER + M_CONV_DIM]
    dt_raw = zxbcdt[..., M_D_INNER + M_CONV_DIM:]
    full = jnp.concatenate([conv0.astype(xn.dtype), xbc], axis=1)
    conv = conv_b + sum(full[:, j:j + L] * conv_w[j] for j in range(M_CONV))
    new_conv = full[:, L:]
    xbc = jax.nn.silu(conv).astype(F32)
    xs = xbc[..., :M_D_INNER].reshape(bsz, L, M_GROUPS, M_HPG, M_HEADDIM)
    Bm = xbc[..., M_D_INNER:M_D_INNER + M_GROUPS * M_STATE].reshape(bsz, L, M_GROUPS, M_STATE)
    Cm = xbc[..., M_D_INNER + M_GROUPS * M_STATE:].reshape(bsz, L, M_GROUPS, M_STATE)
    dt = jax.nn.softplus((dt_raw + dt_bias).astype(F32)).reshape(bsz, L, M_GROUPS, M_HPG)
    A = -jnp.exp(a_log.astype(F32)).reshape(M_GROUPS, M_HPG)
    h0 = ssm0.astype(F32).reshape(bsz, M_GROUPS, M_HPG, M_HEADDIM, M_STATE)
    y, h_fin = ssd(xs, dt, A, Bm, Cm, h0, lead_pad, chunk)
    y = y + d_skip.astype(F32).reshape(M_GROUPS, M_HPG)[..., None] * xs
    y = y.reshape(bsz, L, M_D_INNER)
    yg = (y * jax.nn.silu(z.astype(F32))).reshape(bsz, L, M_GROUPS, M_D_INNER // M_GROUPS)
    yg = yg * lax.rsqrt(jnp.mean(yg * yg, axis=-1, keepdims=True) + NORM_EPS)
    yg = (yg.reshape(bsz, L, M_D_INNER) * norm_w).astype(xn.dtype)
    out = yg @ out_proj
    ssm_new = h_fin.reshape(bsz, M_HEADS, M_HEADDIM, M_STATE).astype(xn.dtype)
    return out, new_conv, ssm_new


def peer_ffn(xn, w_q, sub_keys, u_tab, v_tab):
    bsz, L, D = xn.shape
    T = bsz * L
    xt = xn.reshape(T, D)
    q = (xt @ w_q).reshape(T, P_HEADS, 2, P_QDIM // 2)
    s = jnp.einsum('thzd,zhkd->thzk', q, sub_keys).astype(F32)
    sv, si = lax.top_k(s, P_TOPK)
    cand = sv[:, :, 0, :, None] + sv[:, :, 1, None, :]
    cand_idx = si[:, :, 0, :, None] * P_NKEYS + si[:, :, 1, None, :]
    top_s, pos = lax.top_k(cand.reshape(T, P_HEADS, P_TOPK * P_TOPK), P_TOPK)
    eidx = jnp.take_along_axis(cand_idx.reshape(T, P_HEADS, P_TOPK * P_TOPK), pos, axis=-1)
    gate = jax.nn.softmax(top_s, axis=-1).astype(xn.dtype)
    eidx = eidx.reshape(T, P_HEADS * P_TOPK)
    gate = gate.reshape(T, P_HEADS * P_TOPK)
    pad = (-T) % P_BLOCK
    nb = (T + pad) // P_BLOCK
    xb = jnp.pad(xt, ((0, pad), (0, 0))).reshape(nb, P_BLOCK, D)
    ib = jnp.pad(eidx, ((0, pad), (0, 0))).reshape(nb, P_BLOCK, P_HEADS * P_TOPK)
    gb = jnp.pad(gate, ((0, pad), (0, 0))).reshape(nb, P_BLOCK, P_HEADS * P_TOPK)

    def block(args):
        x_blk, i_blk, g_blk = args
        u = u_tab[i_blk]
        v = v_tab[i_blk]
        act = jax.nn.gelu(jnp.einsum('td,ted->te', x_blk, u), approximate=False)
        return jnp.einsum('te,ted->td', g_blk * act, v)

    out = lax.map(block, (xb, ib, gb)).reshape(nb * P_BLOCK, D)[:T]
    return out.reshape(bsz, L, D)


def setup_inputs(seed: int = 0) -> dict:
    key = jax.random.key(seed)
    ks = iter(jax.random.split(key, 64))
    D = D_MODEL

    def nrm(shape, scale=1.0):
        return jax.random.normal(next(ks), shape, F32) * scale

    def unif(shape, lo, hi):
        return jax.random.uniform(next(ks), shape, F32, lo, hi)

    dt0 = jnp.exp(unif((N_B, M_HEADS), math.log(1e-3), math.log(1e-1)))
    return {
        'x_prompt': nrm((BATCH, SEQ, D)),
        'x_sample': nrm((DEC_BATCH, DEC_SEQ, D)),
        'state_rwkv_shift': nrm((N_A, DEC_BATCH, D)),
        'state_rwkv_wkv': nrm((N_A, DEC_BATCH, RW_HEADS, RW_HEAD, RW_HEAD), 0.3),
        'state_mamba_conv': nrm((N_B, DEC_BATCH, M_CONV - 1, M_CONV_DIM)),
        'state_mamba_ssm': nrm((N_B, DEC_BATCH, M_HEADS, M_HEADDIM, M_STATE), 0.1),
        'meta_tokens': nrm((N_META, D)),
        'norm_mix': 1.0 + nrm((DEPTH, D), 0.05),
        'norm_ffn': 1.0 + nrm((DEPTH, D), 0.05),
        'norm_final': 1.0 + nrm((D,), 0.05),
        'rwkv_mix': unif((N_A, 6, D), 0.0, 1.0),
        'rwkv_w_rkv': nrm((N_A, 3, D, D), D ** -0.5),
        'rwkv_w0': unif((N_A, D), -5.0, 1.0),
        'rwkv_w1': nrm((N_A, D, DECAY_LORA), D ** -0.5),
        'rwkv_w2': nrm((N_A, DECAY_LORA, D), 0.1 * DECAY_LORA ** -0.5),
        'rwkv_a0': nrm((N_A, D), 0.1),
        'rwkv_a1': nrm((N_A, D, AAA_LORA), D ** -0.5),
        'rwkv_a2': nrm((N_A, AAA_LORA, D), 0.1 * AAA_LORA ** -0.5),
        'rwkv_g1': nrm((N_A, D, GATE_LORA), D ** -0.5),
        'rwkv_g2': nrm((N_A, GATE_LORA, D), GATE_LORA ** -0.5),
        'rwkv_k_k': 0.85 + nrm((N_A, D), 0.05),
        'rwkv_k_a': 1.0 + nrm((N_A, D), 0.05),
        'rwkv_r_k': nrm((N_A, RW_HEADS, RW_HEAD), 0.1),
        'rwkv_ln_w': 1.0 + nrm((N_A, D), 0.05),
        'rwkv_ln_b': nrm((N_A, D), 0.02),
        'rwkv_w_o': nrm((N_A, D, D), D ** -0.5),
        'mamba_in_proj': nrm((N_B, D, M_IN_DIM), D ** -0.5),
        'mamba_conv_w': nrm((N_B, M_CONV, M_CONV_DIM), M_CONV ** -0.5),
        'mamba_conv_b': nrm((N_B, M_CONV_DIM), 0.02),
        'mamba_dt_bias': dt0 + jnp.log(-jnp.expm1(-dt0)),
        'mamba_a_log': jnp.log(unif((N_B, M_HEADS), 1.0, 16.0)),
        'mamba_d': 1.0 + nrm((N_B, M_HEADS), 0.05),
        'mamba_norm_w': 1.0 + nrm((N_B, M_D_INNER), 0.05),
        'mamba_out_proj': nrm((N_B, M_D_INNER, D), M_D_INNER ** -0.5),
        'peer_w_q': nrm((DEPTH, D, P_HEADS * P_QDIM), D ** -0.5),
        'peer_sub_keys': nrm((DEPTH, 2, P_HEADS, P_NKEYS, P_QDIM // 2), (P_QDIM // 2) ** -0.5),
        'peer_u': nrm((DEPTH, P_EXPERTS, D), D ** -0.5),
        'peer_v': nrm((DEPTH, P_EXPERTS, D), (P_HEADS * P_TOPK) ** -0.5),
    }


def reference(x_prompt, x_sample, state_rwkv_shift, state_rwkv_wkv, state_mamba_conv, state_mamba_ssm,
              meta_tokens, norm_mix, norm_ffn, norm_final,
              rwkv_mix, rwkv_w_rkv, rwkv_w0, rwkv_w1, rwkv_w2, rwkv_a0, rwkv_a1, rwkv_a2,
              rwkv_g1, rwkv_g2, rwkv_k_k, rwkv_k_a, rwkv_r_k, rwkv_ln_w, rwkv_ln_b, rwkv_w_o,
              mamba_in_proj, mamba_conv_w, mamba_conv_b, mamba_dt_bias, mamba_a_log, mamba_d,
              mamba_norm_w, mamba_out_proj,
              peer_w_q, peer_sub_keys, peer_u, peer_v):

    def trunk(h, shift0, wkv0, conv0, ssm0, lead_pad, chunk):
        shifts, wkvs, convs, ssms = [], [], [], []
        for i in range(DEPTH):
            xn = rms_norm(h, norm_mix[i])
            j = i // 2
            if i % 2 == 0:
                out, s_new, wkv_new = rwkv7_time_mix(
                    xn, shift0[j], wkv0[j], rwkv_mix[j], rwkv_w_rkv[j], rwkv_w0[j], rwkv_w1[j],
                    rwkv_w2[j], rwkv_a0[j], rwkv_a1[j], rwkv_a2[j], rwkv_g1[j], rwkv_g2[j],
                    rwkv_k_k[j], rwkv_k_a[j], rwkv_r_k[j], rwkv_ln_w[j], rwkv_ln_b[j], rwkv_w_o[j])
                shifts.append(s_new)
                wkvs.append(wkv_new)
            else:
                out, c_new, ssm_new = mamba2_mix(
                    xn, conv0[j], ssm0[j], mamba_in_proj[j], mamba_conv_w[j], mamba_conv_b[j],
                    mamba_dt_bias[j], mamba_a_log[j], mamba_d[j], mamba_norm_w[j], mamba_out_proj[j],
                    lead_pad, chunk)
                convs.append(c_new)
                ssms.append(ssm_new)
            h = h + out
            h = h + peer_ffn(rms_norm(h, norm_ffn[i]), peer_w_q[i], peer_sub_keys[i], peer_u[i], peer_v[i])
        return (rms_norm(h, norm_final), jnp.stack(shifts), jnp.stack(wkvs),
                jnp.stack(convs), jnp.stack(ssms))

    bp = x_prompt.shape[0]
    dtp = x_prompt.dtype
    meta = jnp.broadcast_to(meta_tokens.astype(dtp)[None], (bp, N_META, D_MODEL))
    hp = jnp.concatenate([meta, x_prompt], axis=1)
    z_shift = jnp.zeros((N_A, bp, D_MODEL), dtp)
    z_wkv = jnp.zeros((N_A, bp, RW_HEADS, RW_HEAD, RW_HEAD), dtp)
    z_conv = jnp.zeros((N_B, bp, M_CONV - 1, M_CONV_DIM), dtp)
    z_ssm = jnp.zeros((N_B, bp, M_HEADS, M_HEADDIM, M_STATE), dtp)
    yp, p_shift, p_wkv, p_conv, p_ssm = trunk(hp, z_shift, z_wkv, z_conv, z_ssm,
                                              M_CHUNK - N_META, M_CHUNK)
    ys, s_shift, s_wkv, s_conv, s_ssm = trunk(x_sample, state_rwkv_shift, state_rwkv_wkv,
                                              state_mamba_conv, state_mamba_ssm, 0, x_sample.shape[1])
    return (yp[:, N_META:], ys, p_shift, p_wkv, p_conv, p_ssm, s_shift, s_wkv, s_conv, s_ssm)
```

```python
import functools
import math

import jax
import jax.numpy as jnp
from jax import lax
from jax.experimental import pallas as pl
from jax.experimental.pallas import tpu as pltpu

F32 = jnp.float32
BF16 = jnp.bfloat16
HIGHEST = lax.Precision.HIGHEST

D_MODEL = 1024
N_META = 16
NORM_EPS = 1e-5
RW_HEAD = 64
RW_HEADS = D_MODEL // RW_HEAD
RW_LN_EPS = 64e-5
LORA_PAD = 128
GATE_LORA_PAD = 256
M_D_INNER = 2048
M_HEADDIM = 64
M_HEADS = M_D_INNER // M_HEADDIM
M_GROUPS = 4
M_HPG = M_HEADS // M_GROUPS
M_STATE = 128
M_CONV = 4
M_CONV_DIM = M_D_INNER + 2 * M_GROUPS * M_STATE
M_IN_DIM = 2 * M_D_INNER + 2 * M_GROUPS * M_STATE + M_HEADS
M_DT_PAD = 128
M_CHUNK = 128
P_HEADS = 8
P_NKEYS = 128
P_QDIM = 256
P_TOPK = 16
P_EXPERT_TILE = 1024

LANES = 128
SUBLANES = 8
VMEM_LIMIT_BYTES = 56 * 1024 * 1024


def _params(*semantics):
    return pltpu.CompilerParams(dimension_semantics=semantics, vmem_limit_bytes=VMEM_LIMIT_BYTES)


def _const_spec(shape):
    zeros = (0,) * len(shape)
    return pl.BlockSpec(shape, lambda *_: zeros)


def _rms(x, g):
    return x * lax.rsqrt(jnp.mean(x * x, axis=-1, keepdims=True) + NORM_EPS) * g


def _softplus(x):
    return jnp.maximum(x, 0.0) + jnp.log1p(jnp.exp(-jnp.abs(x)))


def _bdot(a, b):
    return jnp.dot(a.astype(BF16), b.astype(BF16), preferred_element_type=F32)


def _hdot(a, b):
    return jnp.dot(a, b, precision=HIGHEST, preferred_element_type=F32)


def _norm_kernel(h_ref, g_ref, o_ref):
    o_ref[...] = _rms(h_ref[...], g_ref[...])


def _norm(h, g, tm):
    t, d = h.shape
    return pl.pallas_call(
        _norm_kernel,
        grid=(t // tm,),
        in_specs=[pl.BlockSpec((tm, d), lambda i: (i, 0)), _const_spec((1, d))],
        out_specs=pl.BlockSpec((tm, d), lambda i: (i, 0)),
        out_shape=jax.ShapeDtypeStruct((t, d), F32),
        compiler_params=_params("parallel"),
        name="rms_norm",
    )(h, g.reshape(1, d))


def _rwkv_proj_kernel(xn_ref, prev_ref, mix_ref, vec_ref, wr_ref, wk_ref, wv_ref,
                      w1_ref, w2_ref, a1_ref, a2_ref, g1_ref, g2_ref,
                      r_ref, w_ref, k_ref, v_ref, kk_ref, a_ref, g_ref):
    xn = xn_ref[...]
    dx = prev_ref[...] - xn
    xr, xw, xk, xv, xa, xg = [xn + dx * mix_ref[j:j + 1, :] for j in range(6)]
    w0, a0, k_k, k_a = [vec_ref[j:j + 1, :] for j in range(4)]
    r_ref[...] = _bdot(xr, wr_ref[...])
    k = _bdot(xk, wk_ref[...])
    v_ref[...] = _bdot(xv, wv_ref[...])
    w_log = -_softplus(-(w0 + _hdot(jnp.tanh(_hdot(xw, w1_ref[...])), w2_ref[...]))) - 0.5
    w_ref[...] = jnp.exp(-jnp.exp(w_log))
    a = jax.nn.sigmoid(a0 + _hdot(_hdot(xa, a1_ref[...]), a2_ref[...]))
    a_ref[...] = a
    g_ref[...] = _bdot(jax.nn.sigmoid(_bdot(xg, g1_ref[...])), g2_ref[...])
    kk_ref[...] = k * k_k
    k_ref[...] = k * (1.0 + (a - 1.0) * k_a)


def _rwkv_proj(xn, prev, p, tm):
    t, d = xn.shape
    tok = pl.BlockSpec((tm, d), lambda i: (i, 0))
    weights = [p["mix"], p["vec"], p["wr"], p["wk"], p["wv"], p["w1"], p["w2"], p["a1"], p["a2"],
               p["g1"], p["g2"]]
    return pl.pallas_call(
        _rwkv_proj_kernel,
        grid=(t // tm,),
        in_specs=[tok, tok] + [_const_spec(w.shape) for w in weights],
        out_specs=[tok] * 7,
        out_shape=[jax.ShapeDtypeStruct((t, d), F32)] * 7,
        compiler_params=_params("parallel"),
        name="rwkv_proj",
    )(xn, prev, *weights)


def _rwkv_scan_kernel(r_ref, w_ref, k_ref, v_ref, kk_ref, a_ref, s0_ref, lnw_ref, lnb_ref, rk_ref,
                      o_ref, sfin_ref, state, o_rows):
    tb = pl.program_id(1)
    steps = r_ref.shape[0]
    n = RW_HEAD

    @pl.when(tb == 0)
    def _():
        state[...] = s0_ref[...]

    def step(t, carry):
        r = r_ref[t]
        w = w_ref[t]
        k = k_ref[t]
        vv = v_ref[t]
        kku = kk_ref[t]
        norm = jnp.sqrt(jnp.sum(kku * kku, axis=0, keepdims=True))
        kk = kku / jnp.maximum(norm, 1e-12)
        alpha = -kk
        beta = kk * a_ref[t]

        def row(vi, c):
            s_v = state[vi]
            u = jnp.sum(s_v * alpha, axis=0, keepdims=True)
            s_new = s_v * w + u * beta + v_ref[t, pl.ds(vi, 1), :] * k
            state[vi] = s_new
            o_rows[pl.ds(vi, 1), :] = jnp.sum(s_new * r, axis=0, keepdims=True)
            return c

        lax.fori_loop(0, n, row, 0, unroll=4)
        o = o_rows[...]
        mu = jnp.mean(o, axis=0, keepdims=True)
        var = jnp.mean(jnp.square(o - mu), axis=0, keepdims=True)
        on = (o - mu) * lax.rsqrt(var + RW_LN_EPS)
        bonus = jnp.sum(r * k * rk_ref[...], axis=0, keepdims=True) * vv
        o_ref[t] = on * lnw_ref[...] + lnb_ref[...] + bonus
        return carry

    lax.fori_loop(0, steps, step, 0)

    @pl.when(tb == pl.num_programs(1) - 1)
    def _():
        sfin_ref[...] = state[...]


def _rwkv_scan(seqs, s0, lnw, lnb, rk, tt):
    length, n, probs = seqs[0].shape
    groups = probs // LANES
    seq_spec = pl.BlockSpec((tt, n, LANES), lambda g, i: (i, 0, g))
    st_spec = pl.BlockSpec((n, n, LANES), lambda g, i: (0, 0, g))
    lane_spec = pl.BlockSpec((n, LANES), lambda g, i: (0, 0))
    return pl.pallas_call(
        _rwkv_scan_kernel,
        grid=(groups, length // tt),
        in_specs=[seq_spec] * 6 + [st_spec] + [lane_spec] * 3,
        out_specs=[seq_spec, st_spec],
        out_shape=[jax.ShapeDtypeStruct((length, n, probs), F32),
                   jax.ShapeDtypeStruct((n, n, probs), F32)],
        scratch_shapes=[pltpu.VMEM((n, n, LANES), F32), pltpu.VMEM((n, LANES), F32)],
        compiler_params=_params("arbitrary", "arbitrary"),
        name="rwkv_scan",
    )(*seqs, s0, lnw, lnb, rk)


def _rwkv_out_kernel(o_ref, g_ref, h_ref, wo_ref, out_ref):
    out_ref[...] = h_ref[...] + _bdot(o_ref[...] * g_ref[...], wo_ref[...])


def _rwkv_out(o, g, h, wo, tm):
    t, d = h.shape
    tok = pl.BlockSpec((tm, d), lambda i: (i, 0))
    return pl.pallas_call(
        _rwkv_out_kernel,
        grid=(t // tm,),
        in_specs=[tok, tok, tok, _const_spec(wo.shape)],
        out_specs=tok,
        out_shape=jax.ShapeDtypeStruct((t, d), F32),
        compiler_params=_params("parallel"),
        name="rwkv_out",
    )(o, g, h, wo)


def _mamba_in_kernel(h_ref, g_ref, w_ref, z_ref, xbc_ref, dt_ref):
    xn = _rms(h_ref[...], g_ref[...]).astype(BF16)
    z_ref[...] = jnp.dot(xn, w_ref[:, :M_D_INNER], preferred_element_type=F32)
    xbc_ref[...] = jnp.dot(xn, w_ref[:, M_D_INNER:M_D_INNER + M_CONV_DIM], preferred_element_type=F32)
    dt_ref[...] = jnp.dot(xn, w_ref[:, M_D_INNER + M_CONV_DIM:], preferred_element_type=F32)


def _mamba_in(h, g, w_in, tm):
    t, d = h.shape
    return pl.pallas_call(
        _mamba_in_kernel,
        grid=(t // tm,),
        in_specs=[pl.BlockSpec((tm, d), lambda i: (i, 0)), _const_spec((1, d)), _const_spec(w_in.shape)],
        out_specs=[pl.BlockSpec((tm, M_D_INNER), lambda i: (i, 0)),
                   pl.BlockSpec((tm, M_CONV_DIM), lambda i: (i, 0)),
                   pl.BlockSpec((tm, M_DT_PAD), lambda i: (i, 0))],
        out_shape=[jax.ShapeDtypeStruct((t, M_D_INNER), F32),
                   jax.ShapeDtypeStruct((t, M_CONV_DIM), F32),
                   jax.ShapeDtypeStruct((t, M_DT_PAD), F32)],
        compiler_params=_params("parallel"),
        name="mamba_in_proj",
    )(h, g.reshape(1, d), w_in)


def _mamba_ssd_kernel(xbc_ref, z_ref, dt_ref, h_ref, conv0_ref, ssm0_ref, convw_ref, vec_ref,
                      dvec_ref, expand_ref, normw_ref, wout_ref,
                      out_ref, ssm_ref,
                      conv_buf, dt_buf, z_buf, state, y_buf, *, seq_len, rows):
    c = pl.program_id(1)
    q = M_CHUNK
    halo = SUBLANES

    @pl.when(c == 0)
    def _():
        state[...] = ssm0_ref[0]
        conv_buf[0:halo, :] = conv0_ref[0]

    @pl.when(c > 0)
    def _():
        conv_buf[0:halo, :] = conv_buf[q:q + halo, :]

    if rows < q:
        conv_buf[halo:, :] = jnp.zeros((q, M_CONV_DIM), F32)
        dt_buf[...] = jnp.zeros((q, M_DT_PAD), F32)
        z_buf[...] = jnp.zeros((q, M_D_INNER), F32)
    conv_buf[halo:halo + rows, :] = xbc_ref[0]
    dt_buf[0:rows, :] = dt_ref[0]
    z_buf[0:rows, :] = z_ref[0]

    valid = (lax.broadcasted_iota(jnp.int32, (q, 1), 0) + c * q) < seq_len
    conv = vec_ref[0:1, :]
    for j in range(M_CONV):
        conv = conv + conv_buf[halo - (M_CONV - 1) + j:halo - (M_CONV - 1) + j + q, :] * convw_ref[j:j + 1, :]
    act = jnp.where(valid, jax.nn.silu(conv), 0.0)
    xs = act[:, :M_D_INNER]
    dt = jnp.where(valid, _softplus(dt_buf[...] + dvec_ref[0:1, :]), 0.0)
    adt = dt * dvec_ref[1:2, :]
    li = lax.broadcasted_iota(jnp.int32, (q, q), 0)
    si = lax.broadcasted_iota(jnp.int32, (q, q), 1)
    causal = li >= si
    acs = _hdot(causal.astype(F32), adt)
    acs_t = acs.T
    expand = expand_ref[...]
    dt_full = _hdot(dt, expand)
    acs_full = _hdot(acs, expand)
    last_full = _hdot(acs[q - 1:q, :], expand)
    xdt = xs * dt_full
    xdt_st = xdt * jnp.exp(last_full - acs_full)
    eacs_full = jnp.exp(acs_full)

    for g in range(M_GROUPS):
        b_g = act[:, M_D_INNER + g * M_STATE:M_D_INNER + (g + 1) * M_STATE].astype(BF16)
        c_g = act[:, M_D_INNER + (M_GROUPS + g) * M_STATE:M_D_INNER + (M_GROUPS + g + 1) * M_STATE].astype(BF16)
        cb = lax.dot_general(c_g, b_g, (((1,), (1,)), ((), ())), preferred_element_type=F32)
        for pair in range(M_HPG // 2):
            ys = []
            for r in (g * M_HPG + 2 * pair, g * M_HPG + 2 * pair + 1):
                lo, hi = r * M_HEADDIM, (r + 1) * M_HEADDIM
                decay = jnp.where(causal, jnp.exp(acs[:, r:r + 1] - acs_t[r:r + 1, :]), 0.0)
                y = jnp.dot((cb * decay).astype(BF16), xdt[:, lo:hi].astype(BF16), preferred_element_type=F32)
                h_r = state[r]
                y_off = lax.dot_general(c_g, h_r.astype(BF16), (((1,), (1,)), ((), ())),
                                        preferred_element_type=F32)
                ys.append(y + y_off * eacs_full[:, lo:hi])
                upd = lax.dot_general(xdt_st[:, lo:hi].astype(BF16), b_g, (((0,), (0,)), ((), ())),
                                      preferred_element_type=F32)
                state[r] = h_r * jnp.exp(acs[q - 1:q, r:r + 1]) + upd
            col = (g * M_HPG + 2 * pair) * M_HEADDIM
            y_buf[:, col:col + 2 * M_HEADDIM] = jnp.concatenate(ys, axis=1)

    y = y_buf[...] + vec_ref[1:2, :M_D_INNER] * xs
    yg = y * jax.nn.silu(z_buf[...])
    width = M_D_INNER // M_GROUPS
    parts = []
    for g in range(M_GROUPS):
        part = yg[:, g * width:(g + 1) * width]
        parts.append(part * lax.rsqrt(jnp.mean(part * part, axis=-1, keepdims=True) + NORM_EPS))
    yn = jnp.concatenate(parts, axis=1) * normw_ref[...]
    res = _bdot(yn, wout_ref[...])
    out_ref[0] = h_ref[0] + res[0:rows, :]

    @pl.when(c == pl.num_programs(1) - 1)
    def _():
        ssm_ref[0] = state[...]


def _mamba_ssd(xbc, z, dt, h, conv0, ssm0, p, rows, seq_len):
    bsz, length, _ = xbc.shape
    chunks = pl.cdiv(length, rows)
    weights = [p["conv_w"], p["vec"], p["dvec"], p["expand"], p["norm_w"], p["w_out"]]

    def seq_spec(width):
        return pl.BlockSpec((1, rows, width), lambda b, c: (b, c, 0))

    kern = functools.partial(_mamba_ssd_kernel, seq_len=seq_len, rows=rows)
    return pl.pallas_call(
        kern,
        grid=(bsz, chunks),
        in_specs=[seq_spec(M_CONV_DIM), seq_spec(M_D_INNER), seq_spec(M_DT_PAD), seq_spec(D_MODEL),
                  pl.BlockSpec((1, SUBLANES, M_CONV_DIM), lambda b, c: (b, 0, 0)),
                  pl.BlockSpec((1, M_HEADS, M_HEADDIM, M_STATE), lambda b, c: (b, 0, 0, 0))]
                 + [_const_spec(w.shape) for w in weights],
        out_specs=[seq_spec(D_MODEL),
                   pl.BlockSpec((1, M_HEADS, M_HEADDIM, M_STATE), lambda b, c: (b, 0, 0, 0))],
        out_shape=[jax.ShapeDtypeStruct((bsz, length, D_MODEL), F32),
                   jax.ShapeDtypeStruct((bsz, M_HEADS, M_HEADDIM, M_STATE), F32)],
        scratch_shapes=[pltpu.VMEM((M_CHUNK + SUBLANES, M_CONV_DIM), F32),
                        pltpu.VMEM((M_CHUNK, M_DT_PAD), F32),
                        pltpu.VMEM((M_CHUNK, M_D_INNER), F32),
                        pltpu.VMEM((M_HEADS, M_HEADDIM, M_STATE), F32),
                        pltpu.VMEM((M_CHUNK, M_D_INNER), F32)],
        compiler_params=_params("arbitrary", "arbitrary"),
        name="mamba_ssd",
    )(xbc, z, dt, h, conv0, ssm0, *weights)


_CANDS = [(ra, rb) for ra in range(P_TOPK) for rb in range(P_TOPK) if (ra + 1) * (rb + 1) <= P_TOPK]


def _peer_route_kernel(h_ref, g_ref, wq_ref, keys_ref,
                       xt_ref, r1_ref, e1_ref, m0_ref, w0_ref,
                       q_buf, rank0, best_vals, cand, counts, inv_z):
    tm = h_ref.shape[0]
    neg_inf = float("-inf")
    xn = _rms(h_ref[...], g_ref[...])
    xt = xn.T.astype(BF16)
    xt_ref[...] = xt
    q_buf[...] = jnp.dot(wq_ref[...], xt, preferred_element_type=F32)
    key_iota = lax.broadcasted_iota(jnp.int32, (P_NKEYS, tm), 0).astype(F32)

    def half(m, carry):
        head = m // 2
        z = m % 2
        row0 = pl.multiple_of(m * P_NKEYS, P_NKEYS)
        s = _hdot(keys_ref[m], q_buf[pl.ds(row0, P_NKEYS), :])

        def extract(j, sc):
            cur, rank = sc
            best = jnp.max(cur, axis=0, keepdims=True)
            first = jnp.min(jnp.where(cur == best, key_iota, float(P_NKEYS)), axis=0, keepdims=True)
            sel = key_iota == first
            best_vals[z, j, pl.ds(head, 1), :] = best
            return jnp.where(sel, neg_inf, cur), jnp.where(sel, j.astype(F32), rank)

        _, rank = lax.fori_loop(0, P_TOPK, extract, (s, jnp.full((P_NKEYS, tm), float(P_NKEYS), F32)))
        e = jnp.exp(s - jnp.max(s, axis=0, keepdims=True))

        @pl.when(z == 0)
        def _():
            rank0[head] = rank
            w0_ref[head] = e

        @pl.when(z == 1)
        def _():
            r1_ref[head] = rank
            e1_ref[head] = e

        return carry

    lax.fori_loop(0, 2 * P_HEADS, half, 0)

    ex0 = [jnp.exp(best_vals[0, ra] - best_vals[0, 0]) for ra in range(P_TOPK)]
    ex1 = [jnp.exp(best_vals[1, rb] - best_vals[1, 0]) for rb in range(P_TOPK)]
    for i, (ra, rb) in enumerate(_CANDS):
        cand[i] = best_vals[0, ra] + best_vals[1, rb]
    counts[...] = jnp.zeros(counts.shape, F32)

    def pick(it, z):
        best = cand[0]
        for i in range(1, len(_CANDS)):
            best = jnp.maximum(best, cand[i])
        found = jnp.zeros((P_HEADS, tm), F32)
        for i, (ra, rb) in enumerate(_CANDS):
            ci = cand[i]
            hit = jnp.where(ci == best, 1.0, 0.0) * (1.0 - found)
            found = found + hit
            cand[i] = jnp.where(hit > 0.0, neg_inf, ci)
            counts[ra] = counts[ra] + hit
            z = z + hit * (ex0[ra] * ex1[rb])
        return z

    inv_z[...] = 1.0 / lax.fori_loop(0, P_TOPK, pick, jnp.zeros((P_HEADS, tm), F32))

    def finish(head, carry):
        rk = rank0[head]
        m0 = jnp.zeros((P_NKEYS, tm), F32)
        for ra in range(P_TOPK):
            m0 = jnp.where(rk == float(ra), counts[ra, pl.ds(head, 1), :], m0)
        m0_ref[head] = m0
        w0_ref[head] = w0_ref[head] * inv_z[pl.ds(head, 1), :]
        return carry

    lax.fori_loop(0, P_HEADS, finish, 0)


def _peer_route(h, g, wq_t, keys, tm):
    t, d = h.shape
    tile = pl.BlockSpec((P_HEADS, P_NKEYS, tm), lambda i: (0, 0, i))
    tile_shape = jax.ShapeDtypeStruct((P_HEADS, P_NKEYS, t), F32)
    return pl.pallas_call(
        _peer_route_kernel,
        grid=(t // tm,),
        in_specs=[pl.BlockSpec((tm, d), lambda i: (i, 0)), _const_spec((1, d)),
                  _const_spec(wq_t.shape), _const_spec(keys.shape)],
        out_specs=[pl.BlockSpec((d, tm), lambda i: (0, i)), tile, tile, tile, tile],
        out_shape=[jax.ShapeDtypeStruct((d, t), BF16), tile_shape, tile_shape, tile_shape, tile_shape],
        scratch_shapes=[pltpu.VMEM((P_HEADS * P_QDIM, tm), F32),
                        pltpu.VMEM((P_HEADS, P_NKEYS, tm), F32),
                        pltpu.VMEM((2, P_TOPK, P_HEADS, tm), F32),
                        pltpu.VMEM((len(_CANDS), P_HEADS, tm), F32),
                        pltpu.VMEM((P_TOPK, P_HEADS, tm), F32),
                        pltpu.VMEM((P_HEADS, tm), F32)],
        compiler_params=_params("parallel"),
        name="peer_route",
    )(h, g.reshape(1, d), wq_t, keys)


def _gelu(x):
    return 0.5 * x * (1.0 + lax.erf(x * (1.0 / math.sqrt(2.0))))


def _peer_dense_kernel(xt_ref, u_ref, vt_ref, r1_ref, e1_ref, m0_ref, w0_ref, h_ref, gfin_ref,
                       out_ref, acc, hid, prob, *, final_norm):
    j = pl.program_id(1)
    keys_per_step = P_EXPERT_TILE // P_NKEYS

    @pl.when(j == 0)
    def _():
        acc[...] = jnp.zeros(acc.shape, F32)

    hid[...] = jnp.dot(u_ref[...], xt_ref[...], preferred_element_type=F32)
    for ii in range(keys_per_step):
        i1 = j * keys_per_step + ii
        gate = None
        for head in range(P_HEADS):
            m0 = m0_ref[head, pl.ds(i1, 1), :]
            w0 = w0_ref[head, pl.ds(i1, 1), :]
            term = w0 * jnp.where(r1_ref[head] < m0, e1_ref[head], 0.0)
            gate = term if gate is None else gate + term
        rows = slice(ii * P_NKEYS, (ii + 1) * P_NKEYS)
        prob[rows, :] = (gate * _gelu(hid[rows, :])).astype(BF16)
    acc[...] += jnp.dot(vt_ref[...], prob[...], preferred_element_type=F32)

    @pl.when(j == pl.num_programs(1) - 1)
    def _():
        res = h_ref[...] + acc[...].T
        if final_norm:
            res = _rms(res, gfin_ref[...])
        out_ref[...] = res


def _peer_dense(xt, u_bf, vt_bf, r1, e1, m0, w0, h, g_final, tm, final_norm):
    t, d = h.shape
    n_exp = u_bf.shape[0]
    tile = pl.BlockSpec((P_HEADS, P_NKEYS, tm), lambda i, j: (0, 0, i))
    kern = functools.partial(_peer_dense_kernel, final_norm=final_norm)
    return pl.pallas_call(
        kern,
        grid=(t // tm, n_exp // P_EXPERT_TILE),
        in_specs=[pl.BlockSpec((d, tm), lambda i, j: (0, i)),
                  pl.BlockSpec((P_EXPERT_TILE, d), lambda i, j: (j, 0)),
                  pl.BlockSpec((d, P_EXPERT_TILE), lambda i, j: (0, j)),
                  tile, tile, tile, tile,
                  pl.BlockSpec((tm, d), lambda i, j: (i, 0)),
                  _const_spec((1, d))],
        out_specs=pl.BlockSpec((tm, d), lambda i, j: (i, 0)),
        out_shape=jax.ShapeDtypeStruct((t, d), F32),
        scratch_shapes=[pltpu.VMEM((d, tm), F32),
                        pltpu.VMEM((P_EXPERT_TILE, tm), F32),
                        pltpu.VMEM((P_EXPERT_TILE, tm), BF16)],
        compiler_params=_params("parallel", "arbitrary"),
        name="peer_dense",
    )(xt, u_bf, vt_bf, r1, e1, m0, w0, h, g_final.reshape(1, d))


def _pad_cols(w, n):
    return jnp.pad(w, ((0, 0), (0, n - w.shape[1])))


def _pad_rows(w, n):
    return jnp.pad(w, ((0, n - w.shape[0]), (0, 0)))


def _prep_rwkv(mix, w_rkv, w0, w1, w2, a0, a1, a2, g1, g2, k_k, k_a, r_k, ln_w, ln_b, w_o):
    d = D_MODEL
    lane = lambda x: jnp.tile(x.reshape(RW_HEADS, RW_HEAD).T, (1, LANES // RW_HEADS))
    return {
        "mix": _pad_rows(mix, SUBLANES),
        "vec": _pad_rows(jnp.stack([w0, a0, k_k, k_a]), SUBLANES),
        "wr": w_rkv[0].astype(BF16), "wk": w_rkv[1].astype(BF16), "wv": w_rkv[2].astype(BF16),
        "w1": _pad_cols(w1, LORA_PAD), "w2": _pad_rows(w2, LORA_PAD),
        "a1": _pad_cols(a1, LORA_PAD), "a2": _pad_rows(a2, LORA_PAD),
        "g1": _pad_cols(g1, GATE_LORA_PAD).astype(BF16), "g2": _pad_rows(g2, GATE_LORA_PAD).astype(BF16),
        "lnw": lane(ln_w), "lnb": lane(ln_b), "rk": lane(r_k.reshape(d)),
        "wo": w_o.astype(BF16),
    }


def _prep_mamba(in_proj, conv_w, conv_b, dt_bias, a_log, d_skip, norm_w, out_proj):
    head_of_channel = jnp.arange(M_D_INNER) // M_HEADDIM
    expand = (jnp.arange(M_DT_PAD)[:, None] == head_of_channel[None, :]).astype(F32)
    d_full = jnp.pad(jnp.repeat(d_skip, M_HEADDIM), (0, M_CONV_DIM - M_D_INNER))
    pad_h = lambda x: jnp.pad(x, (0, M_DT_PAD - M_HEADS))
    return {
        "w_in": _pad_cols(in_proj, M_IN_DIM - M_HEADS + M_DT_PAD).astype(BF16),
        "conv_w": _pad_rows(conv_w, SUBLANES),
        "vec": _pad_rows(jnp.stack([conv_b, d_full]), SUBLANES),
        "dvec": _pad_rows(jnp.stack([pad_h(dt_bias), pad_h(-jnp.exp(a_log))]), SUBLANES),
        "expand": expand,
        "norm_w": norm_w.reshape(1, M_D_INNER),
        "w_out": out_proj.astype(BF16),
    }


def _prep_peer(w_q, sub_keys, u_tab, v_tab):
    keys = jnp.transpose(sub_keys, (1, 0, 2, 3)).reshape(2 * P_HEADS, P_NKEYS, P_QDIM // 2)
    return {"wq_t": w_q.T.astype(BF16), "keys": keys, "u": u_tab.astype(BF16), "vt": v_tab.T.astype(BF16)}


def _to_scan(x, bsz, length):
    x = x.reshape(bsz, length, RW_HEADS, RW_HEAD)
    return jnp.transpose(x, (1, 3, 0, 2)).reshape(length, RW_HEAD, bsz * RW_HEADS)


def _from_scan(x, bsz, length):
    x = x.reshape(length, RW_HEAD, bsz, RW_HEADS)
    return jnp.transpose(x, (2, 0, 3, 1)).reshape(bsz * length, D_MODEL)


def _rwkv_layer(h, g_mix, shift0, wkv0, p, bsz, length, tm, tt):
    xn = _norm(h, g_mix, tm)
    xn3 = xn.reshape(bsz, length, D_MODEL)
    prev = jnp.concatenate([shift0[:, None, :], xn3[:, :-1]], axis=1).reshape(bsz * length, D_MODEL)
    r, w, k, v, kk, a, g = _rwkv_proj(xn, prev, p, tm)
    seqs = [_to_scan(x, bsz, length) for x in (r, w, k, v, kk, a)]
    s0 = jnp.transpose(wkv0, (2, 3, 0, 1)).reshape(RW_HEAD, RW_HEAD, bsz * RW_HEADS)
    o, s_fin = _rwkv_scan(seqs, s0, p["lnw"], p["lnb"], p["rk"], tt)
    o = _from_scan(o, bsz, length)
    wkv = jnp.transpose(s_fin.reshape(RW_HEAD, RW_HEAD, bsz, RW_HEADS), (2, 3, 0, 1))
    return _rwkv_out(o, g, h, p["wo"], tm), xn3[:, -1], wkv


def _mamba_layer(h, g_mix, conv0, ssm0, p, bsz, length, tm, rows):
    z, xbc, dt = _mamba_in(h, g_mix, p["w_in"], tm)
    xbc3 = xbc.reshape(bsz, length, M_CONV_DIM)
    conv_new = jnp.concatenate([conv0, xbc3], axis=1)[:, length:]
    pad_t = (-length) % SUBLANES if length < rows else 0
    seq = lambda x, c: jnp.pad(x.reshape(bsz, length, c), ((0, 0), (0, pad_t), (0, 0)))
    conv0_p = jnp.pad(conv0, ((0, 0), (SUBLANES - (M_CONV - 1), 0), (0, 0)))
    out, ssm = _mamba_ssd(seq(xbc, M_CONV_DIM), seq(z, M_D_INNER), seq(dt, M_DT_PAD), seq(h, D_MODEL),
                          conv0_p, ssm0, p, min(rows, length + pad_t), length)
    return out[:, :length].reshape(bsz * length, D_MODEL), conv_new, ssm


def _peer_layer(h, g_ffn, p, tm, g_final=None):
    xt, r1, e1, m0, w0 = _peer_route(h, g_ffn, p["wq_t"], p["keys"], tm)
    final = g_final is not None
    return _peer_dense(xt, p["u"], p["vt"], r1, e1, m0, w0, h, g_final if final else g_ffn, tm, final)


def _trunk(h, shift0, wkv0, conv0, ssm0, norm_mix, norm_ffn, norm_final, rwkv_p, mamba_p, peer_p,
           bsz, length, tm, tt, peer_tm, rows):
    h, shift, wkv = _rwkv_layer(h, norm_mix[0], shift0, wkv0, rwkv_p, bsz, length, tm, tt)
    h = _peer_layer(h, norm_ffn[0], peer_p[0], peer_tm)
    h, conv, ssm = _mamba_layer(h, norm_mix[1], conv0, ssm0, mamba_p, bsz, length, tm, rows)
    y = _peer_layer(h, norm_ffn[1], peer_p[1], peer_tm, g_final=norm_final)
    return y.reshape(bsz, length, D_MODEL), shift[None], wkv[None], conv[None], ssm[None]


def kernel(x_prompt, x_sample, state_rwkv_shift, state_rwkv_wkv, state_mamba_conv, state_mamba_ssm, meta_tokens, norm_mix, norm_ffn, norm_final, rwkv_mix, rwkv_w_rkv, rwkv_w0, rwkv_w1, rwkv_w2, rwkv_a0, rwkv_a1, rwkv_a2, rwkv_g1, rwkv_g2, rwkv_k_k, rwkv_k_a, rwkv_r_k, rwkv_ln_w, rwkv_ln_b, rwkv_w_o, mamba_in_proj, mamba_conv_w, mamba_conv_b, mamba_dt_bias, mamba_a_log, mamba_d, mamba_norm_w, mamba_out_proj, peer_w_q, peer_sub_keys, peer_u, peer_v):
    rwkv_p = _prep_rwkv(rwkv_mix[0], rwkv_w_rkv[0], rwkv_w0[0], rwkv_w1[0], rwkv_w2[0], rwkv_a0[0],
                        rwkv_a1[0], rwkv_a2[0], rwkv_g1[0], rwkv_g2[0], rwkv_k_k[0], rwkv_k_a[0],
                        rwkv_r_k[0], rwkv_ln_w[0], rwkv_ln_b[0], rwkv_w_o[0])
    mamba_p = _prep_mamba(mamba_in_proj[0], mamba_conv_w[0], mamba_conv_b[0], mamba_dt_bias[0],
                          mamba_a_log[0], mamba_d[0], mamba_norm_w[0], mamba_out_proj[0])
    peer_p = [_prep_peer(peer_w_q[i], peer_sub_keys[i], peer_u[i], peer_v[i]) for i in range(2)]

    bp, lp = x_prompt.shape[0], x_prompt.shape[1] + N_META
    meta = jnp.broadcast_to(meta_tokens[None], (bp, N_META, D_MODEL))
    hp = jnp.concatenate([meta, x_prompt], axis=1).reshape(bp * lp, D_MODEL)
    zeros = lambda *s: jnp.zeros(s, F32)
    yp, p_shift, p_wkv, p_conv, p_ssm = _trunk(
        hp, zeros(bp, D_MODEL), zeros(bp, RW_HEADS, RW_HEAD, RW_HEAD), zeros(bp, M_CONV - 1, M_CONV_DIM),
        zeros(bp, M_HEADS, M_HEADDIM, M_STATE), norm_mix, norm_ffn, norm_final, rwkv_p, mamba_p, peer_p,
        bp, lp, tm=344, tt=48, peer_tm=384, rows=M_CHUNK)

    bs, ls = x_sample.shape[0], x_sample.shape[1]
    ys, s_shift, s_wkv, s_conv, s_ssm = _trunk(
        x_sample.reshape(bs * ls, D_MODEL), state_rwkv_shift[0], state_rwkv_wkv[0], state_mamba_conv[0],
        state_mamba_ssm[0], norm_mix, norm_ffn, norm_final, rwkv_p, mamba_p, peer_p,
        bs, ls, tm=256, tt=ls, peer_tm=512, rows=M_CHUNK)
    return (yp[:, N_META:], ys, p_shift, p_wkv, p_conv, p_ssm, s_shift, s_wkv, s_conv, s_ssm)
```

```python
import functools
import math

import jax
import jax.numpy as jnp
from jax import lax
from jax.experimental import pallas as pl
from jax.experimental.pallas import tpu as pltpu

F32 = jnp.float32
BF16 = jnp.bfloat16
HIGHEST = lax.Precision.HIGHEST

D_MODEL = 1024
N_META = 16
NORM_EPS = 1e-5
RW_HEAD = 64
RW_HEADS = D_MODEL // RW_HEAD
RW_LN_EPS = 64e-5
LORA_PAD = 128
GATE_LORA_PAD = 256
M_D_INNER = 2048
M_HEADDIM = 64
M_HEADS = M_D_INNER // M_HEADDIM
M_GROUPS = 4
M_HPG = M_HEADS // M_GROUPS
M_STATE = 128
M_CONV = 4
M_CONV_DIM = M_D_INNER + 2 * M_GROUPS * M_STATE
M_IN_DIM = 2 * M_D_INNER + 2 * M_GROUPS * M_STATE + M_HEADS
M_DT_PAD = 128
M_CHUNK = 128
P_HEADS = 8
P_NKEYS = 128
P_QDIM = 256
P_TOPK = 16
P_EXPERT_TILE = 1024
P_EXPERT_SUB = 256

LANES = 128
SUBLANES = 8
VMEM_LIMIT_BYTES = 56 * 1024 * 1024


def _params(*semantics):
    return pltpu.CompilerParams(dimension_semantics=semantics, vmem_limit_bytes=VMEM_LIMIT_BYTES)


def _const_spec(shape):
    zeros = (0,) * len(shape)
    return pl.BlockSpec(shape, lambda *_: zeros)


def _rms(x, g):
    return x * lax.rsqrt(jnp.mean(x * x, axis=-1, keepdims=True) + NORM_EPS) * g


def _softplus(x):
    return jnp.maximum(x, 0.0) + jnp.log1p(jnp.exp(-jnp.abs(x)))


def _bdot(a, b):
    return jnp.dot(a.astype(BF16), b.astype(BF16), preferred_element_type=F32)


def _hdot(a, b):
    return jnp.dot(a, b, precision=HIGHEST, preferred_element_type=F32)


def _norm_kernel(h_ref, g_ref, o_ref):
    o_ref[...] = _rms(h_ref[...], g_ref[...])


def _norm(h, g, tm):
    t, d = h.shape
    return pl.pallas_call(
        _norm_kernel,
        grid=(t // tm,),
        in_specs=[pl.BlockSpec((tm, d), lambda i: (i, 0)), _const_spec((1, d))],
        out_specs=pl.BlockSpec((tm, d), lambda i: (i, 0)),
        out_shape=jax.ShapeDtypeStruct((t, d), F32),
        compiler_params=_params("parallel"),
        name="rms_norm",
    )(h, g.reshape(1, d))


def _rwkv_proj_kernel(xn_ref, prev_ref, mix_ref, vec_ref, wr_ref, wk_ref, wv_ref,
                      w1_ref, w2_ref, a1_ref, a2_ref, g1_ref, g2_ref,
                      r_ref, w_ref, k_ref, v_ref, kk_ref, a_ref, g_ref):
    xn = xn_ref[...]
    dx = prev_ref[...] - xn
    xr, xw, xk, xv, xa, xg = [xn + dx * mix_ref[j:j + 1, :] for j in range(6)]
    w0, a0, k_k, k_a = [vec_ref[j:j + 1, :] for j in range(4)]
    r_ref[...] = _bdot(xr, wr_ref[...])
    k = _bdot(xk, wk_ref[...])
    v_ref[...] = _bdot(xv, wv_ref[...])
    w_log = -_softplus(-(w0 + _hdot(jnp.tanh(_hdot(xw, w1_ref[...])), w2_ref[...]))) - 0.5
    w_ref[...] = jnp.exp(-jnp.exp(w_log))
    a = jax.nn.sigmoid(a0 + _hdot(_hdot(xa, a1_ref[...]), a2_ref[...]))
    a_ref[...] = a
    g_ref[...] = _bdot(jax.nn.sigmoid(_bdot(xg, g1_ref[...])), g2_ref[...])
    kk_ref[...] = k * k_k
    k_ref[...] = k * (1.0 + (a - 1.0) * k_a)


def _rwkv_proj(xn, prev, p, tm):
    t, d = xn.shape
    tok = pl.BlockSpec((tm, d), lambda i: (i, 0))
    weights = [p["mix"], p["vec"], p["wr"], p["wk"], p["wv"], p["w1"], p["w2"], p["a1"], p["a2"],
               p["g1"], p["g2"]]
    return pl.pallas_call(
        _rwkv_proj_kernel,
        grid=(t // tm,),
        in_specs=[tok, tok] + [_const_spec(w.shape) for w in weights],
        out_specs=[tok] * 7,
        out_shape=[jax.ShapeDtypeStruct((t, d), F32)] * 7,
        compiler_params=_params("parallel"),
        name="rwkv_proj",
    )(xn, prev, *weights)


def _rwkv_scan_kernel(r_ref, w_ref, k_ref, v_ref, kk_ref, a_ref, s0_ref, lnw_ref, lnb_ref, rk_ref,
                      o_ref, sfin_ref, state, o_rows):
    tb = pl.program_id(1)
    steps = r_ref.shape[0]

    @pl.when(tb == 0)
    def _():
        state[...] = s0_ref[...]

    def step(t, carry):
        r = r_ref[t]
        w = w_ref[t]
        k = k_ref[t]
        vv = v_ref[t]
        kku = kk_ref[t]
        norm = jnp.sqrt(jnp.sum(kku * kku, axis=0, keepdims=True))
        kk = kku / jnp.maximum(norm, 1e-12)
        alpha = -kk
        beta = kk * a_ref[t]

        def row(vi, c):
            s_v = state[vi]
            u = jnp.sum(s_v * alpha, axis=0, keepdims=True)
            s_new = s_v * w + u * beta + v_ref[t, pl.ds(vi, 1), :] * k
            state[vi] = s_new
            o_rows[pl.ds(vi, 1), :] = jnp.sum(s_new * r, axis=0, keepdims=True)
            return c

        lax.fori_loop(0, RW_HEAD, row, 0, unroll=8)
        o = o_rows[...]
        mu = jnp.mean(o, axis=0, keepdims=True)
        var = jnp.mean(jnp.square(o - mu), axis=0, keepdims=True)
        on = (o - mu) * lax.rsqrt(var + RW_LN_EPS)
        bonus = jnp.sum(r * k * rk_ref[...], axis=0, keepdims=True) * vv
        o_ref[t] = on * lnw_ref[...] + lnb_ref[...] + bonus
        return carry

    lax.fori_loop(0, steps, step, 0)

    @pl.when(tb == pl.num_programs(1) - 1)
    def _():
        sfin_ref[...] = state[...]


def _rwkv_scan(seqs, s0, lnw, lnb, rk, tt):
    length, n, probs = seqs[0].shape
    groups = probs // LANES
    seq_spec = pl.BlockSpec((tt, n, LANES), lambda g, i: (i, 0, g))
    st_spec = pl.BlockSpec((n, n, LANES), lambda g, i: (0, 0, g))
    lane_spec = pl.BlockSpec((n, LANES), lambda g, i: (0, 0))
    return pl.pallas_call(
        _rwkv_scan_kernel,
        grid=(groups, length // tt),
        in_specs=[seq_spec] * 6 + [st_spec] + [lane_spec] * 3,
        out_specs=[seq_spec, st_spec],
        out_shape=[jax.ShapeDtypeStruct((length, n, probs), F32),
                   jax.ShapeDtypeStruct((n, n, probs), F32)],
        scratch_shapes=[pltpu.VMEM((n, n, LANES), F32), pltpu.VMEM((n, LANES), F32)],
        compiler_params=_params("arbitrary", "arbitrary"),
        name="rwkv_scan",
    )(*seqs, s0, lnw, lnb, rk)


def _rwkv_out_kernel(o_ref, g_ref, h_ref, wo_ref, out_ref):
    out_ref[...] = h_ref[...] + _bdot(o_ref[...] * g_ref[...], wo_ref[...])


def _rwkv_out(o, g, h, wo, tm):
    t, d = h.shape
    tok = pl.BlockSpec((tm, d), lambda i: (i, 0))
    return pl.pallas_call(
        _rwkv_out_kernel,
        grid=(t // tm,),
        in_specs=[tok, tok, tok, _const_spec(wo.shape)],
        out_specs=tok,
        out_shape=jax.ShapeDtypeStruct((t, d), F32),
        compiler_params=_params("parallel"),
        name="rwkv_out",
    )(o, g, h, wo)


def _mamba_in_kernel(h_ref, g_ref, w_ref, z_ref, xbc_ref, dt_ref):
    xn = _rms(h_ref[...], g_ref[...]).astype(BF16)
    z_ref[...] = jnp.dot(xn, w_ref[:, :M_D_INNER], preferred_element_type=F32)
    xbc_ref[...] = jnp.dot(xn, w_ref[:, M_D_INNER:M_D_INNER + M_CONV_DIM], preferred_element_type=F32)
    dt_ref[...] = jnp.dot(xn, w_ref[:, M_D_INNER + M_CONV_DIM:], preferred_element_type=F32)


def _mamba_in(h, g, w_in, tm):
    t, d = h.shape
    return pl.pallas_call(
        _mamba_in_kernel,
        grid=(t // tm,),
        in_specs=[pl.BlockSpec((tm, d), lambda i: (i, 0)), _const_spec((1, d)), _const_spec(w_in.shape)],
        out_specs=[pl.BlockSpec((tm, M_D_INNER), lambda i: (i, 0)),
                   pl.BlockSpec((tm, M_CONV_DIM), lambda i: (i, 0)),
                   pl.BlockSpec((tm, M_DT_PAD), lambda i: (i, 0))],
        out_shape=[jax.ShapeDtypeStruct((t, M_D_INNER), F32),
                   jax.ShapeDtypeStruct((t, M_CONV_DIM), F32),
                   jax.ShapeDtypeStruct((t, M_DT_PAD), F32)],
        compiler_params=_params("parallel"),
        name="mamba_in_proj",
    )(h, g.reshape(1, d), w_in)


def _mamba_ssd_kernel(xbc_ref, z_ref, dt_ref, h_ref, conv0_ref, ssm0_ref, convw_ref, vec_ref,
                      dvec_ref, expand_ref, normw_ref, wout_ref,
                      out_ref, ssm_ref,
                      conv_buf, dt_buf, z_buf, state, y_buf, *, seq_len, rows):
    c = pl.program_id(1)
    q = M_CHUNK
    halo = SUBLANES

    @pl.when(c == 0)
    def _():
        state[...] = ssm0_ref[0]
        conv_buf[0:halo, :] = conv0_ref[0]

    @pl.when(c > 0)
    def _():
        conv_buf[0:halo, :] = conv_buf[q:q + halo, :]

    if rows < q:
        conv_buf[halo:, :] = jnp.zeros((q, M_CONV_DIM), F32)
        dt_buf[...] = jnp.zeros((q, M_DT_PAD), F32)
        z_buf[...] = jnp.zeros((q, M_D_INNER), F32)
    conv_buf[halo:halo + rows, :] = xbc_ref[0]
    dt_buf[0:rows, :] = dt_ref[0]
    z_buf[0:rows, :] = z_ref[0]

    valid = (lax.broadcasted_iota(jnp.int32, (q, 1), 0) + c * q) < seq_len
    conv = vec_ref[0:1, :]
    for j in range(M_CONV):
        conv = conv + conv_buf[halo - (M_CONV - 1) + j:halo - (M_CONV - 1) + j + q, :] * convw_ref[j:j + 1, :]
    act = jnp.where(valid, jax.nn.silu(conv), 0.0)
    xs = act[:, :M_D_INNER]
    dt = jnp.where(valid, _softplus(dt_buf[...] + dvec_ref[0:1, :]), 0.0)
    adt = dt * dvec_ref[1:2, :]
    li = lax.broadcasted_iota(jnp.int32, (q, q), 0)
    si = lax.broadcasted_iota(jnp.int32, (q, q), 1)
    causal = li >= si
    acs = _hdot(causal.astype(F32), adt)
    acs_t = acs.T
    expand = expand_ref[...]
    dt_full = _hdot(dt, expand)
    acs_full = _hdot(acs, expand)
    last_full = _hdot(acs[q - 1:q, :], expand)
    xdt = xs * dt_full
    xdt_st = xdt * jnp.exp(last_full - acs_full)
    eacs_full = jnp.exp(acs_full)

    for g in range(M_GROUPS):
        b_g = act[:, M_D_INNER + g * M_STATE:M_D_INNER + (g + 1) * M_STATE].astype(BF16)
        c_g = act[:, M_D_INNER + (M_GROUPS + g) * M_STATE:M_D_INNER + (M_GROUPS + g + 1) * M_STATE].astype(BF16)
        cb = lax.dot_general(c_g, b_g, (((1,), (1,)), ((), ())), preferred_element_type=F32)
        for pair in range(M_HPG // 2):
            ys = []
            for r in (g * M_HPG + 2 * pair, g * M_HPG + 2 * pair + 1):
                lo, hi = r * M_HEADDIM, (r + 1) * M_HEADDIM
                decay = jnp.where(causal, jnp.exp(acs[:, r:r + 1] - acs_t[r:r + 1, :]), 0.0)
                y = jnp.dot((cb * decay).astype(BF16), xdt[:, lo:hi].astype(BF16), preferred_element_type=F32)
                h_r = state[r]
                y_off = lax.dot_general(c_g, h_r.astype(BF16), (((1,), (1,)), ((), ())),
                                        preferred_element_type=F32)
                ys.append(y + y_off * eacs_full[:, lo:hi])
                upd = lax.dot_general(xdt_st[:, lo:hi].astype(BF16), b_g, (((0,), (0,)), ((), ())),
                                      preferred_element_type=F32)
                state[r] = h_r * jnp.exp(acs[q - 1:q, r:r + 1]) + upd
            col = (g * M_HPG + 2 * pair) * M_HEADDIM
            y_buf[:, col:col + 2 * M_HEADDIM] = jnp.concatenate(ys, axis=1)

    y = y_buf[...] + vec_ref[1:2, :M_D_INNER] * xs
    yg = y * jax.nn.silu(z_buf[...])
    width = M_D_INNER // M_GROUPS
    parts = []
    for g in range(M_GROUPS):
        part = yg[:, g * width:(g + 1) * width]
        parts.append(part * lax.rsqrt(jnp.mean(part * part, axis=-1, keepdims=True) + NORM_EPS))
    yn = jnp.concatenate(parts, axis=1) * normw_ref[...]
    res = _bdot(yn, wout_ref[...])
    out_ref[0] = h_ref[0] + res[0:rows, :]

    @pl.when(c == pl.num_programs(1) - 1)
    def _():
        ssm_ref[0] = state[...]


def _mamba_ssd(xbc, z, dt, h, conv0, ssm0, p, rows, seq_len):
    bsz, length, _ = xbc.shape
    chunks = pl.cdiv(length, rows)
    weights = [p["conv_w"], p["vec"], p["dvec"], p["expand"], p["norm_w"], p["w_out"]]

    def seq_spec(width):
        return pl.BlockSpec((1, rows, width), lambda b, c: (b, c, 0))

    kern = functools.partial(_mamba_ssd_kernel, seq_len=seq_len, rows=rows)
    return pl.pallas_call(
        kern,
        grid=(bsz, chunks),
        in_specs=[seq_spec(M_CONV_DIM), seq_spec(M_D_INNER), seq_spec(M_DT_PAD), seq_spec(D_MODEL),
                  pl.BlockSpec((1, SUBLANES, M_CONV_DIM), lambda b, c: (b, 0, 0)),
                  pl.BlockSpec((1, M_HEADS, M_HEADDIM, M_STATE), lambda b, c: (b, 0, 0, 0))]
                 + [_const_spec(w.shape) for w in weights],
        out_specs=[seq_spec(D_MODEL),
                   pl.BlockSpec((1, M_HEADS, M_HEADDIM, M_STATE), lambda b, c: (b, 0, 0, 0))],
        out_shape=[jax.ShapeDtypeStruct((bsz, length, D_MODEL), F32),
                   jax.ShapeDtypeStruct((bsz, M_HEADS, M_HEADDIM, M_STATE), F32)],
        scratch_shapes=[pltpu.VMEM((M_CHUNK + SUBLANES, M_CONV_DIM), F32),
                        pltpu.VMEM((M_CHUNK, M_DT_PAD), F32),
                        pltpu.VMEM((M_CHUNK, M_D_INNER), F32),
                        pltpu.VMEM((M_HEADS, M_HEADDIM, M_STATE), F32),
                        pltpu.VMEM((M_CHUNK, M_D_INNER), F32)],
        compiler_params=_params("arbitrary", "arbitrary"),
        name="mamba_ssd",
    )(xbc, z, dt, h, conv0, ssm0, *weights)


_CANDS = [(ra, rb) for ra in range(P_TOPK) for rb in range(P_TOPK) if (ra + 1) * (rb + 1) <= P_TOPK]


def _peer_route_kernel(h_ref, g_ref, wq_ref, keys_ref,
                       xt_ref, r1_ref, e1_ref, m0_ref, w0_ref,
                       q_buf, s_buf, rank0, best_vals, cand, counts, inv_z):
    tm = h_ref.shape[0]
    neg_inf = float("-inf")
    xn = _rms(h_ref[...], g_ref[...])
    xt = xn.T.astype(BF16)
    xt_ref[...] = xt
    q_buf[...] = jnp.dot(wq_ref[...], xt, preferred_element_type=F32)
    key_iota = lax.broadcasted_iota(jnp.int32, (P_NKEYS, tm), 0).astype(F32)

    def half(m, carry):
        head = m // 2
        z = m % 2
        row0 = pl.multiple_of(m * P_NKEYS, P_NKEYS)
        s = _hdot(keys_ref[m], q_buf[pl.ds(row0, P_NKEYS), :])

        s_buf[...] = s

        def next_distinct(j, prev):
            sj = s_buf[...]
            best = jnp.max(jnp.where(sj < prev, sj, neg_inf), axis=0, keepdims=True)
            best_vals[z, j, pl.ds(head, 1), :] = best
            return best

        def extract_ties(j, sc):
            cur, rank = sc
            best = jnp.max(cur, axis=0, keepdims=True)
            first = jnp.min(jnp.where(cur == best, key_iota, float(P_NKEYS)), axis=0, keepdims=True)
            sel = key_iota == first
            best_vals[z, j, pl.ds(head, 1), :] = best
            return jnp.where(sel, neg_inf, cur), jnp.where(sel, j.astype(F32), rank)

        last = lax.fori_loop(0, P_TOPK, next_distinct, jnp.full((1, tm), float("inf"), F32))
        reached = jnp.sum(jnp.where(s >= last, 1.0, 0.0), axis=0, keepdims=True)
        has_ties = jnp.max(jnp.abs(reached - float(P_TOPK))) > 0.0

        def rank_by_count():
            rank = jnp.zeros((P_NKEYS, tm), F32)
            for jj in range(P_TOPK):
                rank = rank + jnp.where(best_vals[z, jj, pl.ds(head, 1), :] > s, 1.0, 0.0)
            return rank

        def rank_with_ties():
            no_rank = jnp.full((P_NKEYS, tm), float(P_NKEYS), F32)
            return lax.fori_loop(0, P_TOPK, extract_ties, (s, no_rank))[1]

        rank = lax.cond(has_ties, rank_with_ties, rank_by_count)
        e = jnp.exp(s - jnp.max(s, axis=0, keepdims=True))

        @pl.when(z == 0)
        def _():
            rank0[head] = rank
            w0_ref[head] = e

        @pl.when(z == 1)
        def _():
            r1_ref[head] = rank.astype(BF16)
            e1_ref[head] = e.astype(BF16)

        return carry

    lax.fori_loop(0, 2 * P_HEADS, half, 0)

    ex0 = [jnp.exp(best_vals[0, ra] - best_vals[0, 0]) for ra in range(P_TOPK)]
    ex1 = [jnp.exp(best_vals[1, rb] - best_vals[1, 0]) for rb in range(P_TOPK)]
    for i, (ra, rb) in enumerate(_CANDS):
        cand[i] = best_vals[0, ra] + best_vals[1, rb]
    counts[...] = jnp.zeros(counts.shape, F32)

    def pick(it, z):
        best = cand[0]
        for i in range(1, len(_CANDS)):
            best = jnp.maximum(best, cand[i])
        found = jnp.zeros((P_HEADS, tm), F32)
        for i, (ra, rb) in enumerate(_CANDS):
            ci = cand[i]
            hit = jnp.where(ci == best, 1.0, 0.0) * (1.0 - found)
            found = found + hit
            cand[i] = jnp.where(hit > 0.0, neg_inf, ci)
            counts[ra] = counts[ra] + hit
            z = z + hit * (ex0[ra] * ex1[rb])
        return z

    inv_z[...] = 1.0 / lax.fori_loop(0, P_TOPK, pick, jnp.zeros((P_HEADS, tm), F32))

    def finish(head, carry):
        rk = rank0[head]
        m0 = jnp.zeros((P_NKEYS, tm), F32)
        for ra in range(P_TOPK):
            m0 = jnp.where(rk == float(ra), counts[ra, pl.ds(head, 1), :], m0)
        m0_ref[head] = m0
        w0_ref[head] = w0_ref[head] * inv_z[pl.ds(head, 1), :]
        return carry

    lax.fori_loop(0, P_HEADS, finish, 0)


def _peer_route(h, g, wq_t, keys, tm):
    t, d = h.shape
    tiles = pl.cdiv(t, tm)
    t_pad = tiles * tm
    tile = pl.BlockSpec((P_HEADS, P_NKEYS, tm), lambda i: (0, 0, i))
    tile_f32 = jax.ShapeDtypeStruct((P_HEADS, P_NKEYS, t_pad), F32)
    tile_bf16 = jax.ShapeDtypeStruct((P_HEADS, P_NKEYS, t_pad), BF16)
    return pl.pallas_call(
        _peer_route_kernel,
        grid=(tiles,),
        in_specs=[pl.BlockSpec((tm, d), lambda i: (i, 0)), _const_spec((1, d)),
                  _const_spec(wq_t.shape), _const_spec(keys.shape)],
        out_specs=[pl.BlockSpec((d, tm), lambda i: (0, i)), tile, tile, tile, tile],
        out_shape=[jax.ShapeDtypeStruct((d, t_pad), BF16), tile_bf16, tile_bf16, tile_f32, tile_f32],
        scratch_shapes=[pltpu.VMEM((P_HEADS * P_QDIM, tm), F32),
                        pltpu.VMEM((P_NKEYS, tm), F32),
                        pltpu.VMEM((P_HEADS, P_NKEYS, tm), F32),
                        pltpu.VMEM((2, P_TOPK, P_HEADS, tm), F32),
                        pltpu.VMEM((len(_CANDS), P_HEADS, tm), F32),
                        pltpu.VMEM((P_TOPK, P_HEADS, tm), F32),
                        pltpu.VMEM((P_HEADS, tm), F32)],
        compiler_params=_params("parallel"),
        name="peer_route",
    )(h, g.reshape(1, d), wq_t, keys)


def _gelu(x):
    return 0.5 * x * (1.0 + lax.erf(x * (1.0 / math.sqrt(2.0))))


def _peer_dense_kernel(xt_ref, u_ref, vt_ref, r1_ref, e1_ref, m0_ref, w0_ref, h_ref, gfin_ref,
                       out_ref, acc, hid, prob, *, final_norm):
    j = pl.program_id(1)
    keys_per_step = P_EXPERT_TILE // P_NKEYS

    @pl.when(j == 0)
    def _():
        acc[...] = jnp.zeros(acc.shape, F32)

    tm = xt_ref.shape[1]
    hid[...] = jnp.dot(u_ref[...], xt_ref[...], preferred_element_type=F32)
    zero = jnp.zeros((), BF16)
    for ii in range(keys_per_step):
        i1 = j * keys_per_step + ii
        rows = slice(ii * P_NKEYS, (ii + 1) * P_NKEYS)
        m0_rows = [m0_ref[head, pl.ds(i1, 1), :].astype(BF16) for head in range(P_HEADS)]
        w0_rows = [w0_ref[head, pl.ds(i1, 1), :].astype(BF16) for head in range(P_HEADS)]
        for lt in range(tm // LANES):
            lanes = slice(lt * LANES, (lt + 1) * LANES)
            gate = None
            for head in range(P_HEADS):
                m0 = m0_rows[head][:, lanes]
                w0 = w0_rows[head][:, lanes]
                term = w0 * jnp.where(r1_ref[head, :, lanes] < m0, e1_ref[head, :, lanes], zero)
                gate = term if gate is None else gate + term
            prob[rows, lanes] = gate * _gelu(hid[rows, lanes]).astype(BF16)
    acc[...] += jnp.dot(vt_ref[...], prob[...], preferred_element_type=F32)

    @pl.when(j == pl.num_programs(1) - 1)
    def _():
        res = h_ref[...] + acc[...].T
        if final_norm:
            res = _rms(res, gfin_ref[...])
        out_ref[...] = res


def _peer_dense(xt, u_bf, vt_bf, r1, e1, m0, w0, h, g_final, tm, final_norm):
    t, d = h.shape
    n_exp = u_bf.shape[0]
    tile = pl.BlockSpec((P_HEADS, P_NKEYS, tm), lambda i, j: (0, 0, i))
    kern = functools.partial(_peer_dense_kernel, final_norm=final_norm)
    return pl.pallas_call(
        kern,
        grid=(pl.cdiv(t, tm), n_exp // P_EXPERT_TILE),
        in_specs=[pl.BlockSpec((d, tm), lambda i, j: (0, i)),
                  pl.BlockSpec((P_EXPERT_TILE, d), lambda i, j: (j, 0)),
                  pl.BlockSpec((d, P_EXPERT_TILE), lambda i, j: (0, j)),
                  tile, tile, tile, tile,
                  pl.BlockSpec((tm, d), lambda i, j: (i, 0)),
                  _const_spec((1, d))],
        out_specs=pl.BlockSpec((tm, d), lambda i, j: (i, 0)),
        out_shape=jax.ShapeDtypeStruct((t, d), F32),
        scratch_shapes=[pltpu.VMEM((d, tm), F32),
                        pltpu.VMEM((P_EXPERT_TILE, tm), F32),
                        pltpu.VMEM((P_EXPERT_TILE, tm), BF16)],
        compiler_params=_params("parallel", "arbitrary"),
        name="peer_dense",
    )(xt, u_bf, vt_bf, r1, e1, m0, w0, h, g_final.reshape(1, d))


def _pad_cols(w, n):
    return jnp.pad(w, ((0, 0), (0, n - w.shape[1])))


def _pad_rows(w, n):
    return jnp.pad(w, ((0, n - w.shape[0]), (0, 0)))


def _prep_rwkv(mix, w_rkv, w0, w1, w2, a0, a1, a2, g1, g2, k_k, k_a, r_k, ln_w, ln_b, w_o):
    d = D_MODEL
    lane = lambda x: jnp.tile(x.reshape(RW_HEADS, RW_HEAD).T, (1, LANES // RW_HEADS))
    return {
        "mix": _pad_rows(mix, SUBLANES),
        "vec": _pad_rows(jnp.stack([w0, a0, k_k, k_a]), SUBLANES),
        "wr": w_rkv[0].astype(BF16), "wk": w_rkv[1].astype(BF16), "wv": w_rkv[2].astype(BF16),
        "w1": _pad_cols(w1, LORA_PAD), "w2": _pad_rows(w2, LORA_PAD),
        "a1": _pad_cols(a1, LORA_PAD), "a2": _pad_rows(a2, LORA_PAD),
        "g1": _pad_cols(g1, GATE_LORA_PAD).astype(BF16), "g2": _pad_rows(g2, GATE_LORA_PAD).astype(BF16),
        "lnw": lane(ln_w), "lnb": lane(ln_b), "rk": lane(r_k.reshape(d)),
        "wo": w_o.astype(BF16),
    }


def _prep_mamba(in_proj, conv_w, conv_b, dt_bias, a_log, d_skip, norm_w, out_proj):
    head_of_channel = jnp.arange(M_D_INNER) // M_HEADDIM
    expand = (jnp.arange(M_DT_PAD)[:, None] == head_of_channel[None, :]).astype(F32)
    d_full = jnp.pad(jnp.repeat(d_skip, M_HEADDIM), (0, M_CONV_DIM - M_D_INNER))
    pad_h = lambda x: jnp.pad(x, (0, M_DT_PAD - M_HEADS))
    return {
        "w_in": _pad_cols(in_proj, M_IN_DIM - M_HEADS + M_DT_PAD).astype(BF16),
        "conv_w": _pad_rows(conv_w, SUBLANES),
        "vec": _pad_rows(jnp.stack([conv_b, d_full]), SUBLANES),
        "dvec": _pad_rows(jnp.stack([pad_h(dt_bias), pad_h(-jnp.exp(a_log))]), SUBLANES),
        "expand": expand,
        "norm_w": norm_w.reshape(1, M_D_INNER),
        "w_out": out_proj.astype(BF16),
    }


def _prep_peer(w_q, sub_keys, u_tab, v_tab):
    keys = jnp.transpose(sub_keys, (1, 0, 2, 3)).reshape(2 * P_HEADS, P_NKEYS, P_QDIM // 2)
    return {"wq_t": w_q.T.astype(BF16), "keys": keys, "u": u_tab.astype(BF16), "vt": v_tab.T.astype(BF16)}


def _to_scan(x, bsz, length):
    x = x.reshape(bsz, length, RW_HEADS, RW_HEAD)
    return jnp.transpose(x, (1, 3, 0, 2)).reshape(length, RW_HEAD, bsz * RW_HEADS)


def _from_scan(x, bsz, length):
    x = x.reshape(length, RW_HEAD, bsz, RW_HEADS)
    return jnp.transpose(x, (2, 0, 3, 1)).reshape(bsz * length, D_MODEL)


def _rwkv_layer(h, g_mix, shift0, wkv0, p, bsz, length, tm, tt):
    xn = _norm(h, g_mix, tm)
    xn3 = xn.reshape(bsz, length, D_MODEL)
    prev = jnp.concatenate([shift0[:, None, :], xn3[:, :-1]], axis=1).reshape(bsz * length, D_MODEL)
    r, w, k, v, kk, a, g = _rwkv_proj(xn, prev, p, tm)
    seqs = [_to_scan(x, bsz, length) for x in (r, w, k, v, kk, a)]
    s0 = jnp.transpose(wkv0, (2, 3, 0, 1)).reshape(RW_HEAD, RW_HEAD, bsz * RW_HEADS)
    o, s_fin = _rwkv_scan(seqs, s0, p["lnw"], p["lnb"], p["rk"], tt)
    o = _from_scan(o, bsz, length)
    wkv = jnp.transpose(s_fin.reshape(RW_HEAD, RW_HEAD, bsz, RW_HEADS), (2, 3, 0, 1))
    return _rwkv_out(o, g, h, p["wo"], tm), xn3[:, -1], wkv


def _mamba_layer(h, g_mix, conv0, ssm0, p, bsz, length, tm, rows):
    z, xbc, dt = _mamba_in(h, g_mix, p["w_in"], tm)
    xbc3 = xbc.reshape(bsz, length, M_CONV_DIM)
    conv_new = jnp.concatenate([conv0, xbc3], axis=1)[:, length:]
    pad_t = (-length) % SUBLANES if length < rows else 0
    seq = lambda x, c: jnp.pad(x.reshape(bsz, length, c), ((0, 0), (0, pad_t), (0, 0)))
    conv0_p = jnp.pad(conv0, ((0, 0), (SUBLANES - (M_CONV - 1), 0), (0, 0)))
    out, ssm = _mamba_ssd(seq(xbc, M_CONV_DIM), seq(z, M_D_INNER), seq(dt, M_DT_PAD), seq(h, D_MODEL),
                          conv0_p, ssm0, p, min(rows, length + pad_t), length)
    return out[:, :length].reshape(bsz * length, D_MODEL), conv_new, ssm


def _peer_layer(h, g_ffn, p, tm, g_final=None):
    xt, r1, e1, m0, w0 = _peer_route(h, g_ffn, p["wq_t"], p["keys"], tm)
    final = g_final is not None
    return _peer_dense(xt, p["u"], p["vt"], r1, e1, m0, w0, h, g_final if final else g_ffn, tm, final)


def _trunk(h, shift0, wkv0, conv0, ssm0, norm_mix, norm_ffn, norm_final, rwkv_p, mamba_p, peer_p,
           bsz, length, tm, tt, peer_tm, rows):
    h, shift, wkv = _rwkv_layer(h, norm_mix[0], shift0, wkv0, rwkv_p, bsz, length, tm, tt)
    h = _peer_layer(h, norm_ffn[0], peer_p[0], peer_tm)
    h, conv, ssm = _mamba_layer(h, norm_mix[1], conv0, ssm0, mamba_p, bsz, length, tm, rows)
    y = _peer_layer(h, norm_ffn[1], peer_p[1], peer_tm, g_final=norm_final)
    return y.reshape(bsz, length, D_MODEL), shift[None], wkv[None], conv[None], ssm[None]


def kernel(x_prompt, x_sample, state_rwkv_shift, state_rwkv_wkv, state_mamba_conv, state_mamba_ssm, meta_tokens, norm_mix, norm_ffn, norm_final, rwkv_mix, rwkv_w_rkv, rwkv_w0, rwkv_w1, rwkv_w2, rwkv_a0, rwkv_a1, rwkv_a2, rwkv_g1, rwkv_g2, rwkv_k_k, rwkv_k_a, rwkv_r_k, rwkv_ln_w, rwkv_ln_b, rwkv_w_o, mamba_in_proj, mamba_conv_w, mamba_conv_b, mamba_dt_bias, mamba_a_log, mamba_d, mamba_norm_w, mamba_out_proj, peer_w_q, peer_sub_keys, peer_u, peer_v):
    rwkv_p = _prep_rwkv(rwkv_mix[0], rwkv_w_rkv[0], rwkv_w0[0], rwkv_w1[0], rwkv_w2[0], rwkv_a0[0],
                        rwkv_a1[0], rwkv_a2[0], rwkv_g1[0], rwkv_g2[0], rwkv_k_k[0], rwkv_k_a[0],
                        rwkv_r_k[0], rwkv_ln_w[0], rwkv_ln_b[0], rwkv_w_o[0])
    mamba_p = _prep_mamba(mamba_in_proj[0], mamba_conv_w[0], mamba_conv_b[0], mamba_dt_bias[0],
                          mamba_a_log[0], mamba_d[0], mamba_norm_w[0], mamba_out_proj[0])
    peer_p = [_prep_peer(peer_w_q[i], peer_sub_keys[i], peer_u[i], peer_v[i]) for i in range(2)]

    bp, lp = x_prompt.shape[0], x_prompt.shape[1] + N_META
    meta = jnp.broadcast_to(meta_tokens[None], (bp, N_META, D_MODEL))
    hp = jnp.concatenate([meta, x_prompt], axis=1).reshape(bp * lp, D_MODEL)
    zeros = lambda *s: jnp.zeros(s, F32)
    yp, p_shift, p_wkv, p_conv, p_ssm = _trunk(
        hp, zeros(bp, D_MODEL), zeros(bp, RW_HEADS, RW_HEAD, RW_HEAD), zeros(bp, M_CONV - 1, M_CONV_DIM),
        zeros(bp, M_HEADS, M_HEADDIM, M_STATE), norm_mix, norm_ffn, norm_final, rwkv_p, mamba_p, peer_p,
        bp, lp, tm=344, tt=48, peer_tm=512, rows=M_CHUNK)

    bs, ls = x_sample.shape[0], x_sample.shape[1]
    ys, s_shift, s_wkv, s_conv, s_ssm = _trunk(
        x_sample.reshape(bs * ls, D_MODEL), state_rwkv_shift[0], state_rwkv_wkv[0], state_mamba_conv[0],
        state_mamba_ssm[0], norm_mix, norm_ffn, norm_final, rwkv_p, mamba_p, peer_p,
        bs, ls, tm=256, tt=ls, peer_tm=512, rows=M_CHUNK)
    return (yp[:, N_META:], ys, p_shift, p_wkv, p_conv, p_ssm, s_shift, s_wkv, s_conv, s_ssm)
```

```python
import functools
import math

import jax
import jax.numpy as jnp
from jax import lax
from jax.experimental import pallas as pl
from jax.experimental.pallas import tpu as pltpu

F32 = jnp.float32
BF16 = jnp.bfloat16
HIGHEST = lax.Precision.HIGHEST

D_MODEL = 1024
N_META = 16
NORM_EPS = 1e-5
RW_HEAD = 64
RW_HEADS = D_MODEL // RW_HEAD
RW_LN_EPS = 64e-5
LORA_PAD = 128
GATE_LORA_PAD = 256
M_D_INNER = 2048
M_HEADDIM = 64
M_HEADS = M_D_INNER // M_HEADDIM
M_GROUPS = 4
M_HPG = M_HEADS // M_GROUPS
M_STATE = 128
M_CONV = 4
M_CONV_DIM = M_D_INNER + 2 * M_GROUPS * M_STATE
M_IN_DIM = 2 * M_D_INNER + 2 * M_GROUPS * M_STATE + M_HEADS
M_DT_PAD = 128
M_CHUNK = 128
P_HEADS = 8
P_NKEYS = 128
P_QDIM = 256
P_TOPK = 16
P_EXPERT_TILE = 1024
P_EXPERT_SUB = 256

LANES = 128
SUBLANES = 8
VMEM_LIMIT_BYTES = 56 * 1024 * 1024


def _params(*semantics):
    return pltpu.CompilerParams(dimension_semantics=semantics, vmem_limit_bytes=VMEM_LIMIT_BYTES)


def _const_spec(shape):
    zeros = (0,) * len(shape)
    return pl.BlockSpec(shape, lambda *_: zeros)


def _rms(x, g):
    return x * lax.rsqrt(jnp.mean(x * x, axis=-1, keepdims=True) + NORM_EPS) * g


def _softplus(x):
    return jnp.maximum(x, 0.0) + jnp.log1p(jnp.exp(-jnp.abs(x)))


def _bdot(a, b):
    return jnp.dot(a.astype(BF16), b.astype(BF16), preferred_element_type=F32)


def _hdot(a, b):
    return jnp.dot(a, b, precision=HIGHEST, preferred_element_type=F32)


def _norm_kernel(h_ref, g_ref, o_ref):
    o_ref[...] = _rms(h_ref[...], g_ref[...])


def _norm(h, g, tm):
    t, d = h.shape
    return pl.pallas_call(
        _norm_kernel,
        grid=(t // tm,),
        in_specs=[pl.BlockSpec((tm, d), lambda i: (i, 0)), _const_spec((1, d))],
        out_specs=pl.BlockSpec((tm, d), lambda i: (i, 0)),
        out_shape=jax.ShapeDtypeStruct((t, d), F32),
        compiler_params=_params("parallel"),
        name="rms_norm",
    )(h, g.reshape(1, d))


def _rwkv_proj_kernel(xn_ref, prev_ref, mix_ref, vec_ref, wr_ref, wk_ref, wv_ref,
                      w1_ref, w2_ref, a1_ref, a2_ref, g1_ref, g2_ref,
                      r_ref, w_ref, k_ref, v_ref, kk_ref, a_ref, g_ref):
    xn = xn_ref[...]
    dx = prev_ref[...] - xn
    xr, xw, xk, xv, xa, xg = [xn + dx * mix_ref[j:j + 1, :] for j in range(6)]
    w0, a0, k_k, k_a = [vec_ref[j:j + 1, :] for j in range(4)]
    r_ref[...] = _bdot(xr, wr_ref[...])
    k = _bdot(xk, wk_ref[...])
    v_ref[...] = _bdot(xv, wv_ref[...])
    w_log = -_softplus(-(w0 + _hdot(jnp.tanh(_hdot(xw, w1_ref[...])), w2_ref[...]))) - 0.5
    w_ref[...] = jnp.exp(-jnp.exp(w_log))
    a = jax.nn.sigmoid(a0 + _hdot(_hdot(xa, a1_ref[...]), a2_ref[...]))
    a_ref[...] = a
    g_ref[...] = _bdot(jax.nn.sigmoid(_bdot(xg, g1_ref[...])), g2_ref[...])
    kk_ref[...] = k * k_k
    k_ref[...] = k * (1.0 + (a - 1.0) * k_a)


def _rwkv_proj(xn, prev, p, tm):
    t, d = xn.shape
    tok = pl.BlockSpec((tm, d), lambda i: (i, 0))
    weights = [p["mix"], p["vec"], p["wr"], p["wk"], p["wv"], p["w1"], p["w2"], p["a1"], p["a2"],
               p["g1"], p["g2"]]
    return pl.pallas_call(
        _rwkv_proj_kernel,
        grid=(t // tm,),
        in_specs=[tok, tok] + [_const_spec(w.shape) for w in weights],
        out_specs=[tok] * 7,
        out_shape=[jax.ShapeDtypeStruct((t, d), F32)] * 7,
        compiler_params=_params("parallel"),
        name="rwkv_proj",
    )(xn, prev, *weights)


def _rwkv_scan_kernel(r_ref, w_ref, k_ref, v_ref, kk_ref, a_ref, s0_ref, lnw_ref, lnb_ref, rk_ref,
                      o_ref, sfin_ref, state, o_rows):
    tb = pl.program_id(1)
    steps = r_ref.shape[0]

    @pl.when(tb == 0)
    def _():
        state[...] = s0_ref[...]

    def step(t, carry):
        r = r_ref[t]
        w = w_ref[t]
        k = k_ref[t]
        vv = v_ref[t]
        kku = kk_ref[t]
        norm = jnp.sqrt(jnp.sum(kku * kku, axis=0, keepdims=True))
        kk = kku / jnp.maximum(norm, 1e-12)
        alpha = -kk
        beta = kk * a_ref[t]

        def row(vi, c):
            s_v = state[vi]
            u = jnp.sum(s_v * alpha, axis=0, keepdims=True)
            s_new = s_v * w + u * beta + v_ref[t, pl.ds(vi, 1), :] * k
            state[vi] = s_new
            o_rows[pl.ds(vi, 1), :] = jnp.sum(s_new * r, axis=0, keepdims=True)
            return c

        lax.fori_loop(0, RW_HEAD, row, 0, unroll=8)
        o = o_rows[...]
        mu = jnp.mean(o, axis=0, keepdims=True)
        var = jnp.mean(jnp.square(o - mu), axis=0, keepdims=True)
        on = (o - mu) * lax.rsqrt(var + RW_LN_EPS)
        bonus = jnp.sum(r * k * rk_ref[...], axis=0, keepdims=True) * vv
        o_ref[t] = on * lnw_ref[...] + lnb_ref[...] + bonus
        return carry

    lax.fori_loop(0, steps, step, 0)

    @pl.when(tb == pl.num_programs(1) - 1)
    def _():
        sfin_ref[...] = state[...]


def _rwkv_scan(seqs, s0, lnw, lnb, rk, tt):
    length, n, probs = seqs[0].shape
    groups = probs // LANES
    seq_spec = pl.BlockSpec((tt, n, LANES), lambda g, i: (i, 0, g))
    st_spec = pl.BlockSpec((n, n, LANES), lambda g, i: (0, 0, g))
    lane_spec = pl.BlockSpec((n, LANES), lambda g, i: (0, 0))
    return pl.pallas_call(
        _rwkv_scan_kernel,
        grid=(groups, length // tt),
        in_specs=[seq_spec] * 6 + [st_spec] + [lane_spec] * 3,
        out_specs=[seq_spec, st_spec],
        out_shape=[jax.ShapeDtypeStruct((length, n, probs), F32),
                   jax.ShapeDtypeStruct((n, n, probs), F32)],
        scratch_shapes=[pltpu.VMEM((n, n, LANES), F32), pltpu.VMEM((n, LANES), F32)],
        compiler_params=_params("arbitrary", "arbitrary"),
        name="rwkv_scan",
    )(*seqs, s0, lnw, lnb, rk)


def _rwkv_out_kernel(o_ref, g_ref, h_ref, wo_ref, out_ref):
    out_ref[...] = h_ref[...] + _bdot(o_ref[...] * g_ref[...], wo_ref[...])


def _rwkv_out(o, g, h, wo, tm):
    t, d = h.shape
    tok = pl.BlockSpec((tm, d), lambda i: (i, 0))
    return pl.pallas_call(
        _rwkv_out_kernel,
        grid=(t // tm,),
        in_specs=[tok, tok, tok, _const_spec(wo.shape)],
        out_specs=tok,
        out_shape=jax.ShapeDtypeStruct((t, d), F32),
        compiler_params=_params("parallel"),
        name="rwkv_out",
    )(o, g, h, wo)


def _mamba_in_kernel(h_ref, g_ref, w_ref, z_ref, xbc_ref, dt_ref):
    xn = _rms(h_ref[...], g_ref[...]).astype(BF16)
    z_ref[...] = jnp.dot(xn, w_ref[:, :M_D_INNER], preferred_element_type=F32)
    xbc_ref[...] = jnp.dot(xn, w_ref[:, M_D_INNER:M_D_INNER + M_CONV_DIM], preferred_element_type=F32)
    dt_ref[...] = jnp.dot(xn, w_ref[:, M_D_INNER + M_CONV_DIM:], preferred_element_type=F32)


def _mamba_in(h, g, w_in, tm):
    t, d = h.shape
    return pl.pallas_call(
        _mamba_in_kernel,
        grid=(t // tm,),
        in_specs=[pl.BlockSpec((tm, d), lambda i: (i, 0)), _const_spec((1, d)), _const_spec(w_in.shape)],
        out_specs=[pl.BlockSpec((tm, M_D_INNER), lambda i: (i, 0)),
                   pl.BlockSpec((tm, M_CONV_DIM), lambda i: (i, 0)),
                   pl.BlockSpec((tm, M_DT_PAD), lambda i: (i, 0))],
        out_shape=[jax.ShapeDtypeStruct((t, M_D_INNER), F32),
                   jax.ShapeDtypeStruct((t, M_CONV_DIM), F32),
                   jax.ShapeDtypeStruct((t, M_DT_PAD), F32)],
        compiler_params=_params("parallel"),
        name="mamba_in_proj",
    )(h, g.reshape(1, d), w_in)


def _mamba_ssd_kernel(xbc_ref, z_ref, dt_ref, h_ref, conv0_ref, ssm0_ref, convw_ref, vec_ref,
                      dvec_ref, expand_ref, normw_ref, wout_ref,
                      out_ref, ssm_ref,
                      conv_buf, src_adt, src_acs, src_b, src_xdt, src_xdt_st, state, y_buf, *, seq_len, rows):
    b = pl.program_id(0)
    c = pl.program_id(1)
    q = M_CHUNK
    halo = SUBLANES
    d_bc = M_GROUPS * M_STATE

    @pl.when(c == 0)
    def _():
        state[...] = ssm0_ref[0]
        conv_buf[0:halo, :] = conv0_ref[0]

    @pl.when(c > 0)
    def _():
        conv_buf[0:halo, :] = conv_buf[rows:rows + halo, :]

    if rows < q:
        @pl.when((b == 0) & (c == 0))
        def _():
            src_adt[...] = jnp.zeros(src_adt.shape, F32)
            src_acs[...] = jnp.zeros(src_acs.shape, F32)
            src_b[...] = jnp.zeros(src_b.shape, BF16)
            src_xdt[...] = jnp.zeros(src_xdt.shape, BF16)
            src_xdt_st[...] = jnp.zeros(src_xdt_st.shape, BF16)

    conv_buf[halo:halo + rows, :] = xbc_ref[0]
    valid = (lax.broadcasted_iota(jnp.int32, (rows, 1), 0) + c * rows) < seq_len
    conv = vec_ref[0:1, :]
    for j in range(M_CONV):
        conv = conv + conv_buf[halo - (M_CONV - 1) + j:halo - (M_CONV - 1) + j + rows, :] * convw_ref[j:j + 1, :]
    act = jnp.where(valid, jax.nn.silu(conv), 0.0)
    xs = act[:, :M_D_INNER]
    dt = jnp.where(valid, _softplus(dt_ref[0] + dvec_ref[0:1, :]), 0.0)
    src_adt[0:rows, :] = dt * dvec_ref[1:2, :]
    li = lax.broadcasted_iota(jnp.int32, (rows, q), 0)
    si = lax.broadcasted_iota(jnp.int32, (rows, q), 1)
    causal = li >= si
    acs = _hdot(causal.astype(F32), src_adt[...])
    src_acs[0:rows, :] = acs
    acs_t = src_acs[...].T
    expand = expand_ref[...]
    dt_full = _hdot(dt, expand)
    acs_full = _hdot(acs, expand)
    last = acs[rows - 1:rows, :]
    last_full = _hdot(last, expand)
    xdt = xs * dt_full
    src_xdt[0:rows, :] = xdt.astype(BF16)
    src_xdt_st[0:rows, :] = (xdt * jnp.exp(last_full - acs_full)).astype(BF16)
    src_b[0:rows, :] = act[:, M_D_INNER:M_D_INNER + d_bc].astype(BF16)
    eacs_full = jnp.exp(acs_full)
    state_decay = jnp.exp(last)
    width = M_HPG * M_HEADDIM

    for g in range(M_GROUPS):
        cols = slice(g * width, (g + 1) * width)
        b_g = src_b[:, g * M_STATE:(g + 1) * M_STATE]
        c_g = act[:, M_D_INNER + d_bc + g * M_STATE:M_D_INNER + d_bc + (g + 1) * M_STATE].astype(BF16)
        cb = lax.dot_general(c_g, b_g, (((1,), (1,)), ((), ())), preferred_element_type=F32)
        h_g = state[g * M_HPG:(g + 1) * M_HPG].reshape(width, M_STATE)
        y_off = lax.dot_general(c_g, h_g.astype(BF16), (((1,), (1,)), ((), ())),
                                preferred_element_type=F32)
        upd = lax.dot_general(src_xdt_st[:, cols], b_g, (((0,), (0,)), ((), ())),
                              preferred_element_type=F32)
        ys = []
        for r in range(g * M_HPG, (g + 1) * M_HPG):
            lo = r * M_HEADDIM
            decay = jnp.where(causal, jnp.exp(acs[:, r:r + 1] - acs_t[r:r + 1, :]), 0.0)
            ys.append(jnp.dot((cb * decay).astype(BF16), src_xdt[:, lo:lo + M_HEADDIM],
                              preferred_element_type=F32))
            sub = slice((r - g * M_HPG) * M_HEADDIM, (r - g * M_HPG + 1) * M_HEADDIM)
            state[r] = state[r] * state_decay[:, r:r + 1] + upd[sub, :]
        y_buf[:, cols] = jnp.concatenate(ys, axis=1) + y_off * eacs_full[:, cols]

    y = y_buf[...] + vec_ref[1:2, :M_D_INNER] * xs
    yg = y * jax.nn.silu(z_ref[0])
    parts = []
    for g in range(M_GROUPS):
        part = yg[:, g * width:(g + 1) * width]
        parts.append(part * lax.rsqrt(jnp.mean(part * part, axis=-1, keepdims=True) + NORM_EPS))
    yn = jnp.concatenate(parts, axis=1) * normw_ref[...]
    out_ref[0] = h_ref[0] + _bdot(yn, wout_ref[...])

    @pl.when(c == pl.num_programs(1) - 1)
    def _():
        ssm_ref[0] = state[...]


def _mamba_ssd(xbc, z, dt, h, conv0, ssm0, p, rows, seq_len):
    bsz, length, _ = xbc.shape
    chunks = pl.cdiv(length, rows)
    weights = [p["conv_w"], p["vec"], p["dvec"], p["expand"], p["norm_w"], p["w_out"]]

    def seq_spec(width):
        return pl.BlockSpec((1, rows, width), lambda b, c: (b, c, 0))

    kern = functools.partial(_mamba_ssd_kernel, seq_len=seq_len, rows=rows)
    return pl.pallas_call(
        kern,
        grid=(bsz, chunks),
        in_specs=[seq_spec(M_CONV_DIM), seq_spec(M_D_INNER), seq_spec(M_DT_PAD), seq_spec(D_MODEL),
                  pl.BlockSpec((1, SUBLANES, M_CONV_DIM), lambda b, c: (b, 0, 0)),
                  pl.BlockSpec((1, M_HEADS, M_HEADDIM, M_STATE), lambda b, c: (b, 0, 0, 0))]
                 + [_const_spec(w.shape) for w in weights],
        out_specs=[seq_spec(D_MODEL),
                   pl.BlockSpec((1, M_HEADS, M_HEADDIM, M_STATE), lambda b, c: (b, 0, 0, 0))],
        out_shape=[jax.ShapeDtypeStruct((bsz, length, D_MODEL), F32),
                   jax.ShapeDtypeStruct((bsz, M_HEADS, M_HEADDIM, M_STATE), F32)],
        scratch_shapes=[pltpu.VMEM((rows + SUBLANES, M_CONV_DIM), F32),
                        pltpu.VMEM((M_CHUNK, M_DT_PAD), F32),
                        pltpu.VMEM((M_CHUNK, M_DT_PAD), F32),
                        pltpu.VMEM((M_CHUNK, M_GROUPS * M_STATE), BF16),
                        pltpu.VMEM((M_CHUNK, M_D_INNER), BF16),
                        pltpu.VMEM((M_CHUNK, M_D_INNER), BF16),
                        pltpu.VMEM((M_HEADS, M_HEADDIM, M_STATE), F32),
                        pltpu.VMEM((rows, M_D_INNER), F32)],
        compiler_params=_params("arbitrary", "arbitrary"),
        name="mamba_ssd",
    )(xbc, z, dt, h, conv0, ssm0, *weights)


_CANDS = [(ra, rb) for ra in range(P_TOPK) for rb in range(P_TOPK) if (ra + 1) * (rb + 1) <= P_TOPK]


def _peer_route_kernel(h_ref, g_ref, wq_ref, keys_ref, keys_lo_ref,
                       xt_ref, r1_ref, e1_ref, m0_ref, w0_ref,
                       q_buf, s_buf, rank0, best_vals, cand, counts, inv_z):
    tm = h_ref.shape[0]
    neg_inf = float("-inf")
    xn = _rms(h_ref[...], g_ref[...])
    xt = xn.T.astype(BF16)
    xt_ref[...] = xt
    q_buf[...] = jnp.dot(wq_ref[...], xt, preferred_element_type=F32)
    key_iota = lax.broadcasted_iota(jnp.int32, (P_NKEYS, tm), 0).astype(F32)

    def half(m, carry):
        head = m // 2
        z = m % 2
        row0 = pl.multiple_of(m * P_NKEYS, P_NKEYS)
        qm = q_buf[pl.ds(row0, P_NKEYS), :]
        q_hi = qm.astype(BF16)
        q_lo = (qm - q_hi.astype(F32)).astype(BF16)
        k_hi = keys_ref[m]
        s = (jnp.dot(k_hi, q_hi, preferred_element_type=F32)
             + (jnp.dot(k_hi, q_lo, preferred_element_type=F32)
                + jnp.dot(keys_lo_ref[m], q_hi, preferred_element_type=F32)))

        s_buf[...] = s

        def next_distinct(j, prev):
            sj = s_buf[...]
            best = jnp.max(jnp.where(sj < prev, sj, neg_inf), axis=0, keepdims=True)
            best_vals[z, j, pl.ds(head, 1), :] = best
            return best

        def extract_ties(j, sc):
            cur, rank = sc
            best = jnp.max(cur, axis=0, keepdims=True)
            first = jnp.min(jnp.where(cur == best, key_iota, float(P_NKEYS)), axis=0, keepdims=True)
            sel = key_iota == first
            best_vals[z, j, pl.ds(head, 1), :] = best
            return jnp.where(sel, neg_inf, cur), jnp.where(sel, jnp.asarray(j, dtype=F32), rank)

        last = lax.fori_loop(0, P_TOPK, next_distinct, jnp.full((1, tm), float("inf"), F32))
        reached = jnp.sum(jnp.where(s >= last, 1.0, 0.0), axis=0, keepdims=True)
        has_ties = jnp.max(jnp.abs(reached - float(P_TOPK))) > 0.0

        def rank_by_count():
            rank = jnp.zeros((P_NKEYS, tm), F32)
            for jj in range(P_TOPK):
                rank = rank + jnp.where(best_vals[z, jj, pl.ds(head, 1), :] > s, 1.0, 0.0)
            return rank

        def rank_with_ties():
            no_rank = jnp.full((P_NKEYS, tm), float(P_NKEYS), F32)
            return lax.fori_loop(0, P_TOPK, extract_ties, (s, no_rank))[1]

        rank = lax.cond(has_ties, rank_with_ties, rank_by_count)
        e = jnp.exp(s - jnp.max(s, axis=0, keepdims=True))

        @pl.when(z == 0)
        def _():
            rank0[head] = rank
            w0_ref[head] = e

        @pl.when(z == 1)
        def _():
            r1_ref[head] = rank.astype(BF16)
            e1_ref[head] = e.astype(BF16)

        return carry

    lax.fori_loop(0, 2 * P_HEADS, half, 0)

    ex0 = [jnp.exp(best_vals[0, ra] - best_vals[0, 0]) for ra in range(P_TOPK)]
    ex1 = [jnp.exp(best_vals[1, rb] - best_vals[1, 0]) for rb in range(P_TOPK)]
    for i, (ra, rb) in enumerate(_CANDS):
        cand[i] = best_vals[0, ra] + best_vals[1, rb]
    counts[...] = jnp.zeros(counts.shape, F32)

    def pick(it, z):
        best = cand[0]
        for i in range(1, len(_CANDS)):
            best = jnp.maximum(best, cand[i])
        found = jnp.zeros((P_HEADS, tm), F32)
        for i, (ra, rb) in enumerate(_CANDS):
            ci = cand[i]
            hit = jnp.where(ci == best, 1.0, 0.0) * (1.0 - found)
            found = found + hit
            cand[i] = jnp.where(hit > 0.0, neg_inf, ci)
            counts[ra] = counts[ra] + hit
            z = z + hit * (ex0[ra] * ex1[rb])
        return z

    inv_z[...] = 1.0 / lax.fori_loop(0, P_TOPK, pick, jnp.zeros((P_HEADS, tm), F32))

    def finish(head, carry):
        rk = rank0[head]
        m0 = jnp.zeros((P_NKEYS, tm), F32)
        for ra in range(P_TOPK):
            m0 = jnp.where(rk == float(ra), counts[ra, pl.ds(head, 1), :], m0)
        m0_ref[head] = m0
        w0_ref[head] = w0_ref[head] * inv_z[pl.ds(head, 1), :]
        return carry

    lax.fori_loop(0, P_HEADS, finish, 0)


def _peer_route(h, g, wq_t, keys, keys_lo, tm):
    t, d = h.shape
    tiles = pl.cdiv(t, tm)
    t_pad = tiles * tm
    tile = pl.BlockSpec((P_HEADS, P_NKEYS, tm), lambda i: (0, 0, i))
    tile_f32 = jax.ShapeDtypeStruct((P_HEADS, P_NKEYS, t_pad), F32)
    tile_bf16 = jax.ShapeDtypeStruct((P_HEADS, P_NKEYS, t_pad), BF16)
    return pl.pallas_call(
        _peer_route_kernel,
        grid=(tiles,),
        in_specs=[pl.BlockSpec((tm, d), lambda i: (i, 0)), _const_spec((1, d)),
                  _const_spec(wq_t.shape), _const_spec(keys.shape), _const_spec(keys_lo.shape)],
        out_specs=[pl.BlockSpec((d, tm), lambda i: (0, i)), tile, tile, tile, tile],
        out_shape=[jax.ShapeDtypeStruct((d, t_pad), BF16), tile_bf16, tile_bf16, tile_f32, tile_f32],
        scratch_shapes=[pltpu.VMEM((P_HEADS * P_QDIM, tm), F32),
                        pltpu.VMEM((P_NKEYS, tm), F32),
                        pltpu.VMEM((P_HEADS, P_NKEYS, tm), F32),
                        pltpu.VMEM((2, P_TOPK, P_HEADS, tm), F32),
                        pltpu.VMEM((len(_CANDS), P_HEADS, tm), F32),
                        pltpu.VMEM((P_TOPK, P_HEADS, tm), F32),
                        pltpu.VMEM((P_HEADS, tm), F32)],
        compiler_params=_params("parallel"),
        name="peer_route",
    )(h, g.reshape(1, d), wq_t, keys, keys_lo)


def _gelu(x):
    return 0.5 * x * (1.0 + lax.erf(x * (1.0 / math.sqrt(2.0))))


def _peer_dense_kernel(xt_ref, u_ref, vt_ref, r1_ref, e1_ref, m0_ref, w0_ref, h_ref, gfin_ref,
                       out_ref, acc, hid, prob, *, final_norm):
    j = pl.program_id(1)
    keys_per_step = P_EXPERT_TILE // P_NKEYS

    @pl.when(j == 0)
    def _():
        acc[...] = jnp.zeros(acc.shape, F32)

    tm = xt_ref.shape[1]
    hid[...] = jnp.dot(u_ref[...], xt_ref[...], preferred_element_type=F32)
    zero = jnp.zeros((), BF16)
    for ii in range(keys_per_step):
        i1 = j * keys_per_step + ii
        rows = slice(ii * P_NKEYS, (ii + 1) * P_NKEYS)
        m0_rows = [m0_ref[head, pl.ds(i1, 1), :].astype(BF16) for head in range(P_HEADS)]
        w0_rows = [w0_ref[head, pl.ds(i1, 1), :].astype(BF16) for head in range(P_HEADS)]
        for lt in range(tm // LANES):
            lanes = slice(lt * LANES, (lt + 1) * LANES)
            gate = None
            for head in range(P_HEADS):
                m0 = m0_rows[head][:, lanes]
                w0 = w0_rows[head][:, lanes]
                term = w0 * jnp.where(r1_ref[head, :, lanes] < m0, e1_ref[head, :, lanes], zero)
                gate = term if gate is None else gate + term
            prob[rows, lanes] = gate * _gelu(hid[rows, lanes]).astype(BF16)
    acc[...] += jnp.dot(vt_ref[...], prob[...], preferred_element_type=F32)

    @pl.when(j == pl.num_programs(1) - 1)
    def _():
        res = h_ref[...] + acc[...].T
        if final_norm:
            res = _rms(res, gfin_ref[...])
        out_ref[...] = res


def _peer_dense(xt, u_bf, vt_bf, r1, e1, m0, w0, h, g_final, tm, final_norm):
    t, d = h.shape
    n_exp = u_bf.shape[0]
    tile = pl.BlockSpec((P_HEADS, P_NKEYS, tm), lambda i, j: (0, 0, i))
    kern = functools.partial(_peer_dense_kernel, final_norm=final_norm)
    return pl.pallas_call(
        kern,
        grid=(pl.cdiv(t, tm), n_exp // P_EXPERT_TILE),
        in_specs=[pl.BlockSpec((d, tm), lambda i, j: (0, i)),
                  pl.BlockSpec((P_EXPERT_TILE, d), lambda i, j: (j, 0)),
                  pl.BlockSpec((d, P_EXPERT_TILE), lambda i, j: (0, j)),
                  tile, tile, tile, tile,
                  pl.BlockSpec((tm, d), lambda i, j: (i, 0)),
                  _const_spec((1, d))],
        out_specs=pl.BlockSpec((tm, d), lambda i, j: (i, 0)),
        out_shape=jax.ShapeDtypeStruct((t, d), F32),
        scratch_shapes=[pltpu.VMEM((d, tm), F32),
                        pltpu.VMEM((P_EXPERT_TILE, tm), F32),
                        pltpu.VMEM((P_EXPERT_TILE, tm), BF16)],
        compiler_params=_params("parallel", "arbitrary"),
        name="peer_dense",
    )(xt, u_bf, vt_bf, r1, e1, m0, w0, h, g_final.reshape(1, d))


def _pad_cols(w, n):
    return jnp.pad(w, ((0, 0), (0, n - w.shape[1])))


def _pad_rows(w, n):
    return jnp.pad(w, ((0, n - w.shape[0]), (0, 0)))


def _prep_rwkv(mix, w_rkv, w0, w1, w2, a0, a1, a2, g1, g2, k_k, k_a, r_k, ln_w, ln_b, w_o):
    d = D_MODEL
    lane = lambda x: jnp.tile(x.reshape(RW_HEADS, RW_HEAD).T, (1, LANES // RW_HEADS))
    return {
        "mix": _pad_rows(mix, SUBLANES),
        "vec": _pad_rows(jnp.stack([w0, a0, k_k, k_a]), SUBLANES),
        "wr": w_rkv[0].astype(BF16), "wk": w_rkv[1].astype(BF16), "wv": w_rkv[2].astype(BF16),
        "w1": _pad_cols(w1, LORA_PAD), "w2": _pad_rows(w2, LORA_PAD),
        "a1": _pad_cols(a1, LORA_PAD), "a2": _pad_rows(a2, LORA_PAD),
        "g1": _pad_cols(g1, GATE_LORA_PAD).astype(BF16), "g2": _pad_rows(g2, GATE_LORA_PAD).astype(BF16),
        "lnw": lane(ln_w), "lnb": lane(ln_b), "rk": lane(r_k.reshape(d)),
        "wo": w_o.astype(BF16),
    }


def _prep_mamba(in_proj, conv_w, conv_b, dt_bias, a_log, d_skip, norm_w, out_proj):
    head_of_channel = jnp.arange(M_D_INNER) // M_HEADDIM
    expand = (jnp.arange(M_DT_PAD)[:, None] == head_of_channel[None, :]).astype(F32)
    d_full = jnp.pad(jnp.repeat(d_skip, M_HEADDIM), (0, M_CONV_DIM - M_D_INNER))
    pad_h = lambda x: jnp.pad(x, (0, M_DT_PAD - M_HEADS))
    return {
        "w_in": _pad_cols(in_proj, M_IN_DIM - M_HEADS + M_DT_PAD).astype(BF16),
        "conv_w": _pad_rows(conv_w, SUBLANES),
        "vec": _pad_rows(jnp.stack([conv_b, d_full]), SUBLANES),
        "dvec": _pad_rows(jnp.stack([pad_h(dt_bias), pad_h(-jnp.exp(a_log))]), SUBLANES),
        "expand": expand,
        "norm_w": norm_w.reshape(1, M_D_INNER),
        "w_out": out_proj.astype(BF16),
    }


def _prep_peer(w_q, sub_keys, u_tab, v_tab):
    keys = jnp.transpose(sub_keys, (1, 0, 2, 3)).reshape(2 * P_HEADS, P_NKEYS, P_QDIM // 2)
    keys_hi = keys.astype(BF16)
    keys_lo = (keys - keys_hi.astype(F32)).astype(BF16)
    return {"wq_t": w_q.T.astype(BF16), "keys": keys_hi, "keys_lo": keys_lo,
            "u": u_tab.astype(BF16), "vt": v_tab.T.astype(BF16)}


def _to_scan(x, bsz, length):
    x = x.reshape(bsz, length, RW_HEADS, RW_HEAD)
    return jnp.transpose(x, (1, 3, 0, 2)).reshape(length, RW_HEAD, bsz * RW_HEADS)


def _from_scan(x, bsz, length):
    x = x.reshape(length, RW_HEAD, bsz, RW_HEADS)
    return jnp.transpose(x, (2, 0, 3, 1)).reshape(bsz * length, D_MODEL)


def _rwkv_layer(h, g_mix, shift0, wkv0, p, bsz, length, tm, tt):
    xn = _norm(h, g_mix, tm)
    xn3 = xn.reshape(bsz, length, D_MODEL)
    prev = jnp.concatenate([shift0[:, None, :], xn3[:, :-1]], axis=1).reshape(bsz * length, D_MODEL)
    r, w, k, v, kk, a, g = _rwkv_proj(xn, prev, p, tm)
    seqs = [_to_scan(x, bsz, length) for x in (r, w, k, v, kk, a)]
    s0 = jnp.transpose(wkv0, (2, 3, 0, 1)).reshape(RW_HEAD, RW_HEAD, bsz * RW_HEADS)
    o, s_fin = _rwkv_scan(seqs, s0, p["lnw"], p["lnb"], p["rk"], tt)
    o = _from_scan(o, bsz, length)
    wkv = jnp.transpose(s_fin.reshape(RW_HEAD, RW_HEAD, bsz, RW_HEADS), (2, 3, 0, 1))
    return _rwkv_out(o, g, h, p["wo"], tm), xn3[:, -1], wkv


def _mamba_layer(h, g_mix, conv0, ssm0, p, bsz, length, tm, rows):
    z, xbc, dt = _mamba_in(h, g_mix, p["w_in"], tm)
    xbc3 = xbc.reshape(bsz, length, M_CONV_DIM)
    conv_new = jnp.concatenate([conv0, xbc3], axis=1)[:, length:]
    pad_t = (-length) % SUBLANES if length < rows else 0
    seq = lambda x, c: jnp.pad(x.reshape(bsz, length, c), ((0, 0), (0, pad_t), (0, 0)))
    conv0_p = jnp.pad(conv0, ((0, 0), (SUBLANES - (M_CONV - 1), 0), (0, 0)))
    out, ssm = _mamba_ssd(seq(xbc, M_CONV_DIM), seq(z, M_D_INNER), seq(dt, M_DT_PAD), seq(h, D_MODEL),
                          conv0_p, ssm0, p, min(rows, length + pad_t), length)
    return out[:, :length].reshape(bsz * length, D_MODEL), conv_new, ssm


def _peer_layer(h, g_ffn, p, tm, g_final=None):
    xt, r1, e1, m0, w0 = _peer_route(h, g_ffn, p["wq_t"], p["keys"], p["keys_lo"], tm)
    final = g_final is not None
    return _peer_dense(xt, p["u"], p["vt"], r1, e1, m0, w0, h, g_final if final else g_ffn, tm, final)


def _trunk(h, shift0, wkv0, conv0, ssm0, norm_mix, norm_ffn, norm_final, rwkv_p, mamba_p, peer_p,
           bsz, length, tm, tt, peer_tm, rows):
    h, shift, wkv = _rwkv_layer(h, norm_mix[0], shift0, wkv0, rwkv_p, bsz, length, tm, tt)
    h = _peer_layer(h, norm_ffn[0], peer_p[0], peer_tm)
    h, conv, ssm = _mamba_layer(h, norm_mix[1], conv0, ssm0, mamba_p, bsz, length, tm, rows)
    y = _peer_layer(h, norm_ffn[1], peer_p[1], peer_tm, g_final=norm_final)
    return y.reshape(bsz, length, D_MODEL), shift[None], wkv[None], conv[None], ssm[None]


def kernel(x_prompt, x_sample, state_rwkv_shift, state_rwkv_wkv, state_mamba_conv, state_mamba_ssm, meta_tokens, norm_mix, norm_ffn, norm_final, rwkv_mix, rwkv_w_rkv, rwkv_w0, rwkv_w1, rwkv_w2, rwkv_a0, rwkv_a1, rwkv_a2, rwkv_g1, rwkv_g2, rwkv_k_k, rwkv_k_a, rwkv_r_k, rwkv_ln_w, rwkv_ln_b, rwkv_w_o, mamba_in_proj, mamba_conv_w, mamba_conv_b, mamba_dt_bias, mamba_a_log, mamba_d, mamba_norm_w, mamba_out_proj, peer_w_q, peer_sub_keys, peer_u, peer_v):
    rwkv_p = _prep_rwkv(rwkv_mix[0], rwkv_w_rkv[0], rwkv_w0[0], rwkv_w1[0], rwkv_w2[0], rwkv_a0[0],
                        rwkv_a1[0], rwkv_a2[0], rwkv_g1[0], rwkv_g2[0], rwkv_k_k[0], rwkv_k_a[0],
                        rwkv_r_k[0], rwkv_ln_w[0], rwkv_ln_b[0], rwkv_w_o[0])
    mamba_p = _prep_mamba(mamba_in_proj[0], mamba_conv_w[0], mamba_conv_b[0], mamba_dt_bias[0],
                          mamba_a_log[0], mamba_d[0], mamba_norm_w[0], mamba_out_proj[0])
    peer_p = [_prep_peer(peer_w_q[i], peer_sub_keys[i], peer_u[i], peer_v[i]) for i in range(2)]

    bp, lp = x_prompt.shape[0], x_prompt.shape[1] + N_META
    meta = jnp.broadcast_to(meta_tokens[None], (bp, N_META, D_MODEL))
    hp = jnp.concatenate([meta, x_prompt], axis=1).reshape(bp * lp, D_MODEL)
    zeros = lambda *s: jnp.zeros(s, F32)
    yp, p_shift, p_wkv, p_conv, p_ssm = _trunk(
        hp, zeros(bp, D_MODEL), zeros(bp, RW_HEADS, RW_HEAD, RW_HEAD), zeros(bp, M_CONV - 1, M_CONV_DIM),
        zeros(bp, M_HEADS, M_HEADDIM, M_STATE), norm_mix, norm_ffn, norm_final, rwkv_p, mamba_p, peer_p,
        bp, lp, tm=344, tt=48, peer_tm=512, rows=M_CHUNK)

    bs, ls = x_sample.shape[0], x_sample.shape[1]
    ys, s_shift, s_wkv, s_conv, s_ssm = _trunk(
        x_sample.reshape(bs * ls, D_MODEL), state_rwkv_shift[0], state_rwkv_wkv[0], state_mamba_conv[0],
        state_mamba_ssm[0], norm_mix, norm_ffn, norm_final, rwkv_p, mamba_p, peer_p,
        bs, ls, tm=256, tt=ls, peer_tm=512, rows=M_CHUNK)
    return (yp[:, N_META:], ys, p_shift, p_wkv, p_conv, p_ssm, s_shift, s_wkv, s_conv, s_ssm)
```

```python
import functools
import math

import jax
import jax.numpy as jnp
from jax import lax
from jax.experimental import pallas as pl
from jax.experimental.pallas import tpu as pltpu

F32 = jnp.float32
BF16 = jnp.bfloat16
HIGHEST = lax.Precision.HIGHEST

D_MODEL = 1024
N_META = 16
NORM_EPS = 1e-5
RW_HEAD = 64
RW_HEADS = D_MODEL // RW_HEAD
RW_LN_EPS = 64e-5
LORA_PAD = 128
GATE_LORA_PAD = 256
M_D_INNER = 2048
M_HEADDIM = 64
M_HEADS = M_D_INNER // M_HEADDIM
M_GROUPS = 4
M_HPG = M_HEADS // M_GROUPS
M_STATE = 128
M_CONV = 4
M_CONV_DIM = M_D_INNER + 2 * M_GROUPS * M_STATE
M_IN_DIM = 2 * M_D_INNER + 2 * M_GROUPS * M_STATE + M_HEADS
M_DT_PAD = 128
M_CHUNK = 128
P_HEADS = 8
P_NKEYS = 128
P_QDIM = 256
P_TOPK = 16
P_EXPERT_TILE = 1024
P_EXPERT_SUB = 256

LANES = 128
SUBLANES = 8
VMEM_LIMIT_BYTES = 56 * 1024 * 1024


def _params(*semantics):
    return pltpu.CompilerParams(dimension_semantics=semantics, vmem_limit_bytes=VMEM_LIMIT_BYTES)


def _const_spec(shape):
    zeros = (0,) * len(shape)
    return pl.BlockSpec(shape, lambda *_: zeros)


def _rms(x, g):
    return x * lax.rsqrt(jnp.mean(x * x, axis=-1, keepdims=True) + NORM_EPS) * g


def _softplus(x):
    return jnp.maximum(x, 0.0) + jnp.log1p(jnp.exp(-jnp.abs(x)))


def _bdot(a, b):
    return jnp.dot(a.astype(BF16), b.astype(BF16), preferred_element_type=F32)


def _hdot(a, b):
    return jnp.dot(a, b, precision=HIGHEST, preferred_element_type=F32)


def _split_bf16(w):
    hi = w.astype(BF16)
    return hi, (w - hi.astype(F32)).astype(BF16)


def _dot3(a, b_hi, b_lo):
    a_hi, a_lo = _split_bf16(a)
    return (jnp.dot(a_hi, b_hi, preferred_element_type=F32)
            + (jnp.dot(a_hi, b_lo, preferred_element_type=F32) + jnp.dot(a_lo, b_hi, preferred_element_type=F32)))


def _norm_kernel(h_ref, g_ref, o_ref):
    o_ref[...] = _rms(h_ref[...], g_ref[...])


def _norm(h, g, tm):
    t, d = h.shape
    return pl.pallas_call(
        _norm_kernel,
        grid=(t // tm,),
        in_specs=[pl.BlockSpec((tm, d), lambda i: (i, 0)), _const_spec((1, d))],
        out_specs=pl.BlockSpec((tm, d), lambda i: (i, 0)),
        out_shape=jax.ShapeDtypeStruct((t, d), F32),
        compiler_params=_params("parallel"),
        name="rms_norm",
    )(h, g.reshape(1, d))


def _rwkv_proj_kernel(xn_ref, prev_ref, mix_ref, vec_ref, wr_ref, wk_ref, wv_ref,
                      w1h_ref, w1l_ref, w2h_ref, w2l_ref, a1h_ref, a1l_ref, a2h_ref, a2l_ref, g1_ref, g2_ref,
                      r_ref, w_ref, k_ref, v_ref, kk_ref, a_ref, g_ref):
    xn = xn_ref[...]
    dx = prev_ref[...] - xn
    xr, xw, xk, xv, xa, xg = [xn + dx * mix_ref[j:j + 1, :] for j in range(6)]
    w0, a0, k_k, k_a = [vec_ref[j:j + 1, :] for j in range(4)]
    r_ref[...] = _bdot(xr, wr_ref[...])
    k = _bdot(xk, wk_ref[...])
    v_ref[...] = _bdot(xv, wv_ref[...])
    w_lora = _dot3(jnp.tanh(_dot3(xw, w1h_ref[...], w1l_ref[...])), w2h_ref[...], w2l_ref[...])
    w_log = -_softplus(-(w0 + w_lora)) - 0.5
    w_ref[...] = jnp.exp(-jnp.exp(w_log))
    a = jax.nn.sigmoid(a0 + _dot3(_dot3(xa, a1h_ref[...], a1l_ref[...]), a2h_ref[...], a2l_ref[...]))
    a_ref[...] = a
    g_ref[...] = _bdot(jax.nn.sigmoid(_bdot(xg, g1_ref[...])), g2_ref[...])
    kk_ref[...] = k * k_k
    k_ref[...] = k * (1.0 + (a - 1.0) * k_a)


def _rwkv_proj(xn, prev, p, tm):
    t, d = xn.shape
    tok = pl.BlockSpec((tm, d), lambda i: (i, 0))
    weights = [p["mix"], p["vec"], p["wr"], p["wk"], p["wv"], *p["w1"], *p["w2"], *p["a1"], *p["a2"],
               p["g1"], p["g2"]]
    return pl.pallas_call(
        _rwkv_proj_kernel,
        grid=(t // tm,),
        in_specs=[tok, tok] + [_const_spec(w.shape) for w in weights],
        out_specs=[tok] * 7,
        out_shape=[jax.ShapeDtypeStruct((t, d), F32)] * 7,
        compiler_params=_params("parallel"),
        name="rwkv_proj",
    )(xn, prev, *weights)


def _rwkv_scan_kernel(r_ref, w_ref, k_ref, v_ref, kk_ref, a_ref, s0_ref, lnw_ref, lnb_ref, rk_ref,
                      o_ref, sfin_ref, state, o_rows):
    tb = pl.program_id(1)
    steps = r_ref.shape[0]

    @pl.when(tb == 0)
    def _():
        state[...] = s0_ref[...]

    def step(t, carry):
        r = r_ref[t]
        w = w_ref[t]
        k = k_ref[t]
        vv = v_ref[t]
        kku = kk_ref[t]
        norm = jnp.sqrt(jnp.sum(kku * kku, axis=0, keepdims=True))
        kk = kku / jnp.maximum(norm, 1e-12)
        alpha = -kk
        beta = kk * a_ref[t]

        def row(vi, c):
            s_v = state[vi]
            u = jnp.sum(s_v * alpha, axis=0, keepdims=True)
            s_new = s_v * w + u * beta + v_ref[t, pl.ds(vi, 1), :] * k
            state[vi] = s_new
            o_rows[pl.ds(vi, 1), :] = jnp.sum(s_new * r, axis=0, keepdims=True)
            return c

        lax.fori_loop(0, RW_HEAD, row, 0, unroll=32)
        o = o_rows[...]
        mu = jnp.mean(o, axis=0, keepdims=True)
        var = jnp.mean(jnp.square(o - mu), axis=0, keepdims=True)
        on = (o - mu) * lax.rsqrt(var + RW_LN_EPS)
        bonus = jnp.sum(r * k * rk_ref[...], axis=0, keepdims=True) * vv
        o_ref[t] = on * lnw_ref[...] + lnb_ref[...] + bonus
        return carry

    lax.fori_loop(0, steps, step, 0)

    @pl.when(tb == pl.num_programs(1) - 1)
    def _():
        sfin_ref[...] = state[...]


def _rwkv_scan(seqs, s0, lnw, lnb, rk, tt):
    length, n, probs = seqs[0].shape
    groups = probs // LANES
    seq_spec = pl.BlockSpec((tt, n, LANES), lambda g, i: (i, 0, g))
    st_spec = pl.BlockSpec((n, n, LANES), lambda g, i: (0, 0, g))
    lane_spec = pl.BlockSpec((n, LANES), lambda g, i: (0, 0))
    return pl.pallas_call(
        _rwkv_scan_kernel,
        grid=(groups, length // tt),
        in_specs=[seq_spec] * 6 + [st_spec] + [lane_spec] * 3,
        out_specs=[seq_spec, st_spec],
        out_shape=[jax.ShapeDtypeStruct((length, n, probs), F32),
                   jax.ShapeDtypeStruct((n, n, probs), F32)],
        scratch_shapes=[pltpu.VMEM((n, n, LANES), F32), pltpu.VMEM((n, LANES), F32)],
        compiler_params=_params("arbitrary", "arbitrary"),
        name="rwkv_scan",
    )(*seqs, s0, lnw, lnb, rk)


def _rwkv_out_kernel(o_ref, g_ref, h_ref, wo_ref, out_ref):
    out_ref[...] = h_ref[...] + _bdot(o_ref[...] * g_ref[...], wo_ref[...])


def _rwkv_out(o, g, h, wo, tm):
    t, d = h.shape
    tok = pl.BlockSpec((tm, d), lambda i: (i, 0))
    return pl.pallas_call(
        _rwkv_out_kernel,
        grid=(t // tm,),
        in_specs=[tok, tok, tok, _const_spec(wo.shape)],
        out_specs=tok,
        out_shape=jax.ShapeDtypeStruct((t, d), F32),
        compiler_params=_params("parallel"),
        name="rwkv_out",
    )(o, g, h, wo)


def _mamba_in_kernel(h_ref, g_ref, w_ref, z_ref, xbc_ref, dt_ref):
    xn = _rms(h_ref[...], g_ref[...]).astype(BF16)
    z_ref[...] = jnp.dot(xn, w_ref[:, :M_D_INNER], preferred_element_type=F32)
    xbc_ref[...] = jnp.dot(xn, w_ref[:, M_D_INNER:M_D_INNER + M_CONV_DIM], preferred_element_type=F32)
    dt_ref[...] = jnp.dot(xn, w_ref[:, M_D_INNER + M_CONV_DIM:], preferred_element_type=F32)


def _mamba_in(h, g, w_in, tm):
    t, d = h.shape
    return pl.pallas_call(
        _mamba_in_kernel,
        grid=(t // tm,),
        in_specs=[pl.BlockSpec((tm, d), lambda i: (i, 0)), _const_spec((1, d)), _const_spec(w_in.shape)],
        out_specs=[pl.BlockSpec((tm, M_D_INNER), lambda i: (i, 0)),
                   pl.BlockSpec((tm, M_CONV_DIM), lambda i: (i, 0)),
                   pl.BlockSpec((tm, M_DT_PAD), lambda i: (i, 0))],
        out_shape=[jax.ShapeDtypeStruct((t, M_D_INNER), F32),
                   jax.ShapeDtypeStruct((t, M_CONV_DIM), F32),
                   jax.ShapeDtypeStruct((t, M_DT_PAD), F32)],
        compiler_params=_params("parallel"),
        name="mamba_in_proj",
    )(h, g.reshape(1, d), w_in)


def _mamba_ssd_kernel(xbc_ref, z_ref, dt_ref, h_ref, conv0_ref, ssm0_ref, convw_ref, vec_ref,
                      dvec_ref, expand_ref, normw_ref, wout_ref,
                      out_ref, ssm_ref,
                      conv_buf, src_adt, src_acs, src_b, src_xdt, src_xdt_st, state, y_buf, *, seq_len, rows):
    b = pl.program_id(0)
    c = pl.program_id(1)
    q = M_CHUNK
    halo = SUBLANES
    d_bc = M_GROUPS * M_STATE

    @pl.when(c == 0)
    def _():
        state[...] = ssm0_ref[0]
        conv_buf[0:halo, :] = conv0_ref[0]

    @pl.when(c > 0)
    def _():
        conv_buf[0:halo, :] = conv_buf[rows:rows + halo, :]

    if rows < q:
        @pl.when((b == 0) & (c == 0))
        def _():
            src_adt[...] = jnp.zeros(src_adt.shape, F32)
            src_acs[...] = jnp.zeros(src_acs.shape, F32)
            src_b[...] = jnp.zeros(src_b.shape, BF16)
            src_xdt[...] = jnp.zeros(src_xdt.shape, BF16)
            src_xdt_st[...] = jnp.zeros(src_xdt_st.shape, BF16)

    conv_buf[halo:halo + rows, :] = xbc_ref[0]
    valid = (lax.broadcasted_iota(jnp.int32, (rows, 1), 0) + c * rows) < seq_len
    conv = vec_ref[0:1, :]
    for j in range(M_CONV):
        conv = conv + conv_buf[halo - (M_CONV - 1) + j:halo - (M_CONV - 1) + j + rows, :] * convw_ref[j:j + 1, :]
    act = jnp.where(valid, jax.nn.silu(conv), 0.0)
    xs = act[:, :M_D_INNER]
    dt = jnp.where(valid, _softplus(dt_ref[0] + dvec_ref[0:1, :]), 0.0)
    src_adt[0:rows, :] = dt * dvec_ref[1:2, :]
    li = lax.broadcasted_iota(jnp.int32, (rows, q), 0)
    si = lax.broadcasted_iota(jnp.int32, (rows, q), 1)
    causal = li >= si
    acs = _hdot(causal.astype(F32), src_adt[...])
    src_acs[0:rows, :] = acs
    acs_t = src_acs[...].T
    expand = expand_ref[...]
    dt_full = _hdot(dt, expand)
    acs_full = _hdot(acs, expand)
    last = acs[rows - 1:rows, :]
    last_full = _hdot(last, expand)
    xdt = xs * dt_full
    src_xdt[0:rows, :] = xdt.astype(BF16)
    src_xdt_st[0:rows, :] = (xdt * jnp.exp(last_full - acs_full)).astype(BF16)
    src_b[0:rows, :] = act[:, M_D_INNER:M_D_INNER + d_bc].astype(BF16)
    eacs_full = jnp.exp(acs_full)
    state_decay = jnp.exp(last)
    width = M_HPG * M_HEADDIM

    for g in range(M_GROUPS):
        cols = slice(g * width, (g + 1) * width)
        b_g = src_b[:, g * M_STATE:(g + 1) * M_STATE]
        c_g = act[:, M_D_INNER + d_bc + g * M_STATE:M_D_INNER + d_bc + (g + 1) * M_STATE].astype(BF16)
        cb = lax.dot_general(c_g, b_g, (((1,), (1,)), ((), ())), preferred_element_type=F32)
        h_g = state[g * M_HPG:(g + 1) * M_HPG].reshape(width, M_STATE)
        y_off = lax.dot_general(c_g, h_g.astype(BF16), (((1,), (1,)), ((), ())),
                                preferred_element_type=F32)
        upd = lax.dot_general(src_xdt_st[:, cols], b_g, (((0,), (0,)), ((), ())),
                              preferred_element_type=F32)
        ys = []
        for r in range(g * M_HPG, (g + 1) * M_HPG):
            lo = r * M_HEADDIM
            decay = jnp.where(causal, jnp.exp(acs[:, r:r + 1] - acs_t[r:r + 1, :]), 0.0)
            ys.append(jnp.dot((cb * decay).astype(BF16), src_xdt[:, lo:lo + M_HEADDIM],
                              preferred_element_type=F32))
            sub = slice((r - g * M_HPG) * M_HEADDIM, (r - g * M_HPG + 1) * M_HEADDIM)
            state[r] = state[r] * state_decay[:, r:r + 1] + upd[sub, :]
        y_buf[:, cols] = jnp.concatenate(ys, axis=1) + y_off * eacs_full[:, cols]

    y = y_buf[...] + vec_ref[1:2, :M_D_INNER] * xs
    yg = y * jax.nn.silu(z_ref[0])
    parts = []
    for g in range(M_GROUPS):
        part = yg[:, g * width:(g + 1) * width]
        parts.append(part * lax.rsqrt(jnp.mean(part * part, axis=-1, keepdims=True) + NORM_EPS))
    yn = jnp.concatenate(parts, axis=1) * normw_ref[...]
    out_ref[0] = h_ref[0] + _bdot(yn, wout_ref[...])

    @pl.when(c == pl.num_programs(1) - 1)
    def _():
        ssm_ref[0] = state[...]


def _mamba_ssd(xbc, z, dt, h, conv0, ssm0, p, rows, seq_len):
    bsz, length, _ = xbc.shape
    chunks = pl.cdiv(length, rows)
    weights = [p["conv_w"], p["vec"], p["dvec"], p["expand"], p["norm_w"], p["w_out"]]

    def seq_spec(width):
        return pl.BlockSpec((1, rows, width), lambda b, c: (b, c, 0))

    kern = functools.partial(_mamba_ssd_kernel, seq_len=seq_len, rows=rows)
    return pl.pallas_call(
        kern,
        grid=(bsz, chunks),
        in_specs=[seq_spec(M_CONV_DIM), seq_spec(M_D_INNER), seq_spec(M_DT_PAD), seq_spec(D_MODEL),
                  pl.BlockSpec((1, SUBLANES, M_CONV_DIM), lambda b, c: (b, 0, 0)),
                  pl.BlockSpec((1, M_HEADS, M_HEADDIM, M_STATE), lambda b, c: (b, 0, 0, 0))]
                 + [_const_spec(w.shape) for w in weights],
        out_specs=[seq_spec(D_MODEL),
                   pl.BlockSpec((1, M_HEADS, M_HEADDIM, M_STATE), lambda b, c: (b, 0, 0, 0))],
        out_shape=[jax.ShapeDtypeStruct((bsz, length, D_MODEL), F32),
                   jax.ShapeDtypeStruct((bsz, M_HEADS, M_HEADDIM, M_STATE), F32)],
        scratch_shapes=[pltpu.VMEM((rows + SUBLANES, M_CONV_DIM), F32),
                        pltpu.VMEM((M_CHUNK, M_DT_PAD), F32),
                        pltpu.VMEM((M_CHUNK, M_DT_PAD), F32),
                        pltpu.VMEM((M_CHUNK, M_GROUPS * M_STATE), BF16),
                        pltpu.VMEM((M_CHUNK, M_D_INNER), BF16),
                        pltpu.VMEM((M_CHUNK, M_D_INNER), BF16),
                        pltpu.VMEM((M_HEADS, M_HEADDIM, M_STATE), F32),
                        pltpu.VMEM((rows, M_D_INNER), F32)],
        compiler_params=_params("arbitrary", "arbitrary"),
        name="mamba_ssd",
    )(xbc, z, dt, h, conv0, ssm0, *weights)


_CANDS = [(ra, rb) for ra in range(P_TOPK) for rb in range(P_TOPK) if (ra + 1) * (rb + 1) <= P_TOPK]


def _peer_route_kernel(h_ref, g_ref, wq_ref, keys_ref, keys_lo_ref,
                       xt_ref, r1_ref, e1_ref, m0_ref, w0_ref,
                       q_buf, s_buf, rank0, best_vals, cand, counts, inv_z):
    tm = h_ref.shape[0]
    neg_inf = float("-inf")
    xn = _rms(h_ref[...], g_ref[...])
    xt = xn.T.astype(BF16)
    xt_ref[...] = xt
    q_buf[...] = jnp.dot(wq_ref[...], xt, preferred_element_type=F32)
    key_iota = lax.broadcasted_iota(jnp.int32, (P_NKEYS, tm), 0).astype(F32)

    def half(m, carry):
        head = m // 2
        z = m % 2
        row0 = pl.multiple_of(m * P_NKEYS, P_NKEYS)
        qm = q_buf[pl.ds(row0, P_NKEYS), :]
        q_hi = qm.astype(BF16)
        q_lo = (qm - q_hi.astype(F32)).astype(BF16)
        k_hi = keys_ref[m]
        s = (jnp.dot(k_hi, q_hi, preferred_element_type=F32)
             + (jnp.dot(k_hi, q_lo, preferred_element_type=F32)
                + jnp.dot(keys_lo_ref[m], q_hi, preferred_element_type=F32)))

        s_buf[...] = s

        def next_distinct(j, prev):
            sj = s_buf[...]
            best = jnp.max(jnp.where(sj < prev, sj, neg_inf), axis=0, keepdims=True)
            best_vals[z, j, pl.ds(head, 1), :] = best
            return best

        def extract_ties(j, sc):
            cur, rank = sc
            best = jnp.max(cur, axis=0, keepdims=True)
            first = jnp.min(jnp.where(cur == best, key_iota, float(P_NKEYS)), axis=0, keepdims=True)
            sel = key_iota == first
            best_vals[z, j, pl.ds(head, 1), :] = best
            return jnp.where(sel, neg_inf, cur), jnp.where(sel, jnp.asarray(j, dtype=F32), rank)

        last = lax.fori_loop(0, P_TOPK, next_distinct, jnp.full((1, tm), float("inf"), F32))
        reached = jnp.sum(jnp.where(s >= last, 1.0, 0.0), axis=0, keepdims=True)
        has_ties = jnp.max(jnp.abs(reached - float(P_TOPK))) > 0.0

        def rank_by_count():
            rank = jnp.zeros((P_NKEYS, tm), F32)
            for jj in range(P_TOPK):
                rank = rank + jnp.where(best_vals[z, jj, pl.ds(head, 1), :] > s, 1.0, 0.0)
            return rank

        def rank_with_ties():
            no_rank = jnp.full((P_NKEYS, tm), float(P_NKEYS), F32)
            return lax.fori_loop(0, P_TOPK, extract_ties, (s, no_rank))[1]

        rank = lax.cond(has_ties, rank_with_ties, rank_by_count)
        e = jnp.exp(s - jnp.max(s, axis=0, keepdims=True))

        @pl.when(z == 0)
        def _():
            rank0[head] = rank
            w0_ref[head] = e

        @pl.when(z == 1)
        def _():
            r1_ref[head] = rank.astype(BF16)
            e1_ref[head] = e.astype(BF16)

        return carry

    lax.fori_loop(0, 2 * P_HEADS, half, 0)

    ex0 = [jnp.exp(best_vals[0, ra] - best_vals[0, 0]) for ra in range(P_TOPK)]
    ex1 = [jnp.exp(best_vals[1, rb] - best_vals[1, 0]) for rb in range(P_TOPK)]
    for i, (ra, rb) in enumerate(_CANDS):
        cand[i] = best_vals[0, ra] + best_vals[1, rb]
    counts[...] = jnp.zeros(counts.shape, F32)

    def pick(it, z):
        best = cand[0]
        for i in range(1, len(_CANDS)):
            best = jnp.maximum(best, cand[i])
        found = jnp.zeros((P_HEADS, tm), F32)
        for i, (ra, rb) in enumerate(_CANDS):
            ci = cand[i]
            hit = jnp.where(ci == best, 1.0, 0.0) * (1.0 - found)
            found = found + hit
            cand[i] = jnp.where(hit > 0.0, neg_inf, ci)
            counts[ra] = counts[ra] + hit
            z = z + hit * (ex0[ra] * ex1[rb])
        return z

    inv_z[...] = 1.0 / lax.fori_loop(0, P_TOPK, pick, jnp.zeros((P_HEADS, tm), F32))

    def finish(head, carry):
        rk = rank0[head]
        m0 = jnp.zeros((P_NKEYS, tm), F32)
        for ra in range(P_TOPK):
            m0 = jnp.where(rk == float(ra), counts[ra, pl.ds(head, 1), :], m0)
        m0_ref[head] = m0
        w0_ref[head] = w0_ref[head] * inv_z[pl.ds(head, 1), :]
        return carry

    lax.fori_loop(0, P_HEADS, finish, 0)


def _peer_route(h, g, wq_t, keys, keys_lo, tm):
    t, d = h.shape
    tiles = pl.cdiv(t, tm)
    t_pad = tiles * tm
    tile = pl.BlockSpec((P_HEADS, P_NKEYS, tm), lambda i: (0, 0, i))
    tile_f32 = jax.ShapeDtypeStruct((P_HEADS, P_NKEYS, t_pad), F32)
    tile_bf16 = jax.ShapeDtypeStruct((P_HEADS, P_NKEYS, t_pad), BF16)
    return pl.pallas_call(
        _peer_route_kernel,
        grid=(tiles,),
        in_specs=[pl.BlockSpec((tm, d), lambda i: (i, 0)), _const_spec((1, d)),
                  _const_spec(wq_t.shape), _const_spec(keys.shape), _const_spec(keys_lo.shape)],
        out_specs=[pl.BlockSpec((d, tm), lambda i: (0, i)), tile, tile, tile, tile],
        out_shape=[jax.ShapeDtypeStruct((d, t_pad), BF16), tile_bf16, tile_bf16, tile_f32, tile_f32],
        scratch_shapes=[pltpu.VMEM((P_HEADS * P_QDIM, tm), F32),
                        pltpu.VMEM((P_NKEYS, tm), F32),
                        pltpu.VMEM((P_HEADS, P_NKEYS, tm), F32),
                        pltpu.VMEM((2, P_TOPK, P_HEADS, tm), F32),
                        pltpu.VMEM((len(_CANDS), P_HEADS, tm), F32),
                        pltpu.VMEM((P_TOPK, P_HEADS, tm), F32),
                        pltpu.VMEM((P_HEADS, tm), F32)],
        compiler_params=_params("parallel"),
        name="peer_route",
    )(h, g.reshape(1, d), wq_t, keys, keys_lo)


def _gelu(x):
    return 0.5 * x * (1.0 + lax.erf(x * (1.0 / math.sqrt(2.0))))


def _peer_dense_kernel(xt_ref, u_ref, vt_ref, r1_ref, e1_ref, m0_ref, w0_ref, h_ref, gfin_ref,
                       out_ref, acc, hid, prob, *, final_norm):
    j = pl.program_id(1)
    keys_per_step = P_EXPERT_TILE // P_NKEYS

    @pl.when(j == 0)
    def _():
        acc[...] = jnp.zeros(acc.shape, F32)

    tm = xt_ref.shape[1]
    hid[...] = jnp.dot(u_ref[...], xt_ref[...], preferred_element_type=F32)
    zero = jnp.zeros((), BF16)
    for ii in range(keys_per_step):
        i1 = j * keys_per_step + ii
        rows = slice(ii * P_NKEYS, (ii + 1) * P_NKEYS)
        m0_rows = [m0_ref[head, pl.ds(i1, 1), :].astype(BF16) for head in range(P_HEADS)]
        w0_rows = [w0_ref[head, pl.ds(i1, 1), :].astype(BF16) for head in range(P_HEADS)]
        for lt in range(tm // LANES):
            lanes = slice(lt * LANES, (lt + 1) * LANES)
            gate = None
            for head in range(P_HEADS):
                m0 = m0_rows[head][:, lanes]
                w0 = w0_rows[head][:, lanes]
                term = w0 * jnp.where(r1_ref[head, :, lanes] < m0, e1_ref[head, :, lanes], zero)
                gate = term if gate is None else gate + term
            prob[rows, lanes] = gate * _gelu(hid[rows, lanes]).astype(BF16)
    acc[...] += jnp.dot(vt_ref[...], prob[...], preferred_element_type=F32)

    @pl.when(j == pl.num_programs(1) - 1)
    def _():
        res = h_ref[...] + acc[...].T
        if final_norm:
            res = _rms(res, gfin_ref[...])
        out_ref[...] = res


def _peer_dense(xt, u_bf, vt_bf, r1, e1, m0, w0, h, g_final, tm, final_norm):
    t, d = h.shape
    n_exp = u_bf.shape[0]
    tile = pl.BlockSpec((P_HEADS, P_NKEYS, tm), lambda i, j: (0, 0, i))
    kern = functools.partial(_peer_dense_kernel, final_norm=final_norm)
    return pl.pallas_call(
        kern,
        grid=(pl.cdiv(t, tm), n_exp // P_EXPERT_TILE),
        in_specs=[pl.BlockSpec((d, tm), lambda i, j: (0, i)),
                  pl.BlockSpec((P_EXPERT_TILE, d), lambda i, j: (j, 0)),
                  pl.BlockSpec((d, P_EXPERT_TILE), lambda i, j: (0, j)),
                  tile, tile, tile, tile,
                  pl.BlockSpec((tm, d), lambda i, j: (i, 0)),
                  _const_spec((1, d))],
        out_specs=pl.BlockSpec((tm, d), lambda i, j: (i, 0)),
        out_shape=jax.ShapeDtypeStruct((t, d), F32),
        scratch_shapes=[pltpu.VMEM((d, tm), F32),
                        pltpu.VMEM((P_EXPERT_TILE, tm), F32),
                        pltpu.VMEM((P_EXPERT_TILE, tm), BF16)],
        compiler_params=_params("parallel", "arbitrary"),
        name="peer_dense",
    )(xt, u_bf, vt_bf, r1, e1, m0, w0, h, g_final.reshape(1, d))


def _pad_cols(w, n):
    return jnp.pad(w, ((0, 0), (0, n - w.shape[1])))


def _pad_rows(w, n):
    return jnp.pad(w, ((0, n - w.shape[0]), (0, 0)))


def _prep_rwkv(mix, w_rkv, w0, w1, w2, a0, a1, a2, g1, g2, k_k, k_a, r_k, ln_w, ln_b, w_o):
    d = D_MODEL
    lane = lambda x: jnp.tile(x.reshape(RW_HEADS, RW_HEAD).T, (1, LANES // RW_HEADS))
    return {
        "mix": _pad_rows(mix, SUBLANES),
        "vec": _pad_rows(jnp.stack([w0, a0, k_k, k_a]), SUBLANES),
        "wr": w_rkv[0].astype(BF16), "wk": w_rkv[1].astype(BF16), "wv": w_rkv[2].astype(BF16),
        "w1": _split_bf16(_pad_cols(w1, LORA_PAD)), "w2": _split_bf16(_pad_rows(w2, LORA_PAD)),
        "a1": _split_bf16(_pad_cols(a1, LORA_PAD)), "a2": _split_bf16(_pad_rows(a2, LORA_PAD)),
        "g1": _pad_cols(g1, GATE_LORA_PAD).astype(BF16), "g2": _pad_rows(g2, GATE_LORA_PAD).astype(BF16),
        "lnw": lane(ln_w), "lnb": lane(ln_b), "rk": lane(r_k.reshape(d)),
        "wo": w_o.astype(BF16),
    }


def _prep_mamba(in_proj, conv_w, conv_b, dt_bias, a_log, d_skip, norm_w, out_proj):
    head_of_channel = jnp.arange(M_D_INNER) // M_HEADDIM
    expand = (jnp.arange(M_DT_PAD)[:, None] == head_of_channel[None, :]).astype(F32)
    d_full = jnp.pad(jnp.repeat(d_skip, M_HEADDIM), (0, M_CONV_DIM - M_D_INNER))
    pad_h = lambda x: jnp.pad(x, (0, M_DT_PAD - M_HEADS))
    return {
        "w_in": _pad_cols(in_proj, M_IN_DIM - M_HEADS + M_DT_PAD).astype(BF16),
        "conv_w": _pad_rows(conv_w, SUBLANES),
        "vec": _pad_rows(jnp.stack([conv_b, d_full]), SUBLANES),
        "dvec": _pad_rows(jnp.stack([pad_h(dt_bias), pad_h(-jnp.exp(a_log))]), SUBLANES),
        "expand": expand,
        "norm_w": norm_w.reshape(1, M_D_INNER),
        "w_out": out_proj.astype(BF16),
    }


def _prep_peer(w_q, sub_keys, u_tab, v_tab):
    keys = jnp.transpose(sub_keys, (1, 0, 2, 3)).reshape(2 * P_HEADS, P_NKEYS, P_QDIM // 2)
    keys_hi, keys_lo = _split_bf16(keys)
    return {"wq_t": w_q.T.astype(BF16), "keys": keys_hi, "keys_lo": keys_lo,
            "u": u_tab.astype(BF16), "vt": v_tab.T.astype(BF16)}


def _to_scan(x, bsz, length):
    x = x.reshape(bsz, length, RW_HEADS, RW_HEAD)
    return jnp.transpose(x, (1, 3, 0, 2)).reshape(length, RW_HEAD, bsz * RW_HEADS)


def _from_scan(x, bsz, length):
    x = x.reshape(length, RW_HEAD, bsz, RW_HEADS)
    return jnp.transpose(x, (2, 0, 3, 1)).reshape(bsz * length, D_MODEL)


def _rwkv_layer(h, g_mix, shift0, wkv0, p, bsz, length, tm, tt):
    xn = _norm(h, g_mix, tm)
    xn3 = xn.reshape(bsz, length, D_MODEL)
    prev = jnp.concatenate([shift0[:, None, :], xn3[:, :-1]], axis=1).reshape(bsz * length, D_MODEL)
    r, w, k, v, kk, a, g = _rwkv_proj(xn, prev, p, tm)
    seqs = [_to_scan(x, bsz, length) for x in (r, w, k, v, kk, a)]
    s0 = jnp.transpose(wkv0, (2, 3, 0, 1)).reshape(RW_HEAD, RW_HEAD, bsz * RW_HEADS)
    o, s_fin = _rwkv_scan(seqs, s0, p["lnw"], p["lnb"], p["rk"], tt)
    o = _from_scan(o, bsz, length)
    wkv = jnp.transpose(s_fin.reshape(RW_HEAD, RW_HEAD, bsz, RW_HEADS), (2, 3, 0, 1))
    return _rwkv_out(o, g, h, p["wo"], tm), xn3[:, -1], wkv


def _mamba_layer(h, g_mix, conv0, ssm0, p, bsz, length, tm, rows):
    z, xbc, dt = _mamba_in(h, g_mix, p["w_in"], tm)
    xbc3 = xbc.reshape(bsz, length, M_CONV_DIM)
    tail = M_CONV - 1
    conv_new = xbc3[:, length - tail:] if length >= tail else jnp.concatenate([conv0, xbc3], axis=1)[:, length:]
    pad_t = (-length) % SUBLANES if length < rows else 0

    def seq(x, c):
        x = x.reshape(bsz, length, c)
        return jnp.pad(x, ((0, 0), (0, pad_t), (0, 0))) if pad_t else x

    conv0_p = jnp.pad(conv0, ((0, 0), (SUBLANES - (M_CONV - 1), 0), (0, 0)))
    out, ssm = _mamba_ssd(seq(xbc, M_CONV_DIM), seq(z, M_D_INNER), seq(dt, M_DT_PAD), seq(h, D_MODEL),
                          conv0_p, ssm0, p, min(rows, length + pad_t), length)
    if pad_t:
        out = out[:, :length]
    return out.reshape(bsz * length, D_MODEL), conv_new, ssm


def _peer_layer(h, g_ffn, p, tm, g_final=None):
    xt, r1, e1, m0, w0 = _peer_route(h, g_ffn, p["wq_t"], p["keys"], p["keys_lo"], tm)
    final = g_final is not None
    return _peer_dense(xt, p["u"], p["vt"], r1, e1, m0, w0, h, g_final if final else g_ffn, tm, final)


def _trunk(h, shift0, wkv0, conv0, ssm0, norm_mix, norm_ffn, norm_final, rwkv_p, mamba_p, peer_p,
           bsz, length, tm, tt, peer_tm, rows):
    h, shift, wkv = _rwkv_layer(h, norm_mix[0], shift0, wkv0, rwkv_p, bsz, length, tm, tt)
    h = _peer_layer(h, norm_ffn[0], peer_p[0], peer_tm)
    h, conv, ssm = _mamba_layer(h, norm_mix[1], conv0, ssm0, mamba_p, bsz, length, tm, rows)
    y = _peer_layer(h, norm_ffn[1], peer_p[1], peer_tm, g_final=norm_final)
    return y.reshape(bsz, length, D_MODEL), shift[None], wkv[None], conv[None], ssm[None]


def kernel(x_prompt, x_sample, state_rwkv_shift, state_rwkv_wkv, state_mamba_conv, state_mamba_ssm, meta_tokens, norm_mix, norm_ffn, norm_final, rwkv_mix, rwkv_w_rkv, rwkv_w0, rwkv_w1, rwkv_w2, rwkv_a0, rwkv_a1, rwkv_a2, rwkv_g1, rwkv_g2, rwkv_k_k, rwkv_k_a, rwkv_r_k, rwkv_ln_w, rwkv_ln_b, rwkv_w_o, mamba_in_proj, mamba_conv_w, mamba_conv_b, mamba_dt_bias, mamba_a_log, mamba_d, mamba_norm_w, mamba_out_proj, peer_w_q, peer_sub_keys, peer_u, peer_v):
    rwkv_p = _prep_rwkv(rwkv_mix[0], rwkv_w_rkv[0], rwkv_w0[0], rwkv_w1[0], rwkv_w2[0], rwkv_a0[0],
                        rwkv_a1[0], rwkv_a2[0], rwkv_g1[0], rwkv_g2[0], rwkv_k_k[0], rwkv_k_a[0],
                        rwkv_r_k[0], rwkv_ln_w[0], rwkv_ln_b[0], rwkv_w_o[0])
    mamba_p = _prep_mamba(mamba_in_proj[0], mamba_conv_w[0], mamba_conv_b[0], mamba_dt_bias[0],
                          mamba_a_log[0], mamba_d[0], mamba_norm_w[0], mamba_out_proj[0])
    peer_p = [_prep_peer(peer_w_q[i], peer_sub_keys[i], peer_u[i], peer_v[i]) for i in range(2)]

    bp, lp = x_prompt.shape[0], x_prompt.shape[1] + N_META
    meta = jnp.broadcast_to(meta_tokens[None], (bp, N_META, D_MODEL))
    hp = jnp.concatenate([meta, x_prompt], axis=1).reshape(bp * lp, D_MODEL)
    zeros = lambda *s: jnp.zeros(s, F32)
    yp, p_shift, p_wkv, p_conv, p_ssm = _trunk(
        hp, zeros(bp, D_MODEL), zeros(bp, RW_HEADS, RW_HEAD, RW_HEAD), zeros(bp, M_CONV - 1, M_CONV_DIM),
        zeros(bp, M_HEADS, M_HEADDIM, M_STATE), norm_mix, norm_ffn, norm_final, rwkv_p, mamba_p, peer_p,
        bp, lp, tm=344, tt=48, peer_tm=512, rows=M_CHUNK)

    bs, ls = x_sample.shape[0], x_sample.shape[1]
    ys, s_shift, s_wkv, s_conv, s_ssm = _trunk(
        x_sample.reshape(bs * ls, D_MODEL), state_rwkv_shift[0], state_rwkv_wkv[0], state_mamba_conv[0],
        state_mamba_ssm[0], norm_mix, norm_ffn, norm_final, rwkv_p, mamba_p, peer_p,
        bs, ls, tm=256, tt=ls, peer_tm=512, rows=M_CHUNK)
    return (yp[:, N_META:], ys, p_shift, p_wkv, p_conv, p_ssm, s_shift, s_wkv, s_conv, s_ssm)
```

```python
import functools
import math

import jax
import jax.numpy as jnp
from jax import lax
from jax.experimental import pallas as pl
from jax.experimental.pallas import tpu as pltpu

F32 = jnp.float32
BF16 = jnp.bfloat16
HIGHEST = lax.Precision.HIGHEST

D_MODEL = 1024
N_META = 16
NORM_EPS = 1e-5
RW_HEAD = 64
RW_HEADS = D_MODEL // RW_HEAD
RW_LN_EPS = 64e-5
LORA_PAD = 128
GATE_LORA_PAD = 256
M_D_INNER = 2048
M_HEADDIM = 64
M_HEADS = M_D_INNER // M_HEADDIM
M_GROUPS = 4
M_HPG = M_HEADS // M_GROUPS
M_STATE = 128
M_CONV = 4
M_CONV_DIM = M_D_INNER + 2 * M_GROUPS * M_STATE
M_IN_DIM = 2 * M_D_INNER + 2 * M_GROUPS * M_STATE + M_HEADS
M_DT_PAD = 128
M_CHUNK = 128
P_HEADS = 8
P_NKEYS = 128
P_QDIM = 256
P_TOPK = 16
P_EXPERT_TILE = 1024
P_EXPERT_SUB = 256

LANES = 128
SUBLANES = 8
VMEM_LIMIT_BYTES = 56 * 1024 * 1024


def _params(*semantics):
    return pltpu.CompilerParams(dimension_semantics=semantics, vmem_limit_bytes=VMEM_LIMIT_BYTES)


def _const_spec(shape):
    zeros = (0,) * len(shape)
    return pl.BlockSpec(shape, lambda *_: zeros)


def _rms(x, g):
    return x * lax.rsqrt(jnp.mean(x * x, axis=-1, keepdims=True) + NORM_EPS) * g


def _softplus(x):
    return jnp.maximum(x, 0.0) + jnp.log1p(jnp.exp(-jnp.abs(x)))


def _bdot(a, b):
    return jnp.dot(a.astype(BF16), b.astype(BF16), preferred_element_type=F32)


def _hdot(a, b):
    return jnp.dot(a, b, precision=HIGHEST, preferred_element_type=F32)


def _split_bf16(w):
    hi = w.astype(BF16)
    return hi, (w - hi.astype(F32)).astype(BF16)


def _dot3(a, b_hi, b_lo):
    a_hi, a_lo = _split_bf16(a)
    return (jnp.dot(a_hi, b_hi, preferred_element_type=F32)
            + (jnp.dot(a_hi, b_lo, preferred_element_type=F32) + jnp.dot(a_lo, b_hi, preferred_element_type=F32)))


def _norm_kernel(h_ref, g_ref, o_ref):
    o_ref[...] = _rms(h_ref[...], g_ref[...])


def _norm(h, g, tm):
    t, d = h.shape
    return pl.pallas_call(
        _norm_kernel,
        grid=(t // tm,),
        in_specs=[pl.BlockSpec((tm, d), lambda i: (i, 0)), _const_spec((1, d))],
        out_specs=pl.BlockSpec((tm, d), lambda i: (i, 0)),
        out_shape=jax.ShapeDtypeStruct((t, d), F32),
        compiler_params=_params("parallel"),
        name="rms_norm",
    )(h, g.reshape(1, d))


def _rwkv_proj_kernel(xn_ref, prev_ref, mix_ref, vec_ref, wr_ref, wk_ref, wv_ref,
                      w1h_ref, w1l_ref, w2h_ref, w2l_ref, a1h_ref, a1l_ref, a2h_ref, a2l_ref, g1_ref, g2_ref,
                      r_ref, w_ref, k_ref, v_ref, kk_ref, a_ref, g_ref):
    xn = xn_ref[...]
    dx = prev_ref[...] - xn
    xr, xw, xk, xv, xa, xg = [xn + dx * mix_ref[j:j + 1, :] for j in range(6)]
    w0, a0, k_k, k_a = [vec_ref[j:j + 1, :] for j in range(4)]
    r_ref[...] = _bdot(xr, wr_ref[...])
    k = _bdot(xk, wk_ref[...])
    v_ref[...] = _bdot(xv, wv_ref[...])
    w_lora = _dot3(jnp.tanh(_dot3(xw, w1h_ref[...], w1l_ref[...])), w2h_ref[...], w2l_ref[...])
    w_log = -_softplus(-(w0 + w_lora)) - 0.5
    w_ref[...] = jnp.exp(-jnp.exp(w_log))
    a = jax.nn.sigmoid(a0 + _dot3(_dot3(xa, a1h_ref[...], a1l_ref[...]), a2h_ref[...], a2l_ref[...]))
    a_ref[...] = a
    g_ref[...] = _bdot(jax.nn.sigmoid(_bdot(xg, g1_ref[...])), g2_ref[...])
    kk_ref[...] = k * k_k
    k_ref[...] = k * (1.0 + (a - 1.0) * k_a)


def _rwkv_proj(xn, prev, p, tm):
    t, d = xn.shape
    tok = pl.BlockSpec((tm, d), lambda i: (i, 0))
    weights = [p["mix"], p["vec"], p["wr"], p["wk"], p["wv"], *p["w1"], *p["w2"], *p["a1"], *p["a2"],
               p["g1"], p["g2"]]
    return pl.pallas_call(
        _rwkv_proj_kernel,
        grid=(t // tm,),
        in_specs=[tok, tok] + [_const_spec(w.shape) for w in weights],
        out_specs=[tok] * 7,
        out_shape=[jax.ShapeDtypeStruct((t, d), F32)] * 7,
        compiler_params=_params("parallel"),
        name="rwkv_proj",
    )(xn, prev, *weights)


def _rwkv_scan_kernel(r_ref, w_ref, k_ref, v_ref, kk_ref, a_ref, s0_ref, lnw_ref, lnb_ref, rk_ref,
                      o_ref, sfin_ref, state, o_rows):
    tb = pl.program_id(1)
    steps = r_ref.shape[0]

    @pl.when(tb == 0)
    def _():
        state[...] = s0_ref[...]

    def step(t, carry):
        r = r_ref[t]
        w = w_ref[t]
        k = k_ref[t]
        vv = v_ref[t]
        kku = kk_ref[t]
        norm = jnp.sqrt(jnp.sum(kku * kku, axis=0, keepdims=True))
        kk = kku / jnp.maximum(norm, 1e-12)
        alpha = -kk
        beta = kk * a_ref[t]

        def row(vi, c):
            s_v = state[vi]
            u = jnp.sum(s_v * alpha, axis=0, keepdims=True)
            s_new = s_v * w + u * beta + v_ref[t, pl.ds(vi, 1), :] * k
            state[vi] = s_new
            o_rows[pl.ds(vi, 1), :] = jnp.sum(s_new * r, axis=0, keepdims=True)
            return c

        lax.fori_loop(0, RW_HEAD, row, 0, unroll=32)
        o = o_rows[...]
        mu = jnp.mean(o, axis=0, keepdims=True)
        var = jnp.mean(jnp.square(o - mu), axis=0, keepdims=True)
        on = (o - mu) * lax.rsqrt(var + RW_LN_EPS)
        bonus = jnp.sum(r * k * rk_ref[...], axis=0, keepdims=True) * vv
        o_ref[t] = on * lnw_ref[...] + lnb_ref[...] + bonus
        return carry

    lax.fori_loop(0, steps, step, 0)

    @pl.when(tb == pl.num_programs(1) - 1)
    def _():
        sfin_ref[...] = state[...]


def _rwkv_scan(seqs, s0, lnw, lnb, rk, tt):
    length, n, probs = seqs[0].shape
    groups = probs // LANES
    seq_spec = pl.BlockSpec((tt, n, LANES), lambda g, i: (i, 0, g))
    st_spec = pl.BlockSpec((n, n, LANES), lambda g, i: (0, 0, g))
    lane_spec = pl.BlockSpec((n, LANES), lambda g, i: (0, 0))
    return pl.pallas_call(
        _rwkv_scan_kernel,
        grid=(groups, length // tt),
        in_specs=[seq_spec] * 6 + [st_spec] + [lane_spec] * 3,
        out_specs=[seq_spec, st_spec],
        out_shape=[jax.ShapeDtypeStruct((length, n, probs), F32),
                   jax.ShapeDtypeStruct((n, n, probs), F32)],
        scratch_shapes=[pltpu.VMEM((n, n, LANES), F32), pltpu.VMEM((n, LANES), F32)],
        compiler_params=_params("arbitrary", "arbitrary"),
        name="rwkv_scan",
    )(*seqs, s0, lnw, lnb, rk)


def _rwkv_out_kernel(o_ref, g_ref, h_ref, wo_ref, out_ref):
    out_ref[...] = h_ref[...] + _bdot(o_ref[...] * g_ref[...], wo_ref[...])


def _rwkv_out(o, g, h, wo, tm):
    t, d = h.shape
    tok = pl.BlockSpec((tm, d), lambda i: (i, 0))
    return pl.pallas_call(
        _rwkv_out_kernel,
        grid=(t // tm,),
        in_specs=[tok, tok, tok, _const_spec(wo.shape)],
        out_specs=tok,
        out_shape=jax.ShapeDtypeStruct((t, d), F32),
        compiler_params=_params("parallel"),
        name="rwkv_out",
    )(o, g, h, wo)


def _mamba_in_kernel(h_ref, g_ref, w_ref, z_ref, xbc_ref, dt_ref):
    xn = _rms(h_ref[...], g_ref[...]).astype(BF16)
    z_ref[...] = jnp.dot(xn, w_ref[:, :M_D_INNER], preferred_element_type=F32)
    xbc_ref[...] = jnp.dot(xn, w_ref[:, M_D_INNER:M_D_INNER + M_CONV_DIM], preferred_element_type=F32)
    dt_ref[...] = jnp.dot(xn, w_ref[:, M_D_INNER + M_CONV_DIM:], preferred_element_type=F32)


def _mamba_in(h, g, w_in, tm):
    t, d = h.shape
    return pl.pallas_call(
        _mamba_in_kernel,
        grid=(t // tm,),
        in_specs=[pl.BlockSpec((tm, d), lambda i: (i, 0)), _const_spec((1, d)), _const_spec(w_in.shape)],
        out_specs=[pl.BlockSpec((tm, M_D_INNER), lambda i: (i, 0)),
                   pl.BlockSpec((tm, M_CONV_DIM), lambda i: (i, 0)),
                   pl.BlockSpec((tm, M_DT_PAD), lambda i: (i, 0))],
        out_shape=[jax.ShapeDtypeStruct((t, M_D_INNER), F32),
                   jax.ShapeDtypeStruct((t, M_CONV_DIM), F32),
                   jax.ShapeDtypeStruct((t, M_DT_PAD), F32)],
        compiler_params=_params("parallel"),
        name="mamba_in_proj",
    )(h, g.reshape(1, d), w_in)


def _mamba_ssd_kernel(xbc_ref, z_ref, dt_ref, h_ref, conv0_ref, ssm0_ref, convw_ref, vec_ref,
                      dvec_ref, expand_ref, normw_ref, wout_ref,
                      out_ref, ssm_ref,
                      conv_buf, src_adt, src_acs, src_b, src_xdt, src_xdt_st, state, y_buf, *, seq_len, rows):
    b = pl.program_id(0)
    c = pl.program_id(1)
    q = M_CHUNK
    halo = SUBLANES
    d_bc = M_GROUPS * M_STATE

    @pl.when(c == 0)
    def _():
        state[...] = ssm0_ref[0]
        conv_buf[0:halo, :] = conv0_ref[0]

    @pl.when(c > 0)
    def _():
        conv_buf[0:halo, :] = conv_buf[rows:rows + halo, :]

    if rows < q:
        @pl.when((b == 0) & (c == 0))
        def _():
            src_adt[...] = jnp.zeros(src_adt.shape, F32)
            src_acs[...] = jnp.zeros(src_acs.shape, F32)
            src_b[...] = jnp.zeros(src_b.shape, BF16)
            src_xdt[...] = jnp.zeros(src_xdt.shape, BF16)
            src_xdt_st[...] = jnp.zeros(src_xdt_st.shape, BF16)

    conv_buf[halo:halo + rows, :] = xbc_ref[0]
    valid = (lax.broadcasted_iota(jnp.int32, (rows, 1), 0) + c * rows) < seq_len
    conv = vec_ref[0:1, :]
    for j in range(M_CONV):
        conv = conv + conv_buf[halo - (M_CONV - 1) + j:halo - (M_CONV - 1) + j + rows, :] * convw_ref[j:j + 1, :]
    act = jnp.where(valid, jax.nn.silu(conv), 0.0)
    xs = act[:, :M_D_INNER]
    dt = jnp.where(valid, _softplus(dt_ref[0] + dvec_ref[0:1, :]), 0.0)
    src_adt[0:rows, :] = dt * dvec_ref[1:2, :]
    li = lax.broadcasted_iota(jnp.int32, (rows, q), 0)
    si = lax.broadcasted_iota(jnp.int32, (rows, q), 1)
    causal = li >= si
    acs = _hdot(causal.astype(F32), src_adt[...])
    src_acs[0:rows, :] = acs
    acs_t = src_acs[...].T
    expand = expand_ref[...]
    dt_full = _hdot(dt, expand)
    acs_full = _hdot(acs, expand)
    last = acs[rows - 1:rows, :]
    last_full = _hdot(last, expand)
    xdt = xs * dt_full
    src_xdt[0:rows, :] = xdt.astype(BF16)
    src_xdt_st[0:rows, :] = (xdt * jnp.exp(last_full - acs_full)).astype(BF16)
    src_b[0:rows, :] = act[:, M_D_INNER:M_D_INNER + d_bc].astype(BF16)
    eacs_full = jnp.exp(acs_full)
    state_decay = jnp.exp(last)
    width = M_HPG * M_HEADDIM

    for g in range(M_GROUPS):
        cols = slice(g * width, (g + 1) * width)
        b_g = src_b[:, g * M_STATE:(g + 1) * M_STATE]
        c_g = act[:, M_D_INNER + d_bc + g * M_STATE:M_D_INNER + d_bc + (g + 1) * M_STATE].astype(BF16)
        cb = lax.dot_general(c_g, b_g, (((1,), (1,)), ((), ())), preferred_element_type=F32)
        h_g = state[g * M_HPG:(g + 1) * M_HPG].reshape(width, M_STATE)
        y_off = lax.dot_general(c_g, h_g.astype(BF16), (((1,), (1,)), ((), ())),
                                preferred_element_type=F32)
        upd = lax.dot_general(src_xdt_st[:, cols], b_g, (((0,), (0,)), ((), ())),
                              preferred_element_type=F32)
        ys = []
        for r in range(g * M_HPG, (g + 1) * M_HPG):
            lo = r * M_HEADDIM
            decay = jnp.where(causal, jnp.exp(acs[:, r:r + 1] - acs_t[r:r + 1, :]), 0.0)
            ys.append(jnp.dot((cb * decay).astype(BF16), src_xdt[:, lo:lo + M_HEADDIM],
                              preferred_element_type=F32))
            sub = slice((r - g * M_HPG) * M_HEADDIM, (r - g * M_HPG + 1) * M_HEADDIM)
            state[r] = state[r] * state_decay[:, r:r + 1] + upd[sub, :]
        y_buf[:, cols] = jnp.concatenate(ys, axis=1) + y_off * eacs_full[:, cols]

    y = y_buf[...] + vec_ref[1:2, :M_D_INNER] * xs
    yg = y * jax.nn.silu(z_ref[0])
    parts = []
    for g in range(M_GROUPS):
        part = yg[:, g * width:(g + 1) * width]
        parts.append(part * lax.rsqrt(jnp.mean(part * part, axis=-1, keepdims=True) + NORM_EPS))
    yn = jnp.concatenate(parts, axis=1) * normw_ref[...]
    out_ref[0] = h_ref[0] + _bdot(yn, wout_ref[...])

    @pl.when(c == pl.num_programs(1) - 1)
    def _():
        ssm_ref[0] = state[...]


def _mamba_ssd(xbc, z, dt, h, conv0, ssm0, p, rows, seq_len):
    bsz, length, _ = xbc.shape
    chunks = pl.cdiv(length, rows)
    weights = [p["conv_w"], p["vec"], p["dvec"], p["expand"], p["norm_w"], p["w_out"]]

    def seq_spec(width):
        return pl.BlockSpec((1, rows, width), lambda b, c: (b, c, 0))

    kern = functools.partial(_mamba_ssd_kernel, seq_len=seq_len, rows=rows)
    return pl.pallas_call(
        kern,
        grid=(bsz, chunks),
        in_specs=[seq_spec(M_CONV_DIM), seq_spec(M_D_INNER), seq_spec(M_DT_PAD), seq_spec(D_MODEL),
                  pl.BlockSpec((1, SUBLANES, M_CONV_DIM), lambda b, c: (b, 0, 0)),
                  pl.BlockSpec((1, M_HEADS, M_HEADDIM, M_STATE), lambda b, c: (b, 0, 0, 0))]
                 + [_const_spec(w.shape) for w in weights],
        out_specs=[seq_spec(D_MODEL),
                   pl.BlockSpec((1, M_HEADS, M_HEADDIM, M_STATE), lambda b, c: (b, 0, 0, 0))],
        out_shape=[jax.ShapeDtypeStruct((bsz, length, D_MODEL), F32),
                   jax.ShapeDtypeStruct((bsz, M_HEADS, M_HEADDIM, M_STATE), F32)],
        scratch_shapes=[pltpu.VMEM((rows + SUBLANES, M_CONV_DIM), F32),
                        pltpu.VMEM((M_CHUNK, M_DT_PAD), F32),
                        pltpu.VMEM((M_CHUNK, M_DT_PAD), F32),
                        pltpu.VMEM((M_CHUNK, M_GROUPS * M_STATE), BF16),
                        pltpu.VMEM((M_CHUNK, M_D_INNER), BF16),
                        pltpu.VMEM((M_CHUNK, M_D_INNER), BF16),
                        pltpu.VMEM((M_HEADS, M_HEADDIM, M_STATE), F32),
                        pltpu.VMEM((rows, M_D_INNER), F32)],
        compiler_params=_params("arbitrary", "arbitrary"),
        name="mamba_ssd",
    )(xbc, z, dt, h, conv0, ssm0, *weights)


_CANDS = [(ra, rb) for ra in range(P_TOPK) for rb in range(P_TOPK) if (ra + 1) * (rb + 1) <= P_TOPK]


def _peer_route_kernel(h_ref, g_ref, wq_ref, keys_ref, keys_lo_ref,
                       xt_ref, r1_ref, e1_ref, m0_ref, w0_ref,
                       q_buf, s_buf, rank0, best_vals, cand, counts, inv_z):
    tm = h_ref.shape[0]
    neg_inf = float("-inf")
    xn = _rms(h_ref[...], g_ref[...])
    xt = xn.T.astype(BF16)
    xt_ref[...] = xt
    q_buf[...] = jnp.dot(wq_ref[...], xt, preferred_element_type=F32)
    key_iota = lax.broadcasted_iota(jnp.int32, (P_NKEYS, tm), 0).astype(F32)

    def half(m, carry):
        head = m // 2
        z = m % 2
        row0 = pl.multiple_of(m * P_NKEYS, P_NKEYS)
        qm = q_buf[pl.ds(row0, P_NKEYS), :]
        q_hi = qm.astype(BF16)
        q_lo = (qm - q_hi.astype(F32)).astype(BF16)
        k_hi = keys_ref[m]
        s = (jnp.dot(k_hi, q_hi, preferred_element_type=F32)
             + (jnp.dot(k_hi, q_lo, preferred_element_type=F32)
                + jnp.dot(keys_lo_ref[m], q_hi, preferred_element_type=F32)))

        s_buf[...] = s

        def next_distinct(j, prev):
            sj = s_buf[...]
            best = jnp.max(jnp.where(sj < prev, sj, neg_inf), axis=0, keepdims=True)
            best_vals[z, j, pl.ds(head, 1), :] = best
            return best

        def extract_ties(j, sc):
            cur, rank = sc
            best = jnp.max(cur, axis=0, keepdims=True)
            first = jnp.min(jnp.where(cur == best, key_iota, float(P_NKEYS)), axis=0, keepdims=True)
            sel = key_iota == first
            best_vals[z, j, pl.ds(head, 1), :] = best
            return jnp.where(sel, neg_inf, cur), jnp.where(sel, jnp.asarray(j, dtype=F32), rank)

        last = lax.fori_loop(0, P_TOPK, next_distinct, jnp.full((1, tm), float("inf"), F32))
        reached = jnp.sum(jnp.where(s >= last, 1.0, 0.0), axis=0, keepdims=True)
        has_ties = jnp.max(jnp.abs(reached - float(P_TOPK))) > 0.0

        def rank_by_count():
            rank = jnp.zeros((P_NKEYS, tm), F32)
            for jj in range(P_TOPK):
                rank = rank + jnp.where(best_vals[z, jj, pl.ds(head, 1), :] > s, 1.0, 0.0)
            return rank

        def rank_with_ties():
            no_rank = jnp.full((P_NKEYS, tm), float(P_NKEYS), F32)
            return lax.fori_loop(0, P_TOPK, extract_ties, (s, no_rank))[1]

        rank = lax.cond(has_ties, rank_with_ties, rank_by_count)
        e = jnp.exp(s - jnp.max(s, axis=0, keepdims=True))

        @pl.when(z == 0)
        def _():
            rank0[head] = rank
            w0_ref[head] = e

        @pl.when(z == 1)
        def _():
            r1_ref[head] = rank.astype(BF16)
            e1_ref[head] = e.astype(BF16)

        return carry

    lax.fori_loop(0, 2 * P_HEADS, half, 0)

    ex0 = [jnp.exp(best_vals[0, ra] - best_vals[0, 0]) for ra in range(P_TOPK)]
    ex1 = [jnp.exp(best_vals[1, rb] - best_vals[1, 0]) for rb in range(P_TOPK)]
    for i, (ra, rb) in enumerate(_CANDS):
        cand[i] = best_vals[0, ra] + best_vals[1, rb]
    counts[...] = jnp.zeros(counts.shape, F32)

    def pick(it, z):
        best = cand[0]
        for i in range(1, len(_CANDS)):
            best = jnp.maximum(best, cand[i])
        found = jnp.zeros((P_HEADS, tm), F32)
        for i, (ra, rb) in enumerate(_CANDS):
            ci = cand[i]
            hit = jnp.where(ci == best, 1.0, 0.0) * (1.0 - found)
            found = found + hit
            cand[i] = jnp.where(hit > 0.0, neg_inf, ci)
            counts[ra] = counts[ra] + hit
            z = z + hit * (ex0[ra] * ex1[rb])
        return z

    inv_z[...] = 1.0 / lax.fori_loop(0, P_TOPK, pick, jnp.zeros((P_HEADS, tm), F32))

    def finish(head, carry):
        rk = rank0[head]
        m0 = jnp.zeros((P_NKEYS, tm), F32)
        for ra in range(P_TOPK):
            m0 = jnp.where(rk == float(ra), counts[ra, pl.ds(head, 1), :], m0)
        m0_ref[head] = m0
        w0_ref[head] = w0_ref[head] * inv_z[pl.ds(head, 1), :]
        return carry

    lax.fori_loop(0, P_HEADS, finish, 0)


def _peer_route(h, g, wq_t, keys, keys_lo, tm):
    t, d = h.shape
    tiles = pl.cdiv(t, tm)
    t_pad = tiles * tm
    tile = pl.BlockSpec((P_HEADS, P_NKEYS, tm), lambda i: (0, 0, i))
    tile_f32 = jax.ShapeDtypeStruct((P_HEADS, P_NKEYS, t_pad), F32)
    tile_bf16 = jax.ShapeDtypeStruct((P_HEADS, P_NKEYS, t_pad), BF16)
    return pl.pallas_call(
        _peer_route_kernel,
        grid=(tiles,),
        in_specs=[pl.BlockSpec((tm, d), lambda i: (i, 0)), _const_spec((1, d)),
                  _const_spec(wq_t.shape), _const_spec(keys.shape), _const_spec(keys_lo.shape)],
        out_specs=[pl.BlockSpec((d, tm), lambda i: (0, i)), tile, tile, tile, tile],
        out_shape=[jax.ShapeDtypeStruct((d, t_pad), BF16), tile_bf16, tile_bf16, tile_f32, tile_f32],
        scratch_shapes=[pltpu.VMEM((P_HEADS * P_QDIM, tm), F32),
                        pltpu.VMEM((P_NKEYS, tm), F32),
                        pltpu.VMEM((P_HEADS, P_NKEYS, tm), F32),
                        pltpu.VMEM((2, P_TOPK, P_HEADS, tm), F32),
                        pltpu.VMEM((len(_CANDS), P_HEADS, tm), F32),
                        pltpu.VMEM((P_TOPK, P_HEADS, tm), F32),
                        pltpu.VMEM((P_HEADS, tm), F32)],
        compiler_params=_params("parallel"),
        name="peer_route",
    )(h, g.reshape(1, d), wq_t, keys, keys_lo)


def _gelu(x):
    return 0.5 * x * (1.0 + lax.erf(x * (1.0 / math.sqrt(2.0))))


def _peer_dense_kernel(xt_ref, u_ref, vt_ref, r1_ref, e1_ref, m0_ref, w0_ref, h_ref, gfin_ref,
                       out_ref, acc, hid, prob, *, final_norm):
    j = pl.program_id(1)
    keys_per_step = P_EXPERT_TILE // P_NKEYS

    @pl.when(j == 0)
    def _():
        acc[...] = jnp.zeros(acc.shape, F32)

    tm = xt_ref.shape[1]
    hid[...] = jnp.dot(u_ref[...], xt_ref[...], preferred_element_type=F32)
    zero = jnp.zeros((), BF16)
    for ii in range(keys_per_step):
        i1 = j * keys_per_step + ii
        rows = slice(ii * P_NKEYS, (ii + 1) * P_NKEYS)
        m0_rows = [m0_ref[head, pl.ds(i1, 1), :].astype(BF16) for head in range(P_HEADS)]
        w0_rows = [w0_ref[head, pl.ds(i1, 1), :].astype(BF16) for head in range(P_HEADS)]
        for lt in range(tm // LANES):
            lanes = slice(lt * LANES, (lt + 1) * LANES)
            gate = None
            for head in range(P_HEADS):
                m0 = m0_rows[head][:, lanes]
                w0 = w0_rows[head][:, lanes]
                term = w0 * jnp.where(r1_ref[head, :, lanes] < m0, e1_ref[head, :, lanes], zero)
                gate = term if gate is None else gate + term
            prob[rows, lanes] = gate * _gelu(hid[rows, lanes]).astype(BF16)
    acc[...] += jnp.dot(vt_ref[...], prob[...], preferred_element_type=F32)

    @pl.when(j == pl.num_programs(1) - 1)
    def _():
        res = h_ref[...] + acc[...].T
        if final_norm:
            res = _rms(res, gfin_ref[...])
        out_ref[...] = res


def _peer_dense(xt, u_bf, vt_bf, r1, e1, m0, w0, h, g_final, tm, final_norm):
    t, d = h.shape
    n_exp = u_bf.shape[0]
    tile = pl.BlockSpec((P_HEADS, P_NKEYS, tm), lambda i, j: (0, 0, i))
    kern = functools.partial(_peer_dense_kernel, final_norm=final_norm)
    return pl.pallas_call(
        kern,
        grid=(pl.cdiv(t, tm), n_exp // P_EXPERT_TILE),
        in_specs=[pl.BlockSpec((d, tm), lambda i, j: (0, i)),
                  pl.BlockSpec((P_EXPERT_TILE, d), lambda i, j: (j, 0)),
                  pl.BlockSpec((d, P_EXPERT_TILE), lambda i, j: (0, j)),
                  tile, tile, tile, tile,
                  pl.BlockSpec((tm, d), lambda i, j: (i, 0)),
                  _const_spec((1, d))],
        out_specs=pl.BlockSpec((tm, d), lambda i, j: (i, 0)),
        out_shape=jax.ShapeDtypeStruct((t, d), F32),
        scratch_shapes=[pltpu.VMEM((d, tm), F32),
                        pltpu.VMEM((P_EXPERT_TILE, tm), F32),
                        pltpu.VMEM((P_EXPERT_TILE, tm), BF16)],
        compiler_params=_params("parallel", "arbitrary"),
        name="peer_dense",
    )(xt, u_bf, vt_bf, r1, e1, m0, w0, h, g_final.reshape(1, d))


def _pad_cols(w, n):
    return jnp.pad(w, ((0, 0), (0, n - w.shape[1])))


def _pad_rows(w, n):
    return jnp.pad(w, ((0, n - w.shape[0]), (0, 0)))


def _prep_rwkv(mix, w_rkv, w0, w1, w2, a0, a1, a2, g1, g2, k_k, k_a, r_k, ln_w, ln_b, w_o):
    d = D_MODEL
    lane = lambda x: jnp.repeat(x.reshape(RW_HEADS, RW_HEAD).T, LANES // RW_HEADS, axis=1)
    return {
        "mix": _pad_rows(mix, SUBLANES),
        "vec": _pad_rows(jnp.stack([w0, a0, k_k, k_a]), SUBLANES),
        "wr": w_rkv[0].astype(BF16), "wk": w_rkv[1].astype(BF16), "wv": w_rkv[2].astype(BF16),
        "w1": _split_bf16(_pad_cols(w1, LORA_PAD)), "w2": _split_bf16(_pad_rows(w2, LORA_PAD)),
        "a1": _split_bf16(_pad_cols(a1, LORA_PAD)), "a2": _split_bf16(_pad_rows(a2, LORA_PAD)),
        "g1": _pad_cols(g1, GATE_LORA_PAD).astype(BF16), "g2": _pad_rows(g2, GATE_LORA_PAD).astype(BF16),
        "lnw": lane(ln_w), "lnb": lane(ln_b), "rk": lane(r_k.reshape(d)),
        "wo": w_o.astype(BF16),
    }


def _prep_mamba(in_proj, conv_w, conv_b, dt_bias, a_log, d_skip, norm_w, out_proj):
    head_of_channel = jnp.arange(M_D_INNER) // M_HEADDIM
    expand = (jnp.arange(M_DT_PAD)[:, None] == head_of_channel[None, :]).astype(F32)
    d_full = jnp.pad(jnp.repeat(d_skip, M_HEADDIM), (0, M_CONV_DIM - M_D_INNER))
    pad_h = lambda x: jnp.pad(x, (0, M_DT_PAD - M_HEADS))
    return {
        "w_in": _pad_cols(in_proj, M_IN_DIM - M_HEADS + M_DT_PAD).astype(BF16),
        "conv_w": _pad_rows(conv_w, SUBLANES),
        "vec": _pad_rows(jnp.stack([conv_b, d_full]), SUBLANES),
        "dvec": _pad_rows(jnp.stack([pad_h(dt_bias), pad_h(-jnp.exp(a_log))]), SUBLANES),
        "expand": expand,
        "norm_w": norm_w.reshape(1, M_D_INNER),
        "w_out": out_proj.astype(BF16),
    }


def _prep_peer(w_q, sub_keys, u_tab, v_tab):
    keys = jnp.transpose(sub_keys, (1, 0, 2, 3)).reshape(2 * P_HEADS, P_NKEYS, P_QDIM // 2)
    keys_hi, keys_lo = _split_bf16(keys)
    return {"wq_t": w_q.T.astype(BF16), "keys": keys_hi, "keys_lo": keys_lo,
            "u": u_tab.astype(BF16), "vt": v_tab.T.astype(BF16)}


SCAN_GROUP = LANES // RW_HEADS
RELAYOUT_STEPS = 128
RELAYOUT_PITCH = RW_HEAD + SUBLANES


def _to_scan_kernel(x_ref, o_ref, stage):
    for b in range(SCAN_GROUP):
        for hp in range(RW_HEADS // 2):
            t = x_ref[b, :, hp * LANES:(hp + 1) * LANES].T
            for hl in range(2):
                lane = (2 * hp + hl) * SCAN_GROUP + b
                stage[lane * RELAYOUT_PITCH:lane * RELAYOUT_PITCH + RW_HEAD, :] = t[hl * RW_HEAD:(hl + 1) * RW_HEAD, :]
    for k in range(RW_HEAD):
        by_lane = stage[pl.ds(k, LANES, stride=RELAYOUT_PITCH), :]
        o_ref[pl.ds(k, RELAYOUT_STEPS, stride=RW_HEAD), :] = by_lane.T


def _to_scan_pallas(x, length):
    x3 = x.reshape(SCAN_GROUP, length, D_MODEL)
    out = pl.pallas_call(
        _to_scan_kernel,
        grid=(pl.cdiv(length, RELAYOUT_STEPS),),
        in_specs=[pl.BlockSpec((SCAN_GROUP, RELAYOUT_STEPS, D_MODEL), lambda i: (0, i, 0))],
        out_specs=pl.BlockSpec((RELAYOUT_STEPS * RW_HEAD, LANES), lambda i: (i, 0)),
        out_shape=jax.ShapeDtypeStruct((length * RW_HEAD, LANES), F32),
        scratch_shapes=[pltpu.VMEM((LANES * RELAYOUT_PITCH, LANES), F32)],
        compiler_params=_params("parallel"),
        name="to_scan_layout",
    )(x3)
    return out.reshape(length, RW_HEAD, LANES)


def _to_scan(x, bsz, length):
    if bsz == SCAN_GROUP and length >= RELAYOUT_STEPS:
        return _to_scan_pallas(x, length)
    x = x.reshape(bsz // SCAN_GROUP, SCAN_GROUP, length, RW_HEADS, RW_HEAD)
    return jnp.transpose(x, (2, 4, 0, 3, 1)).reshape(length, RW_HEAD, bsz * RW_HEADS)


def _from_scan(x, bsz, length):
    x = x.reshape(length, RW_HEAD, bsz // SCAN_GROUP, RW_HEADS, SCAN_GROUP)
    return jnp.transpose(x, (2, 4, 0, 3, 1)).reshape(bsz * length, D_MODEL)


def _state_to_scan(wkv, bsz):
    x = wkv.reshape(bsz // SCAN_GROUP, SCAN_GROUP, RW_HEADS, RW_HEAD, RW_HEAD)
    return jnp.transpose(x, (3, 4, 0, 2, 1)).reshape(RW_HEAD, RW_HEAD, bsz * RW_HEADS)


def _state_from_scan(s, bsz):
    x = s.reshape(RW_HEAD, RW_HEAD, bsz // SCAN_GROUP, RW_HEADS, SCAN_GROUP)
    return jnp.transpose(x, (2, 4, 3, 0, 1)).reshape(bsz, RW_HEADS, RW_HEAD, RW_HEAD)


def _rwkv_layer(h, g_mix, shift0, wkv0, p, bsz, length, tm, tt):
    xn = _norm(h, g_mix, tm)
    xn3 = xn.reshape(bsz, length, D_MODEL)
    prev = jnp.concatenate([shift0[:, None, :], xn3[:, :-1]], axis=1).reshape(bsz * length, D_MODEL)
    r, w, k, v, kk, a, g = _rwkv_proj(xn, prev, p, tm)
    seqs = [_to_scan(x, bsz, length) for x in (r, w, k, v, kk, a)]
    o, s_fin = _rwkv_scan(seqs, _state_to_scan(wkv0, bsz), p["lnw"], p["lnb"], p["rk"], tt)
    o = _from_scan(o, bsz, length)
    return _rwkv_out(o, g, h, p["wo"], tm), xn3[:, -1], _state_from_scan(s_fin, bsz)


def _mamba_layer(h, g_mix, conv0, ssm0, p, bsz, length, tm, rows):
    z, xbc, dt = _mamba_in(h, g_mix, p["w_in"], tm)
    xbc3 = xbc.reshape(bsz, length, M_CONV_DIM)
    tail = M_CONV - 1
    conv_new = xbc3[:, length - tail:] if length >= tail else jnp.concatenate([conv0, xbc3], axis=1)[:, length:]
    pad_t = (-length) % SUBLANES if length < rows else 0

    def seq(x, c):
        x = x.reshape(bsz, length, c)
        return jnp.pad(x, ((0, 0), (0, pad_t), (0, 0))) if pad_t else x

    conv0_p = jnp.pad(conv0, ((0, 0), (SUBLANES - (M_CONV - 1), 0), (0, 0)))
    out, ssm = _mamba_ssd(seq(xbc, M_CONV_DIM), seq(z, M_D_INNER), seq(dt, M_DT_PAD), seq(h, D_MODEL),
                          conv0_p, ssm0, p, min(rows, length + pad_t), length)
    if pad_t:
        out = out[:, :length]
    return out.reshape(bsz * length, D_MODEL), conv_new, ssm


def _peer_layer(h, g_ffn, p, tm, g_final=None):
    xt, r1, e1, m0, w0 = _peer_route(h, g_ffn, p["wq_t"], p["keys"], p["keys_lo"], tm)
    final = g_final is not None
    return _peer_dense(xt, p["u"], p["vt"], r1, e1, m0, w0, h, g_final if final else g_ffn, tm, final)


def _trunk(h, shift0, wkv0, conv0, ssm0, norm_mix, norm_ffn, norm_final, rwkv_p, mamba_p, peer_p,
           bsz, length, tm, tt, peer_tm, rows):
    h, shift, wkv = _rwkv_layer(h, norm_mix[0], shift0, wkv0, rwkv_p, bsz, length, tm, tt)
    h = _peer_layer(h, norm_ffn[0], peer_p[0], peer_tm)
    h, conv, ssm = _mamba_layer(h, norm_mix[1], conv0, ssm0, mamba_p, bsz, length, tm, rows)
    y = _peer_layer(h, norm_ffn[1], peer_p[1], peer_tm, g_final=norm_final)
    return y.reshape(bsz, length, D_MODEL), shift[None], wkv[None], conv[None], ssm[None]


def kernel(x_prompt, x_sample, state_rwkv_shift, state_rwkv_wkv, state_mamba_conv, state_mamba_ssm, meta_tokens, norm_mix, norm_ffn, norm_final, rwkv_mix, rwkv_w_rkv, rwkv_w0, rwkv_w1, rwkv_w2, rwkv_a0, rwkv_a1, rwkv_a2, rwkv_g1, rwkv_g2, rwkv_k_k, rwkv_k_a, rwkv_r_k, rwkv_ln_w, rwkv_ln_b, rwkv_w_o, mamba_in_proj, mamba_conv_w, mamba_conv_b, mamba_dt_bias, mamba_a_log, mamba_d, mamba_norm_w, mamba_out_proj, peer_w_q, peer_sub_keys, peer_u, peer_v):
    rwkv_p = _prep_rwkv(rwkv_mix[0], rwkv_w_rkv[0], rwkv_w0[0], rwkv_w1[0], rwkv_w2[0], rwkv_a0[0],
                        rwkv_a1[0], rwkv_a2[0], rwkv_g1[0], rwkv_g2[0], rwkv_k_k[0], rwkv_k_a[0],
                        rwkv_r_k[0], rwkv_ln_w[0], rwkv_ln_b[0], rwkv_w_o[0])
    mamba_p = _prep_mamba(mamba_in_proj[0], mamba_conv_w[0], mamba_conv_b[0], mamba_dt_bias[0],
                          mamba_a_log[0], mamba_d[0], mamba_norm_w[0], mamba_out_proj[0])
    peer_p = [_prep_peer(peer_w_q[i], peer_sub_keys[i], peer_u[i], peer_v[i]) for i in range(2)]

    bp, lp = x_prompt.shape[0], x_prompt.shape[1] + N_META
    meta = jnp.broadcast_to(meta_tokens[None], (bp, N_META, D_MODEL))
    hp = jnp.concatenate([meta, x_prompt], axis=1).reshape(bp * lp, D_MODEL)
    zeros = lambda *s: jnp.zeros(s, F32)
    yp, p_shift, p_wkv, p_conv, p_ssm = _trunk(
        hp, zeros(bp, D_MODEL), zeros(bp, RW_HEADS, RW_HEAD, RW_HEAD), zeros(bp, M_CONV - 1, M_CONV_DIM),
        zeros(bp, M_HEADS, M_HEADDIM, M_STATE), norm_mix, norm_ffn, norm_final, rwkv_p, mamba_p, peer_p,
        bp, lp, tm=344, tt=48, peer_tm=512, rows=M_CHUNK)

    bs, ls = x_sample.shape[0], x_sample.shape[1]
    ys, s_shift, s_wkv, s_conv, s_ssm = _trunk(
        x_sample.reshape(bs * ls, D_MODEL), state_rwkv_shift[0], state_rwkv_wkv[0], state_mamba_conv[0],
        state_mamba_ssm[0], norm_mix, norm_ffn, norm_final, rwkv_p, mamba_p, peer_p,
        bs, ls, tm=256, tt=ls, peer_tm=512, rows=M_CHUNK)
    return (yp[:, N_META:], ys, p_shift, p_wkv, p_conv, p_ssm, s_shift, s_wkv, s_conv, s_ssm)
```

```python
import functools
import math

import jax
import jax.numpy as jnp
from jax import lax
from jax.experimental import pallas as pl
from jax.experimental.pallas import tpu as pltpu

F32 = jnp.float32
BF16 = jnp.bfloat16
HIGHEST = lax.Precision.HIGHEST

D_MODEL = 1024
N_META = 16
NORM_EPS = 1e-5
RW_HEAD = 64
RW_HEADS = D_MODEL // RW_HEAD
RW_LN_EPS = 64e-5
LORA_PAD = 128
GATE_LORA_PAD = 256
M_D_INNER = 2048
M_HEADDIM = 64
M_HEADS = M_D_INNER // M_HEADDIM
M_GROUPS = 4
M_HPG = M_HEADS // M_GROUPS
M_STATE = 128
M_CONV = 4
M_CONV_DIM = M_D_INNER + 2 * M_GROUPS * M_STATE
M_IN_DIM = 2 * M_D_INNER + 2 * M_GROUPS * M_STATE + M_HEADS
M_DT_PAD = 128
M_CHUNK = 128
P_HEADS = 8
P_NKEYS = 128
P_QDIM = 256
P_TOPK = 16
P_EXPERT_TILE = 1024
P_EXPERT_SUB = 256

LANES = 128
SUBLANES = 8
VMEM_LIMIT_BYTES = 56 * 1024 * 1024


def _params(*semantics):
    return pltpu.CompilerParams(dimension_semantics=semantics, vmem_limit_bytes=VMEM_LIMIT_BYTES)


def _const_spec(shape):
    zeros = (0,) * len(shape)
    return pl.BlockSpec(shape, lambda *_: zeros)


def _rms(x, g):
    return x * lax.rsqrt(jnp.mean(x * x, axis=-1, keepdims=True) + NORM_EPS) * g


def _softplus(x):
    return jnp.maximum(x, 0.0) + jnp.log1p(jnp.exp(-jnp.abs(x)))


def _bdot(a, b):
    return jnp.dot(a.astype(BF16), b.astype(BF16), preferred_element_type=F32)


def _hdot(a, b):
    return jnp.dot(a, b, precision=HIGHEST, preferred_element_type=F32)


def _split_bf16(w):
    hi = w.astype(BF16)
    return hi, (w - hi.astype(F32)).astype(BF16)


def _dot3(a, b_hi, b_lo):
    a_hi, a_lo = _split_bf16(a)
    return (jnp.dot(a_hi, b_hi, preferred_element_type=F32)
            + (jnp.dot(a_hi, b_lo, preferred_element_type=F32) + jnp.dot(a_lo, b_hi, preferred_element_type=F32)))


def _norm_kernel(h_ref, g_ref, o_ref):
    o_ref[...] = _rms(h_ref[...], g_ref[...])


def _norm(h, g, tm):
    t, d = h.shape
    return pl.pallas_call(
        _norm_kernel,
        grid=(t // tm,),
        in_specs=[pl.BlockSpec((tm, d), lambda i: (i, 0)), _const_spec((1, d))],
        out_specs=pl.BlockSpec((tm, d), lambda i: (i, 0)),
        out_shape=jax.ShapeDtypeStruct((t, d), F32),
        compiler_params=_params("parallel"),
        name="rms_norm",
    )(h, g.reshape(1, d))


def _rwkv_proj_kernel(xn_ref, prev_ref, mix_ref, vec_ref, wr_ref, wk_ref, wv_ref,
                      w1h_ref, w1l_ref, w2h_ref, w2l_ref, a1h_ref, a1l_ref, a2h_ref, a2l_ref, g1_ref, g2_ref,
                      r_ref, w_ref, k_ref, v_ref, kk_ref, a_ref, g_ref):
    xn = xn_ref[...]
    dx = prev_ref[...] - xn
    xr, xw, xk, xv, xa, xg = [xn + dx * mix_ref[j:j + 1, :] for j in range(6)]
    w0, a0, k_k, k_a = [vec_ref[j:j + 1, :] for j in range(4)]
    r_ref[...] = _bdot(xr, wr_ref[...])
    k = _bdot(xk, wk_ref[...])
    v_ref[...] = _bdot(xv, wv_ref[...])
    w_lora = _dot3(jnp.tanh(_dot3(xw, w1h_ref[...], w1l_ref[...])), w2h_ref[...], w2l_ref[...])
    w_log = -_softplus(-(w0 + w_lora)) - 0.5
    w_ref[...] = jnp.exp(-jnp.exp(w_log))
    a = jax.nn.sigmoid(a0 + _dot3(_dot3(xa, a1h_ref[...], a1l_ref[...]), a2h_ref[...], a2l_ref[...]))
    a_ref[...] = a
    g_ref[...] = _bdot(jax.nn.sigmoid(_bdot(xg, g1_ref[...])), g2_ref[...])
    kk_ref[...] = k * k_k
    k_ref[...] = k * (1.0 + (a - 1.0) * k_a)


def _rwkv_proj(xn, prev, p, tm):
    t, d = xn.shape
    tok = pl.BlockSpec((tm, d), lambda i: (i, 0))
    weights = [p["mix"], p["vec"], p["wr"], p["wk"], p["wv"], *p["w1"], *p["w2"], *p["a1"], *p["a2"],
               p["g1"], p["g2"]]
    return pl.pallas_call(
        _rwkv_proj_kernel,
        grid=(t // tm,),
        in_specs=[tok, tok] + [_const_spec(w.shape) for w in weights],
        out_specs=[tok] * 7,
        out_shape=[jax.ShapeDtypeStruct((t, d), F32)] * 7,
        compiler_params=_params("parallel"),
        name="rwkv_proj",
    )(xn, prev, *weights)


def _rwkv_scan_kernel(r_ref, w_ref, k_ref, v_ref, kk_ref, a_ref, s0_ref, lnw_ref, lnb_ref, rk_ref,
                      o_ref, sfin_ref, state, o_rows):
    tb = pl.program_id(1)
    steps = r_ref.shape[0]

    @pl.when(tb == 0)
    def _():
        state[...] = s0_ref[...]

    def step(t, carry):
        r = r_ref[t]
        w = w_ref[t]
        k = k_ref[t]
        vv = v_ref[t]
        kku = kk_ref[t]
        norm = jnp.sqrt(jnp.sum(kku * kku, axis=0, keepdims=True))
        kk = kku / jnp.maximum(norm, 1e-12)
        alpha = -kk
        beta = kk * a_ref[t]

        def row(vi, c):
            s_v = state[vi]
            u = jnp.sum(s_v * alpha, axis=0, keepdims=True)
            s_new = s_v * w + u * beta + v_ref[t, pl.ds(vi, 1), :] * k
            state[vi] = s_new
            o_rows[pl.ds(vi, 1), :] = jnp.sum(s_new * r, axis=0, keepdims=True)
            return c

        lax.fori_loop(0, RW_HEAD, row, 0, unroll=32)
        o = o_rows[...]
        mu = jnp.mean(o, axis=0, keepdims=True)
        var = jnp.mean(jnp.square(o - mu), axis=0, keepdims=True)
        on = (o - mu) * lax.rsqrt(var + RW_LN_EPS)
        bonus = jnp.sum(r * k * rk_ref[...], axis=0, keepdims=True) * vv
        o_ref[t] = on * lnw_ref[...] + lnb_ref[...] + bonus
        return carry

    lax.fori_loop(0, steps, step, 0)

    @pl.when(tb == pl.num_programs(1) - 1)
    def _():
        sfin_ref[...] = state[...]


def _rwkv_scan(seqs, s0, lnw, lnb, rk, tt):
    length, n, probs = seqs[0].shape
    groups = probs // LANES
    seq_spec = pl.BlockSpec((tt, n, LANES), lambda g, i: (i, 0, g))
    st_spec = pl.BlockSpec((n, n, LANES), lambda g, i: (0, 0, g))
    lane_spec = pl.BlockSpec((n, LANES), lambda g, i: (0, 0))
    return pl.pallas_call(
        _rwkv_scan_kernel,
        grid=(groups, length // tt),
        in_specs=[seq_spec] * 6 + [st_spec] + [lane_spec] * 3,
        out_specs=[seq_spec, st_spec],
        out_shape=[jax.ShapeDtypeStruct((length, n, probs), F32),
                   jax.ShapeDtypeStruct((n, n, probs), F32)],
        scratch_shapes=[pltpu.VMEM((n, n, LANES), F32), pltpu.VMEM((n, LANES), F32)],
        compiler_params=_params("arbitrary", "arbitrary"),
        name="rwkv_scan",
    )(*seqs, s0, lnw, lnb, rk)


def _rwkv_out_kernel(o_ref, g_ref, h_ref, wo_ref, out_ref):
    out_ref[...] = h_ref[...] + _bdot(o_ref[...] * g_ref[...], wo_ref[...])


def _rwkv_out(o, g, h, wo, tm):
    t, d = h.shape
    tok = pl.BlockSpec((tm, d), lambda i: (i, 0))
    return pl.pallas_call(
        _rwkv_out_kernel,
        grid=(t // tm,),
        in_specs=[tok, tok, tok, _const_spec(wo.shape)],
        out_specs=tok,
        out_shape=jax.ShapeDtypeStruct((t, d), F32),
        compiler_params=_params("parallel"),
        name="rwkv_out",
    )(o, g, h, wo)


def _mamba_in_kernel(h_ref, g_ref, w_ref, z_ref, xbc_ref, dt_ref):
    xn = _rms(h_ref[...], g_ref[...]).astype(BF16)
    z_ref[...] = jnp.dot(xn, w_ref[:, :M_D_INNER], preferred_element_type=F32)
    xbc_ref[...] = jnp.dot(xn, w_ref[:, M_D_INNER:M_D_INNER + M_CONV_DIM], preferred_element_type=F32)
    dt_ref[...] = jnp.dot(xn, w_ref[:, M_D_INNER + M_CONV_DIM:], preferred_element_type=F32)


def _mamba_in(h, g, w_in, tm):
    t, d = h.shape
    return pl.pallas_call(
        _mamba_in_kernel,
        grid=(t // tm,),
        in_specs=[pl.BlockSpec((tm, d), lambda i: (i, 0)), _const_spec((1, d)), _const_spec(w_in.shape)],
        out_specs=[pl.BlockSpec((tm, M_D_INNER), lambda i: (i, 0)),
                   pl.BlockSpec((tm, M_CONV_DIM), lambda i: (i, 0)),
                   pl.BlockSpec((tm, M_DT_PAD), lambda i: (i, 0))],
        out_shape=[jax.ShapeDtypeStruct((t, M_D_INNER), F32),
                   jax.ShapeDtypeStruct((t, M_CONV_DIM), F32),
                   jax.ShapeDtypeStruct((t, M_DT_PAD), F32)],
        compiler_params=_params("parallel"),
        name="mamba_in_proj",
    )(h, g.reshape(1, d), w_in)


def _mamba_ssd_kernel(xbc_ref, z_ref, dt_ref, h_ref, conv0_ref, ssm0_ref, convw_ref, vec_ref,
                      dvec_ref, expand_ref, normw_ref, wout_ref,
                      out_ref, ssm_ref,
                      conv_buf, src_adt, src_acs, src_b, src_xdt, src_xdt_st, state, y_buf, *, seq_len, rows):
    b = pl.program_id(0)
    c = pl.program_id(1)
    q = M_CHUNK
    halo = SUBLANES
    d_bc = M_GROUPS * M_STATE

    @pl.when(c == 0)
    def _():
        state[...] = ssm0_ref[0]
        conv_buf[0:halo, :] = conv0_ref[0]

    @pl.when(c > 0)
    def _():
        conv_buf[0:halo, :] = conv_buf[rows:rows + halo, :]

    if rows < q:
        @pl.when((b == 0) & (c == 0))
        def _():
            src_adt[...] = jnp.zeros(src_adt.shape, F32)
            src_acs[...] = jnp.zeros(src_acs.shape, F32)
            src_b[...] = jnp.zeros(src_b.shape, BF16)
            src_xdt[...] = jnp.zeros(src_xdt.shape, BF16)
            src_xdt_st[...] = jnp.zeros(src_xdt_st.shape, BF16)

    conv_buf[halo:halo + rows, :] = xbc_ref[0]
    valid = (lax.broadcasted_iota(jnp.int32, (rows, 1), 0) + c * rows) < seq_len
    conv = vec_ref[0:1, :]
    for j in range(M_CONV):
        conv = conv + conv_buf[halo - (M_CONV - 1) + j:halo - (M_CONV - 1) + j + rows, :] * convw_ref[j:j + 1, :]
    act = jnp.where(valid, jax.nn.silu(conv), 0.0)
    xs = act[:, :M_D_INNER]
    dt = jnp.where(valid, _softplus(dt_ref[0] + dvec_ref[0:1, :]), 0.0)
    src_adt[0:rows, :] = dt * dvec_ref[1:2, :]
    li = lax.broadcasted_iota(jnp.int32, (rows, q), 0)
    si = lax.broadcasted_iota(jnp.int32, (rows, q), 1)
    causal = li >= si
    acs = _hdot(causal.astype(F32), src_adt[...])
    src_acs[0:rows, :] = acs
    acs_t = src_acs[...].T
    expand = expand_ref[...]
    dt_full = _hdot(dt, expand)
    acs_full = _hdot(acs, expand)
    last = acs[rows - 1:rows, :]
    last_full = _hdot(last, expand)
    xdt = xs * dt_full
    src_xdt[0:rows, :] = xdt.astype(BF16)
    src_xdt_st[0:rows, :] = (xdt * jnp.exp(last_full - acs_full)).astype(BF16)
    src_b[0:rows, :] = act[:, M_D_INNER:M_D_INNER + d_bc].astype(BF16)
    eacs_full = jnp.exp(acs_full)
    state_decay = jnp.exp(last)
    width = M_HPG * M_HEADDIM

    for g in range(M_GROUPS):
        cols = slice(g * width, (g + 1) * width)
        b_g = src_b[:, g * M_STATE:(g + 1) * M_STATE]
        c_g = act[:, M_D_INNER + d_bc + g * M_STATE:M_D_INNER + d_bc + (g + 1) * M_STATE].astype(BF16)
        cb = lax.dot_general(c_g, b_g, (((1,), (1,)), ((), ())), preferred_element_type=F32)
        h_g = state[g * M_HPG:(g + 1) * M_HPG].reshape(width, M_STATE)
        y_off = lax.dot_general(c_g, h_g.astype(BF16), (((1,), (1,)), ((), ())),
                                preferred_element_type=F32)
        upd = lax.dot_general(src_xdt_st[:, cols], b_g, (((0,), (0,)), ((), ())),
                              preferred_element_type=F32)
        ys = []
        for r in range(g * M_HPG, (g + 1) * M_HPG):
            lo = r * M_HEADDIM
            decay = jnp.where(causal, jnp.exp(acs[:, r:r + 1] - acs_t[r:r + 1, :]), 0.0)
            ys.append(jnp.dot((cb * decay).astype(BF16), src_xdt[:, lo:lo + M_HEADDIM],
                              preferred_element_type=F32))
            sub = slice((r - g * M_HPG) * M_HEADDIM, (r - g * M_HPG + 1) * M_HEADDIM)
            state[r] = state[r] * state_decay[:, r:r + 1] + upd[sub, :]
        y_buf[:, cols] = jnp.concatenate(ys, axis=1) + y_off * eacs_full[:, cols]

    y = y_buf[...] + vec_ref[1:2, :M_D_INNER] * xs
    yg = y * jax.nn.silu(z_ref[0])
    parts = []
    for g in range(M_GROUPS):
        part = yg[:, g * width:(g + 1) * width]
        parts.append(part * lax.rsqrt(jnp.mean(part * part, axis=-1, keepdims=True) + NORM_EPS))
    yn = jnp.concatenate(parts, axis=1) * normw_ref[...]
    out_ref[0] = h_ref[0] + _bdot(yn, wout_ref[...])

    @pl.when(c == pl.num_programs(1) - 1)
    def _():
        ssm_ref[0] = state[...]


def _mamba_ssd(xbc, z, dt, h, conv0, ssm0, p, rows, seq_len):
    bsz, length, _ = xbc.shape
    chunks = pl.cdiv(length, rows)
    weights = [p["conv_w"], p["vec"], p["dvec"], p["expand"], p["norm_w"], p["w_out"]]

    def seq_spec(width):
        return pl.BlockSpec((1, rows, width), lambda b, c: (b, c, 0))

    kern = functools.partial(_mamba_ssd_kernel, seq_len=seq_len, rows=rows)
    return pl.pallas_call(
        kern,
        grid=(bsz, chunks),
        in_specs=[seq_spec(M_CONV_DIM), seq_spec(M_D_INNER), seq_spec(M_DT_PAD), seq_spec(D_MODEL),
                  pl.BlockSpec((1, SUBLANES, M_CONV_DIM), lambda b, c: (b, 0, 0)),
                  pl.BlockSpec((1, M_HEADS, M_HEADDIM, M_STATE), lambda b, c: (b, 0, 0, 0))]
                 + [_const_spec(w.shape) for w in weights],
        out_specs=[seq_spec(D_MODEL),
                   pl.BlockSpec((1, M_HEADS, M_HEADDIM, M_STATE), lambda b, c: (b, 0, 0, 0))],
        out_shape=[jax.ShapeDtypeStruct((bsz, length, D_MODEL), F32),
                   jax.ShapeDtypeStruct((bsz, M_HEADS, M_HEADDIM, M_STATE), F32)],
        scratch_shapes=[pltpu.VMEM((rows + SUBLANES, M_CONV_DIM), F32),
                        pltpu.VMEM((M_CHUNK, M_DT_PAD), F32),
                        pltpu.VMEM((M_CHUNK, M_DT_PAD), F32),
                        pltpu.VMEM((M_CHUNK, M_GROUPS * M_STATE), BF16),
                        pltpu.VMEM((M_CHUNK, M_D_INNER), BF16),
                        pltpu.VMEM((M_CHUNK, M_D_INNER), BF16),
                        pltpu.VMEM((M_HEADS, M_HEADDIM, M_STATE), F32),
                        pltpu.VMEM((rows, M_D_INNER), F32)],
        compiler_params=_params("arbitrary", "arbitrary"),
        name="mamba_ssd",
    )(xbc, z, dt, h, conv0, ssm0, *weights)


_CANDS = [(ra, rb) for ra in range(P_TOPK) for rb in range(P_TOPK) if (ra + 1) * (rb + 1) <= P_TOPK]


def _peer_route_kernel(h_ref, g_ref, wq_ref, keys_ref, keys_lo_ref,
                       xt_ref, r1_ref, e1_ref, m0_ref, w0_ref,
                       q_buf, s_buf, code0, code_book, best_vals, cand, counts, inv_z):
    tm = h_ref.shape[0]
    neg_inf = float("-inf")
    xn = _rms(h_ref[...], g_ref[...])
    xt = xn.T.astype(BF16)
    xt_ref[...] = xt
    q_buf[...] = jnp.dot(wq_ref[...], xt, preferred_element_type=F32)
    key_iota = lax.broadcasted_iota(jnp.int32, (P_NKEYS, tm), 0).astype(F32)

    def half(m, carry):
        head = m // 2
        z = m % 2
        row0 = pl.multiple_of(m * P_NKEYS, P_NKEYS)
        qm = q_buf[pl.ds(row0, P_NKEYS), :]
        q_hi = qm.astype(BF16)
        q_lo = (qm - q_hi.astype(F32)).astype(BF16)
        k_hi = keys_ref[m]
        s = (jnp.dot(k_hi, q_hi, preferred_element_type=F32)
             + (jnp.dot(k_hi, q_lo, preferred_element_type=F32)
                + jnp.dot(keys_lo_ref[m], q_hi, preferred_element_type=F32)))

        s_buf[...] = s

        def next_distinct(j, prev):
            sj = s_buf[...]
            best = jnp.max(jnp.where(sj < prev, sj, neg_inf), axis=0, keepdims=True)
            best_vals[z, j, pl.ds(head, 1), :] = best
            return best

        def extract_ties(j, sc):
            cur, rank = sc
            best = jnp.max(cur, axis=0, keepdims=True)
            first = jnp.min(jnp.where(cur == best, key_iota, float(P_NKEYS)), axis=0, keepdims=True)
            sel = key_iota == first
            best_vals[z, j, pl.ds(head, 1), :] = best
            return jnp.where(sel, neg_inf, cur), jnp.where(sel, jnp.asarray(j, dtype=F32), rank)

        last = lax.fori_loop(0, P_TOPK, next_distinct, jnp.full((1, tm), float("inf"), F32))
        reached = jnp.sum(jnp.where(s >= last, 1.0, 0.0), axis=0, keepdims=True)
        has_ties = jnp.max(jnp.abs(reached - float(P_TOPK))) > 0.0

        def rank_by_count():
            rank = jnp.zeros((P_NKEYS, tm), F32)
            for jj in range(P_TOPK):
                rank = rank + jnp.where(best_vals[z, jj, pl.ds(head, 1), :] > s, 1.0, 0.0)
            return rank

        def rank_with_ties():
            no_rank = jnp.full((P_NKEYS, tm), float(P_NKEYS), F32)
            return lax.fori_loop(0, P_TOPK, extract_ties, (s, no_rank))[1]

        e = jnp.exp(s - jnp.max(s, axis=0, keepdims=True))

        @pl.when(z == 0)
        def _():
            w0_ref[head] = e

            @pl.when(has_ties)
            def _():
                code0[head] = rank_with_ties()
                for ra in range(P_TOPK):
                    code_book[ra, pl.ds(head, 1), :] = jnp.full((1, tm), float(ra), F32)

            @pl.when(jnp.logical_not(has_ties))
            def _():
                code0[head] = s
                for ra in range(P_TOPK):
                    code_book[ra, pl.ds(head, 1), :] = best_vals[0, ra, pl.ds(head, 1), :]

        @pl.when(z == 1)
        def _():
            rank = lax.cond(has_ties, rank_with_ties, rank_by_count)
            for lt in range(tm // LANES):
                r1_ref[head, lt] = rank[:, lt * LANES:(lt + 1) * LANES].astype(BF16)
                e1_ref[head, lt] = e[:, lt * LANES:(lt + 1) * LANES].astype(BF16)

        return carry

    lax.fori_loop(0, 2 * P_HEADS, half, 0)

    ex0 = [jnp.exp(best_vals[0, ra] - best_vals[0, 0]) for ra in range(P_TOPK)]
    ex1 = [jnp.exp(best_vals[1, rb] - best_vals[1, 0]) for rb in range(P_TOPK)]
    for i, (ra, rb) in enumerate(_CANDS):
        cand[i] = best_vals[0, ra] + best_vals[1, rb]

    def next_sum(it, prev):
        best = None
        for i in range(len(_CANDS)):
            ci = cand[i]
            below = jnp.where(ci < prev, ci, neg_inf)
            best = below if best is None else jnp.maximum(best, below)
        return best

    last_sum = lax.fori_loop(0, P_TOPK, next_sum, jnp.full((P_HEADS, tm), float("inf"), F32))
    hits = [jnp.where(cand[i] >= last_sum, 1.0, 0.0) for i in range(len(_CANDS))]
    n_hits = functools.reduce(lambda x, y: x + y, hits)
    sums_tie = jnp.max(jnp.abs(n_hits - float(P_TOPK))) > 0.0

    def select_distinct():
        z = jnp.zeros((P_HEADS, tm), F32)
        per_rank = [None] * P_TOPK
        for i, (ra, rb) in enumerate(_CANDS):
            per_rank[ra] = hits[i] if per_rank[ra] is None else per_rank[ra] + hits[i]
            z = z + hits[i] * (ex0[ra] * ex1[rb])
        for ra in range(P_TOPK):
            counts[ra] = per_rank[ra]
        return z

    def select_with_ties():
        counts[...] = jnp.zeros(counts.shape, F32)

        def pick(it, z):
            best = cand[0]
            for i in range(1, len(_CANDS)):
                best = jnp.maximum(best, cand[i])
            found = jnp.zeros((P_HEADS, tm), F32)
            for i, (ra, rb) in enumerate(_CANDS):
                ci = cand[i]
                hit = jnp.where(ci == best, 1.0, 0.0) * (1.0 - found)
                found = found + hit
                cand[i] = jnp.where(hit > 0.0, neg_inf, ci)
                counts[ra] = counts[ra] + hit
                z = z + hit * (ex0[ra] * ex1[rb])
            return z

        return lax.fori_loop(0, P_TOPK, pick, jnp.zeros((P_HEADS, tm), F32))

    inv_z[...] = 1.0 / lax.cond(sums_tie, select_with_ties, select_distinct)

    def finish(head, carry):
        code = code0[head]
        m0 = jnp.zeros((P_NKEYS, tm), F32)
        for ra in range(P_TOPK):
            m0 = jnp.where(code == code_book[ra, pl.ds(head, 1), :], counts[ra, pl.ds(head, 1), :], m0)
        m0_ref[head] = m0
        w0_ref[head] = w0_ref[head] * inv_z[pl.ds(head, 1), :]
        return carry

    lax.fori_loop(0, P_HEADS, finish, 0)


def _peer_route(h, g, wq_t, keys, keys_lo, tm):
    t, d = h.shape
    tiles = pl.cdiv(t, tm)
    t_pad = tiles * tm
    tile = pl.BlockSpec((P_HEADS, P_NKEYS, tm), lambda i: (0, 0, i))
    tile_f32 = jax.ShapeDtypeStruct((P_HEADS, P_NKEYS, t_pad), F32)
    tile_bf16 = jax.ShapeDtypeStruct((P_HEADS, t_pad // LANES, P_NKEYS, LANES), BF16)
    lane_tiles = pl.BlockSpec((P_HEADS, tm // LANES, P_NKEYS, LANES), lambda i: (0, i, 0, 0))
    return pl.pallas_call(
        _peer_route_kernel,
        grid=(tiles,),
        in_specs=[pl.BlockSpec((tm, d), lambda i: (i, 0)), _const_spec((1, d)),
                  _const_spec(wq_t.shape), _const_spec(keys.shape), _const_spec(keys_lo.shape)],
        out_specs=[pl.BlockSpec((d, tm), lambda i: (0, i)), lane_tiles, lane_tiles, tile, tile],
        out_shape=[jax.ShapeDtypeStruct((d, t_pad), BF16), tile_bf16, tile_bf16, tile_f32, tile_f32],
        scratch_shapes=[pltpu.VMEM((P_HEADS * P_QDIM, tm), F32),
                        pltpu.VMEM((P_NKEYS, tm), F32),
                        pltpu.VMEM((P_HEADS, P_NKEYS, tm), F32),
                        pltpu.VMEM((P_TOPK, P_HEADS, tm), F32),
                        pltpu.VMEM((2, P_TOPK, P_HEADS, tm), F32),
                        pltpu.VMEM((len(_CANDS), P_HEADS, tm), F32),
                        pltpu.VMEM((P_TOPK, P_HEADS, tm), F32),
                        pltpu.VMEM((P_HEADS, tm), F32)],
        compiler_params=_params("parallel"),
        name="peer_route",
    )(h, g.reshape(1, d), wq_t, keys, keys_lo)


def _gelu(x):
    return 0.5 * x * (1.0 + lax.erf(x * (1.0 / math.sqrt(2.0))))


def _peer_dense_kernel(xt_ref, u_ref, vt_ref, r1_ref, e1_ref, m0_ref, w0_ref, h_ref, gfin_ref,
                       out_ref, acc, hid, prob, *, final_norm):
    j = pl.program_id(1)
    keys_per_step = P_EXPERT_TILE // P_NKEYS

    @pl.when(j == 0)
    def _():
        acc[...] = jnp.zeros(acc.shape, F32)

    tm = xt_ref.shape[1]
    hid[...] = jnp.dot(u_ref[...], xt_ref[...], preferred_element_type=F32)
    zero = jnp.zeros((), BF16)
    for ii in range(keys_per_step):
        i1 = j * keys_per_step + ii
        rows = slice(ii * P_NKEYS, (ii + 1) * P_NKEYS)
        m0_rows = [m0_ref[head, pl.ds(i1, 1), :] for head in range(P_HEADS)]
        w0_rows = [w0_ref[head, pl.ds(i1, 1), :] for head in range(P_HEADS)]

        def packed_rows(row):
            one = jnp.broadcast_to(row, (2 * SUBLANES, LANES)).astype(BF16)
            return jnp.tile(one, (P_NKEYS // (2 * SUBLANES), 1))

        for lt in range(tm // LANES):
            lanes = slice(lt * LANES, (lt + 1) * LANES)
            gate = None
            for head in range(P_HEADS):
                m0 = packed_rows(m0_rows[head][:, lanes])
                w0 = packed_rows(w0_rows[head][:, lanes])
                term = w0 * jnp.where(r1_ref[head, lt] < m0, e1_ref[head, lt], zero)
                gate = term if gate is None else gate + term
            prob[rows, lanes] = gate * _gelu(hid[rows, lanes]).astype(BF16)
    acc[...] += jnp.dot(vt_ref[...], prob[...], preferred_element_type=F32)

    @pl.when(j == pl.num_programs(1) - 1)
    def _():
        res = h_ref[...] + acc[...].T
        if final_norm:
            res = _rms(res, gfin_ref[...])
        out_ref[...] = res


def _peer_dense(xt, u_bf, vt_bf, r1, e1, m0, w0, h, g_final, tm, final_norm):
    t, d = h.shape
    n_exp = u_bf.shape[0]
    tile = pl.BlockSpec((P_HEADS, P_NKEYS, tm), lambda i, j: (0, 0, i))
    lane_tiles = pl.BlockSpec((P_HEADS, tm // LANES, P_NKEYS, LANES), lambda i, j: (0, i, 0, 0))
    kern = functools.partial(_peer_dense_kernel, final_norm=final_norm)
    return pl.pallas_call(
        kern,
        grid=(pl.cdiv(t, tm), n_exp // P_EXPERT_TILE),
        in_specs=[pl.BlockSpec((d, tm), lambda i, j: (0, i)),
                  pl.BlockSpec((P_EXPERT_TILE, d), lambda i, j: (j, 0)),
                  pl.BlockSpec((d, P_EXPERT_TILE), lambda i, j: (0, j)),
                  lane_tiles, lane_tiles, tile, tile,
                  pl.BlockSpec((tm, d), lambda i, j: (i, 0)),
                  _const_spec((1, d))],
        out_specs=pl.BlockSpec((tm, d), lambda i, j: (i, 0)),
        out_shape=jax.ShapeDtypeStruct((t, d), F32),
        scratch_shapes=[pltpu.VMEM((d, tm), F32),
                        pltpu.VMEM((P_EXPERT_TILE, tm), F32),
                        pltpu.VMEM((P_EXPERT_TILE, tm), BF16)],
        compiler_params=_params("parallel", "arbitrary"),
        name="peer_dense",
    )(xt, u_bf, vt_bf, r1, e1, m0, w0, h, g_final.reshape(1, d))


def _pad_cols(w, n):
    return jnp.pad(w, ((0, 0), (0, n - w.shape[1])))


def _pad_rows(w, n):
    return jnp.pad(w, ((0, n - w.shape[0]), (0, 0)))


def _prep_rwkv(mix, w_rkv, w0, w1, w2, a0, a1, a2, g1, g2, k_k, k_a, r_k, ln_w, ln_b, w_o):
    d = D_MODEL
    lane = lambda x: jnp.repeat(x.reshape(RW_HEADS, RW_HEAD).T, LANES // RW_HEADS, axis=1)
    return {
        "mix": _pad_rows(mix, SUBLANES),
        "vec": _pad_rows(jnp.stack([w0, a0, k_k, k_a]), SUBLANES),
        "wr": w_rkv[0].astype(BF16), "wk": w_rkv[1].astype(BF16), "wv": w_rkv[2].astype(BF16),
        "w1": _split_bf16(_pad_cols(w1, LORA_PAD)), "w2": _split_bf16(_pad_rows(w2, LORA_PAD)),
        "a1": _split_bf16(_pad_cols(a1, LORA_PAD)), "a2": _split_bf16(_pad_rows(a2, LORA_PAD)),
        "g1": _pad_cols(g1, GATE_LORA_PAD).astype(BF16), "g2": _pad_rows(g2, GATE_LORA_PAD).astype(BF16),
        "lnw": lane(ln_w), "lnb": lane(ln_b), "rk": lane(r_k.reshape(d)),
        "wo": w_o.astype(BF16),
    }


def _prep_mamba(in_proj, conv_w, conv_b, dt_bias, a_log, d_skip, norm_w, out_proj):
    head_of_channel = jnp.arange(M_D_INNER) // M_HEADDIM
    expand = (jnp.arange(M_DT_PAD)[:, None] == head_of_channel[None, :]).astype(F32)
    d_full = jnp.pad(jnp.repeat(d_skip, M_HEADDIM), (0, M_CONV_DIM - M_D_INNER))
    pad_h = lambda x: jnp.pad(x, (0, M_DT_PAD - M_HEADS))
    return {
        "w_in": _pad_cols(in_proj, M_IN_DIM - M_HEADS + M_DT_PAD).astype(BF16),
        "conv_w": _pad_rows(conv_w, SUBLANES),
        "vec": _pad_rows(jnp.stack([conv_b, d_full]), SUBLANES),
        "dvec": _pad_rows(jnp.stack([pad_h(dt_bias), pad_h(-jnp.exp(a_log))]), SUBLANES),
        "expand": expand,
        "norm_w": norm_w.reshape(1, M_D_INNER),
        "w_out": out_proj.astype(BF16),
    }


def _prep_peer(w_q, sub_keys, u_tab, v_tab):
    keys = jnp.transpose(sub_keys, (1, 0, 2, 3)).reshape(2 * P_HEADS, P_NKEYS, P_QDIM // 2)
    keys_hi, keys_lo = _split_bf16(keys)
    return {"wq_t": w_q.T.astype(BF16), "keys": keys_hi, "keys_lo": keys_lo,
            "u": u_tab.astype(BF16), "vt": v_tab.T.astype(BF16)}


SCAN_GROUP = LANES // RW_HEADS
RELAYOUT_STEPS = 128
RELAYOUT_PITCH = RW_HEAD + SUBLANES


def _to_scan_kernel(x_ref, o_ref, stage):
    for b in range(SCAN_GROUP):
        for hp in range(RW_HEADS // 2):
            t = x_ref[b, :, hp * LANES:(hp + 1) * LANES].T
            for hl in range(2):
                lane = (2 * hp + hl) * SCAN_GROUP + b
                stage[lane * RELAYOUT_PITCH:lane * RELAYOUT_PITCH + RW_HEAD, :] = t[hl * RW_HEAD:(hl + 1) * RW_HEAD, :]
    for k in range(RW_HEAD):
        by_lane = stage[pl.ds(k, LANES, stride=RELAYOUT_PITCH), :]
        o_ref[pl.ds(k, RELAYOUT_STEPS, stride=RW_HEAD), :] = by_lane.T


def _to_scan_pallas(x, length):
    x3 = x.reshape(SCAN_GROUP, length, D_MODEL)
    out = pl.pallas_call(
        _to_scan_kernel,
        grid=(pl.cdiv(length, RELAYOUT_STEPS),),
        in_specs=[pl.BlockSpec((SCAN_GROUP, RELAYOUT_STEPS, D_MODEL), lambda i: (0, i, 0))],
        out_specs=pl.BlockSpec((RELAYOUT_STEPS * RW_HEAD, LANES), lambda i: (i, 0)),
        out_shape=jax.ShapeDtypeStruct((length * RW_HEAD, LANES), F32),
        scratch_shapes=[pltpu.VMEM((LANES * RELAYOUT_PITCH, LANES), F32)],
        compiler_params=_params("parallel"),
        name="to_scan_layout",
    )(x3)
    return out.reshape(length, RW_HEAD, LANES)


def _to_scan(x, bsz, length):
    if bsz == SCAN_GROUP and length >= RELAYOUT_STEPS:
        return _to_scan_pallas(x, length)
    x = x.reshape(bsz // SCAN_GROUP, SCAN_GROUP, length, RW_HEADS, RW_HEAD)
    return jnp.transpose(x, (2, 4, 0, 3, 1)).reshape(length, RW_HEAD, bsz * RW_HEADS)


def _from_scan(x, bsz, length):
    x = x.reshape(length, RW_HEAD, bsz // SCAN_GROUP, RW_HEADS, SCAN_GROUP)
    return jnp.transpose(x, (2, 4, 0, 3, 1)).reshape(bsz * length, D_MODEL)


def _state_to_scan(wkv, bsz):
    x = wkv.reshape(bsz // SCAN_GROUP, SCAN_GROUP, RW_HEADS, RW_HEAD, RW_HEAD)
    return jnp.transpose(x, (3, 4, 0, 2, 1)).reshape(RW_HEAD, RW_HEAD, bsz * RW_HEADS)


def _state_from_scan(s, bsz):
    x = s.reshape(RW_HEAD, RW_HEAD, bsz // SCAN_GROUP, RW_HEADS, SCAN_GROUP)
    return jnp.transpose(x, (2, 4, 3, 0, 1)).reshape(bsz, RW_HEADS, RW_HEAD, RW_HEAD)


def _rwkv_layer(h, g_mix, shift0, wkv0, p, bsz, length, tm, tt):
    xn = _norm(h, g_mix, tm)
    xn3 = xn.reshape(bsz, length, D_MODEL)
    prev = jnp.concatenate([shift0[:, None, :], xn3[:, :-1]], axis=1).reshape(bsz * length, D_MODEL)
    r, w, k, v, kk, a, g = _rwkv_proj(xn, prev, p, tm)
    seqs = [_to_scan(x, bsz, length) for x in (r, w, k, v, kk, a)]
    o, s_fin = _rwkv_scan(seqs, _state_to_scan(wkv0, bsz), p["lnw"], p["lnb"], p["rk"], tt)
    o = _from_scan(o, bsz, length)
    return _rwkv_out(o, g, h, p["wo"], tm), xn3[:, -1], _state_from_scan(s_fin, bsz)


def _mamba_layer(h, g_mix, conv0, ssm0, p, bsz, length, tm, rows):
    z, xbc, dt = _mamba_in(h, g_mix, p["w_in"], tm)
    xbc3 = xbc.reshape(bsz, length, M_CONV_DIM)
    tail = M_CONV - 1
    conv_new = xbc3[:, length - tail:] if length >= tail else jnp.concatenate([conv0, xbc3], axis=1)[:, length:]
    pad_t = (-length) % SUBLANES if length < rows else 0

    def seq(x, c):
        x = x.reshape(bsz, length, c)
        return jnp.pad(x, ((0, 0), (0, pad_t), (0, 0))) if pad_t else x

    conv0_p = jnp.pad(conv0, ((0, 0), (SUBLANES - (M_CONV - 1), 0), (0, 0)))
    out, ssm = _mamba_ssd(seq(xbc, M_CONV_DIM), seq(z, M_D_INNER), seq(dt, M_DT_PAD), seq(h, D_MODEL),
                          conv0_p, ssm0, p, min(rows, length + pad_t), length)
    if pad_t:
        out = out[:, :length]
    return out.reshape(bsz * length, D_MODEL), conv_new, ssm


def _peer_layer(h, g_ffn, p, tm, g_final=None):
    xt, r1, e1, m0, w0 = _peer_route(h, g_ffn, p["wq_t"], p["keys"], p["keys_lo"], tm)
    final = g_final is not None
    return _peer_dense(xt, p["u"], p["vt"], r1, e1, m0, w0, h, g_final if final else g_ffn, tm, final)


def _trunk(h, shift0, wkv0, conv0, ssm0, norm_mix, norm_ffn, norm_final, rwkv_p, mamba_p, peer_p,
           bsz, length, tm, tt, peer_tm, rows):
    h, shift, wkv = _rwkv_layer(h, norm_mix[0], shift0, wkv0, rwkv_p, bsz, length, tm, tt)
    h = _peer_layer(h, norm_ffn[0], peer_p[0], peer_tm)
    h, conv, ssm = _mamba_layer(h, norm_mix[1], conv0, ssm0, mamba_p, bsz, length, tm, rows)
    y = _peer_layer(h, norm_ffn[1], peer_p[1], peer_tm, g_final=norm_final)
    return y.reshape(bsz, length, D_MODEL), shift[None], wkv[None], conv[None], ssm[None]


def kernel(x_prompt, x_sample, state_rwkv_shift, state_rwkv_wkv, state_mamba_conv, state_mamba_ssm, meta_tokens, norm_mix, norm_ffn, norm_final, rwkv_mix, rwkv_w_rkv, rwkv_w0, rwkv_w1, rwkv_w2, rwkv_a0, rwkv_a1, rwkv_a2, rwkv_g1, rwkv_g2, rwkv_k_k, rwkv_k_a, rwkv_r_k, rwkv_ln_w, rwkv_ln_b, rwkv_w_o, mamba_in_proj, mamba_conv_w, mamba_conv_b, mamba_dt_bias, mamba_a_log, mamba_d, mamba_norm_w, mamba_out_proj, peer_w_q, peer_sub_keys, peer_u, peer_v):
    rwkv_p = _prep_rwkv(rwkv_mix[0], rwkv_w_rkv[0], rwkv_w0[0], rwkv_w1[0], rwkv_w2[0], rwkv_a0[0],
                        rwkv_a1[0], rwkv_a2[0], rwkv_g1[0], rwkv_g2[0], rwkv_k_k[0], rwkv_k_a[0],
                        rwkv_r_k[0], rwkv_ln_w[0], rwkv_ln_b[0], rwkv_w_o[0])
    mamba_p = _prep_mamba(mamba_in_proj[0], mamba_conv_w[0], mamba_conv_b[0], mamba_dt_bias[0],
                          mamba_a_log[0], mamba_d[0], mamba_norm_w[0], mamba_out_proj[0])
    peer_p = [_prep_peer(peer_w_q[i], peer_sub_keys[i], peer_u[i], peer_v[i]) for i in range(2)]

    bp, lp = x_prompt.shape[0], x_prompt.shape[1] + N_META
    meta = jnp.broadcast_to(meta_tokens[None], (bp, N_META, D_MODEL))
    hp = jnp.concatenate([meta, x_prompt], axis=1).reshape(bp * lp, D_MODEL)
    zeros = lambda *s: jnp.zeros(s, F32)
    yp, p_shift, p_wkv, p_conv, p_ssm = _trunk(
        hp, zeros(bp, D_MODEL), zeros(bp, RW_HEADS, RW_HEAD, RW_HEAD), zeros(bp, M_CONV - 1, M_CONV_DIM),
        zeros(bp, M_HEADS, M_HEADDIM, M_STATE), norm_mix, norm_ffn, norm_final, rwkv_p, mamba_p, peer_p,
        bp, lp, tm=344, tt=48, peer_tm=512, rows=M_CHUNK)

    bs, ls = x_sample.shape[0], x_sample.shape[1]
    ys, s_shift, s_wkv, s_conv, s_ssm = _trunk(
        x_sample.reshape(bs * ls, D_MODEL), state_rwkv_shift[0], state_rwkv_wkv[0], state_mamba_conv[0],
        state_mamba_ssm[0], norm_mix, norm_ffn, norm_final, rwkv_p, mamba_p, peer_p,
        bs, ls, tm=256, tt=ls, peer_tm=512, rows=M_CHUNK)
    return (yp[:, N_META:], ys, p_shift, p_wkv, p_conv, p_ssm, s_shift, s_wkv, s_conv, s_ssm)
```

```python
import functools
import math

import jax
import jax.numpy as jnp
from jax import lax
from jax.experimental import pallas as pl
from jax.experimental.pallas import tpu as pltpu

F32 = jnp.float32
BF16 = jnp.bfloat16
HIGHEST = lax.Precision.HIGHEST

D_MODEL = 1024
N_META = 16
NORM_EPS = 1e-5
RW_HEAD = 64
RW_HEADS = D_MODEL // RW_HEAD
RW_LN_EPS = 64e-5
LORA_PAD = 128
GATE_LORA_PAD = 256
M_D_INNER = 2048
M_HEADDIM = 64
M_HEADS = M_D_INNER // M_HEADDIM
M_GROUPS = 4
M_HPG = M_HEADS // M_GROUPS
M_STATE = 128
M_CONV = 4
M_CONV_DIM = M_D_INNER + 2 * M_GROUPS * M_STATE
M_IN_DIM = 2 * M_D_INNER + 2 * M_GROUPS * M_STATE + M_HEADS
M_DT_PAD = 128
M_CHUNK = 128
P_HEADS = 8
P_NKEYS = 128
P_QDIM = 256
P_TOPK = 16
P_EXPERT_TILE = 1024
P_EXPERT_SUB = 256

LANES = 128
SUBLANES = 8
VMEM_LIMIT_BYTES = 56 * 1024 * 1024


def _params(*semantics):
    return pltpu.CompilerParams(dimension_semantics=semantics, vmem_limit_bytes=VMEM_LIMIT_BYTES)


def _const_spec(shape):
    zeros = (0,) * len(shape)
    return pl.BlockSpec(shape, lambda *_: zeros)


def _rms(x, g):
    return x * lax.rsqrt(jnp.mean(x * x, axis=-1, keepdims=True) + NORM_EPS) * g


def _softplus(x):
    return jnp.maximum(x, 0.0) + jnp.log1p(jnp.exp(-jnp.abs(x)))


def _bdot(a, b):
    return jnp.dot(a.astype(BF16), b.astype(BF16), preferred_element_type=F32)


def _hdot(a, b):
    return jnp.dot(a, b, precision=HIGHEST, preferred_element_type=F32)


def _split_bf16(w):
    hi = w.astype(BF16)
    return hi, (w - hi.astype(F32)).astype(BF16)


def _dot3(a, b_hi, b_lo):
    a_hi, a_lo = _split_bf16(a)
    return (jnp.dot(a_hi, b_hi, preferred_element_type=F32)
            + (jnp.dot(a_hi, b_lo, preferred_element_type=F32) + jnp.dot(a_lo, b_hi, preferred_element_type=F32)))


def _norm_kernel(h_ref, g_ref, o_ref):
    o_ref[...] = _rms(h_ref[...], g_ref[...])


def _norm(h, g, tm):
    t, d = h.shape
    return pl.pallas_call(
        _norm_kernel,
        grid=(t // tm,),
        in_specs=[pl.BlockSpec((tm, d), lambda i: (i, 0)), _const_spec((1, d))],
        out_specs=pl.BlockSpec((tm, d), lambda i: (i, 0)),
        out_shape=jax.ShapeDtypeStruct((t, d), F32),
        compiler_params=_params("parallel"),
        name="rms_norm",
    )(h, g.reshape(1, d))


def _rwkv_proj_kernel(xn_ref, prev_ref, *refs):
    _rwkv_proj_body(xn_ref[...], prev_ref[...], *refs)


def _rwkv_proj_fused_kernel(h_ref, before_ref, shift_ref, gain_ref, *refs, tiles_per_seq):
    *proj_refs, last_ref = refs
    i = pl.program_id(0)
    xn = _rms(h_ref[...], gain_ref[...])
    before = _rms(before_ref[SUBLANES - 1:SUBLANES, :], gain_ref[...])
    first = jnp.where(i % tiles_per_seq == 0, shift_ref[0], before)
    row = lax.broadcasted_iota(jnp.int32, xn.shape, 0)
    prev = jnp.where(row == 0, first, pltpu.roll(xn, 1, axis=0))
    _rwkv_proj_body(xn, prev, *proj_refs)

    @pl.when(i % tiles_per_seq == tiles_per_seq - 1)
    def _():
        last_ref[0] = xn[xn.shape[0] - 1:, :]


def _rwkv_proj_body(xn, prev, mix_ref, vec_ref, wr_ref, wk_ref, wv_ref,
                    w1h_ref, w1l_ref, w2h_ref, w2l_ref, a1h_ref, a1l_ref, a2h_ref, a2l_ref, g1_ref, g2_ref,
                    r_ref, w_ref, k_ref, v_ref, kk_ref, a_ref, g_ref):
    dx = prev - xn
    xr, xw, xk, xv, xa, xg = [xn + dx * mix_ref[j:j + 1, :] for j in range(6)]
    w0, a0, k_k, k_a = [vec_ref[j:j + 1, :] for j in range(4)]
    r_ref[...] = _bdot(xr, wr_ref[...])
    k = _bdot(xk, wk_ref[...])
    v_ref[...] = _bdot(xv, wv_ref[...])
    w_lora = _dot3(jnp.tanh(_dot3(xw, w1h_ref[...], w1l_ref[...])), w2h_ref[...], w2l_ref[...])
    w_log = -_softplus(-(w0 + w_lora)) - 0.5
    w_ref[...] = jnp.exp(-jnp.exp(w_log))
    a = jax.nn.sigmoid(a0 + _dot3(_dot3(xa, a1h_ref[...], a1l_ref[...]), a2h_ref[...], a2l_ref[...]))
    a_ref[...] = a
    g_ref[...] = _bdot(jax.nn.sigmoid(_bdot(xg, g1_ref[...])), g2_ref[...])
    kk_ref[...] = k * k_k
    k_ref[...] = k * (1.0 + (a - 1.0) * k_a)


def _rwkv_proj(xn, prev, p, tm):
    t, d = xn.shape
    tok = pl.BlockSpec((tm, d), lambda i: (i, 0))
    weights = [p["mix"], p["vec"], p["wr"], p["wk"], p["wv"], *p["w1"], *p["w2"], *p["a1"], *p["a2"],
               p["g1"], p["g2"]]
    return pl.pallas_call(
        _rwkv_proj_kernel,
        grid=(t // tm,),
        in_specs=[tok, tok] + [_const_spec(w.shape) for w in weights],
        out_specs=[tok] * 7,
        out_shape=[jax.ShapeDtypeStruct((t, d), F32)] * 7,
        compiler_params=_params("parallel"),
        name="rwkv_proj",
    )(xn, prev, *weights)


def _rwkv_proj_fused(h, gain, shift0, p, tm, length):
    t, d = h.shape
    bsz = t // length
    tiles_per_seq = length // tm
    tok = pl.BlockSpec((tm, d), lambda i: (i, 0))
    before = pl.BlockSpec((SUBLANES, d), lambda i: (jnp.maximum(i * (tm // SUBLANES) - 1, 0), 0))
    per_seq = pl.BlockSpec((1, 1, d), lambda i: (i // tiles_per_seq, 0, 0))
    weights = [p["mix"], p["vec"], p["wr"], p["wk"], p["wv"], *p["w1"], *p["w2"], *p["a1"], *p["a2"],
               p["g1"], p["g2"]]
    kern = functools.partial(_rwkv_proj_fused_kernel, tiles_per_seq=tiles_per_seq)
    *proj, last = pl.pallas_call(
        kern,
        grid=(t // tm,),
        in_specs=[tok, before, per_seq, _const_spec((1, d))] + [_const_spec(w.shape) for w in weights],
        out_specs=[tok] * 7 + [per_seq],
        out_shape=[jax.ShapeDtypeStruct((t, d), F32)] * 7 + [jax.ShapeDtypeStruct((bsz, 1, d), F32)],
        compiler_params=_params("arbitrary"),
        name="rwkv_norm_proj",
    )(h, h, shift0.reshape(bsz, 1, d), gain.reshape(1, d), *weights)
    return proj, last.reshape(bsz, d)


def _rwkv_scan_kernel(r_ref, w_ref, k_ref, v_ref, kk_ref, a_ref, s0_ref, lnw_ref, lnb_ref, rk_ref,
                      o_ref, sfin_ref, state, o_rows):
    tb = pl.program_id(1)
    steps = r_ref.shape[0]

    @pl.when(tb == 0)
    def _():
        state[...] = s0_ref[...]

    def step(t, carry):
        r = r_ref[t]
        w = w_ref[t]
        k = k_ref[t]
        vv = v_ref[t]
        kku = kk_ref[t]
        norm = jnp.sqrt(jnp.sum(kku * kku, axis=0, keepdims=True))
        kk = kku / jnp.maximum(norm, 1e-12)
        alpha = -kk
        beta = kk * a_ref[t]

        def row(vi, c):
            s_v = state[vi]
            u = jnp.sum(s_v * alpha, axis=0, keepdims=True)
            s_new = s_v * w + u * beta + v_ref[t, pl.ds(vi, 1), :] * k
            state[vi] = s_new
            o_rows[pl.ds(vi, 1), :] = jnp.sum(s_new * r, axis=0, keepdims=True)
            return c

        lax.fori_loop(0, RW_HEAD, row, 0, unroll=32)
        o = o_rows[...]
        mu = jnp.mean(o, axis=0, keepdims=True)
        var = jnp.mean(jnp.square(o - mu), axis=0, keepdims=True)
        on = (o - mu) * lax.rsqrt(var + RW_LN_EPS)
        bonus = jnp.sum(r * k * rk_ref[...], axis=0, keepdims=True) * vv
        o_ref[t] = on * lnw_ref[...] + lnb_ref[...] + bonus
        return carry

    lax.fori_loop(0, steps, step, 0)

    @pl.when(tb == pl.num_programs(1) - 1)
    def _():
        sfin_ref[...] = state[...]


def _rwkv_scan(seqs, s0, lnw, lnb, rk, tt):
    length, n, probs = seqs[0].shape
    groups = probs // LANES
    seq_spec = pl.BlockSpec((tt, n, LANES), lambda g, i: (i, 0, g))
    st_spec = pl.BlockSpec((n, n, LANES), lambda g, i: (0, 0, g))
    lane_spec = pl.BlockSpec((n, LANES), lambda g, i: (0, 0))
    return pl.pallas_call(
        _rwkv_scan_kernel,
        grid=(groups, length // tt),
        in_specs=[seq_spec] * 6 + [st_spec] + [lane_spec] * 3,
        out_specs=[seq_spec, st_spec],
        out_shape=[jax.ShapeDtypeStruct((length, n, probs), F32),
                   jax.ShapeDtypeStruct((n, n, probs), F32)],
        scratch_shapes=[pltpu.VMEM((n, n, LANES), F32), pltpu.VMEM((n, LANES), F32)],
        compiler_params=_params("arbitrary", "arbitrary"),
        name="rwkv_scan",
    )(*seqs, s0, lnw, lnb, rk)


def _rwkv_out_kernel(o_ref, g_ref, h_ref, wo_ref, out_ref):
    out_ref[...] = h_ref[...] + _bdot(o_ref[...] * g_ref[...], wo_ref[...])


def _rwkv_out(o, g, h, wo, tm):
    t, d = h.shape
    tok = pl.BlockSpec((tm, d), lambda i: (i, 0))
    return pl.pallas_call(
        _rwkv_out_kernel,
        grid=(t // tm,),
        in_specs=[tok, tok, tok, _const_spec(wo.shape)],
        out_specs=tok,
        out_shape=jax.ShapeDtypeStruct((t, d), F32),
        compiler_params=_params("parallel"),
        name="rwkv_out",
    )(o, g, h, wo)


def _mamba_in_kernel(h_ref, g_ref, w_ref, z_ref, xbc_ref, dt_ref):
    xn = _rms(h_ref[...], g_ref[...]).astype(BF16)
    z_ref[...] = jnp.dot(xn, w_ref[:, :M_D_INNER], preferred_element_type=F32)
    xbc_ref[...] = jnp.dot(xn, w_ref[:, M_D_INNER:M_D_INNER + M_CONV_DIM], preferred_element_type=F32)
    dt_ref[...] = jnp.dot(xn, w_ref[:, M_D_INNER + M_CONV_DIM:], preferred_element_type=F32)


def _mamba_in(h, g, w_in, tm):
    t, d = h.shape
    return pl.pallas_call(
        _mamba_in_kernel,
        grid=(t // tm,),
        in_specs=[pl.BlockSpec((tm, d), lambda i: (i, 0)), _const_spec((1, d)), _const_spec(w_in.shape)],
        out_specs=[pl.BlockSpec((tm, M_D_INNER), lambda i: (i, 0)),
                   pl.BlockSpec((tm, M_CONV_DIM), lambda i: (i, 0)),
                   pl.BlockSpec((tm, M_DT_PAD), lambda i: (i, 0))],
        out_shape=[jax.ShapeDtypeStruct((t, M_D_INNER), F32),
                   jax.ShapeDtypeStruct((t, M_CONV_DIM), F32),
                   jax.ShapeDtypeStruct((t, M_DT_PAD), F32)],
        compiler_params=_params("parallel"),
        name="mamba_in_proj",
    )(h, g.reshape(1, d), w_in)


def _mamba_ssd_kernel(xbc_ref, z_ref, dt_ref, h_ref, conv0_ref, ssm0_ref, convw_ref, vec_ref,
                      dvec_ref, expand_ref, normw_ref, wout_ref,
                      out_ref, ssm_ref,
                      conv_buf, src_adt, src_acs, src_b, src_xdt, src_xdt_st, state, y_buf, *, seq_len, rows):
    b = pl.program_id(0)
    c = pl.program_id(1)
    q = M_CHUNK
    halo = SUBLANES
    d_bc = M_GROUPS * M_STATE

    @pl.when(c == 0)
    def _():
        state[...] = ssm0_ref[0]
        conv_buf[0:halo, :] = conv0_ref[0]

    @pl.when(c > 0)
    def _():
        conv_buf[0:halo, :] = conv_buf[rows:rows + halo, :]

    if rows < q:
        @pl.when((b == 0) & (c == 0))
        def _():
            src_adt[...] = jnp.zeros(src_adt.shape, F32)
            src_acs[...] = jnp.zeros(src_acs.shape, F32)
            src_b[...] = jnp.zeros(src_b.shape, BF16)
            src_xdt[...] = jnp.zeros(src_xdt.shape, BF16)
            src_xdt_st[...] = jnp.zeros(src_xdt_st.shape, BF16)

    conv_buf[halo:halo + rows, :] = xbc_ref[0]
    valid = (lax.broadcasted_iota(jnp.int32, (rows, 1), 0) + c * rows) < seq_len
    conv = vec_ref[0:1, :]
    for j in range(M_CONV):
        conv = conv + conv_buf[halo - (M_CONV - 1) + j:halo - (M_CONV - 1) + j + rows, :] * convw_ref[j:j + 1, :]
    act = jnp.where(valid, jax.nn.silu(conv), 0.0)
    xs = act[:, :M_D_INNER]
    dt = jnp.where(valid, _softplus(dt_ref[0] + dvec_ref[0:1, :]), 0.0)
    src_adt[0:rows, :] = dt * dvec_ref[1:2, :]
    li = lax.broadcasted_iota(jnp.int32, (rows, q), 0)
    si = lax.broadcasted_iota(jnp.int32, (rows, q), 1)
    causal = li >= si
    acs = _hdot(causal.astype(F32), src_adt[...])
    src_acs[0:rows, :] = acs
    acs_t = src_acs[...].T
    expand = expand_ref[...]
    dt_full = _hdot(dt, expand)
    acs_full = _hdot(acs, expand)
    last = acs[rows - 1:rows, :]
    last_full = _hdot(last, expand)
    xdt = xs * dt_full
    src_xdt[0:rows, :] = xdt.astype(BF16)
    src_xdt_st[0:rows, :] = (xdt * jnp.exp(last_full - acs_full)).astype(BF16)
    src_b[0:rows, :] = act[:, M_D_INNER:M_D_INNER + d_bc].astype(BF16)
    eacs_full = jnp.exp(acs_full)
    state_decay = jnp.exp(last)
    width = M_HPG * M_HEADDIM

    for g in range(M_GROUPS):
        cols = slice(g * width, (g + 1) * width)
        b_g = src_b[:, g * M_STATE:(g + 1) * M_STATE]
        c_g = act[:, M_D_INNER + d_bc + g * M_STATE:M_D_INNER + d_bc + (g + 1) * M_STATE].astype(BF16)
        cb = lax.dot_general(c_g, b_g, (((1,), (1,)), ((), ())), preferred_element_type=F32)
        h_g = state[g * M_HPG:(g + 1) * M_HPG].reshape(width, M_STATE)
        y_off = lax.dot_general(c_g, h_g.astype(BF16), (((1,), (1,)), ((), ())),
                                preferred_element_type=F32)
        upd = lax.dot_general(src_xdt_st[:, cols], b_g, (((0,), (0,)), ((), ())),
                              preferred_element_type=F32)
        ys = []
        for r in range(g * M_HPG, (g + 1) * M_HPG):
            lo = r * M_HEADDIM
            decay = jnp.where(causal, jnp.exp(acs[:, r:r + 1] - acs_t[r:r + 1, :]), 0.0)
            ys.append(jnp.dot((cb * decay).astype(BF16), src_xdt[:, lo:lo + M_HEADDIM],
                              preferred_element_type=F32))
            sub = slice((r - g * M_HPG) * M_HEADDIM, (r - g * M_HPG + 1) * M_HEADDIM)
            state[r] = state[r] * state_decay[:, r:r + 1] + upd[sub, :]
        y_buf[:, cols] = jnp.concatenate(ys, axis=1) + y_off * eacs_full[:, cols]

    y = y_buf[...] + vec_ref[1:2, :M_D_INNER] * xs
    yg = y * jax.nn.silu(z_ref[0])
    parts = []
    for g in range(M_GROUPS):
        part = yg[:, g * width:(g + 1) * width]
        parts.append(part * lax.rsqrt(jnp.mean(part * part, axis=-1, keepdims=True) + NORM_EPS))
    yn = jnp.concatenate(parts, axis=1) * normw_ref[...]
    out_ref[0] = h_ref[0] + _bdot(yn, wout_ref[...])

    @pl.when(c == pl.num_programs(1) - 1)
    def _():
        ssm_ref[0] = state[...]


def _mamba_ssd(xbc, z, dt, h, conv0, ssm0, p, rows, seq_len):
    bsz, length, _ = xbc.shape
    chunks = pl.cdiv(length, rows)
    weights = [p["conv_w"], p["vec"], p["dvec"], p["expand"], p["norm_w"], p["w_out"]]

    def seq_spec(width):
        return pl.BlockSpec((1, rows, width), lambda b, c: (b, c, 0))

    kern = functools.partial(_mamba_ssd_kernel, seq_len=seq_len, rows=rows)
    return pl.pallas_call(
        kern,
        grid=(bsz, chunks),
        in_specs=[seq_spec(M_CONV_DIM), seq_spec(M_D_INNER), seq_spec(M_DT_PAD), seq_spec(D_MODEL),
                  pl.BlockSpec((1, SUBLANES, M_CONV_DIM), lambda b, c: (b, 0, 0)),
                  pl.BlockSpec((1, M_HEADS, M_HEADDIM, M_STATE), lambda b, c: (b, 0, 0, 0))]
                 + [_const_spec(w.shape) for w in weights],
        out_specs=[seq_spec(D_MODEL),
                   pl.BlockSpec((1, M_HEADS, M_HEADDIM, M_STATE), lambda b, c: (b, 0, 0, 0))],
        out_shape=[jax.ShapeDtypeStruct((bsz, length, D_MODEL), F32),
                   jax.ShapeDtypeStruct((bsz, M_HEADS, M_HEADDIM, M_STATE), F32)],
        scratch_shapes=[pltpu.VMEM((rows + SUBLANES, M_CONV_DIM), F32),
                        pltpu.VMEM((M_CHUNK, M_DT_PAD), F32),
                        pltpu.VMEM((M_CHUNK, M_DT_PAD), F32),
                        pltpu.VMEM((M_CHUNK, M_GROUPS * M_STATE), BF16),
                        pltpu.VMEM((M_CHUNK, M_D_INNER), BF16),
                        pltpu.VMEM((M_CHUNK, M_D_INNER), BF16),
                        pltpu.VMEM((M_HEADS, M_HEADDIM, M_STATE), F32),
                        pltpu.VMEM((rows, M_D_INNER), F32)],
        compiler_params=_params("arbitrary", "arbitrary"),
        name="mamba_ssd",
    )(xbc, z, dt, h, conv0, ssm0, *weights)


_CANDS = [(ra, rb) for ra in range(P_TOPK) for rb in range(P_TOPK) if (ra + 1) * (rb + 1) <= P_TOPK]


def _peer_route_kernel(h_ref, g_ref, wq_ref, keys_ref, keys_lo_ref,
                       xt_ref, r1_ref, e1_ref, m0_ref, w0_ref,
                       q_buf, s_buf, code0, code_book, best_vals, cand, counts, inv_z):
    tm = h_ref.shape[0]
    neg_inf = float("-inf")
    xn = _rms(h_ref[...], g_ref[...])
    xt = xn.T.astype(BF16)
    xt_ref[...] = xt
    q_buf[...] = jnp.dot(wq_ref[...], xt, preferred_element_type=F32)
    key_iota = lax.broadcasted_iota(jnp.int32, (P_NKEYS, tm), 0).astype(F32)

    def half(m, carry):
        head = m // 2
        z = m % 2
        row0 = pl.multiple_of(m * P_NKEYS, P_NKEYS)
        qm = q_buf[pl.ds(row0, P_NKEYS), :]
        q_hi = qm.astype(BF16)
        q_lo = (qm - q_hi.astype(F32)).astype(BF16)
        k_hi = keys_ref[m]
        s = (jnp.dot(k_hi, q_hi, preferred_element_type=F32)
             + (jnp.dot(k_hi, q_lo, preferred_element_type=F32)
                + jnp.dot(keys_lo_ref[m], q_hi, preferred_element_type=F32)))

        s_buf[...] = s

        def next_distinct(j, prev):
            sj = s_buf[...]
            best = jnp.max(jnp.where(sj < prev, sj, neg_inf), axis=0, keepdims=True)
            best_vals[z, j, pl.ds(head, 1), :] = best
            return best

        def extract_ties(j, sc):
            cur, rank = sc
            best = jnp.max(cur, axis=0, keepdims=True)
            first = jnp.min(jnp.where(cur == best, key_iota, float(P_NKEYS)), axis=0, keepdims=True)
            sel = key_iota == first
            best_vals[z, j, pl.ds(head, 1), :] = best
            return jnp.where(sel, neg_inf, cur), jnp.where(sel, jnp.asarray(j, dtype=F32), rank)

        last = lax.fori_loop(0, P_TOPK, next_distinct, jnp.full((1, tm), float("inf"), F32))
        reached = jnp.sum(jnp.where(s >= last, 1.0, 0.0), axis=0, keepdims=True)
        has_ties = jnp.max(jnp.abs(reached - float(P_TOPK))) > 0.0

        def rank_by_count():
            rank = jnp.zeros((P_NKEYS, tm), F32)
            for jj in range(P_TOPK):
                rank = rank + jnp.where(best_vals[z, jj, pl.ds(head, 1), :] > s, 1.0, 0.0)
            return rank

        def rank_with_ties():
            no_rank = jnp.full((P_NKEYS, tm), float(P_NKEYS), F32)
            return lax.fori_loop(0, P_TOPK, extract_ties, (s, no_rank))[1]

        e = jnp.exp(s - jnp.max(s, axis=0, keepdims=True))

        @pl.when(z == 0)
        def _():
            w0_ref[head] = e

            @pl.when(has_ties)
            def _():
                code0[head] = rank_with_ties()
                for ra in range(P_TOPK):
                    code_book[ra, pl.ds(head, 1), :] = jnp.full((1, tm), float(ra), F32)

            @pl.when(jnp.logical_not(has_ties))
            def _():
                code0[head] = s
                for ra in range(P_TOPK):
                    code_book[ra, pl.ds(head, 1), :] = best_vals[0, ra, pl.ds(head, 1), :]

        @pl.when(z == 1)
        def _():
            rank = lax.cond(has_ties, rank_with_ties, rank_by_count)
            for lt in range(tm // LANES):
                r1_ref[head, lt] = rank[:, lt * LANES:(lt + 1) * LANES].astype(BF16)
                e1_ref[head, lt] = e[:, lt * LANES:(lt + 1) * LANES].astype(BF16)

        return carry

    lax.fori_loop(0, 2 * P_HEADS, half, 0)

    ex0 = [jnp.exp(best_vals[0, ra] - best_vals[0, 0]) for ra in range(P_TOPK)]
    ex1 = [jnp.exp(best_vals[1, rb] - best_vals[1, 0]) for rb in range(P_TOPK)]
    for i, (ra, rb) in enumerate(_CANDS):
        cand[i] = best_vals[0, ra] + best_vals[1, rb]

    def next_sum(it, prev):
        best = None
        for i in range(len(_CANDS)):
            ci = cand[i]
            below = jnp.where(ci < prev, ci, neg_inf)
            best = below if best is None else jnp.maximum(best, below)
        return best

    last_sum = lax.fori_loop(0, P_TOPK, next_sum, jnp.full((P_HEADS, tm), float("inf"), F32))
    hits = [jnp.where(cand[i] >= last_sum, 1.0, 0.0) for i in range(len(_CANDS))]
    n_hits = functools.reduce(lambda x, y: x + y, hits)
    sums_tie = jnp.max(jnp.abs(n_hits - float(P_TOPK))) > 0.0

    def select_distinct():
        z = jnp.zeros((P_HEADS, tm), F32)
        per_rank = [None] * P_TOPK
        for i, (ra, rb) in enumerate(_CANDS):
            per_rank[ra] = hits[i] if per_rank[ra] is None else per_rank[ra] + hits[i]
            z = z + hits[i] * (ex0[ra] * ex1[rb])
        for ra in range(P_TOPK):
            counts[ra] = per_rank[ra]
        return z

    def select_with_ties():
        counts[...] = jnp.zeros(counts.shape, F32)

        def pick(it, z):
            best = cand[0]
            for i in range(1, len(_CANDS)):
                best = jnp.maximum(best, cand[i])
            found = jnp.zeros((P_HEADS, tm), F32)
            for i, (ra, rb) in enumerate(_CANDS):
                ci = cand[i]
                hit = jnp.where(ci == best, 1.0, 0.0) * (1.0 - found)
                found = found + hit
                cand[i] = jnp.where(hit > 0.0, neg_inf, ci)
                counts[ra] = counts[ra] + hit
                z = z + hit * (ex0[ra] * ex1[rb])
            return z

        return lax.fori_loop(0, P_TOPK, pick, jnp.zeros((P_HEADS, tm), F32))

    inv_z[...] = 1.0 / lax.cond(sums_tie, select_with_ties, select_distinct)

    def finish(head, carry):
        code = code0[head]
        m0 = jnp.zeros((P_NKEYS, tm), F32)
        for ra in range(P_TOPK):
            m0 = jnp.where(code == code_book[ra, pl.ds(head, 1), :], counts[ra, pl.ds(head, 1), :], m0)
        m0_ref[head] = m0
        w0_ref[head] = w0_ref[head] * inv_z[pl.ds(head, 1), :]
        return carry

    lax.fori_loop(0, P_HEADS, finish, 0)


def _peer_route(h, g, wq_t, keys, keys_lo, tm):
    t, d = h.shape
    tiles = pl.cdiv(t, tm)
    t_pad = tiles * tm
    tile = pl.BlockSpec((P_HEADS, P_NKEYS, tm), lambda i: (0, 0, i))
    tile_f32 = jax.ShapeDtypeStruct((P_HEADS, P_NKEYS, t_pad), F32)
    tile_bf16 = jax.ShapeDtypeStruct((P_HEADS, t_pad // LANES, P_NKEYS, LANES), BF16)
    lane_tiles = pl.BlockSpec((P_HEADS, tm // LANES, P_NKEYS, LANES), lambda i: (0, i, 0, 0))
    return pl.pallas_call(
        _peer_route_kernel,
        grid=(tiles,),
        in_specs=[pl.BlockSpec((tm, d), lambda i: (i, 0)), _const_spec((1, d)),
                  _const_spec(wq_t.shape), _const_spec(keys.shape), _const_spec(keys_lo.shape)],
        out_specs=[pl.BlockSpec((d, tm), lambda i: (0, i)), lane_tiles, lane_tiles, tile, tile],
        out_shape=[jax.ShapeDtypeStruct((d, t_pad), BF16), tile_bf16, tile_bf16, tile_f32, tile_f32],
        scratch_shapes=[pltpu.VMEM((P_HEADS * P_QDIM, tm), F32),
                        pltpu.VMEM((P_NKEYS, tm), F32),
                        pltpu.VMEM((P_HEADS, P_NKEYS, tm), F32),
                        pltpu.VMEM((P_TOPK, P_HEADS, tm), F32),
                        pltpu.VMEM((2, P_TOPK, P_HEADS, tm), F32),
                        pltpu.VMEM((len(_CANDS), P_HEADS, tm), F32),
                        pltpu.VMEM((P_TOPK, P_HEADS, tm), F32),
                        pltpu.VMEM((P_HEADS, tm), F32)],
        compiler_params=_params("parallel"),
        name="peer_route",
    )(h, g.reshape(1, d), wq_t, keys, keys_lo)


def _gelu(x):
    return 0.5 * x * (1.0 + lax.erf(x * (1.0 / math.sqrt(2.0))))


def _peer_dense_kernel(xt_ref, u_ref, vt_ref, r1_ref, e1_ref, m0_ref, w0_ref, h_ref, gfin_ref,
                       out_ref, acc, hid, prob, *, final_norm):
    j = pl.program_id(1)
    keys_per_step = P_EXPERT_TILE // P_NKEYS

    @pl.when(j == 0)
    def _():
        acc[...] = jnp.zeros(acc.shape, F32)

    tm = xt_ref.shape[1]
    hid[...] = jnp.dot(u_ref[...], xt_ref[...], preferred_element_type=F32)
    zero = jnp.zeros((), BF16)
    for ii in range(keys_per_step):
        i1 = j * keys_per_step + ii
        rows = slice(ii * P_NKEYS, (ii + 1) * P_NKEYS)
        m0_rows = [m0_ref[head, pl.ds(i1, 1), :] for head in range(P_HEADS)]
        w0_rows = [w0_ref[head, pl.ds(i1, 1), :] for head in range(P_HEADS)]

        def packed_rows(row):
            one = jnp.broadcast_to(row, (2 * SUBLANES, LANES)).astype(BF16)
            return jnp.tile(one, (P_NKEYS // (2 * SUBLANES), 1))

        for lt in range(tm // LANES):
            lanes = slice(lt * LANES, (lt + 1) * LANES)
            gate = None
            for head in range(P_HEADS):
                m0 = packed_rows(m0_rows[head][:, lanes])
                w0 = packed_rows(w0_rows[head][:, lanes])
                term = w0 * jnp.where(r1_ref[head, lt] < m0, e1_ref[head, lt], zero)
                gate = term if gate is None else gate + term
            prob[rows, lanes] = gate * _gelu(hid[rows, lanes]).astype(BF16)
    acc[...] += jnp.dot(vt_ref[...], prob[...], preferred_element_type=F32)

    @pl.when(j == pl.num_programs(1) - 1)
    def _():
        res = h_ref[...] + acc[...].T
        if final_norm:
            res = _rms(res, gfin_ref[...])
        out_ref[...] = res


def _peer_dense(xt, u_bf, vt_bf, r1, e1, m0, w0, h, g_final, tm, final_norm):
    t, d = h.shape
    n_exp = u_bf.shape[0]
    tile = pl.BlockSpec((P_HEADS, P_NKEYS, tm), lambda i, j: (0, 0, i))
    lane_tiles = pl.BlockSpec((P_HEADS, tm // LANES, P_NKEYS, LANES), lambda i, j: (0, i, 0, 0))
    kern = functools.partial(_peer_dense_kernel, final_norm=final_norm)
    return pl.pallas_call(
        kern,
        grid=(pl.cdiv(t, tm), n_exp // P_EXPERT_TILE),
        in_specs=[pl.BlockSpec((d, tm), lambda i, j: (0, i)),
                  pl.BlockSpec((P_EXPERT_TILE, d), lambda i, j: (j, 0)),
                  pl.BlockSpec((d, P_EXPERT_TILE), lambda i, j: (0, j)),
                  lane_tiles, lane_tiles, tile, tile,
                  pl.BlockSpec((tm, d), lambda i, j: (i, 0)),
                  _const_spec((1, d))],
        out_specs=pl.BlockSpec((tm, d), lambda i, j: (i, 0)),
        out_shape=jax.ShapeDtypeStruct((t, d), F32),
        scratch_shapes=[pltpu.VMEM((d, tm), F32),
                        pltpu.VMEM((P_EXPERT_TILE, tm), F32),
                        pltpu.VMEM((P_EXPERT_TILE, tm), BF16)],
        compiler_params=_params("parallel", "arbitrary"),
        name="peer_dense",
    )(xt, u_bf, vt_bf, r1, e1, m0, w0, h, g_final.reshape(1, d))


def _pad_cols(w, n):
    return jnp.pad(w, ((0, 0), (0, n - w.shape[1])))


def _pad_rows(w, n):
    return jnp.pad(w, ((0, n - w.shape[0]), (0, 0)))


def _prep_rwkv(mix, w_rkv, w0, w1, w2, a0, a1, a2, g1, g2, k_k, k_a, r_k, ln_w, ln_b, w_o):
    d = D_MODEL
    lane = lambda x: jnp.repeat(x.reshape(RW_HEADS, RW_HEAD).T, LANES // RW_HEADS, axis=1)
    return {
        "mix": _pad_rows(mix, SUBLANES),
        "vec": _pad_rows(jnp.stack([w0, a0, k_k, k_a]), SUBLANES),
        "wr": w_rkv[0].astype(BF16), "wk": w_rkv[1].astype(BF16), "wv": w_rkv[2].astype(BF16),
        "w1": _split_bf16(_pad_cols(w1, LORA_PAD)), "w2": _split_bf16(_pad_rows(w2, LORA_PAD)),
        "a1": _split_bf16(_pad_cols(a1, LORA_PAD)), "a2": _split_bf16(_pad_rows(a2, LORA_PAD)),
        "g1": _pad_cols(g1, GATE_LORA_PAD).astype(BF16), "g2": _pad_rows(g2, GATE_LORA_PAD).astype(BF16),
        "lnw": lane(ln_w), "lnb": lane(ln_b), "rk": lane(r_k.reshape(d)),
        "wo": w_o.astype(BF16),
    }


def _prep_mamba(in_proj, conv_w, conv_b, dt_bias, a_log, d_skip, norm_w, out_proj):
    head_of_channel = jnp.arange(M_D_INNER) // M_HEADDIM
    expand = (jnp.arange(M_DT_PAD)[:, None] == head_of_channel[None, :]).astype(F32)
    d_full = jnp.pad(jnp.repeat(d_skip, M_HEADDIM), (0, M_CONV_DIM - M_D_INNER))
    pad_h = lambda x: jnp.pad(x, (0, M_DT_PAD - M_HEADS))
    return {
        "w_in": _pad_cols(in_proj, M_IN_DIM - M_HEADS + M_DT_PAD).astype(BF16),
        "conv_w": _pad_rows(conv_w, SUBLANES),
        "vec": _pad_rows(jnp.stack([conv_b, d_full]), SUBLANES),
        "dvec": _pad_rows(jnp.stack([pad_h(dt_bias), pad_h(-jnp.exp(a_log))]), SUBLANES),
        "expand": expand,
        "norm_w": norm_w.reshape(1, M_D_INNER),
        "w_out": out_proj.astype(BF16),
    }


def _prep_peer(w_q, sub_keys, u_tab, v_tab):
    keys = jnp.transpose(sub_keys, (1, 0, 2, 3)).reshape(2 * P_HEADS, P_NKEYS, P_QDIM // 2)
    keys_hi, keys_lo = _split_bf16(keys)
    return {"wq_t": w_q.T.astype(BF16), "keys": keys_hi, "keys_lo": keys_lo,
            "u": u_tab.astype(BF16), "vt": v_tab.T.astype(BF16)}


SCAN_GROUP = LANES // RW_HEADS
RELAYOUT_STEPS = 128
RELAYOUT_PITCH = RW_HEAD + SUBLANES


def _to_scan_kernel(x_ref, o_ref, stage):
    for b in range(SCAN_GROUP):
        for hp in range(RW_HEADS // 2):
            t = x_ref[b, :, hp * LANES:(hp + 1) * LANES].T
            for hl in range(2):
                lane = (2 * hp + hl) * SCAN_GROUP + b
                stage[lane * RELAYOUT_PITCH:lane * RELAYOUT_PITCH + RW_HEAD, :] = t[hl * RW_HEAD:(hl + 1) * RW_HEAD, :]
    for k in range(RW_HEAD):
        by_lane = stage[pl.ds(k, LANES, stride=RELAYOUT_PITCH), :]
        o_ref[pl.ds(k, RELAYOUT_STEPS, stride=RW_HEAD), :] = by_lane.T


def _to_scan_pallas(x, length):
    x3 = x.reshape(SCAN_GROUP, length, D_MODEL)
    out = pl.pallas_call(
        _to_scan_kernel,
        grid=(pl.cdiv(length, RELAYOUT_STEPS),),
        in_specs=[pl.BlockSpec((SCAN_GROUP, RELAYOUT_STEPS, D_MODEL), lambda i: (0, i, 0))],
        out_specs=pl.BlockSpec((RELAYOUT_STEPS * RW_HEAD, LANES), lambda i: (i, 0)),
        out_shape=jax.ShapeDtypeStruct((length * RW_HEAD, LANES), F32),
        scratch_shapes=[pltpu.VMEM((LANES * RELAYOUT_PITCH, LANES), F32)],
        compiler_params=_params("parallel"),
        name="to_scan_layout",
    )(x3)
    return out.reshape(length, RW_HEAD, LANES)


def _to_scan(x, bsz, length):
    if bsz == SCAN_GROUP and length >= RELAYOUT_STEPS:
        return _to_scan_pallas(x, length)
    x = x.reshape(bsz // SCAN_GROUP, SCAN_GROUP, length, RW_HEADS, RW_HEAD)
    return jnp.transpose(x, (2, 4, 0, 3, 1)).reshape(length, RW_HEAD, bsz * RW_HEADS)


def _from_scan(x, bsz, length):
    x = x.reshape(length, RW_HEAD, bsz // SCAN_GROUP, RW_HEADS, SCAN_GROUP)
    return jnp.transpose(x, (2, 4, 0, 3, 1)).reshape(bsz * length, D_MODEL)


def _state_to_scan(wkv, bsz):
    x = wkv.reshape(bsz // SCAN_GROUP, SCAN_GROUP, RW_HEADS, RW_HEAD, RW_HEAD)
    return jnp.transpose(x, (3, 4, 0, 2, 1)).reshape(RW_HEAD, RW_HEAD, bsz * RW_HEADS)


def _state_from_scan(s, bsz):
    x = s.reshape(RW_HEAD, RW_HEAD, bsz // SCAN_GROUP, RW_HEADS, SCAN_GROUP)
    return jnp.transpose(x, (2, 4, 3, 0, 1)).reshape(bsz, RW_HEADS, RW_HEAD, RW_HEAD)


def _rwkv_layer(h, g_mix, shift0, wkv0, p, bsz, length, tm, tt):
    if length % tm == 0:
        (r, w, k, v, kk, a, g), shift = _rwkv_proj_fused(h, g_mix, shift0, p, tm, length)
    else:
        xn = _norm(h, g_mix, tm)
        xn3 = xn.reshape(bsz, length, D_MODEL)
        prev = jnp.concatenate([shift0[:, None, :], xn3[:, :-1]], axis=1).reshape(bsz * length, D_MODEL)
        r, w, k, v, kk, a, g = _rwkv_proj(xn, prev, p, tm)
        shift = xn3[:, -1]
    seqs = [_to_scan(x, bsz, length) for x in (r, w, k, v, kk, a)]
    o, s_fin = _rwkv_scan(seqs, _state_to_scan(wkv0, bsz), p["lnw"], p["lnb"], p["rk"], tt)
    o = _from_scan(o, bsz, length)
    return _rwkv_out(o, g, h, p["wo"], tm), shift, _state_from_scan(s_fin, bsz)


def _mamba_layer(h, g_mix, conv0, ssm0, p, bsz, length, tm, rows):
    z, xbc, dt = _mamba_in(h, g_mix, p["w_in"], tm)
    xbc3 = xbc.reshape(bsz, length, M_CONV_DIM)
    tail = M_CONV - 1
    conv_new = xbc3[:, length - tail:] if length >= tail else jnp.concatenate([conv0, xbc3], axis=1)[:, length:]
    pad_t = (-length) % SUBLANES if length < rows else 0

    def seq(x, c):
        x = x.reshape(bsz, length, c)
        return jnp.pad(x, ((0, 0), (0, pad_t), (0, 0))) if pad_t else x

    conv0_p = jnp.pad(conv0, ((0, 0), (SUBLANES - (M_CONV - 1), 0), (0, 0)))
    out, ssm = _mamba_ssd(seq(xbc, M_CONV_DIM), seq(z, M_D_INNER), seq(dt, M_DT_PAD), seq(h, D_MODEL),
                          conv0_p, ssm0, p, min(rows, length + pad_t), length)
    if pad_t:
        out = out[:, :length]
    return out.reshape(bsz * length, D_MODEL), conv_new, ssm


def _peer_layer(h, g_ffn, p, tm, g_final=None):
    xt, r1, e1, m0, w0 = _peer_route(h, g_ffn, p["wq_t"], p["keys"], p["keys_lo"], tm)
    final = g_final is not None
    return _peer_dense(xt, p["u"], p["vt"], r1, e1, m0, w0, h, g_final if final else g_ffn, tm, final)


def _trunk(h, shift0, wkv0, conv0, ssm0, norm_mix, norm_ffn, norm_final, rwkv_p, mamba_p, peer_p,
           bsz, length, tm, tt, peer_tm, rows):
    h, shift, wkv = _rwkv_layer(h, norm_mix[0], shift0, wkv0, rwkv_p, bsz, length, tm, tt)
    h = _peer_layer(h, norm_ffn[0], peer_p[0], peer_tm)
    h, conv, ssm = _mamba_layer(h, norm_mix[1], conv0, ssm0, mamba_p, bsz, length, tm, rows)
    y = _peer_layer(h, norm_ffn[1], peer_p[1], peer_tm, g_final=norm_final)
    return y.reshape(bsz, length, D_MODEL), shift[None], wkv[None], conv[None], ssm[None]


def kernel(x_prompt, x_sample, state_rwkv_shift, state_rwkv_wkv, state_mamba_conv, state_mamba_ssm, meta_tokens, norm_mix, norm_ffn, norm_final, rwkv_mix, rwkv_w_rkv, rwkv_w0, rwkv_w1, rwkv_w2, rwkv_a0, rwkv_a1, rwkv_a2, rwkv_g1, rwkv_g2, rwkv_k_k, rwkv_k_a, rwkv_r_k, rwkv_ln_w, rwkv_ln_b, rwkv_w_o, mamba_in_proj, mamba_conv_w, mamba_conv_b, mamba_dt_bias, mamba_a_log, mamba_d, mamba_norm_w, mamba_out_proj, peer_w_q, peer_sub_keys, peer_u, peer_v):
    rwkv_p = _prep_rwkv(rwkv_mix[0], rwkv_w_rkv[0], rwkv_w0[0], rwkv_w1[0], rwkv_w2[0], rwkv_a0[0],
                        rwkv_a1[0], rwkv_a2[0], rwkv_g1[0], rwkv_g2[0], rwkv_k_k[0], rwkv_k_a[0],
                        rwkv_r_k[0], rwkv_ln_w[0], rwkv_ln_b[0], rwkv_w_o[0])
    mamba_p = _prep_mamba(mamba_in_proj[0], mamba_conv_w[0], mamba_conv_b[0], mamba_dt_bias[0],
                          mamba_a_log[0], mamba_d[0], mamba_norm_w[0], mamba_out_proj[0])
    peer_p = [_prep_peer(peer_w_q[i], peer_sub_keys[i], peer_u[i], peer_v[i]) for i in range(2)]

    bp, lp = x_prompt.shape[0], x_prompt.shape[1] + N_META
    meta = jnp.broadcast_to(meta_tokens[None], (bp, N_META, D_MODEL))
    hp = jnp.concatenate([meta, x_prompt], axis=1).reshape(bp * lp, D_MODEL)
    zeros = lambda *s: jnp.zeros(s, F32)
    yp, p_shift, p_wkv, p_conv, p_ssm = _trunk(
        hp, zeros(bp, D_MODEL), zeros(bp, RW_HEADS, RW_HEAD, RW_HEAD), zeros(bp, M_CONV - 1, M_CONV_DIM),
        zeros(bp, M_HEADS, M_HEADDIM, M_STATE), norm_mix, norm_ffn, norm_final, rwkv_p, mamba_p, peer_p,
        bp, lp, tm=344, tt=48, peer_tm=512, rows=M_CHUNK)

    bs, ls = x_sample.shape[0], x_sample.shape[1]
    ys, s_shift, s_wkv, s_conv, s_ssm = _trunk(
        x_sample.reshape(bs * ls, D_MODEL), state_rwkv_shift[0], state_rwkv_wkv[0], state_mamba_conv[0],
        state_mamba_ssm[0], norm_mix, norm_ffn, norm_final, rwkv_p, mamba_p, peer_p,
        bs, ls, tm=256, tt=ls, peer_tm=512, rows=M_CHUNK)
    return (yp[:, N_META:], ys, p_shift, p_wkv, p_conv, p_ssm, s_shift, s_wkv, s_conv, s_ssm)
```

```python
import functools
import math

import jax
import jax.numpy as jnp
from jax import lax
from jax.experimental import pallas as pl
from jax.experimental.pallas import tpu as pltpu

F32 = jnp.float32
BF16 = jnp.bfloat16

D_MODEL = 1024
N_META = 16
NORM_EPS = 1e-5
RW_HEAD = 64
RW_HEADS = D_MODEL // RW_HEAD
RW_LN_EPS = 64e-5
LORA_PAD = 128
GATE_LORA_PAD = 256
M_D_INNER = 2048
M_HEADDIM = 64
M_HEADS = M_D_INNER // M_HEADDIM
M_GROUPS = 4
M_HPG = M_HEADS // M_GROUPS
M_STATE = 128
M_CONV = 4
M_CONV_DIM = M_D_INNER + 2 * M_GROUPS * M_STATE
M_IN_DIM = 2 * M_D_INNER + 2 * M_GROUPS * M_STATE + M_HEADS
M_DT_PAD = 128
M_CHUNK = 128
P_HEADS = 8
P_NKEYS = 128
P_QDIM = 256
P_TOPK = 16
P_EXPERT_TILE = 2048
P_EXPERT_SUB = 256

LANES = 128
SUBLANES = 8
VMEM_LIMIT_BYTES = 56 * 1024 * 1024


def _params(*semantics):
    return pltpu.CompilerParams(dimension_semantics=semantics, vmem_limit_bytes=VMEM_LIMIT_BYTES)


def _const_spec(shape):
    zeros = (0,) * len(shape)
    return pl.BlockSpec(shape, lambda *_: zeros)


def _rms(x, g):
    return x * lax.rsqrt(jnp.mean(x * x, axis=-1, keepdims=True) + NORM_EPS) * g


def _softplus(x):
    return jnp.maximum(x, 0.0) + jnp.log1p(jnp.exp(-jnp.abs(x)))


def _bdot(a, b):
    return jnp.dot(a.astype(BF16), b.astype(BF16), preferred_element_type=F32)


def _split_bf16(w):
    hi = w.astype(BF16)
    return hi, (w - hi.astype(F32)).astype(BF16)


def _split3_bf16(x):
    p0 = x.astype(BF16)
    r0 = x - p0.astype(F32)
    p1 = r0.astype(BF16)
    p2 = (r0 - p1.astype(F32)).astype(BF16)
    return p0, p1, p2


def _dot_select(a, sel):
    p0, p1, p2 = _split3_bf16(a)
    return (jnp.dot(p0, sel, preferred_element_type=F32)
            + (jnp.dot(p1, sel, preferred_element_type=F32) + jnp.dot(p2, sel, preferred_element_type=F32)))


def _select_dot(sel, b):
    p0, p1, p2 = _split3_bf16(b)
    return (jnp.dot(sel, p0, preferred_element_type=F32)
            + (jnp.dot(sel, p1, preferred_element_type=F32) + jnp.dot(sel, p2, preferred_element_type=F32)))


def _dot3(a, b_hi, b_lo):
    a_hi, a_lo = _split_bf16(a)
    return (jnp.dot(a_hi, b_hi, preferred_element_type=F32)
            + (jnp.dot(a_hi, b_lo, preferred_element_type=F32) + jnp.dot(a_lo, b_hi, preferred_element_type=F32)))


def _norm_kernel(h_ref, g_ref, o_ref):
    o_ref[...] = _rms(h_ref[...], g_ref[...])


def _norm(h, g, tm):
    t, d = h.shape
    return pl.pallas_call(
        _norm_kernel,
        grid=(t // tm,),
        in_specs=[pl.BlockSpec((tm, d), lambda i: (i, 0)), _const_spec((1, d))],
        out_specs=pl.BlockSpec((tm, d), lambda i: (i, 0)),
        out_shape=jax.ShapeDtypeStruct((t, d), F32),
        compiler_params=_params("parallel"),
        name="rms_norm",
    )(h, g.reshape(1, d))


def _rwkv_proj_kernel(xn_ref, prev_ref, *refs):
    _rwkv_proj_body(xn_ref[...], prev_ref[...], *refs)


def _rwkv_proj_fused_kernel(h_ref, before_ref, shift_ref, gain_ref, *refs, tiles_per_seq):
    *proj_refs, last_ref = refs
    i = pl.program_id(0)
    xn = _rms(h_ref[...], gain_ref[...])
    before = _rms(before_ref[SUBLANES - 1:SUBLANES, :], gain_ref[...])
    first = jnp.where(i % tiles_per_seq == 0, shift_ref[0], before)
    row = lax.broadcasted_iota(jnp.int32, xn.shape, 0)
    prev = jnp.where(row == 0, first, pltpu.roll(xn, 1, axis=0))
    _rwkv_proj_body(xn, prev, *proj_refs)

    @pl.when(i % tiles_per_seq == tiles_per_seq - 1)
    def _():
        last_ref[0] = xn[xn.shape[0] - 1:, :]


def _rwkv_proj_body(xn, prev, mix_ref, vec_ref, wr_ref, wk_ref, wv_ref,
                    w1h_ref, w1l_ref, w2h_ref, w2l_ref, a1h_ref, a1l_ref, a2h_ref, a2l_ref, g1_ref, g2_ref,
                    r_ref, w_ref, k_ref, v_ref, kk_ref, a_ref, g_ref):
    dx = prev - xn
    xr, xw, xk, xv, xa, xg = [xn + dx * mix_ref[j:j + 1, :] for j in range(6)]
    w0, a0, k_k, k_a = [vec_ref[j:j + 1, :] for j in range(4)]
    r_ref[...] = _bdot(xr, wr_ref[...])
    k = _bdot(xk, wk_ref[...])
    v_ref[...] = _bdot(xv, wv_ref[...])
    w_lora = _dot3(jnp.tanh(_dot3(xw, w1h_ref[...], w1l_ref[...])), w2h_ref[...], w2l_ref[...])
    w_log = -_softplus(-(w0 + w_lora)) - 0.5
    w_ref[...] = jnp.exp(-jnp.exp(w_log))
    a = jax.nn.sigmoid(a0 + _dot3(_dot3(xa, a1h_ref[...], a1l_ref[...]), a2h_ref[...], a2l_ref[...]))
    a_ref[...] = a
    g_ref[...] = _bdot(jax.nn.sigmoid(_bdot(xg, g1_ref[...])), g2_ref[...])
    kk_ref[...] = k * k_k
    k_ref[...] = k * (1.0 + (a - 1.0) * k_a)


def _rwkv_proj(xn, prev, p, tm):
    t, d = xn.shape
    tok = pl.BlockSpec((tm, d), lambda i: (i, 0))
    weights = [p["mix"], p["vec"], p["wr"], p["wk"], p["wv"], *p["w1"], *p["w2"], *p["a1"], *p["a2"],
               p["g1"], p["g2"]]
    return pl.pallas_call(
        _rwkv_proj_kernel,
        grid=(t // tm,),
        in_specs=[tok, tok] + [_const_spec(w.shape) for w in weights],
        out_specs=[tok] * 7,
        out_shape=[jax.ShapeDtypeStruct((t, d), F32)] * 7,
        compiler_params=_params("parallel"),
        name="rwkv_proj",
    )(xn, prev, *weights)


def _rwkv_proj_fused(h, gain, shift0, p, tm, length):
    t, d = h.shape
    bsz = t // length
    tiles_per_seq = length // tm
    tok = pl.BlockSpec((tm, d), lambda i: (i, 0))
    before = pl.BlockSpec((SUBLANES, d), lambda i: (jnp.maximum(i * (tm // SUBLANES) - 1, 0), 0))
    per_seq = pl.BlockSpec((1, 1, d), lambda i: (i // tiles_per_seq, 0, 0))
    weights = [p["mix"], p["vec"], p["wr"], p["wk"], p["wv"], *p["w1"], *p["w2"], *p["a1"], *p["a2"],
               p["g1"], p["g2"]]
    kern = functools.partial(_rwkv_proj_fused_kernel, tiles_per_seq=tiles_per_seq)
    *proj, last = pl.pallas_call(
        kern,
        grid=(t // tm,),
        in_specs=[tok, before, per_seq, _const_spec((1, d))] + [_const_spec(w.shape) for w in weights],
        out_specs=[tok] * 7 + [per_seq],
        out_shape=[jax.ShapeDtypeStruct((t, d), F32)] * 7 + [jax.ShapeDtypeStruct((bsz, 1, d), F32)],
        compiler_params=_params("arbitrary"),
        name="rwkv_norm_proj",
    )(h, h, shift0.reshape(bsz, 1, d), gain.reshape(1, d), *weights)
    return proj, last.reshape(bsz, d)


def _rwkv_scan_kernel(r_ref, w_ref, k_ref, v_ref, kk_ref, a_ref, s0_ref, lnw_ref, lnb_ref, rk_ref,
                      o_ref, sfin_ref, state, o_rows):
    tb = pl.program_id(1)
    steps = r_ref.shape[0]

    @pl.when(tb == 0)
    def _():
        state[...] = s0_ref[...]

    def step(t, carry):
        r = r_ref[t]
        w = w_ref[t]
        k = k_ref[t]
        vv = v_ref[t]
        kku = kk_ref[t]
        norm = jnp.sqrt(jnp.sum(kku * kku, axis=0, keepdims=True))
        kk = kku / jnp.maximum(norm, 1e-12)
        alpha = -kk
        beta = kk * a_ref[t]

        def row(vi, c):
            s_v = state[vi]
            u = jnp.sum(s_v * alpha, axis=0, keepdims=True)
            s_new = s_v * w + u * beta + v_ref[t, pl.ds(vi, 1), :] * k
            state[vi] = s_new
            o_rows[pl.ds(vi, 1), :] = jnp.sum(s_new * r, axis=0, keepdims=True)
            return c

        lax.fori_loop(0, RW_HEAD, row, 0, unroll=32)
        o = o_rows[...]
        mu = jnp.mean(o, axis=0, keepdims=True)
        var = jnp.mean(jnp.square(o - mu), axis=0, keepdims=True)
        on = (o - mu) * lax.rsqrt(var + RW_LN_EPS)
        bonus = jnp.sum(r * k * rk_ref[...], axis=0, keepdims=True) * vv
        o_ref[t] = on * lnw_ref[...] + lnb_ref[...] + bonus
        return carry

    lax.fori_loop(0, steps, step, 0)

    @pl.when(tb == pl.num_programs(1) - 1)
    def _():
        sfin_ref[...] = state[...]


def _rwkv_scan(seqs, s0, lnw, lnb, rk, tt):
    length, n, probs = seqs[0].shape
    groups = probs // LANES
    seq_spec = pl.BlockSpec((tt, n, LANES), lambda g, i: (i, 0, g))
    st_spec = pl.BlockSpec((n, n, LANES), lambda g, i: (0, 0, g))
    lane_spec = pl.BlockSpec((n, LANES), lambda g, i: (0, 0))
    return pl.pallas_call(
        _rwkv_scan_kernel,
        grid=(groups, length // tt),
        in_specs=[seq_spec] * 6 + [st_spec] + [lane_spec] * 3,
        out_specs=[seq_spec, st_spec],
        out_shape=[jax.ShapeDtypeStruct((length, n, probs), F32),
                   jax.ShapeDtypeStruct((n, n, probs), F32)],
        scratch_shapes=[pltpu.VMEM((n, n, LANES), F32), pltpu.VMEM((n, LANES), F32)],
        compiler_params=_params("arbitrary", "arbitrary"),
        name="rwkv_scan",
    )(*seqs, s0, lnw, lnb, rk)


def _rwkv_out_kernel(o_ref, g_ref, h_ref, wo_ref, out_ref):
    out_ref[...] = h_ref[...] + _bdot(o_ref[...] * g_ref[...], wo_ref[...])


def _rwkv_out(o, g, h, wo, tm):
    t, d = h.shape
    tok = pl.BlockSpec((tm, d), lambda i: (i, 0))
    return pl.pallas_call(
        _rwkv_out_kernel,
        grid=(t // tm,),
        in_specs=[tok, tok, tok, _const_spec(wo.shape)],
        out_specs=tok,
        out_shape=jax.ShapeDtypeStruct((t, d), F32),
        compiler_params=_params("parallel"),
        name="rwkv_out",
    )(o, g, h, wo)


def _mamba_in_kernel(h_ref, g_ref, w_ref, z_ref, xbc_ref, dt_ref):
    xn = _rms(h_ref[...], g_ref[...]).astype(BF16)
    z_ref[...] = jnp.dot(xn, w_ref[:, :M_D_INNER], preferred_element_type=F32)
    xbc_ref[...] = jnp.dot(xn, w_ref[:, M_D_INNER:M_D_INNER + M_CONV_DIM], preferred_element_type=F32)
    dt_ref[...] = jnp.dot(xn, w_ref[:, M_D_INNER + M_CONV_DIM:], preferred_element_type=F32)


def _mamba_in(h, g, w_in, tm):
    t, d = h.shape
    return pl.pallas_call(
        _mamba_in_kernel,
        grid=(t // tm,),
        in_specs=[pl.BlockSpec((tm, d), lambda i: (i, 0)), _const_spec((1, d)), _const_spec(w_in.shape)],
        out_specs=[pl.BlockSpec((tm, M_D_INNER), lambda i: (i, 0)),
                   pl.BlockSpec((tm, M_CONV_DIM), lambda i: (i, 0)),
                   pl.BlockSpec((tm, M_DT_PAD), lambda i: (i, 0))],
        out_shape=[jax.ShapeDtypeStruct((t, M_D_INNER), F32),
                   jax.ShapeDtypeStruct((t, M_CONV_DIM), F32),
                   jax.ShapeDtypeStruct((t, M_DT_PAD), F32)],
        compiler_params=_params("parallel"),
        name="mamba_in_proj",
    )(h, g.reshape(1, d), w_in)


def _mamba_ssd_kernel(xbc_ref, z_ref, dt_ref, h_ref, conv0_ref, ssm0_ref, convw_ref, vec_ref,
                      dvec_ref, expand_ref, normw_ref, wout_ref,
                      out_ref, ssm_ref,
                      conv_buf, src_adt, src_acs, src_b, src_xdt, src_xdt_st, state, y_buf, *, seq_len, rows):
    b = pl.program_id(0)
    c = pl.program_id(1)
    q = M_CHUNK
    halo = SUBLANES
    d_bc = M_GROUPS * M_STATE

    @pl.when(c == 0)
    def _():
        state[...] = ssm0_ref[0]
        conv_buf[0:halo, :] = conv0_ref[0]

    @pl.when(c > 0)
    def _():
        conv_buf[0:halo, :] = conv_buf[rows:rows + halo, :]

    if rows < q:
        @pl.when((b == 0) & (c == 0))
        def _():
            src_adt[...] = jnp.zeros(src_adt.shape, F32)
            src_acs[...] = jnp.zeros(src_acs.shape, F32)
            src_b[...] = jnp.zeros(src_b.shape, BF16)
            src_xdt[...] = jnp.zeros(src_xdt.shape, BF16)
            src_xdt_st[...] = jnp.zeros(src_xdt_st.shape, BF16)

    conv_buf[halo:halo + rows, :] = xbc_ref[0]
    valid = (lax.broadcasted_iota(jnp.int32, (rows, 1), 0) + c * rows) < seq_len
    conv = vec_ref[0:1, :]
    for j in range(M_CONV):
        conv = conv + conv_buf[halo - (M_CONV - 1) + j:halo - (M_CONV - 1) + j + rows, :] * convw_ref[j:j + 1, :]
    act = jnp.where(valid, jax.nn.silu(conv), 0.0)
    xs = act[:, :M_D_INNER]
    dt = jnp.where(valid, _softplus(dt_ref[0] + dvec_ref[0:1, :]), 0.0)
    src_adt[0:rows, :] = dt * dvec_ref[1:2, :]
    li = lax.broadcasted_iota(jnp.int32, (rows, q), 0)
    si = lax.broadcasted_iota(jnp.int32, (rows, q), 1)
    causal = li >= si
    acs = _select_dot(causal.astype(BF16), src_adt[...])
    src_acs[0:rows, :] = acs
    acs_t = src_acs[...].T
    expand = expand_ref[...]
    dt_full = _dot_select(dt, expand)
    acs_full = _dot_select(acs, expand)
    last = acs[rows - 1:rows, :]
    last_full = acs_full[rows - 1:rows, :]
    xdt = xs * dt_full
    src_xdt[0:rows, :] = xdt.astype(BF16)
    src_xdt_st[0:rows, :] = (xdt * jnp.exp(last_full - acs_full)).astype(BF16)
    src_b[0:rows, :] = act[:, M_D_INNER:M_D_INNER + d_bc].astype(BF16)
    eacs_full = jnp.exp(acs_full)
    state_decay = jnp.exp(last)
    width = M_HPG * M_HEADDIM

    for g in range(M_GROUPS):
        cols = slice(g * width, (g + 1) * width)
        b_g = src_b[:, g * M_STATE:(g + 1) * M_STATE]
        c_g = act[:, M_D_INNER + d_bc + g * M_STATE:M_D_INNER + d_bc + (g + 1) * M_STATE].astype(BF16)
        cb = lax.dot_general(c_g, b_g, (((1,), (1,)), ((), ())), preferred_element_type=F32)
        h_g = state[g * M_HPG:(g + 1) * M_HPG].reshape(width, M_STATE)
        y_off = lax.dot_general(c_g, h_g.astype(BF16), (((1,), (1,)), ((), ())),
                                preferred_element_type=F32)
        upd = lax.dot_general(src_xdt_st[:, cols], b_g, (((0,), (0,)), ((), ())),
                              preferred_element_type=F32)
        ys = []
        for r in range(g * M_HPG, (g + 1) * M_HPG):
            lo = r * M_HEADDIM
            decay = jnp.where(causal, jnp.exp(acs[:, r:r + 1] - acs_t[r:r + 1, :]), 0.0)
            ys.append(jnp.dot((cb * decay).astype(BF16), src_xdt[:, lo:lo + M_HEADDIM],
                              preferred_element_type=F32))
            sub = slice((r - g * M_HPG) * M_HEADDIM, (r - g * M_HPG + 1) * M_HEADDIM)
            state[r] = state[r] * state_decay[:, r:r + 1] + upd[sub, :]
        y_buf[:, cols] = jnp.concatenate(ys, axis=1) + y_off * eacs_full[:, cols]

    y = y_buf[...] + vec_ref[1:2, :M_D_INNER] * xs
    yg = y * jax.nn.silu(z_ref[0])
    parts = []
    for g in range(M_GROUPS):
        part = yg[:, g * width:(g + 1) * width]
        parts.append(part * lax.rsqrt(jnp.mean(part * part, axis=-1, keepdims=True) + NORM_EPS))
    yn = jnp.concatenate(parts, axis=1) * normw_ref[...]
    out_ref[0] = h_ref[0] + _bdot(yn, wout_ref[...])

    @pl.when(c == pl.num_programs(1) - 1)
    def _():
        ssm_ref[0] = state[...]


def _mamba_ssd(xbc, z, dt, h, conv0, ssm0, p, rows, seq_len):
    bsz, length, _ = xbc.shape
    chunks = pl.cdiv(length, rows)
    weights = [p["conv_w"], p["vec"], p["dvec"], p["expand"], p["norm_w"], p["w_out"]]

    def seq_spec(width):
        return pl.BlockSpec((1, rows, width), lambda b, c: (b, c, 0))

    kern = functools.partial(_mamba_ssd_kernel, seq_len=seq_len, rows=rows)
    return pl.pallas_call(
        kern,
        grid=(bsz, chunks),
        in_specs=[seq_spec(M_CONV_DIM), seq_spec(M_D_INNER), seq_spec(M_DT_PAD), seq_spec(D_MODEL),
                  pl.BlockSpec((1, SUBLANES, M_CONV_DIM), lambda b, c: (b, 0, 0)),
                  pl.BlockSpec((1, M_HEADS, M_HEADDIM, M_STATE), lambda b, c: (b, 0, 0, 0))]
                 + [_const_spec(w.shape) for w in weights],
        out_specs=[seq_spec(D_MODEL),
                   pl.BlockSpec((1, M_HEADS, M_HEADDIM, M_STATE), lambda b, c: (b, 0, 0, 0))],
        out_shape=[jax.ShapeDtypeStruct((bsz, length, D_MODEL), F32),
                   jax.ShapeDtypeStruct((bsz, M_HEADS, M_HEADDIM, M_STATE), F32)],
        scratch_shapes=[pltpu.VMEM((rows + SUBLANES, M_CONV_DIM), F32),
                        pltpu.VMEM((M_CHUNK, M_DT_PAD), F32),
                        pltpu.VMEM((M_CHUNK, M_DT_PAD), F32),
                        pltpu.VMEM((M_CHUNK, M_GROUPS * M_STATE), BF16),
                        pltpu.VMEM((M_CHUNK, M_D_INNER), BF16),
                        pltpu.VMEM((M_CHUNK, M_D_INNER), BF16),
                        pltpu.VMEM((M_HEADS, M_HEADDIM, M_STATE), F32),
                        pltpu.VMEM((rows, M_D_INNER), F32)],
        compiler_params=_params("arbitrary", "arbitrary"),
        name="mamba_ssd",
    )(xbc, z, dt, h, conv0, ssm0, *weights)


_CANDS = [(ra, rb) for ra in range(P_TOPK) for rb in range(P_TOPK) if (ra + 1) * (rb + 1) <= P_TOPK]


def _peer_route_kernel(h_ref, g_ref, wq_ref, keys_ref, keys_lo_ref,
                       xt_ref, r1_ref, e1_ref, m0_ref, w0_ref,
                       q_buf, s_buf, code0, code_book, best_vals, cand, counts, inv_z):
    tm = h_ref.shape[0]
    neg_inf = float("-inf")
    xn = _rms(h_ref[...], g_ref[...])
    xt = xn.T.astype(BF16)
    xt_ref[...] = xt
    q_buf[...] = jnp.dot(wq_ref[...], xt, preferred_element_type=F32)
    key_iota = lax.broadcasted_iota(jnp.int32, (P_NKEYS, tm), 0).astype(F32)

    def half(m, carry):
        head = m // 2
        z = m % 2
        row0 = pl.multiple_of(m * P_NKEYS, P_NKEYS)
        qm = q_buf[pl.ds(row0, P_NKEYS), :]
        q_hi = qm.astype(BF16)
        q_lo = (qm - q_hi.astype(F32)).astype(BF16)
        k_hi = keys_ref[m]
        s = (jnp.dot(k_hi, q_hi, preferred_element_type=F32)
             + (jnp.dot(k_hi, q_lo, preferred_element_type=F32)
                + jnp.dot(keys_lo_ref[m], q_hi, preferred_element_type=F32)))

        s_buf[...] = s

        def next_distinct(j, prev):
            sj = s_buf[...]
            best = jnp.max(jnp.where(sj < prev, sj, neg_inf), axis=0, keepdims=True)
            best_vals[z, j, pl.ds(head, 1), :] = best
            return best

        def extract_ties(j, sc):
            cur, rank = sc
            best = jnp.max(cur, axis=0, keepdims=True)
            first = jnp.min(jnp.where(cur == best, key_iota, float(P_NKEYS)), axis=0, keepdims=True)
            sel = key_iota == first
            best_vals[z, j, pl.ds(head, 1), :] = best
            return jnp.where(sel, neg_inf, cur), jnp.where(sel, jnp.asarray(j, dtype=F32), rank)

        last = lax.fori_loop(0, P_TOPK, next_distinct, jnp.full((1, tm), float("inf"), F32))
        reached = jnp.sum(jnp.where(s >= last, 1.0, 0.0), axis=0, keepdims=True)
        has_ties = jnp.max(jnp.abs(reached - float(P_TOPK))) > 0.0

        def rank_by_count():
            rank = jnp.zeros((P_NKEYS, tm), F32)
            for jj in range(P_TOPK):
                rank = rank + jnp.where(best_vals[z, jj, pl.ds(head, 1), :] > s, 1.0, 0.0)
            return rank

        def rank_with_ties():
            no_rank = jnp.full((P_NKEYS, tm), float(P_NKEYS), F32)
            return lax.fori_loop(0, P_TOPK, extract_ties, (s, no_rank))[1]

        e = jnp.exp(s - jnp.max(s, axis=0, keepdims=True))

        @pl.when(z == 0)
        def _():
            w0_ref[head] = e

            @pl.when(has_ties)
            def _():
                code0[head] = rank_with_ties()
                for ra in range(P_TOPK):
                    code_book[ra, pl.ds(head, 1), :] = jnp.full((1, tm), float(ra), F32)

            @pl.when(jnp.logical_not(has_ties))
            def _():
                code0[head] = s
                for ra in range(P_TOPK):
                    code_book[ra, pl.ds(head, 1), :] = best_vals[0, ra, pl.ds(head, 1), :]

        @pl.when(z == 1)
        def _():
            rank = lax.cond(has_ties, rank_with_ties, rank_by_count)
            for lt in range(tm // LANES):
                r1_ref[head, lt] = rank[:, lt * LANES:(lt + 1) * LANES].astype(BF16)
                e1_ref[head, lt] = e[:, lt * LANES:(lt + 1) * LANES].astype(BF16)

        return carry

    lax.fori_loop(0, 2 * P_HEADS, half, 0)

    ex0 = [jnp.exp(best_vals[0, ra] - best_vals[0, 0]) for ra in range(P_TOPK)]
    ex1 = [jnp.exp(best_vals[1, rb] - best_vals[1, 0]) for rb in range(P_TOPK)]
    for i, (ra, rb) in enumerate(_CANDS):
        cand[i] = best_vals[0, ra] + best_vals[1, rb]

    def next_sum(it, prev):
        best = None
        for i in range(len(_CANDS)):
            ci = cand[i]
            below = jnp.where(ci < prev, ci, neg_inf)
            best = below if best is None else jnp.maximum(best, below)
        return best

    last_sum = lax.fori_loop(0, P_TOPK, next_sum, jnp.full((P_HEADS, tm), float("inf"), F32))
    hits = [jnp.where(cand[i] >= last_sum, 1.0, 0.0) for i in range(len(_CANDS))]
    n_hits = functools.reduce(lambda x, y: x + y, hits)
    sums_tie = jnp.max(jnp.abs(n_hits - float(P_TOPK))) > 0.0

    def select_distinct():
        z = jnp.zeros((P_HEADS, tm), F32)
        per_rank = [None] * P_TOPK
        for i, (ra, rb) in enumerate(_CANDS):
            per_rank[ra] = hits[i] if per_rank[ra] is None else per_rank[ra] + hits[i]
            z = z + hits[i] * (ex0[ra] * ex1[rb])
        for ra in range(P_TOPK):
            counts[ra] = per_rank[ra]
        return z

    def select_with_ties():
        counts[...] = jnp.zeros(counts.shape, F32)

        def pick(it, z):
            best = cand[0]
            for i in range(1, len(_CANDS)):
                best = jnp.maximum(best, cand[i])
            found = jnp.zeros((P_HEADS, tm), F32)
            for i, (ra, rb) in enumerate(_CANDS):
                ci = cand[i]
                hit = jnp.where(ci == best, 1.0, 0.0) * (1.0 - found)
                found = found + hit
                cand[i] = jnp.where(hit > 0.0, neg_inf, ci)
                counts[ra] = counts[ra] + hit
                z = z + hit * (ex0[ra] * ex1[rb])
            return z

        return lax.fori_loop(0, P_TOPK, pick, jnp.zeros((P_HEADS, tm), F32))

    inv_z[...] = 1.0 / lax.cond(sums_tie, select_with_ties, select_distinct)

    def finish(head, carry):
        code = code0[head]
        m0 = jnp.zeros((P_NKEYS, tm), F32)
        for ra in range(P_TOPK):
            m0 = jnp.where(code == code_book[ra, pl.ds(head, 1), :], counts[ra, pl.ds(head, 1), :], m0)
        m0_ref[head] = m0
        w0_ref[head] = w0_ref[head] * inv_z[pl.ds(head, 1), :]
        return carry

    lax.fori_loop(0, P_HEADS, finish, 0)


def _peer_route(h, g, wq_t, keys, keys_lo, tm):
    t, d = h.shape
    tiles = pl.cdiv(t, tm)
    t_pad = tiles * tm
    tile = pl.BlockSpec((P_HEADS, P_NKEYS, tm), lambda i: (0, 0, i))
    tile_f32 = jax.ShapeDtypeStruct((P_HEADS, P_NKEYS, t_pad), F32)
    tile_bf16 = jax.ShapeDtypeStruct((P_HEADS, t_pad // LANES, P_NKEYS, LANES), BF16)
    lane_tiles = pl.BlockSpec((P_HEADS, tm // LANES, P_NKEYS, LANES), lambda i: (0, i, 0, 0))
    return pl.pallas_call(
        _peer_route_kernel,
        grid=(tiles,),
        in_specs=[pl.BlockSpec((tm, d), lambda i: (i, 0)), _const_spec((1, d)),
                  _const_spec(wq_t.shape), _const_spec(keys.shape), _const_spec(keys_lo.shape)],
        out_specs=[pl.BlockSpec((d, tm), lambda i: (0, i)), lane_tiles, lane_tiles, tile, tile],
        out_shape=[jax.ShapeDtypeStruct((d, t_pad), BF16), tile_bf16, tile_bf16, tile_f32, tile_f32],
        scratch_shapes=[pltpu.VMEM((P_HEADS * P_QDIM, tm), F32),
                        pltpu.VMEM((P_NKEYS, tm), F32),
                        pltpu.VMEM((P_HEADS, P_NKEYS, tm), F32),
                        pltpu.VMEM((P_TOPK, P_HEADS, tm), F32),
                        pltpu.VMEM((2, P_TOPK, P_HEADS, tm), F32),
                        pltpu.VMEM((len(_CANDS), P_HEADS, tm), F32),
                        pltpu.VMEM((P_TOPK, P_HEADS, tm), F32),
                        pltpu.VMEM((P_HEADS, tm), F32)],
        compiler_params=_params("parallel"),
        name="peer_route",
    )(h, g.reshape(1, d), wq_t, keys, keys_lo)


def _gelu(x):
    return 0.5 * x * (1.0 + lax.erf(x * (1.0 / math.sqrt(2.0))))


def _peer_dense_kernel(xt_ref, u_ref, vt_ref, r1_ref, e1_ref, m0_ref, w0_ref, h_ref, gfin_ref,
                       out_ref, acc, hid, prob, *, final_norm):
    j = pl.program_id(1)
    keys_per_step = P_EXPERT_TILE // P_NKEYS

    @pl.when(j == 0)
    def _():
        acc[...] = jnp.zeros(acc.shape, F32)

    tm = xt_ref.shape[1]
    hid[...] = jnp.dot(u_ref[...], xt_ref[...], preferred_element_type=F32)
    zero = jnp.zeros((), BF16)
    for ii in range(keys_per_step):
        i1 = j * keys_per_step + ii
        rows = slice(ii * P_NKEYS, (ii + 1) * P_NKEYS)
        m0_rows = [m0_ref[head, pl.ds(i1, 1), :] for head in range(P_HEADS)]
        w0_rows = [w0_ref[head, pl.ds(i1, 1), :] for head in range(P_HEADS)]

        def packed_rows(row):
            one = jnp.broadcast_to(row, (2 * SUBLANES, LANES)).astype(BF16)
            return jnp.tile(one, (P_NKEYS // (2 * SUBLANES), 1))

        for lt in range(tm // LANES):
            lanes = slice(lt * LANES, (lt + 1) * LANES)
            gate = None
            for head in range(P_HEADS):
                m0 = packed_rows(m0_rows[head][:, lanes])
                w0 = packed_rows(w0_rows[head][:, lanes])
                term = w0 * jnp.where(r1_ref[head, lt] < m0, e1_ref[head, lt], zero)
                gate = term if gate is None else gate + term
            prob[rows, lanes] = gate * _gelu(hid[rows, lanes]).astype(BF16)
    acc[...] += jnp.dot(vt_ref[...], prob[...], preferred_element_type=F32)

    @pl.when(j == pl.num_programs(1) - 1)
    def _():
        res = h_ref[...] + acc[...].T
        if final_norm:
            res = _rms(res, gfin_ref[...])
        out_ref[...] = res


def _peer_dense(xt, u_bf, vt_bf, r1, e1, m0, w0, h, g_final, tm, final_norm):
    t, d = h.shape
    n_exp = u_bf.shape[0]
    tile = pl.BlockSpec((P_HEADS, P_NKEYS, tm), lambda i, j: (0, 0, i))
    lane_tiles = pl.BlockSpec((P_HEADS, tm // LANES, P_NKEYS, LANES), lambda i, j: (0, i, 0, 0))
    kern = functools.partial(_peer_dense_kernel, final_norm=final_norm)
    return pl.pallas_call(
        kern,
        grid=(pl.cdiv(t, tm), n_exp // P_EXPERT_TILE),
        in_specs=[pl.BlockSpec((d, tm), lambda i, j: (0, i)),
                  pl.BlockSpec((P_EXPERT_TILE, d), lambda i, j: (j, 0)),
                  pl.BlockSpec((d, P_EXPERT_TILE), lambda i, j: (0, j)),
                  lane_tiles, lane_tiles, tile, tile,
                  pl.BlockSpec((tm, d), lambda i, j: (i, 0)),
                  _const_spec((1, d))],
        out_specs=pl.BlockSpec((tm, d), lambda i, j: (i, 0)),
        out_shape=jax.ShapeDtypeStruct((t, d), F32),
        scratch_shapes=[pltpu.VMEM((d, tm), F32),
                        pltpu.VMEM((P_EXPERT_TILE, tm), F32),
                        pltpu.VMEM((P_EXPERT_TILE, tm), BF16)],
        compiler_params=_params("parallel", "arbitrary"),
        name="peer_dense",
    )(xt, u_bf, vt_bf, r1, e1, m0, w0, h, g_final.reshape(1, d))


def _pad_cols(w, n):
    return jnp.pad(w, ((0, 0), (0, n - w.shape[1])))


def _pad_rows(w, n):
    return jnp.pad(w, ((0, n - w.shape[0]), (0, 0)))


def _prep_rwkv(mix, w_rkv, w0, w1, w2, a0, a1, a2, g1, g2, k_k, k_a, r_k, ln_w, ln_b, w_o):
    d = D_MODEL
    lane = lambda x: jnp.repeat(x.reshape(RW_HEADS, RW_HEAD).T, LANES // RW_HEADS, axis=1)
    return {
        "mix": _pad_rows(mix, SUBLANES),
        "vec": _pad_rows(jnp.stack([w0, a0, k_k, k_a]), SUBLANES),
        "wr": w_rkv[0].astype(BF16), "wk": w_rkv[1].astype(BF16), "wv": w_rkv[2].astype(BF16),
        "w1": _split_bf16(_pad_cols(w1, LORA_PAD)), "w2": _split_bf16(_pad_rows(w2, LORA_PAD)),
        "a1": _split_bf16(_pad_cols(a1, LORA_PAD)), "a2": _split_bf16(_pad_rows(a2, LORA_PAD)),
        "g1": _pad_cols(g1, GATE_LORA_PAD).astype(BF16), "g2": _pad_rows(g2, GATE_LORA_PAD).astype(BF16),
        "lnw": lane(ln_w), "lnb": lane(ln_b), "rk": lane(r_k.reshape(d)),
        "wo": w_o.astype(BF16),
    }


def _prep_mamba(in_proj, conv_w, conv_b, dt_bias, a_log, d_skip, norm_w, out_proj):
    head_of_channel = jnp.arange(M_D_INNER) // M_HEADDIM
    expand = (jnp.arange(M_DT_PAD)[:, None] == head_of_channel[None, :]).astype(BF16)
    d_full = jnp.pad(jnp.repeat(d_skip, M_HEADDIM), (0, M_CONV_DIM - M_D_INNER))
    pad_h = lambda x: jnp.pad(x, (0, M_DT_PAD - M_HEADS))
    return {
        "w_in": _pad_cols(in_proj, M_IN_DIM - M_HEADS + M_DT_PAD).astype(BF16),
        "conv_w": _pad_rows(conv_w, SUBLANES),
        "vec": _pad_rows(jnp.stack([conv_b, d_full]), SUBLANES),
        "dvec": _pad_rows(jnp.stack([pad_h(dt_bias), pad_h(-jnp.exp(a_log))]), SUBLANES),
        "expand": expand,
        "norm_w": norm_w.reshape(1, M_D_INNER),
        "w_out": out_proj.astype(BF16),
    }


def _prep_peer(w_q, sub_keys, u_tab, v_tab):
    keys = jnp.transpose(sub_keys, (1, 0, 2, 3)).reshape(2 * P_HEADS, P_NKEYS, P_QDIM // 2)
    keys_hi, keys_lo = _split_bf16(keys)
    return {"wq_t": w_q.T.astype(BF16), "keys": keys_hi, "keys_lo": keys_lo,
            "u": u_tab.astype(BF16), "vt": v_tab.T.astype(BF16)}


SCAN_GROUP = LANES // RW_HEADS
RELAYOUT_STEPS = 128
RELAYOUT_PITCH = RW_HEAD + SUBLANES


def _to_scan_kernel(x_ref, o_ref, stage):
    for b in range(SCAN_GROUP):
        for hp in range(RW_HEADS // 2):
            t = x_ref[b, :, hp * LANES:(hp + 1) * LANES].T
            for hl in range(2):
                lane = (2 * hp + hl) * SCAN_GROUP + b
                stage[lane * RELAYOUT_PITCH:lane * RELAYOUT_PITCH + RW_HEAD, :] = t[hl * RW_HEAD:(hl + 1) * RW_HEAD, :]
    for k in range(RW_HEAD):
        by_lane = stage[pl.ds(k, LANES, stride=RELAYOUT_PITCH), :]
        o_ref[pl.ds(k, RELAYOUT_STEPS, stride=RW_HEAD), :] = by_lane.T


def _to_scan_pallas(x, length):
    x3 = x.reshape(SCAN_GROUP, length, D_MODEL)
    out = pl.pallas_call(
        _to_scan_kernel,
        grid=(pl.cdiv(length, RELAYOUT_STEPS),),
        in_specs=[pl.BlockSpec((SCAN_GROUP, RELAYOUT_STEPS, D_MODEL), lambda i: (0, i, 0))],
        out_specs=pl.BlockSpec((RELAYOUT_STEPS * RW_HEAD, LANES), lambda i: (i, 0)),
        out_shape=jax.ShapeDtypeStruct((length * RW_HEAD, LANES), F32),
        scratch_shapes=[pltpu.VMEM((LANES * RELAYOUT_PITCH, LANES), F32)],
        compiler_params=_params("parallel"),
        name="to_scan_layout",
    )(x3)
    return out.reshape(length, RW_HEAD, LANES)


def _to_scan(x, bsz, length):
    if bsz == SCAN_GROUP and length >= RELAYOUT_STEPS:
        return _to_scan_pallas(x, length)
    x = x.reshape(bsz // SCAN_GROUP, SCAN_GROUP, length, RW_HEADS, RW_HEAD)
    return jnp.transpose(x, (2, 4, 0, 3, 1)).reshape(length, RW_HEAD, bsz * RW_HEADS)


def _from_scan(x, bsz, length):
    x = x.reshape(length, RW_HEAD, bsz // SCAN_GROUP, RW_HEADS, SCAN_GROUP)
    return jnp.transpose(x, (2, 4, 0, 3, 1)).reshape(bsz * length, D_MODEL)


def _state_to_scan(wkv, bsz):
    x = wkv.reshape(bsz // SCAN_GROUP, SCAN_GROUP, RW_HEADS, RW_HEAD, RW_HEAD)
    return jnp.transpose(x, (3, 4, 0, 2, 1)).reshape(RW_HEAD, RW_HEAD, bsz * RW_HEADS)


def _state_from_scan(s, bsz):
    x = s.reshape(RW_HEAD, RW_HEAD, bsz // SCAN_GROUP, RW_HEADS, SCAN_GROUP)
    return jnp.transpose(x, (2, 4, 3, 0, 1)).reshape(bsz, RW_HEADS, RW_HEAD, RW_HEAD)


def _rwkv_layer(h, g_mix, shift0, wkv0, p, bsz, length, tm, tt):
    if length % tm == 0:
        (r, w, k, v, kk, a, g), shift = _rwkv_proj_fused(h, g_mix, shift0, p, tm, length)
    else:
        xn = _norm(h, g_mix, tm)
        xn3 = xn.reshape(bsz, length, D_MODEL)
        prev = jnp.concatenate([shift0[:, None, :], xn3[:, :-1]], axis=1).reshape(bsz * length, D_MODEL)
        r, w, k, v, kk, a, g = _rwkv_proj(xn, prev, p, tm)
        shift = xn3[:, -1]
    seqs = [_to_scan(x, bsz, length) for x in (r, w, k, v, kk, a)]
    o, s_fin = _rwkv_scan(seqs, _state_to_scan(wkv0, bsz), p["lnw"], p["lnb"], p["rk"], tt)
    o = _from_scan(o, bsz, length)
    return _rwkv_out(o, g, h, p["wo"], tm), shift, _state_from_scan(s_fin, bsz)


def _mamba_layer(h, g_mix, conv0, ssm0, p, bsz, length, tm, rows):
    z, xbc, dt = _mamba_in(h, g_mix, p["w_in"], tm)
    xbc3 = xbc.reshape(bsz, length, M_CONV_DIM)
    tail = M_CONV - 1
    conv_new = xbc3[:, length - tail:] if length >= tail else jnp.concatenate([conv0, xbc3], axis=1)[:, length:]
    pad_t = (-length) % SUBLANES if length < rows else 0

    def seq(x, c):
        x = x.reshape(bsz, length, c)
        return jnp.pad(x, ((0, 0), (0, pad_t), (0, 0))) if pad_t else x

    conv0_p = jnp.pad(conv0, ((0, 0), (SUBLANES - (M_CONV - 1), 0), (0, 0)))
    out, ssm = _mamba_ssd(seq(xbc, M_CONV_DIM), seq(z, M_D_INNER), seq(dt, M_DT_PAD), seq(h, D_MODEL),
                          conv0_p, ssm0, p, min(rows, length + pad_t), length)
    if pad_t:
        out = out[:, :length]
    return out.reshape(bsz * length, D_MODEL), conv_new, ssm


def _peer_layer(h, g_ffn, p, tm, g_final=None):
    xt, r1, e1, m0, w0 = _peer_route(h, g_ffn, p["wq_t"], p["keys"], p["keys_lo"], tm)
    final = g_final is not None
    return _peer_dense(xt, p["u"], p["vt"], r1, e1, m0, w0, h, g_final if final else g_ffn, tm, final)


def _trunk(h, shift0, wkv0, conv0, ssm0, norm_mix, norm_ffn, norm_final, rwkv_p, mamba_p, peer_p,
           bsz, length, tm, tt, peer_tm, rows):
    h, shift, wkv = _rwkv_layer(h, norm_mix[0], shift0, wkv0, rwkv_p, bsz, length, tm, tt)
    h = _peer_layer(h, norm_ffn[0], peer_p[0], peer_tm)
    h, conv, ssm = _mamba_layer(h, norm_mix[1], conv0, ssm0, mamba_p, bsz, length, tm, rows)
    y = _peer_layer(h, norm_ffn[1], peer_p[1], peer_tm, g_final=norm_final)
    return y.reshape(bsz, length, D_MODEL), shift[None], wkv[None], conv[None], ssm[None]


def kernel(x_prompt, x_sample, state_rwkv_shift, state_rwkv_wkv, state_mamba_conv, state_mamba_ssm, meta_tokens, norm_mix, norm_ffn, norm_final, rwkv_mix, rwkv_w_rkv, rwkv_w0, rwkv_w1, rwkv_w2, rwkv_a0, rwkv_a1, rwkv_a2, rwkv_g1, rwkv_g2, rwkv_k_k, rwkv_k_a, rwkv_r_k, rwkv_ln_w, rwkv_ln_b, rwkv_w_o, mamba_in_proj, mamba_conv_w, mamba_conv_b, mamba_dt_bias, mamba_a_log, mamba_d, mamba_norm_w, mamba_out_proj, peer_w_q, peer_sub_keys, peer_u, peer_v):
    rwkv_p = _prep_rwkv(rwkv_mix[0], rwkv_w_rkv[0], rwkv_w0[0], rwkv_w1[0], rwkv_w2[0], rwkv_a0[0],
                        rwkv_a1[0], rwkv_a2[0], rwkv_g1[0], rwkv_g2[0], rwkv_k_k[0], rwkv_k_a[0],
                        rwkv_r_k[0], rwkv_ln_w[0], rwkv_ln_b[0], rwkv_w_o[0])
    mamba_p = _prep_mamba(mamba_in_proj[0], mamba_conv_w[0], mamba_conv_b[0], mamba_dt_bias[0],
                          mamba_a_log[0], mamba_d[0], mamba_norm_w[0], mamba_out_proj[0])
    peer_p = [_prep_peer(peer_w_q[i], peer_sub_keys[i], peer_u[i], peer_v[i]) for i in range(2)]

    bp, lp = x_prompt.shape[0], x_prompt.shape[1] + N_META
    meta = jnp.broadcast_to(meta_tokens[None], (bp, N_META, D_MODEL))
    hp = jnp.concatenate([meta, x_prompt], axis=1).reshape(bp * lp, D_MODEL)
    zeros = lambda *s: jnp.zeros(s, F32)
    yp, p_shift, p_wkv, p_conv, p_ssm = _trunk(
        hp, zeros(bp, D_MODEL), zeros(bp, RW_HEADS, RW_HEAD, RW_HEAD), zeros(bp, M_CONV - 1, M_CONV_DIM),
        zeros(bp, M_HEADS, M_HEADDIM, M_STATE), norm_mix, norm_ffn, norm_final, rwkv_p, mamba_p, peer_p,
        bp, lp, tm=344, tt=48, peer_tm=512, rows=M_CHUNK)

    bs, ls = x_sample.shape[0], x_sample.shape[1]
    ys, s_shift, s_wkv, s_conv, s_ssm = _trunk(
        x_sample.reshape(bs * ls, D_MODEL), state_rwkv_shift[0], state_rwkv_wkv[0], state_mamba_conv[0],
        state_mamba_ssm[0], norm_mix, norm_ffn, norm_final, rwkv_p, mamba_p, peer_p,
        bs, ls, tm=256, tt=ls, peer_tm=512, rows=M_CHUNK)
    return (yp[:, N_META:], ys, p_shift, p_wkv, p_conv, p_ssm, s_shift, s_wkv, s_conv, s_ssm)
```

```python
import functools
import math

import jax
import jax.numpy as jnp
from jax import lax
from jax.experimental import pallas as pl
from jax.experimental.pallas import tpu as pltpu

F32 = jnp.float32
BF16 = jnp.bfloat16

D_MODEL = 1024
N_META = 16
NORM_EPS = 1e-5
RW_HEAD = 64
RW_HEADS = D_MODEL // RW_HEAD
RW_LN_EPS = 64e-5
LORA_PAD = 128
GATE_LORA_PAD = 256
M_D_INNER = 2048
M_HEADDIM = 64
M_HEADS = M_D_INNER // M_HEADDIM
M_GROUPS = 4
M_HPG = M_HEADS // M_GROUPS
M_STATE = 128
M_CONV = 4
M_CONV_DIM = M_D_INNER + 2 * M_GROUPS * M_STATE
M_IN_DIM = 2 * M_D_INNER + 2 * M_GROUPS * M_STATE + M_HEADS
M_DT_PAD = 128
M_CHUNK = 128
P_HEADS = 8
P_NKEYS = 128
P_QDIM = 256
P_TOPK = 16
P_EXPERT_TILE = 2048
P_EXPERT_SUB = 256

LANES = 128
SUBLANES = 8
VMEM_LIMIT_BYTES = 56 * 1024 * 1024


def _params(*semantics):
    return pltpu.CompilerParams(dimension_semantics=semantics, vmem_limit_bytes=VMEM_LIMIT_BYTES)


def _const_spec(shape):
    zeros = (0,) * len(shape)
    return pl.BlockSpec(shape, lambda *_: zeros)


def _rms(x, g):
    return x * lax.rsqrt(jnp.mean(x * x, axis=-1, keepdims=True) + NORM_EPS) * g


def _softplus(x):
    return jnp.maximum(x, 0.0) + jnp.log1p(jnp.exp(-jnp.abs(x)))


def _bdot(a, b):
    return jnp.dot(a.astype(BF16), b.astype(BF16), preferred_element_type=F32)


def _split_bf16(w):
    hi = w.astype(BF16)
    return hi, (w - hi.astype(F32)).astype(BF16)


def _split3_bf16(x):
    p0 = x.astype(BF16)
    r0 = x - p0.astype(F32)
    p1 = r0.astype(BF16)
    p2 = (r0 - p1.astype(F32)).astype(BF16)
    return p0, p1, p2


def _dot_select(a, sel):
    p0, p1, p2 = _split3_bf16(a)
    return (jnp.dot(p0, sel, preferred_element_type=F32)
            + (jnp.dot(p1, sel, preferred_element_type=F32) + jnp.dot(p2, sel, preferred_element_type=F32)))


def _select_dot(sel, b):
    p0, p1, p2 = _split3_bf16(b)
    return (jnp.dot(sel, p0, preferred_element_type=F32)
            + (jnp.dot(sel, p1, preferred_element_type=F32) + jnp.dot(sel, p2, preferred_element_type=F32)))


def _dot3(a, b_hi, b_lo):
    a_hi, a_lo = _split_bf16(a)
    return (jnp.dot(a_hi, b_hi, preferred_element_type=F32)
            + (jnp.dot(a_hi, b_lo, preferred_element_type=F32) + jnp.dot(a_lo, b_hi, preferred_element_type=F32)))


def _norm_kernel(h_ref, g_ref, o_ref):
    o_ref[...] = _rms(h_ref[...], g_ref[...])


def _norm(h, g, tm):
    t, d = h.shape
    return pl.pallas_call(
        _norm_kernel,
        grid=(t // tm,),
        in_specs=[pl.BlockSpec((tm, d), lambda i: (i, 0)), _const_spec((1, d))],
        out_specs=pl.BlockSpec((tm, d), lambda i: (i, 0)),
        out_shape=jax.ShapeDtypeStruct((t, d), F32),
        compiler_params=_params("parallel"),
        name="rms_norm",
    )(h, g.reshape(1, d))


def _rwkv_proj_kernel(xn_ref, prev_ref, *refs):
    _rwkv_proj_body(xn_ref[...], prev_ref[...], *refs)


def _rwkv_proj_fused_kernel(h_ref, before_ref, shift_ref, gain_ref, *refs, tiles_per_seq):
    *proj_refs, last_ref = refs
    i = pl.program_id(0)
    xn = _rms(h_ref[...], gain_ref[...])
    before = _rms(before_ref[SUBLANES - 1:SUBLANES, :], gain_ref[...])
    first = jnp.where(i % tiles_per_seq == 0, shift_ref[0], before)
    row = lax.broadcasted_iota(jnp.int32, xn.shape, 0)
    prev = jnp.where(row == 0, first, pltpu.roll(xn, 1, axis=0))
    _rwkv_proj_body(xn, prev, *proj_refs)

    @pl.when(i % tiles_per_seq == tiles_per_seq - 1)
    def _():
        last_ref[0] = xn[xn.shape[0] - 1:, :]


def _rwkv_proj_body(xn, prev, mix_ref, vec_ref, wr_ref, wk_ref, wv_ref,
                    w1h_ref, w1l_ref, w2h_ref, w2l_ref, a1h_ref, a1l_ref, a2h_ref, a2l_ref, g1_ref, g2_ref,
                    r_ref, w_ref, k_ref, v_ref, kk_ref, a_ref, g_ref):
    dx = prev - xn
    xr, xw, xk, xv, xa, xg = [xn + dx * mix_ref[j:j + 1, :] for j in range(6)]
    w0, a0, k_k, k_a = [vec_ref[j:j + 1, :] for j in range(4)]
    r_ref[...] = _bdot(xr, wr_ref[...])
    k = _bdot(xk, wk_ref[...])
    v_ref[...] = _bdot(xv, wv_ref[...])
    w_lora = _dot3(jnp.tanh(_dot3(xw, w1h_ref[...], w1l_ref[...])), w2h_ref[...], w2l_ref[...])
    w_log = -_softplus(-(w0 + w_lora)) - 0.5
    w_ref[...] = jnp.exp(-jnp.exp(w_log))
    a = jax.nn.sigmoid(a0 + _dot3(_dot3(xa, a1h_ref[...], a1l_ref[...]), a2h_ref[...], a2l_ref[...]))
    a_ref[...] = a
    g_ref[...] = _bdot(jax.nn.sigmoid(_bdot(xg, g1_ref[...])), g2_ref[...])
    kk_ref[...] = k * k_k
    k_ref[...] = k * (1.0 + (a - 1.0) * k_a)


def _rwkv_proj(xn, prev, p, tm):
    t, d = xn.shape
    tok = pl.BlockSpec((tm, d), lambda i: (i, 0))
    weights = [p["mix"], p["vec"], p["wr"], p["wk"], p["wv"], *p["w1"], *p["w2"], *p["a1"], *p["a2"],
               p["g1"], p["g2"]]
    return pl.pallas_call(
        _rwkv_proj_kernel,
        grid=(t // tm,),
        in_specs=[tok, tok] + [_const_spec(w.shape) for w in weights],
        out_specs=[tok] * 7,
        out_shape=[jax.ShapeDtypeStruct((t, d), F32)] * 7,
        compiler_params=_params("parallel"),
        name="rwkv_proj",
    )(xn, prev, *weights)


def _rwkv_proj_fused(h, gain, shift0, p, tm, length):
    t, d = h.shape
    bsz = t // length
    tiles_per_seq = length // tm
    tok = pl.BlockSpec((tm, d), lambda i: (i, 0))
    before = pl.BlockSpec((SUBLANES, d), lambda i: (jnp.maximum(i * (tm // SUBLANES) - 1, 0), 0))
    per_seq = pl.BlockSpec((1, 1, d), lambda i: (i // tiles_per_seq, 0, 0))
    weights = [p["mix"], p["vec"], p["wr"], p["wk"], p["wv"], *p["w1"], *p["w2"], *p["a1"], *p["a2"],
               p["g1"], p["g2"]]
    kern = functools.partial(_rwkv_proj_fused_kernel, tiles_per_seq=tiles_per_seq)
    *proj, last = pl.pallas_call(
        kern,
        grid=(t // tm,),
        in_specs=[tok, before, per_seq, _const_spec((1, d))] + [_const_spec(w.shape) for w in weights],
        out_specs=[tok] * 7 + [per_seq],
        out_shape=[jax.ShapeDtypeStruct((t, d), F32)] * 7 + [jax.ShapeDtypeStruct((bsz, 1, d), F32)],
        compiler_params=_params("arbitrary"),
        name="rwkv_norm_proj",
    )(h, h, shift0.reshape(bsz, 1, d), gain.reshape(1, d), *weights)
    return proj, last.reshape(bsz, d)


def _rwkv_scan_kernel(r_ref, w_ref, k_ref, v_ref, kk_ref, a_ref, s0_ref, lnw_ref, lnb_ref, rk_ref,
                      o_ref, sfin_ref, state, o_rows):
    tb = pl.program_id(1)
    steps = r_ref.shape[0]

    @pl.when(tb == 0)
    def _():
        state[...] = s0_ref[...]

    def step(t, carry):
        r = r_ref[t]
        w = w_ref[t]
        k = k_ref[t]
        vv = v_ref[t]
        kku = kk_ref[t]
        norm = jnp.sqrt(jnp.sum(kku * kku, axis=0, keepdims=True))
        kk = kku / jnp.maximum(norm, 1e-12)
        alpha = -kk
        beta = kk * a_ref[t]

        def row(vi, c):
            s_v = state[vi]
            u = jnp.sum(s_v * alpha, axis=0, keepdims=True)
            s_new = s_v * w + u * beta + v_ref[t, pl.ds(vi, 1), :] * k
            state[vi] = s_new
            o_rows[pl.ds(vi, 1), :] = jnp.sum(s_new * r, axis=0, keepdims=True)
            return c

        lax.fori_loop(0, RW_HEAD, row, 0, unroll=32)
        o = o_rows[...]
        mu = jnp.mean(o, axis=0, keepdims=True)
        var = jnp.mean(jnp.square(o - mu), axis=0, keepdims=True)
        on = (o - mu) * lax.rsqrt(var + RW_LN_EPS)
        bonus = jnp.sum(r * k * rk_ref[...], axis=0, keepdims=True) * vv
        o_ref[t] = on * lnw_ref[...] + lnb_ref[...] + bonus
        return carry

    lax.fori_loop(0, steps, step, 0)

    @pl.when(tb == pl.num_programs(1) - 1)
    def _():
        sfin_ref[...] = state[...]


def _rwkv_scan(seqs, s0, lnw, lnb, rk, tt):
    length, n, probs = seqs[0].shape
    groups = probs // LANES
    seq_spec = pl.BlockSpec((tt, n, LANES), lambda g, i: (i, 0, g))
    st_spec = pl.BlockSpec((n, n, LANES), lambda g, i: (0, 0, g))
    lane_spec = pl.BlockSpec((n, LANES), lambda g, i: (0, 0))
    return pl.pallas_call(
        _rwkv_scan_kernel,
        grid=(groups, length // tt),
        in_specs=[seq_spec] * 6 + [st_spec] + [lane_spec] * 3,
        out_specs=[seq_spec, st_spec],
        out_shape=[jax.ShapeDtypeStruct((length, n, probs), F32),
                   jax.ShapeDtypeStruct((n, n, probs), F32)],
        scratch_shapes=[pltpu.VMEM((n, n, LANES), F32), pltpu.VMEM((n, LANES), F32)],
        compiler_params=_params("arbitrary", "arbitrary"),
        name="rwkv_scan",
    )(*seqs, s0, lnw, lnb, rk)


def _rwkv_out_kernel(o_ref, g_ref, h_ref, wo_ref, out_ref):
    out_ref[...] = h_ref[...] + _bdot(o_ref[...] * g_ref[...], wo_ref[...])


def _rwkv_out(o, g, h, wo, tm):
    t, d = h.shape
    tok = pl.BlockSpec((tm, d), lambda i: (i, 0))
    return pl.pallas_call(
        _rwkv_out_kernel,
        grid=(t // tm,),
        in_specs=[tok, tok, tok, _const_spec(wo.shape)],
        out_specs=tok,
        out_shape=jax.ShapeDtypeStruct((t, d), F32),
        compiler_params=_params("parallel"),
        name="rwkv_out",
    )(o, g, h, wo)


def _mamba_in_kernel(h_ref, g_ref, w_ref, z_ref, xbc_ref, dt_ref):
    xn = _rms(h_ref[...], g_ref[...]).astype(BF16)
    z_ref[...] = jnp.dot(xn, w_ref[:, :M_D_INNER], preferred_element_type=F32)
    xbc_ref[...] = jnp.dot(xn, w_ref[:, M_D_INNER:M_D_INNER + M_CONV_DIM], preferred_element_type=F32)
    dt_ref[...] = jnp.dot(xn, w_ref[:, M_D_INNER + M_CONV_DIM:], preferred_element_type=F32)


def _mamba_in(h, g, w_in, tm):
    t, d = h.shape
    return pl.pallas_call(
        _mamba_in_kernel,
        grid=(t // tm,),
        in_specs=[pl.BlockSpec((tm, d), lambda i: (i, 0)), _const_spec((1, d)), _const_spec(w_in.shape)],
        out_specs=[pl.BlockSpec((tm, M_D_INNER), lambda i: (i, 0)),
                   pl.BlockSpec((tm, M_CONV_DIM), lambda i: (i, 0)),
                   pl.BlockSpec((tm, M_DT_PAD), lambda i: (i, 0))],
        out_shape=[jax.ShapeDtypeStruct((t, M_D_INNER), F32),
                   jax.ShapeDtypeStruct((t, M_CONV_DIM), F32),
                   jax.ShapeDtypeStruct((t, M_DT_PAD), F32)],
        compiler_params=_params("parallel"),
        name="mamba_in_proj",
    )(h, g.reshape(1, d), w_in)


def _mamba_ssd_kernel(xbc_ref, z_ref, dt_ref, h_ref, conv0_ref, ssm0_ref, convw_ref, vec_ref,
                      dvec_ref, expand_ref, normw_ref, wout_ref,
                      out_ref, ssm_ref,
                      conv_buf, src_adt, src_acs, src_b, src_xdt, src_xdt_st, state, y_buf, *, seq_len, rows):
    b = pl.program_id(0)
    c = pl.program_id(1)
    q = M_CHUNK
    halo = SUBLANES
    d_bc = M_GROUPS * M_STATE

    @pl.when(c == 0)
    def _():
        state[...] = ssm0_ref[0]
        conv_buf[0:halo, :] = conv0_ref[0]

    @pl.when(c > 0)
    def _():
        conv_buf[0:halo, :] = conv_buf[rows:rows + halo, :]

    if rows < q:
        @pl.when((b == 0) & (c == 0))
        def _():
            src_adt[...] = jnp.zeros(src_adt.shape, F32)
            src_acs[...] = jnp.zeros(src_acs.shape, F32)
            src_b[...] = jnp.zeros(src_b.shape, BF16)
            src_xdt[...] = jnp.zeros(src_xdt.shape, BF16)
            src_xdt_st[...] = jnp.zeros(src_xdt_st.shape, BF16)

    conv_buf[halo:halo + rows, :] = xbc_ref[0]
    valid = (lax.broadcasted_iota(jnp.int32, (rows, 1), 0) + c * rows) < seq_len
    conv = vec_ref[0:1, :]
    for j in range(M_CONV):
        conv = conv + conv_buf[halo - (M_CONV - 1) + j:halo - (M_CONV - 1) + j + rows, :] * convw_ref[j:j + 1, :]
    act = jnp.where(valid, jax.nn.silu(conv), 0.0)
    xs = act[:, :M_D_INNER]
    dt = jnp.where(valid, _softplus(dt_ref[0] + dvec_ref[0:1, :]), 0.0)
    src_adt[0:rows, :] = dt * dvec_ref[1:2, :]
    li = lax.broadcasted_iota(jnp.int32, (rows, q), 0)
    si = lax.broadcasted_iota(jnp.int32, (rows, q), 1)
    causal = li >= si
    acs = _select_dot(causal.astype(BF16), src_adt[...])
    src_acs[0:rows, :] = acs
    acs_t = src_acs[...].T
    expand = expand_ref[...]
    dt_full = _dot_select(dt, expand)
    acs_full = _dot_select(acs, expand)
    last = acs[rows - 1:rows, :]
    last_full = acs_full[rows - 1:rows, :]
    xdt = xs * dt_full
    src_xdt[0:rows, :] = xdt.astype(BF16)
    src_xdt_st[0:rows, :] = (xdt * jnp.exp(last_full - acs_full)).astype(BF16)
    src_b[0:rows, :] = act[:, M_D_INNER:M_D_INNER + d_bc].astype(BF16)
    eacs_full = jnp.exp(acs_full)
    state_decay = jnp.exp(last)
    width = M_HPG * M_HEADDIM

    for g in range(M_GROUPS):
        cols = slice(g * width, (g + 1) * width)
        b_g = src_b[:, g * M_STATE:(g + 1) * M_STATE]
        c_g = act[:, M_D_INNER + d_bc + g * M_STATE:M_D_INNER + d_bc + (g + 1) * M_STATE].astype(BF16)
        cb = lax.dot_general(c_g, b_g, (((1,), (1,)), ((), ())), preferred_element_type=F32)
        h_g = state[g * M_HPG:(g + 1) * M_HPG].reshape(width, M_STATE)
        y_off = lax.dot_general(c_g, h_g.astype(BF16), (((1,), (1,)), ((), ())),
                                preferred_element_type=F32)
        upd = lax.dot_general(src_xdt_st[:, cols], b_g, (((0,), (0,)), ((), ())),
                              preferred_element_type=F32)
        ys = []
        for r in range(g * M_HPG, (g + 1) * M_HPG):
            lo = r * M_HEADDIM
            decay = jnp.where(causal, jnp.exp(acs[:, r:r + 1] - acs_t[r:r + 1, :]), 0.0)
            ys.append(jnp.dot((cb * decay).astype(BF16), src_xdt[:, lo:lo + M_HEADDIM],
                              preferred_element_type=F32))
            sub = slice((r - g * M_HPG) * M_HEADDIM, (r - g * M_HPG + 1) * M_HEADDIM)
            state[r] = state[r] * state_decay[:, r:r + 1] + upd[sub, :]
        y_buf[:, cols] = jnp.concatenate(ys, axis=1) + y_off * eacs_full[:, cols]

    y = y_buf[...] + vec_ref[1:2, :M_D_INNER] * xs
    yg = y * jax.nn.silu(z_ref[0])
    parts = []
    for g in range(M_GROUPS):
        part = yg[:, g * width:(g + 1) * width]
        parts.append(part * lax.rsqrt(jnp.mean(part * part, axis=-1, keepdims=True) + NORM_EPS))
    yn = jnp.concatenate(parts, axis=1) * normw_ref[...]
    out_ref[0] = h_ref[0] + _bdot(yn, wout_ref[...])

    @pl.when(c == pl.num_programs(1) - 1)
    def _():
        ssm_ref[0] = state[...]


def _mamba_ssd(xbc, z, dt, h, conv0, ssm0, p, rows, seq_len):
    bsz, length, _ = xbc.shape
    chunks = pl.cdiv(length, rows)
    weights = [p["conv_w"], p["vec"], p["dvec"], p["expand"], p["norm_w"], p["w_out"]]

    def seq_spec(width):
        return pl.BlockSpec((1, rows, width), lambda b, c: (b, c, 0))

    kern = functools.partial(_mamba_ssd_kernel, seq_len=seq_len, rows=rows)
    return pl.pallas_call(
        kern,
        grid=(bsz, chunks),
        in_specs=[seq_spec(M_CONV_DIM), seq_spec(M_D_INNER), seq_spec(M_DT_PAD), seq_spec(D_MODEL),
                  pl.BlockSpec((1, SUBLANES, M_CONV_DIM), lambda b, c: (b, 0, 0)),
                  pl.BlockSpec((1, M_HEADS, M_HEADDIM, M_STATE), lambda b, c: (b, 0, 0, 0))]
                 + [_const_spec(w.shape) for w in weights],
        out_specs=[seq_spec(D_MODEL),
                   pl.BlockSpec((1, M_HEADS, M_HEADDIM, M_STATE), lambda b, c: (b, 0, 0, 0))],
        out_shape=[jax.ShapeDtypeStruct((bsz, length, D_MODEL), F32),
                   jax.ShapeDtypeStruct((bsz, M_HEADS, M_HEADDIM, M_STATE), F32)],
        scratch_shapes=[pltpu.VMEM((rows + SUBLANES, M_CONV_DIM), F32),
                        pltpu.VMEM((M_CHUNK, M_DT_PAD), F32),
                        pltpu.VMEM((M_CHUNK, M_DT_PAD), F32),
                        pltpu.VMEM((M_CHUNK, M_GROUPS * M_STATE), BF16),
                        pltpu.VMEM((M_CHUNK, M_D_INNER), BF16),
                        pltpu.VMEM((M_CHUNK, M_D_INNER), BF16),
                        pltpu.VMEM((M_HEADS, M_HEADDIM, M_STATE), F32),
                        pltpu.VMEM((rows, M_D_INNER), F32)],
        compiler_params=_params("arbitrary", "arbitrary"),
        name="mamba_ssd",
    )(xbc, z, dt, h, conv0, ssm0, *weights)


_CANDS = [(ra, rb) for ra in range(P_TOPK) for rb in range(P_TOPK) if (ra + 1) * (rb + 1) <= P_TOPK]


def _peer_route_kernel(h_ref, g_ref, wq_ref, keys_ref, keys_lo_ref,
                       xt_ref, r1_ref, e1_ref, m0_ref, w0_ref,
                       q_buf, s_buf, code0, code_book, best_vals, cand, counts, inv_z):
    tm = h_ref.shape[0]
    neg_inf = float("-inf")
    xn = _rms(h_ref[...], g_ref[...])
    xt = xn.T.astype(BF16)
    xt_ref[...] = xt
    q_buf[...] = jnp.dot(wq_ref[...], xt, preferred_element_type=F32)
    key_iota = lax.broadcasted_iota(jnp.int32, (P_NKEYS, tm), 0).astype(F32)

    def half(m, carry):
        head = m // 2
        z = m % 2
        row0 = pl.multiple_of(m * P_NKEYS, P_NKEYS)
        qm = q_buf[pl.ds(row0, P_NKEYS), :]
        q_hi = qm.astype(BF16)
        q_lo = (qm - q_hi.astype(F32)).astype(BF16)
        k_hi = keys_ref[m]
        s = (jnp.dot(k_hi, q_hi, preferred_element_type=F32)
             + (jnp.dot(k_hi, q_lo, preferred_element_type=F32)
                + jnp.dot(keys_lo_ref[m], q_hi, preferred_element_type=F32)))

        s_buf[...] = s

        def next_distinct(j, prev):
            sj = s_buf[...]
            best = jnp.max(jnp.where(sj < prev, sj, neg_inf), axis=0, keepdims=True)
            best_vals[z, j, pl.ds(head, 1), :] = best
            return best

        def extract_ties(j, sc):
            cur, rank = sc
            best = jnp.max(cur, axis=0, keepdims=True)
            first = jnp.min(jnp.where(cur == best, key_iota, float(P_NKEYS)), axis=0, keepdims=True)
            sel = key_iota == first
            best_vals[z, j, pl.ds(head, 1), :] = best
            return jnp.where(sel, neg_inf, cur), jnp.where(sel, jnp.asarray(j, dtype=F32), rank)

        last = lax.fori_loop(0, P_TOPK, next_distinct, jnp.full((1, tm), float("inf"), F32))
        reached = jnp.sum(jnp.where(s >= last, 1.0, 0.0), axis=0, keepdims=True)
        has_ties = jnp.max(jnp.abs(reached - float(P_TOPK))) > 0.0

        def rank_by_count():
            rank = jnp.zeros((P_NKEYS, tm), F32)
            for jj in range(P_TOPK):
                rank = rank + jnp.where(best_vals[z, jj, pl.ds(head, 1), :] > s, 1.0, 0.0)
            return rank

        def rank_with_ties():
            no_rank = jnp.full((P_NKEYS, tm), float(P_NKEYS), F32)
            return lax.fori_loop(0, P_TOPK, extract_ties, (s, no_rank))[1]

        e = jnp.exp(s - jnp.max(s, axis=0, keepdims=True))

        @pl.when(z == 0)
        def _():
            w0_ref[head] = e

            @pl.when(has_ties)
            def _():
                code0[head] = rank_with_ties()
                for ra in range(P_TOPK):
                    code_book[ra, pl.ds(head, 1), :] = jnp.full((1, tm), float(ra), F32)

            @pl.when(jnp.logical_not(has_ties))
            def _():
                code0[head] = s
                for ra in range(P_TOPK):
                    code_book[ra, pl.ds(head, 1), :] = best_vals[0, ra, pl.ds(head, 1), :]

        @pl.when(z == 1)
        def _():
            rank = lax.cond(has_ties, rank_with_ties, rank_by_count)
            for lt in range(tm // LANES):
                r1_ref[head, lt] = rank[:, lt * LANES:(lt + 1) * LANES].astype(BF16)
                e1_ref[head, lt] = e[:, lt * LANES:(lt + 1) * LANES].astype(BF16)

        return carry

    lax.fori_loop(0, 2 * P_HEADS, half, 0)

    ex0 = [jnp.exp(best_vals[0, ra] - best_vals[0, 0]) for ra in range(P_TOPK)]
    ex1 = [jnp.exp(best_vals[1, rb] - best_vals[1, 0]) for rb in range(P_TOPK)]
    for i, (ra, rb) in enumerate(_CANDS):
        cand[i] = best_vals[0, ra] + best_vals[1, rb]

    def next_sum(it, prev):
        best = None
        for i in range(len(_CANDS)):
            ci = cand[i]
            below = jnp.where(ci < prev, ci, neg_inf)
            best = below if best is None else jnp.maximum(best, below)
        return best

    last_sum = lax.fori_loop(0, P_TOPK, next_sum, jnp.full((P_HEADS, tm), float("inf"), F32))
    hits = [jnp.where(cand[i] >= last_sum, 1.0, 0.0) for i in range(len(_CANDS))]
    n_hits = functools.reduce(lambda x, y: x + y, hits)
    sums_tie = jnp.max(jnp.abs(n_hits - float(P_TOPK))) > 0.0

    def select_distinct():
        z = jnp.zeros((P_HEADS, tm), F32)
        per_rank = [None] * P_TOPK
        for i, (ra, rb) in enumerate(_CANDS):
            per_rank[ra] = hits[i] if per_rank[ra] is None else per_rank[ra] + hits[i]
            z = z + hits[i] * (ex0[ra] * ex1[rb])
        for ra in range(P_TOPK):
            counts[ra] = per_rank[ra]
        return z

    def select_with_ties():
        counts[...] = jnp.zeros(counts.shape, F32)

        def pick(it, z):
            best = cand[0]
            for i in range(1, len(_CANDS)):
                best = jnp.maximum(best, cand[i])
            found = jnp.zeros((P_HEADS, tm), F32)
            for i, (ra, rb) in enumerate(_CANDS):
                ci = cand[i]
                hit = jnp.where(ci == best, 1.0, 0.0) * (1.0 - found)
                found = found + hit
                cand[i] = jnp.where(hit > 0.0, neg_inf, ci)
                counts[ra] = counts[ra] + hit
                z = z + hit * (ex0[ra] * ex1[rb])
            return z

        return lax.fori_loop(0, P_TOPK, pick, jnp.zeros((P_HEADS, tm), F32))

    inv_z[...] = 1.0 / lax.cond(sums_tie, select_with_ties, select_distinct)

    def finish(head, carry):
        code = code0[head]
        m0 = jnp.zeros((P_NKEYS, tm), F32)
        for ra in range(P_TOPK):
            m0 = jnp.where(code == code_book[ra, pl.ds(head, 1), :], counts[ra, pl.ds(head, 1), :], m0)
        m0_ref[head] = m0
        w0_ref[head] = w0_ref[head] * inv_z[pl.ds(head, 1), :]
        return carry

    lax.fori_loop(0, P_HEADS, finish, 0)


def _peer_route(h, g, wq_t, keys, keys_lo, tm):
    t, d = h.shape
    tiles = pl.cdiv(t, tm)
    t_pad = tiles * tm
    tile = pl.BlockSpec((P_HEADS, P_NKEYS, tm), lambda i: (0, 0, i))
    tile_f32 = jax.ShapeDtypeStruct((P_HEADS, P_NKEYS, t_pad), F32)
    tile_bf16 = jax.ShapeDtypeStruct((P_HEADS, t_pad // LANES, P_NKEYS, LANES), BF16)
    lane_tiles = pl.BlockSpec((P_HEADS, tm // LANES, P_NKEYS, LANES), lambda i: (0, i, 0, 0))
    return pl.pallas_call(
        _peer_route_kernel,
        grid=(tiles,),
        in_specs=[pl.BlockSpec((tm, d), lambda i: (i, 0)), _const_spec((1, d)),
                  _const_spec(wq_t.shape), _const_spec(keys.shape), _const_spec(keys_lo.shape)],
        out_specs=[pl.BlockSpec((d, tm), lambda i: (0, i)), lane_tiles, lane_tiles, tile, tile],
        out_shape=[jax.ShapeDtypeStruct((d, t_pad), BF16), tile_bf16, tile_bf16, tile_f32, tile_f32],
        scratch_shapes=[pltpu.VMEM((P_HEADS * P_QDIM, tm), F32),
                        pltpu.VMEM((P_NKEYS, tm), F32),
                        pltpu.VMEM((P_HEADS, P_NKEYS, tm), F32),
                        pltpu.VMEM((P_TOPK, P_HEADS, tm), F32),
                        pltpu.VMEM((2, P_TOPK, P_HEADS, tm), F32),
                        pltpu.VMEM((len(_CANDS), P_HEADS, tm), F32),
                        pltpu.VMEM((P_TOPK, P_HEADS, tm), F32),
                        pltpu.VMEM((P_HEADS, tm), F32)],
        compiler_params=_params("parallel"),
        name="peer_route",
    )(h, g.reshape(1, d), wq_t, keys, keys_lo)


def _gelu(x):
    return 0.5 * x * (1.0 + lax.erf(x * (1.0 / math.sqrt(2.0))))


def _peer_dense_kernel(xt_ref, u_ref, vt_ref, r1_ref, e1_ref, m0_ref, w0_ref, h_ref, gfin_ref,
                       out_ref, acc, hid, prob, *, final_norm):
    j = pl.program_id(1)
    keys_per_step = P_EXPERT_TILE // P_NKEYS

    @pl.when(j == 0)
    def _():
        acc[...] = jnp.zeros(acc.shape, F32)

    tm = xt_ref.shape[1]
    hid[...] = jnp.dot(u_ref[...], xt_ref[...], preferred_element_type=F32)
    zero = jnp.zeros((), BF16)
    for ii in range(keys_per_step):
        i1 = j * keys_per_step + ii
        rows = slice(ii * P_NKEYS, (ii + 1) * P_NKEYS)
        m0_rows = [m0_ref[head, pl.ds(i1, 1), :] for head in range(P_HEADS)]
        w0_rows = [w0_ref[head, pl.ds(i1, 1), :] for head in range(P_HEADS)]

        def packed_rows(row):
            one = jnp.broadcast_to(row, (2 * SUBLANES, LANES)).astype(BF16)
            return jnp.tile(one, (P_NKEYS // (2 * SUBLANES), 1))

        for lt in range(tm // LANES):
            lanes = slice(lt * LANES, (lt + 1) * LANES)
            gate = None
            for head in range(P_HEADS):
                m0 = packed_rows(m0_rows[head][:, lanes])
                w0 = packed_rows(w0_rows[head][:, lanes])
                term = w0 * jnp.where(r1_ref[head, lt] < m0, e1_ref[head, lt], zero)
                gate = term if gate is None else gate + term
            prob[rows, lanes] = gate * _gelu(hid[rows, lanes]).astype(BF16)
    acc[...] += jnp.dot(vt_ref[...], prob[...], preferred_element_type=F32)

    @pl.when(j == pl.num_programs(1) - 1)
    def _():
        res = h_ref[...] + acc[...].T
        if final_norm:
            res = _rms(res, gfin_ref[...])
        out_ref[...] = res


def _peer_dense(xt, u_bf, vt_bf, r1, e1, m0, w0, h, g_final, tm, final_norm):
    t, d = h.shape
    n_exp = u_bf.shape[0]
    tile = pl.BlockSpec((P_HEADS, P_NKEYS, tm), lambda i, j: (0, 0, i))
    lane_tiles = pl.BlockSpec((P_HEADS, tm // LANES, P_NKEYS, LANES), lambda i, j: (0, i, 0, 0))
    kern = functools.partial(_peer_dense_kernel, final_norm=final_norm)
    return pl.pallas_call(
        kern,
        grid=(pl.cdiv(t, tm), n_exp // P_EXPERT_TILE),
        in_specs=[pl.BlockSpec((d, tm), lambda i, j: (0, i)),
                  pl.BlockSpec((P_EXPERT_TILE, d), lambda i, j: (j, 0)),
                  pl.BlockSpec((d, P_EXPERT_TILE), lambda i, j: (0, j)),
                  lane_tiles, lane_tiles, tile, tile,
                  pl.BlockSpec((tm, d), lambda i, j: (i, 0)),
                  _const_spec((1, d))],
        out_specs=pl.BlockSpec((tm, d), lambda i, j: (i, 0)),
        out_shape=jax.ShapeDtypeStruct((t, d), F32),
        scratch_shapes=[pltpu.VMEM((d, tm), F32),
                        pltpu.VMEM((P_EXPERT_TILE, tm), F32),
                        pltpu.VMEM((P_EXPERT_TILE, tm), BF16)],
        compiler_params=_params("parallel", "arbitrary"),
        name="peer_dense",
    )(xt, u_bf, vt_bf, r1, e1, m0, w0, h, g_final.reshape(1, d))


def _pad_cols(w, n):
    return jnp.pad(w, ((0, 0), (0, n - w.shape[1])))


def _pad_rows(w, n):
    return jnp.pad(w, ((0, n - w.shape[0]), (0, 0)))


def _prep_rwkv(mix, w_rkv, w0, w1, w2, a0, a1, a2, g1, g2, k_k, k_a, r_k, ln_w, ln_b, w_o):
    d = D_MODEL
    lane = lambda x: jnp.repeat(x.reshape(RW_HEADS, RW_HEAD).T, LANES // RW_HEADS, axis=1)
    return {
        "mix": _pad_rows(mix, SUBLANES),
        "vec": _pad_rows(jnp.stack([w0, a0, k_k, k_a]), SUBLANES),
        "wr": w_rkv[0].astype(BF16), "wk": w_rkv[1].astype(BF16), "wv": w_rkv[2].astype(BF16),
        "w1": _split_bf16(_pad_cols(w1, LORA_PAD)), "w2": _split_bf16(_pad_rows(w2, LORA_PAD)),
        "a1": _split_bf16(_pad_cols(a1, LORA_PAD)), "a2": _split_bf16(_pad_rows(a2, LORA_PAD)),
        "g1": _pad_cols(g1, GATE_LORA_PAD).astype(BF16), "g2": _pad_rows(g2, GATE_LORA_PAD).astype(BF16),
        "lnw": lane(ln_w), "lnb": lane(ln_b), "rk": lane(r_k.reshape(d)),
        "wo": w_o.astype(BF16),
    }


def _prep_mamba(in_proj, conv_w, conv_b, dt_bias, a_log, d_skip, norm_w, out_proj):
    head_of_channel = jnp.arange(M_D_INNER) // M_HEADDIM
    expand = (jnp.arange(M_DT_PAD)[:, None] == head_of_channel[None, :]).astype(BF16)
    d_full = jnp.pad(jnp.repeat(d_skip, M_HEADDIM), (0, M_CONV_DIM - M_D_INNER))
    pad_h = lambda x: jnp.pad(x, (0, M_DT_PAD - M_HEADS))
    return {
        "w_in": _pad_cols(in_proj, M_IN_DIM - M_HEADS + M_DT_PAD).astype(BF16),
        "conv_w": _pad_rows(conv_w, SUBLANES),
        "vec": _pad_rows(jnp.stack([conv_b, d_full]), SUBLANES),
        "dvec": _pad_rows(jnp.stack([pad_h(dt_bias), pad_h(-jnp.exp(a_log))]), SUBLANES),
        "expand": expand,
        "norm_w": norm_w.reshape(1, M_D_INNER),
        "w_out": out_proj.astype(BF16),
    }


def _prep_peer(w_q, sub_keys, u_tab, v_tab):
    keys = jnp.transpose(sub_keys, (1, 0, 2, 3)).reshape(2 * P_HEADS, P_NKEYS, P_QDIM // 2)
    keys_hi, keys_lo = _split_bf16(keys)
    return {"wq_t": w_q.T.astype(BF16), "keys": keys_hi, "keys_lo": keys_lo,
            "u": u_tab.astype(BF16), "vt": v_tab.T.astype(BF16)}


SCAN_GROUP = LANES // RW_HEADS
RELAYOUT_STEPS = 128
RELAYOUT_PITCH = RW_HEAD + SUBLANES


def _to_scan_kernel(x_ref, o_ref, stage):
    for b in range(SCAN_GROUP):
        for hp in range(RW_HEADS // 2):
            t = x_ref[b, :, hp * LANES:(hp + 1) * LANES].T
            for hl in range(2):
                lane = (2 * hp + hl) * SCAN_GROUP + b
                stage[lane * RELAYOUT_PITCH:lane * RELAYOUT_PITCH + RW_HEAD, :] = t[hl * RW_HEAD:(hl + 1) * RW_HEAD, :]
    for k in range(RW_HEAD):
        by_lane = stage[pl.ds(k, LANES, stride=RELAYOUT_PITCH), :]
        o_ref[pl.ds(k, RELAYOUT_STEPS, stride=RW_HEAD), :] = by_lane.T


def _to_scan_pallas(x, length):
    x3 = x.reshape(SCAN_GROUP, length, D_MODEL)
    out = pl.pallas_call(
        _to_scan_kernel,
        grid=(pl.cdiv(length, RELAYOUT_STEPS),),
        in_specs=[pl.BlockSpec((SCAN_GROUP, RELAYOUT_STEPS, D_MODEL), lambda i: (0, i, 0))],
        out_specs=pl.BlockSpec((RELAYOUT_STEPS * RW_HEAD, LANES), lambda i: (i, 0)),
        out_shape=jax.ShapeDtypeStruct((length * RW_HEAD, LANES), F32),
        scratch_shapes=[pltpu.VMEM((LANES * RELAYOUT_PITCH, LANES), F32)],
        compiler_params=_params("parallel"),
        name="to_scan_layout",
    )(x3)
    return out.reshape(length, RW_HEAD, LANES)


def _rwkv_out_scan_kernel(o_ref, g_ref, h_ref, wo_ref, out_ref, stage):
    for v in range(RW_HEAD):
        by_step = o_ref[pl.ds(v, RELAYOUT_STEPS, stride=RW_HEAD), :]
        stage[pl.ds(v, LANES, stride=RELAYOUT_PITCH), :] = by_step.T
    for b in range(SCAN_GROUP):
        tiles = []
        for hp in range(RW_HEADS // 2):
            halves = []
            for hl in range(2):
                lane = (2 * hp + hl) * SCAN_GROUP + b
                halves.append(stage[lane * RELAYOUT_PITCH:lane * RELAYOUT_PITCH + RW_HEAD, :])
            tiles.append(jnp.concatenate(halves, axis=0).T)
        o_b = jnp.concatenate(tiles, axis=1)
        out_ref[b] = h_ref[b] + _bdot(o_b * g_ref[b], wo_ref[...])


def _rwkv_out_scan(o_scan, g, h, wo, length):
    d = D_MODEL
    seq = pl.BlockSpec((SCAN_GROUP, RELAYOUT_STEPS, d), lambda i: (0, i, 0))
    out = pl.pallas_call(
        _rwkv_out_scan_kernel,
        grid=(pl.cdiv(length, RELAYOUT_STEPS),),
        in_specs=[pl.BlockSpec((RELAYOUT_STEPS * RW_HEAD, LANES), lambda i: (i, 0)), seq, seq,
                  _const_spec(wo.shape)],
        out_specs=seq,
        out_shape=jax.ShapeDtypeStruct((SCAN_GROUP, length, d), F32),
        scratch_shapes=[pltpu.VMEM((LANES * RELAYOUT_PITCH, LANES), F32)],
        compiler_params=_params("parallel"),
        name="rwkv_out_from_scan",
    )(o_scan.reshape(length * RW_HEAD, LANES), g.reshape(SCAN_GROUP, length, d), h.reshape(SCAN_GROUP, length, d), wo)
    return out.reshape(SCAN_GROUP * length, d)


def _to_scan(x, bsz, length):
    if bsz == SCAN_GROUP and length >= RELAYOUT_STEPS:
        return _to_scan_pallas(x, length)
    x = x.reshape(bsz // SCAN_GROUP, SCAN_GROUP, length, RW_HEADS, RW_HEAD)
    return jnp.transpose(x, (2, 4, 0, 3, 1)).reshape(length, RW_HEAD, bsz * RW_HEADS)


def _from_scan(x, bsz, length):
    x = x.reshape(length, RW_HEAD, bsz // SCAN_GROUP, RW_HEADS, SCAN_GROUP)
    return jnp.transpose(x, (2, 4, 0, 3, 1)).reshape(bsz * length, D_MODEL)


def _state_to_scan(wkv, bsz):
    x = wkv.reshape(bsz // SCAN_GROUP, SCAN_GROUP, RW_HEADS, RW_HEAD, RW_HEAD)
    return jnp.transpose(x, (3, 4, 0, 2, 1)).reshape(RW_HEAD, RW_HEAD, bsz * RW_HEADS)


def _state_from_scan(s, bsz):
    x = s.reshape(RW_HEAD, RW_HEAD, bsz // SCAN_GROUP, RW_HEADS, SCAN_GROUP)
    return jnp.transpose(x, (2, 4, 3, 0, 1)).reshape(bsz, RW_HEADS, RW_HEAD, RW_HEAD)


def _rwkv_layer(h, g_mix, shift0, wkv0, p, bsz, length, tm, tt):
    if length % tm == 0:
        (r, w, k, v, kk, a, g), shift = _rwkv_proj_fused(h, g_mix, shift0, p, tm, length)
    else:
        xn = _norm(h, g_mix, tm)
        xn3 = xn.reshape(bsz, length, D_MODEL)
        prev = jnp.concatenate([shift0[:, None, :], xn3[:, :-1]], axis=1).reshape(bsz * length, D_MODEL)
        r, w, k, v, kk, a, g = _rwkv_proj(xn, prev, p, tm)
        shift = xn3[:, -1]
    seqs = [_to_scan(x, bsz, length) for x in (r, w, k, v, kk, a)]
    o, s_fin = _rwkv_scan(seqs, _state_to_scan(wkv0, bsz), p["lnw"], p["lnb"], p["rk"], tt)
    if bsz == SCAN_GROUP and length >= RELAYOUT_STEPS:
        h_new = _rwkv_out_scan(o, g, h, p["wo"], length)
    else:
        h_new = _rwkv_out(_from_scan(o, bsz, length), g, h, p["wo"], tm)
    return h_new, shift, _state_from_scan(s_fin, bsz)


def _mamba_layer(h, g_mix, conv0, ssm0, p, bsz, length, tm, rows):
    z, xbc, dt = _mamba_in(h, g_mix, p["w_in"], tm)
    xbc3 = xbc.reshape(bsz, length, M_CONV_DIM)
    tail = M_CONV - 1
    conv_new = xbc3[:, length - tail:] if length >= tail else jnp.concatenate([conv0, xbc3], axis=1)[:, length:]
    pad_t = (-length) % SUBLANES if length < rows else 0

    def seq(x, c):
        x = x.reshape(bsz, length, c)
        return jnp.pad(x, ((0, 0), (0, pad_t), (0, 0))) if pad_t else x

    conv0_p = jnp.pad(conv0, ((0, 0), (SUBLANES - (M_CONV - 1), 0), (0, 0)))
    out, ssm = _mamba_ssd(seq(xbc, M_CONV_DIM), seq(z, M_D_INNER), seq(dt, M_DT_PAD), seq(h, D_MODEL),
                          conv0_p, ssm0, p, min(rows, length + pad_t), length)
    if pad_t:
        out = out[:, :length]
    return out.reshape(bsz * length, D_MODEL), conv_new, ssm


def _peer_layer(h, g_ffn, p, tm, g_final=None):
    xt, r1, e1, m0, w0 = _peer_route(h, g_ffn, p["wq_t"], p["keys"], p["keys_lo"], tm)
    final = g_final is not None
    return _peer_dense(xt, p["u"], p["vt"], r1, e1, m0, w0, h, g_final if final else g_ffn, tm, final)


def _trunk(h, shift0, wkv0, conv0, ssm0, norm_mix, norm_ffn, norm_final, rwkv_p, mamba_p, peer_p,
           bsz, length, tm, tt, peer_tm, rows):
    h, shift, wkv = _rwkv_layer(h, norm_mix[0], shift0, wkv0, rwkv_p, bsz, length, tm, tt)
    h = _peer_layer(h, norm_ffn[0], peer_p[0], peer_tm)
    h, conv, ssm = _mamba_layer(h, norm_mix[1], conv0, ssm0, mamba_p, bsz, length, tm, rows)
    y = _peer_layer(h, norm_ffn[1], peer_p[1], peer_tm, g_final=norm_final)
    return y.reshape(bsz, length, D_MODEL), shift[None], wkv[None], conv[None], ssm[None]


def kernel(x_prompt, x_sample, state_rwkv_shift, state_rwkv_wkv, state_mamba_conv, state_mamba_ssm, meta_tokens, norm_mix, norm_ffn, norm_final, rwkv_mix, rwkv_w_rkv, rwkv_w0, rwkv_w1, rwkv_w2, rwkv_a0, rwkv_a1, rwkv_a2, rwkv_g1, rwkv_g2, rwkv_k_k, rwkv_k_a, rwkv_r_k, rwkv_ln_w, rwkv_ln_b, rwkv_w_o, mamba_in_proj, mamba_conv_w, mamba_conv_b, mamba_dt_bias, mamba_a_log, mamba_d, mamba_norm_w, mamba_out_proj, peer_w_q, peer_sub_keys, peer_u, peer_v):
    rwkv_p = _prep_rwkv(rwkv_mix[0], rwkv_w_rkv[0], rwkv_w0[0], rwkv_w1[0], rwkv_w2[0], rwkv_a0[0],
                        rwkv_a1[0], rwkv_a2[0], rwkv_g1[0], rwkv_g2[0], rwkv_k_k[0], rwkv_k_a[0],
                        rwkv_r_k[0], rwkv_ln_w[0], rwkv_ln_b[0], rwkv_w_o[0])
    mamba_p = _prep_mamba(mamba_in_proj[0], mamba_conv_w[0], mamba_conv_b[0], mamba_dt_bias[0],
                          mamba_a_log[0], mamba_d[0], mamba_norm_w[0], mamba_out_proj[0])
    peer_p = [_prep_peer(peer_w_q[i], peer_sub_keys[i], peer_u[i], peer_v[i]) for i in range(2)]

    bp, lp = x_prompt.shape[0], x_prompt.shape[1] + N_META
    meta = jnp.broadcast_to(meta_tokens[None], (bp, N_META, D_MODEL))
    hp = jnp.concatenate([meta, x_prompt], axis=1).reshape(bp * lp, D_MODEL)
    zeros = lambda *s: jnp.zeros(s, F32)
    yp, p_shift, p_wkv, p_conv, p_ssm = _trunk(
        hp, zeros(bp, D_MODEL), zeros(bp, RW_HEADS, RW_HEAD, RW_HEAD), zeros(bp, M_CONV - 1, M_CONV_DIM),
        zeros(bp, M_HEADS, M_HEADDIM, M_STATE), norm_mix, norm_ffn, norm_final, rwkv_p, mamba_p, peer_p,
        bp, lp, tm=344, tt=48, peer_tm=512, rows=M_CHUNK)

    bs, ls = x_sample.shape[0], x_sample.shape[1]
    ys, s_shift, s_wkv, s_conv, s_ssm = _trunk(
        x_sample.reshape(bs * ls, D_MODEL), state_rwkv_shift[0], state_rwkv_wkv[0], state_mamba_conv[0],
        state_mamba_ssm[0], norm_mix, norm_ffn, norm_final, rwkv_p, mamba_p, peer_p,
        bs, ls, tm=256, tt=ls, peer_tm=512, rows=M_CHUNK)
    return (yp[:, N_META:], ys, p_shift, p_wkv, p_conv, p_ssm, s_shift, s_wkv, s_conv, s_ssm)
```

```python
import functools
import math

import jax
import jax.numpy as jnp
from jax import lax
from jax.experimental import pallas as pl
from jax.experimental.pallas import tpu as pltpu

F32 = jnp.float32
BF16 = jnp.bfloat16

D_MODEL = 1024
N_META = 16
NORM_EPS = 1e-5
RW_HEAD = 64
RW_HEADS = D_MODEL // RW_HEAD
RW_LN_EPS = 64e-5
LORA_PAD = 128
GATE_LORA_PAD = 256
M_D_INNER = 2048
M_HEADDIM = 64
M_HEADS = M_D_INNER // M_HEADDIM
M_GROUPS = 4
M_HPG = M_HEADS // M_GROUPS
M_STATE = 128
M_CONV = 4
M_CONV_DIM = M_D_INNER + 2 * M_GROUPS * M_STATE
M_IN_DIM = 2 * M_D_INNER + 2 * M_GROUPS * M_STATE + M_HEADS
M_DT_PAD = 128
M_CHUNK = 128
P_HEADS = 8
P_NKEYS = 128
P_QDIM = 256
P_TOPK = 16
P_EXPERT_TILE = 2048
P_EXPERT_SUB = 256

LANES = 128
SUBLANES = 8
VMEM_LIMIT_BYTES = 56 * 1024 * 1024


def _params(*semantics):
    return pltpu.CompilerParams(dimension_semantics=semantics, vmem_limit_bytes=VMEM_LIMIT_BYTES)


def _const_spec(shape):
    zeros = (0,) * len(shape)
    return pl.BlockSpec(shape, lambda *_: zeros)


def _rms(x, g):
    return x * lax.rsqrt(jnp.mean(x * x, axis=-1, keepdims=True) + NORM_EPS) * g


def _softplus(x):
    return jnp.maximum(x, 0.0) + jnp.log1p(jnp.exp(-jnp.abs(x)))


def _bdot(a, b):
    return jnp.dot(a.astype(BF16), b.astype(BF16), preferred_element_type=F32)


def _split_bf16(w):
    hi = w.astype(BF16)
    return hi, (w - hi.astype(F32)).astype(BF16)


def _split3_bf16(x):
    p0 = x.astype(BF16)
    r0 = x - p0.astype(F32)
    p1 = r0.astype(BF16)
    p2 = (r0 - p1.astype(F32)).astype(BF16)
    return p0, p1, p2


def _dot_select(a, sel):
    p0, p1, p2 = _split3_bf16(a)
    return (jnp.dot(p0, sel, preferred_element_type=F32)
            + (jnp.dot(p1, sel, preferred_element_type=F32) + jnp.dot(p2, sel, preferred_element_type=F32)))


def _select_dot(sel, b):
    p0, p1, p2 = _split3_bf16(b)
    return (jnp.dot(sel, p0, preferred_element_type=F32)
            + (jnp.dot(sel, p1, preferred_element_type=F32) + jnp.dot(sel, p2, preferred_element_type=F32)))


def _dot3(a, b_hi, b_lo):
    a_hi, a_lo = _split_bf16(a)
    return (jnp.dot(a_hi, b_hi, preferred_element_type=F32)
            + (jnp.dot(a_hi, b_lo, preferred_element_type=F32) + jnp.dot(a_lo, b_hi, preferred_element_type=F32)))


def _norm_kernel(h_ref, g_ref, o_ref):
    o_ref[...] = _rms(h_ref[...], g_ref[...])


def _norm(h, g, tm):
    t, d = h.shape
    return pl.pallas_call(
        _norm_kernel,
        grid=(t // tm,),
        in_specs=[pl.BlockSpec((tm, d), lambda i: (i, 0)), _const_spec((1, d))],
        out_specs=pl.BlockSpec((tm, d), lambda i: (i, 0)),
        out_shape=jax.ShapeDtypeStruct((t, d), F32),
        compiler_params=_params("parallel"),
        name="rms_norm",
    )(h, g.reshape(1, d))


def _rwkv_proj_kernel(xn_ref, prev_ref, *refs):
    _rwkv_proj_body(xn_ref[...], prev_ref[...], *refs)


def _rwkv_proj_fused_kernel(h_ref, before_ref, shift_ref, gain_ref, *refs, tiles_per_seq):
    *proj_refs, last_ref = refs
    i = pl.program_id(0)
    xn = _rms(h_ref[...], gain_ref[...])
    before = _rms(before_ref[SUBLANES - 1:SUBLANES, :], gain_ref[...])
    first = jnp.where(i % tiles_per_seq == 0, shift_ref[0], before)
    row = lax.broadcasted_iota(jnp.int32, xn.shape, 0)
    prev = jnp.where(row == 0, first, pltpu.roll(xn, 1, axis=0))
    _rwkv_proj_body(xn, prev, *proj_refs)

    @pl.when(i % tiles_per_seq == tiles_per_seq - 1)
    def _():
        last_ref[0] = xn[xn.shape[0] - 1:, :]


def _rwkv_proj_body(xn, prev, mix_ref, vec_ref, wr_ref, wk_ref, wv_ref,
                    w1h_ref, w1l_ref, w2h_ref, w2l_ref, a1h_ref, a1l_ref, a2h_ref, a2l_ref, g1_ref, g2_ref,
                    r_ref, w_ref, k_ref, v_ref, kk_ref, a_ref, g_ref):
    dx = prev - xn
    xr, xw, xk, xv, xa, xg = [xn + dx * mix_ref[j:j + 1, :] for j in range(6)]
    w0, a0, k_k, k_a = [vec_ref[j:j + 1, :] for j in range(4)]
    r_ref[...] = _bdot(xr, wr_ref[...])
    k = _bdot(xk, wk_ref[...])
    v_ref[...] = _bdot(xv, wv_ref[...])
    w_lora = _dot3(jnp.tanh(_dot3(xw, w1h_ref[...], w1l_ref[...])), w2h_ref[...], w2l_ref[...])
    w_log = -_softplus(-(w0 + w_lora)) - 0.5
    w_ref[...] = jnp.exp(-jnp.exp(w_log))
    a = jax.nn.sigmoid(a0 + _dot3(_dot3(xa, a1h_ref[...], a1l_ref[...]), a2h_ref[...], a2l_ref[...]))
    a_ref[...] = a
    g_ref[...] = _bdot(jax.nn.sigmoid(_bdot(xg, g1_ref[...])), g2_ref[...])
    kk_ref[...] = k * k_k
    k_ref[...] = k * (1.0 + (a - 1.0) * k_a)


def _rwkv_proj(xn, prev, p, tm):
    t, d = xn.shape
    tok = pl.BlockSpec((tm, d), lambda i: (i, 0))
    weights = [p["mix"], p["vec"], p["wr"], p["wk"], p["wv"], *p["w1"], *p["w2"], *p["a1"], *p["a2"],
               p["g1"], p["g2"]]
    return pl.pallas_call(
        _rwkv_proj_kernel,
        grid=(t // tm,),
        in_specs=[tok, tok] + [_const_spec(w.shape) for w in weights],
        out_specs=[tok] * 7,
        out_shape=[jax.ShapeDtypeStruct((t, d), F32)] * 7,
        compiler_params=_params("parallel"),
        name="rwkv_proj",
    )(xn, prev, *weights)


def _rwkv_proj_fused(h, gain, shift0, p, tm, length):
    t, d = h.shape
    bsz = t // length
    tiles_per_seq = length // tm
    tok = pl.BlockSpec((tm, d), lambda i: (i, 0))
    before = pl.BlockSpec((SUBLANES, d), lambda i: (jnp.maximum(i * (tm // SUBLANES) - 1, 0), 0))
    per_seq = pl.BlockSpec((1, 1, d), lambda i: (i // tiles_per_seq, 0, 0))
    weights = [p["mix"], p["vec"], p["wr"], p["wk"], p["wv"], *p["w1"], *p["w2"], *p["a1"], *p["a2"],
               p["g1"], p["g2"]]
    kern = functools.partial(_rwkv_proj_fused_kernel, tiles_per_seq=tiles_per_seq)
    *proj, last = pl.pallas_call(
        kern,
        grid=(t // tm,),
        in_specs=[tok, before, per_seq, _const_spec((1, d))] + [_const_spec(w.shape) for w in weights],
        out_specs=[tok] * 7 + [per_seq],
        out_shape=[jax.ShapeDtypeStruct((t, d), F32)] * 7 + [jax.ShapeDtypeStruct((bsz, 1, d), F32)],
        compiler_params=_params("arbitrary"),
        name="rwkv_norm_proj",
    )(h, h, shift0.reshape(bsz, 1, d), gain.reshape(1, d), *weights)
    return proj, last.reshape(bsz, d)


def _rwkv_scan_kernel(r_ref, w_ref, k_ref, v_ref, kk_ref, a_ref, s0_ref, lnw_ref, lnb_ref, rk_ref,
                      o_ref, sfin_ref, state, o_rows):
    tb = pl.program_id(1)
    steps = r_ref.shape[0]

    @pl.when(tb == 0)
    def _():
        state[...] = s0_ref[...]

    live = slice(0, RW_HEAD)

    def step(t, carry):
        r = r_ref[t, live, :]
        w = w_ref[t, live, :]
        k = k_ref[t, live, :]
        vv = v_ref[t, live, :]
        kku = kk_ref[t, live, :]
        norm = jnp.sqrt(jnp.sum(kku * kku, axis=0, keepdims=True))
        kk = kku / jnp.maximum(norm, 1e-12)
        alpha = -kk
        beta = kk * a_ref[t, live, :]

        def row(vi, c):
            s_v = state[vi]
            u = jnp.sum(s_v * alpha, axis=0, keepdims=True)
            s_new = s_v * w + u * beta + v_ref[t, pl.ds(vi, 1), :] * k
            state[vi] = s_new
            o_rows[pl.ds(vi, 1), :] = jnp.sum(s_new * r, axis=0, keepdims=True)
            return c

        lax.fori_loop(0, RW_HEAD, row, 0, unroll=32)
        o = o_rows[...]
        mu = jnp.mean(o, axis=0, keepdims=True)
        var = jnp.mean(jnp.square(o - mu), axis=0, keepdims=True)
        on = (o - mu) * lax.rsqrt(var + RW_LN_EPS)
        bonus = jnp.sum(r * k * rk_ref[...], axis=0, keepdims=True) * vv
        o_ref[t, live, :] = on * lnw_ref[...] + lnb_ref[...] + bonus
        return carry

    lax.fori_loop(0, steps, step, 0)

    @pl.when(tb == pl.num_programs(1) - 1)
    def _():
        sfin_ref[...] = state[...]


def _rwkv_scan(seqs, s0, lnw, lnb, rk, tt):
    length, pitch, probs = seqs[0].shape
    n = RW_HEAD
    groups = probs // LANES
    seq_spec = pl.BlockSpec((tt, pitch, LANES), lambda g, i: (i, 0, g))
    st_spec = pl.BlockSpec((n, n, LANES), lambda g, i: (0, 0, g))
    lane_spec = pl.BlockSpec((n, LANES), lambda g, i: (0, 0))
    return pl.pallas_call(
        _rwkv_scan_kernel,
        grid=(groups, length // tt),
        in_specs=[seq_spec] * 6 + [st_spec] + [lane_spec] * 3,
        out_specs=[seq_spec, st_spec],
        out_shape=[jax.ShapeDtypeStruct((length, pitch, probs), F32),
                   jax.ShapeDtypeStruct((n, n, probs), F32)],
        scratch_shapes=[pltpu.VMEM((n, n, LANES), F32), pltpu.VMEM((n, LANES), F32)],
        compiler_params=_params("arbitrary", "arbitrary"),
        name="rwkv_scan",
    )(*seqs, s0, lnw, lnb, rk)


def _rwkv_out_kernel(o_ref, g_ref, h_ref, wo_ref, out_ref):
    out_ref[...] = h_ref[...] + _bdot(o_ref[...] * g_ref[...], wo_ref[...])


def _rwkv_out(o, g, h, wo, tm):
    t, d = h.shape
    tok = pl.BlockSpec((tm, d), lambda i: (i, 0))
    return pl.pallas_call(
        _rwkv_out_kernel,
        grid=(t // tm,),
        in_specs=[tok, tok, tok, _const_spec(wo.shape)],
        out_specs=tok,
        out_shape=jax.ShapeDtypeStruct((t, d), F32),
        compiler_params=_params("parallel"),
        name="rwkv_out",
    )(o, g, h, wo)


def _mamba_in_kernel(h_ref, g_ref, w_ref, z_ref, xbc_ref, dt_ref):
    xn = _rms(h_ref[...], g_ref[...]).astype(BF16)
    z_ref[...] = jnp.dot(xn, w_ref[:, :M_D_INNER], preferred_element_type=F32)
    xbc_ref[...] = jnp.dot(xn, w_ref[:, M_D_INNER:M_D_INNER + M_CONV_DIM], preferred_element_type=F32)
    dt_ref[...] = jnp.dot(xn, w_ref[:, M_D_INNER + M_CONV_DIM:], preferred_element_type=F32)


def _mamba_in(h, g, w_in, tm):
    t, d = h.shape
    return pl.pallas_call(
        _mamba_in_kernel,
        grid=(t // tm,),
        in_specs=[pl.BlockSpec((tm, d), lambda i: (i, 0)), _const_spec((1, d)), _const_spec(w_in.shape)],
        out_specs=[pl.BlockSpec((tm, M_D_INNER), lambda i: (i, 0)),
                   pl.BlockSpec((tm, M_CONV_DIM), lambda i: (i, 0)),
                   pl.BlockSpec((tm, M_DT_PAD), lambda i: (i, 0))],
        out_shape=[jax.ShapeDtypeStruct((t, M_D_INNER), F32),
                   jax.ShapeDtypeStruct((t, M_CONV_DIM), F32),
                   jax.ShapeDtypeStruct((t, M_DT_PAD), F32)],
        compiler_params=_params("parallel"),
        name="mamba_in_proj",
    )(h, g.reshape(1, d), w_in)


def _mamba_ssd_kernel(xbc_ref, z_ref, dt_ref, h_ref, conv0_ref, ssm0_ref, convw_ref, vec_ref,
                      dvec_ref, expand_ref, normw_ref, wout_ref,
                      out_ref, ssm_ref,
                      conv_buf, src_adt, src_acs, src_b, src_xdt, src_xdt_st, state, y_buf, *, seq_len, rows):
    b = pl.program_id(0)
    c = pl.program_id(1)
    q = M_CHUNK
    halo = SUBLANES
    d_bc = M_GROUPS * M_STATE

    @pl.when(c == 0)
    def _():
        state[...] = ssm0_ref[0]
        conv_buf[0:halo, :] = conv0_ref[0]

    @pl.when(c > 0)
    def _():
        conv_buf[0:halo, :] = conv_buf[rows:rows + halo, :]

    if rows < q:
        @pl.when((b == 0) & (c == 0))
        def _():
            src_adt[...] = jnp.zeros(src_adt.shape, F32)
            src_acs[...] = jnp.zeros(src_acs.shape, F32)
            src_b[...] = jnp.zeros(src_b.shape, BF16)
            src_xdt[...] = jnp.zeros(src_xdt.shape, BF16)
            src_xdt_st[...] = jnp.zeros(src_xdt_st.shape, BF16)

    conv_buf[halo:halo + rows, :] = xbc_ref[0]
    valid = (lax.broadcasted_iota(jnp.int32, (rows, 1), 0) + c * rows) < seq_len
    conv = vec_ref[0:1, :]
    for j in range(M_CONV):
        conv = conv + conv_buf[halo - (M_CONV - 1) + j:halo - (M_CONV - 1) + j + rows, :] * convw_ref[j:j + 1, :]
    act = jnp.where(valid, jax.nn.silu(conv), 0.0)
    xs = act[:, :M_D_INNER]
    dt = jnp.where(valid, _softplus(dt_ref[0] + dvec_ref[0:1, :]), 0.0)
    src_adt[0:rows, :] = dt * dvec_ref[1:2, :]
    li = lax.broadcasted_iota(jnp.int32, (rows, q), 0)
    si = lax.broadcasted_iota(jnp.int32, (rows, q), 1)
    causal = li >= si
    acs = _select_dot(causal.astype(BF16), src_adt[...])
    src_acs[0:rows, :] = acs
    acs_t = src_acs[...].T
    expand = expand_ref[...]
    dt_full = _dot_select(dt, expand)
    acs_full = _dot_select(acs, expand)
    last = acs[rows - 1:rows, :]
    last_full = acs_full[rows - 1:rows, :]
    xdt = xs * dt_full
    src_xdt[0:rows, :] = xdt.astype(BF16)
    src_xdt_st[0:rows, :] = (xdt * jnp.exp(last_full - acs_full)).astype(BF16)
    src_b[0:rows, :] = act[:, M_D_INNER:M_D_INNER + d_bc].astype(BF16)
    eacs_full = jnp.exp(acs_full)
    state_decay = jnp.exp(last)
    width = M_HPG * M_HEADDIM

    for g in range(M_GROUPS):
        cols = slice(g * width, (g + 1) * width)
        b_g = src_b[:, g * M_STATE:(g + 1) * M_STATE]
        c_g = act[:, M_D_INNER + d_bc + g * M_STATE:M_D_INNER + d_bc + (g + 1) * M_STATE].astype(BF16)
        cb = lax.dot_general(c_g, b_g, (((1,), (1,)), ((), ())), preferred_element_type=F32)
        h_g = state[g * M_HPG:(g + 1) * M_HPG].reshape(width, M_STATE)
        y_off = lax.dot_general(c_g, h_g.astype(BF16), (((1,), (1,)), ((), ())),
                                preferred_element_type=F32)
        upd = lax.dot_general(src_xdt_st[:, cols], b_g, (((0,), (0,)), ((), ())),
                              preferred_element_type=F32)
        ys = []
        for r in range(g * M_HPG, (g + 1) * M_HPG):
            lo = r * M_HEADDIM
            decay = jnp.where(causal, jnp.exp(acs[:, r:r + 1] - acs_t[r:r + 1, :]), 0.0)
            ys.append(jnp.dot((cb * decay).astype(BF16), src_xdt[:, lo:lo + M_HEADDIM],
                              preferred_element_type=F32))
            sub = slice((r - g * M_HPG) * M_HEADDIM, (r - g * M_HPG + 1) * M_HEADDIM)
            state[r] = state[r] * state_decay[:, r:r + 1] + upd[sub, :]
        y_buf[:, cols] = jnp.concatenate(ys, axis=1) + y_off * eacs_full[:, cols]

    y = y_buf[...] + vec_ref[1:2, :M_D_INNER] * xs
    yg = y * jax.nn.silu(z_ref[0])
    parts = []
    for g in range(M_GROUPS):
        part = yg[:, g * width:(g + 1) * width]
        parts.append(part * lax.rsqrt(jnp.mean(part * part, axis=-1, keepdims=True) + NORM_EPS))
    yn = jnp.concatenate(parts, axis=1) * normw_ref[...]
    out_ref[0] = h_ref[0] + _bdot(yn, wout_ref[...])

    @pl.when(c == pl.num_programs(1) - 1)
    def _():
        ssm_ref[0] = state[...]


def _mamba_ssd(xbc, z, dt, h, conv0, ssm0, p, rows, seq_len):
    bsz, length, _ = xbc.shape
    chunks = pl.cdiv(length, rows)
    weights = [p["conv_w"], p["vec"], p["dvec"], p["expand"], p["norm_w"], p["w_out"]]

    def seq_spec(width):
        return pl.BlockSpec((1, rows, width), lambda b, c: (b, c, 0))

    kern = functools.partial(_mamba_ssd_kernel, seq_len=seq_len, rows=rows)
    return pl.pallas_call(
        kern,
        grid=(bsz, chunks),
        in_specs=[seq_spec(M_CONV_DIM), seq_spec(M_D_INNER), seq_spec(M_DT_PAD), seq_spec(D_MODEL),
                  pl.BlockSpec((1, SUBLANES, M_CONV_DIM), lambda b, c: (b, 0, 0)),
                  pl.BlockSpec((1, M_HEADS, M_HEADDIM, M_STATE), lambda b, c: (b, 0, 0, 0))]
                 + [_const_spec(w.shape) for w in weights],
        out_specs=[seq_spec(D_MODEL),
                   pl.BlockSpec((1, M_HEADS, M_HEADDIM, M_STATE), lambda b, c: (b, 0, 0, 0))],
        out_shape=[jax.ShapeDtypeStruct((bsz, length, D_MODEL), F32),
                   jax.ShapeDtypeStruct((bsz, M_HEADS, M_HEADDIM, M_STATE), F32)],
        scratch_shapes=[pltpu.VMEM((rows + SUBLANES, M_CONV_DIM), F32),
                        pltpu.VMEM((M_CHUNK, M_DT_PAD), F32),
                        pltpu.VMEM((M_CHUNK, M_DT_PAD), F32),
                        pltpu.VMEM((M_CHUNK, M_GROUPS * M_STATE), BF16),
                        pltpu.VMEM((M_CHUNK, M_D_INNER), BF16),
                        pltpu.VMEM((M_CHUNK, M_D_INNER), BF16),
                        pltpu.VMEM((M_HEADS, M_HEADDIM, M_STATE), F32),
                        pltpu.VMEM((rows, M_D_INNER), F32)],
        compiler_params=_params("arbitrary", "arbitrary"),
        name="mamba_ssd",
    )(xbc, z, dt, h, conv0, ssm0, *weights)


_CANDS = [(ra, rb) for ra in range(P_TOPK) for rb in range(P_TOPK) if (ra + 1) * (rb + 1) <= P_TOPK]


def _peer_route_kernel(h_ref, g_ref, wq_ref, keys_ref, keys_lo_ref,
                       xt_ref, r1_ref, e1_ref, m0_ref, w0_ref,
                       q_buf, s_buf, code0, code_book, best_vals, cand, counts, inv_z):
    tm = h_ref.shape[0]
    neg_inf = float("-inf")
    xn = _rms(h_ref[...], g_ref[...])
    xt = xn.T.astype(BF16)
    xt_ref[...] = xt
    q_buf[...] = jnp.dot(wq_ref[...], xt, preferred_element_type=F32)
    key_iota = lax.broadcasted_iota(jnp.int32, (P_NKEYS, tm), 0).astype(F32)

    def half(m, carry):
        head = m // 2
        z = m % 2
        row0 = pl.multiple_of(m * P_NKEYS, P_NKEYS)
        qm = q_buf[pl.ds(row0, P_NKEYS), :]
        q_hi = qm.astype(BF16)
        q_lo = (qm - q_hi.astype(F32)).astype(BF16)
        k_hi = keys_ref[m]
        s = (jnp.dot(k_hi, q_hi, preferred_element_type=F32)
             + (jnp.dot(k_hi, q_lo, preferred_element_type=F32)
                + jnp.dot(keys_lo_ref[m], q_hi, preferred_element_type=F32)))

        s_buf[...] = s

        def next_distinct(j, prev):
            sj = s_buf[...]
            best = jnp.max(jnp.where(sj < prev, sj, neg_inf), axis=0, keepdims=True)
            best_vals[z, j, pl.ds(head, 1), :] = best
            return best

        def extract_ties(j, sc):
            cur, rank = sc
            best = jnp.max(cur, axis=0, keepdims=True)
            first = jnp.min(jnp.where(cur == best, key_iota, float(P_NKEYS)), axis=0, keepdims=True)
            sel = key_iota == first
            best_vals[z, j, pl.ds(head, 1), :] = best
            return jnp.where(sel, neg_inf, cur), jnp.where(sel, jnp.asarray(j, dtype=F32), rank)

        last = lax.fori_loop(0, P_TOPK, next_distinct, jnp.full((1, tm), float("inf"), F32))
        reached = jnp.sum(jnp.where(s >= last, 1.0, 0.0), axis=0, keepdims=True)
        has_ties = jnp.max(jnp.abs(reached - float(P_TOPK))) > 0.0

        def rank_by_count():
            rank = jnp.zeros((P_NKEYS, tm), F32)
            for jj in range(P_TOPK):
                rank = rank + jnp.where(best_vals[z, jj, pl.ds(head, 1), :] > s, 1.0, 0.0)
            return rank

        def rank_with_ties():
            no_rank = jnp.full((P_NKEYS, tm), float(P_NKEYS), F32)
            return lax.fori_loop(0, P_TOPK, extract_ties, (s, no_rank))[1]

        e = jnp.exp(s - jnp.max(s, axis=0, keepdims=True))

        @pl.when(z == 0)
        def _():
            w0_ref[head] = e

            @pl.when(has_ties)
            def _():
                code0[head] = rank_with_ties()
                for ra in range(P_TOPK):
                    code_book[ra, pl.ds(head, 1), :] = jnp.full((1, tm), float(ra), F32)

            @pl.when(jnp.logical_not(has_ties))
            def _():
                code0[head] = s
                for ra in range(P_TOPK):
                    code_book[ra, pl.ds(head, 1), :] = best_vals[0, ra, pl.ds(head, 1), :]

        @pl.when(z == 1)
        def _():
            rank = lax.cond(has_ties, rank_with_ties, rank_by_count)
            for lt in range(tm // LANES):
                r1_ref[head, lt] = rank[:, lt * LANES:(lt + 1) * LANES].astype(BF16)
                e1_ref[head, lt] = e[:, lt * LANES:(lt + 1) * LANES].astype(BF16)

        return carry

    lax.fori_loop(0, 2 * P_HEADS, half, 0)

    ex0 = [jnp.exp(best_vals[0, ra] - best_vals[0, 0]) for ra in range(P_TOPK)]
    ex1 = [jnp.exp(best_vals[1, rb] - best_vals[1, 0]) for rb in range(P_TOPK)]
    for i, (ra, rb) in enumerate(_CANDS):
        cand[i] = best_vals[0, ra] + best_vals[1, rb]

    def next_sum(it, prev):
        best = None
        for i in range(len(_CANDS)):
            ci = cand[i]
            below = jnp.where(ci < prev, ci, neg_inf)
            best = below if best is None else jnp.maximum(best, below)
        return best

    last_sum = lax.fori_loop(0, P_TOPK, next_sum, jnp.full((P_HEADS, tm), float("inf"), F32))
    hits = [jnp.where(cand[i] >= last_sum, 1.0, 0.0) for i in range(len(_CANDS))]
    n_hits = functools.reduce(lambda x, y: x + y, hits)
    sums_tie = jnp.max(jnp.abs(n_hits - float(P_TOPK))) > 0.0

    def select_distinct():
        z = jnp.zeros((P_HEADS, tm), F32)
        per_rank = [None] * P_TOPK
        for i, (ra, rb) in enumerate(_CANDS):
            per_rank[ra] = hits[i] if per_rank[ra] is None else per_rank[ra] + hits[i]
            z = z + hits[i] * (ex0[ra] * ex1[rb])
        for ra in range(P_TOPK):
            counts[ra] = per_rank[ra]
        return z

    def select_with_ties():
        counts[...] = jnp.zeros(counts.shape, F32)

        def pick(it, z):
            best = cand[0]
            for i in range(1, len(_CANDS)):
                best = jnp.maximum(best, cand[i])
            found = jnp.zeros((P_HEADS, tm), F32)
            for i, (ra, rb) in enumerate(_CANDS):
                ci = cand[i]
                hit = jnp.where(ci == best, 1.0, 0.0) * (1.0 - found)
                found = found + hit
                cand[i] = jnp.where(hit > 0.0, neg_inf, ci)
                counts[ra] = counts[ra] + hit
                z = z + hit * (ex0[ra] * ex1[rb])
            return z

        return lax.fori_loop(0, P_TOPK, pick, jnp.zeros((P_HEADS, tm), F32))

    inv_z[...] = 1.0 / lax.cond(sums_tie, select_with_ties, select_distinct)

    def finish(head, carry):
        code = code0[head]
        m0 = jnp.zeros((P_NKEYS, tm), F32)
        for ra in range(P_TOPK):
            m0 = jnp.where(code == code_book[ra, pl.ds(head, 1), :], counts[ra, pl.ds(head, 1), :], m0)
        m0_ref[head] = m0
        w0_ref[head] = w0_ref[head] * inv_z[pl.ds(head, 1), :]
        return carry

    lax.fori_loop(0, P_HEADS, finish, 0)


def _peer_route(h, g, wq_t, keys, keys_lo, tm):
    t, d = h.shape
    tiles = pl.cdiv(t, tm)
    t_pad = tiles * tm
    tile = pl.BlockSpec((P_HEADS, P_NKEYS, tm), lambda i: (0, 0, i))
    tile_f32 = jax.ShapeDtypeStruct((P_HEADS, P_NKEYS, t_pad), F32)
    tile_bf16 = jax.ShapeDtypeStruct((P_HEADS, t_pad // LANES, P_NKEYS, LANES), BF16)
    lane_tiles = pl.BlockSpec((P_HEADS, tm // LANES, P_NKEYS, LANES), lambda i: (0, i, 0, 0))
    return pl.pallas_call(
        _peer_route_kernel,
        grid=(tiles,),
        in_specs=[pl.BlockSpec((tm, d), lambda i: (i, 0)), _const_spec((1, d)),
                  _const_spec(wq_t.shape), _const_spec(keys.shape), _const_spec(keys_lo.shape)],
        out_specs=[pl.BlockSpec((d, tm), lambda i: (0, i)), lane_tiles, lane_tiles, tile, tile],
        out_shape=[jax.ShapeDtypeStruct((d, t_pad), BF16), tile_bf16, tile_bf16, tile_f32, tile_f32],
        scratch_shapes=[pltpu.VMEM((P_HEADS * P_QDIM, tm), F32),
                        pltpu.VMEM((P_NKEYS, tm), F32),
                        pltpu.VMEM((P_HEADS, P_NKEYS, tm), F32),
                        pltpu.VMEM((P_TOPK, P_HEADS, tm), F32),
                        pltpu.VMEM((2, P_TOPK, P_HEADS, tm), F32),
                        pltpu.VMEM((len(_CANDS), P_HEADS, tm), F32),
                        pltpu.VMEM((P_TOPK, P_HEADS, tm), F32),
                        pltpu.VMEM((P_HEADS, tm), F32)],
        compiler_params=_params("parallel"),
        name="peer_route",
    )(h, g.reshape(1, d), wq_t, keys, keys_lo)


def _gelu(x):
    return 0.5 * x * (1.0 + lax.erf(x * (1.0 / math.sqrt(2.0))))


def _peer_dense_kernel(xt_ref, u_ref, vt_ref, r1_ref, e1_ref, m0_ref, w0_ref, h_ref, gfin_ref,
                       out_ref, acc, hid, prob, *, final_norm):
    j = pl.program_id(1)
    keys_per_step = P_EXPERT_TILE // P_NKEYS

    @pl.when(j == 0)
    def _():
        acc[...] = jnp.zeros(acc.shape, F32)

    tm = xt_ref.shape[1]
    hid[...] = jnp.dot(u_ref[...], xt_ref[...], preferred_element_type=F32)
    zero = jnp.zeros((), BF16)
    for ii in range(keys_per_step):
        i1 = j * keys_per_step + ii
        rows = slice(ii * P_NKEYS, (ii + 1) * P_NKEYS)
        m0_rows = [m0_ref[head, pl.ds(i1, 1), :] for head in range(P_HEADS)]
        w0_rows = [w0_ref[head, pl.ds(i1, 1), :] for head in range(P_HEADS)]

        def packed_rows(row):
            one = jnp.broadcast_to(row, (2 * SUBLANES, LANES)).astype(BF16)
            return jnp.tile(one, (P_NKEYS // (2 * SUBLANES), 1))

        for lt in range(tm // LANES):
            lanes = slice(lt * LANES, (lt + 1) * LANES)
            gate = None
            for head in range(P_HEADS):
                m0 = packed_rows(m0_rows[head][:, lanes])
                w0 = packed_rows(w0_rows[head][:, lanes])
                term = w0 * jnp.where(r1_ref[head, lt] < m0, e1_ref[head, lt], zero)
                gate = term if gate is None else gate + term
            prob[rows, lanes] = gate * _gelu(hid[rows, lanes]).astype(BF16)
    acc[...] += jnp.dot(vt_ref[...], prob[...], preferred_element_type=F32)

    @pl.when(j == pl.num_programs(1) - 1)
    def _():
        res = h_ref[...] + acc[...].T
        if final_norm:
            res = _rms(res, gfin_ref[...])
        out_ref[...] = res


def _peer_dense(xt, u_bf, vt_bf, r1, e1, m0, w0, h, g_final, tm, final_norm):
    t, d = h.shape
    n_exp = u_bf.shape[0]
    tile = pl.BlockSpec((P_HEADS, P_NKEYS, tm), lambda i, j: (0, 0, i))
    lane_tiles = pl.BlockSpec((P_HEADS, tm // LANES, P_NKEYS, LANES), lambda i, j: (0, i, 0, 0))
    kern = functools.partial(_peer_dense_kernel, final_norm=final_norm)
    return pl.pallas_call(
        kern,
        grid=(pl.cdiv(t, tm), n_exp // P_EXPERT_TILE),
        in_specs=[pl.BlockSpec((d, tm), lambda i, j: (0, i)),
                  pl.BlockSpec((P_EXPERT_TILE, d), lambda i, j: (j, 0)),
                  pl.BlockSpec((d, P_EXPERT_TILE), lambda i, j: (0, j)),
                  lane_tiles, lane_tiles, tile, tile,
                  pl.BlockSpec((tm, d), lambda i, j: (i, 0)),
                  _const_spec((1, d))],
        out_specs=pl.BlockSpec((tm, d), lambda i, j: (i, 0)),
        out_shape=jax.ShapeDtypeStruct((t, d), F32),
        scratch_shapes=[pltpu.VMEM((d, tm), F32),
                        pltpu.VMEM((P_EXPERT_TILE, tm), F32),
                        pltpu.VMEM((P_EXPERT_TILE, tm), BF16)],
        compiler_params=_params("parallel", "arbitrary"),
        name="peer_dense",
    )(xt, u_bf, vt_bf, r1, e1, m0, w0, h, g_final.reshape(1, d))


def _pad_cols(w, n):
    return jnp.pad(w, ((0, 0), (0, n - w.shape[1])))


def _pad_rows(w, n):
    return jnp.pad(w, ((0, n - w.shape[0]), (0, 0)))


def _prep_rwkv(mix, w_rkv, w0, w1, w2, a0, a1, a2, g1, g2, k_k, k_a, r_k, ln_w, ln_b, w_o):
    d = D_MODEL
    lane = lambda x: jnp.repeat(x.reshape(RW_HEADS, RW_HEAD).T, LANES // RW_HEADS, axis=1)
    return {
        "mix": _pad_rows(mix, SUBLANES),
        "vec": _pad_rows(jnp.stack([w0, a0, k_k, k_a]), SUBLANES),
        "wr": w_rkv[0].astype(BF16), "wk": w_rkv[1].astype(BF16), "wv": w_rkv[2].astype(BF16),
        "w1": _split_bf16(_pad_cols(w1, LORA_PAD)), "w2": _split_bf16(_pad_rows(w2, LORA_PAD)),
        "a1": _split_bf16(_pad_cols(a1, LORA_PAD)), "a2": _split_bf16(_pad_rows(a2, LORA_PAD)),
        "g1": _pad_cols(g1, GATE_LORA_PAD).astype(BF16), "g2": _pad_rows(g2, GATE_LORA_PAD).astype(BF16),
        "lnw": lane(ln_w), "lnb": lane(ln_b), "rk": lane(r_k.reshape(d)),
        "wo": w_o.astype(BF16),
    }


def _prep_mamba(in_proj, conv_w, conv_b, dt_bias, a_log, d_skip, norm_w, out_proj):
    head_of_channel = jnp.arange(M_D_INNER) // M_HEADDIM
    expand = (jnp.arange(M_DT_PAD)[:, None] == head_of_channel[None, :]).astype(BF16)
    d_full = jnp.pad(jnp.repeat(d_skip, M_HEADDIM), (0, M_CONV_DIM - M_D_INNER))
    pad_h = lambda x: jnp.pad(x, (0, M_DT_PAD - M_HEADS))
    return {
        "w_in": _pad_cols(in_proj, M_IN_DIM - M_HEADS + M_DT_PAD).astype(BF16),
        "conv_w": _pad_rows(conv_w, SUBLANES),
        "vec": _pad_rows(jnp.stack([conv_b, d_full]), SUBLANES),
        "dvec": _pad_rows(jnp.stack([pad_h(dt_bias), pad_h(-jnp.exp(a_log))]), SUBLANES),
        "expand": expand,
        "norm_w": norm_w.reshape(1, M_D_INNER),
        "w_out": out_proj.astype(BF16),
    }


def _prep_peer(w_q, sub_keys, u_tab, v_tab):
    keys = jnp.transpose(sub_keys, (1, 0, 2, 3)).reshape(2 * P_HEADS, P_NKEYS, P_QDIM // 2)
    keys_hi, keys_lo = _split_bf16(keys)
    return {"wq_t": w_q.T.astype(BF16), "keys": keys_hi, "keys_lo": keys_lo,
            "u": u_tab.astype(BF16), "vt": v_tab.T.astype(BF16)}


SCAN_GROUP = LANES // RW_HEADS
RELAYOUT_STEPS = 128
RELAYOUT_PITCH = RW_HEAD + SUBLANES
SCAN_PITCH = RW_HEAD + SUBLANES


def _to_scan_kernel(x_ref, o_ref, stage):
    for b in range(SCAN_GROUP):
        for hp in range(RW_HEADS // 2):
            t = x_ref[b, :, hp * LANES:(hp + 1) * LANES].T
            for hl in range(2):
                lane = (2 * hp + hl) * SCAN_GROUP + b
                stage[lane * RELAYOUT_PITCH:lane * RELAYOUT_PITCH + RW_HEAD, :] = t[hl * RW_HEAD:(hl + 1) * RW_HEAD, :]
    for k in range(RW_HEAD):
        by_lane = stage[pl.ds(k, LANES, stride=RELAYOUT_PITCH), :]
        o_ref[pl.ds(k, RELAYOUT_STEPS, stride=SCAN_PITCH), :] = by_lane.T


def _to_scan_pallas(x, length):
    x3 = x.reshape(SCAN_GROUP, length, D_MODEL)
    out = pl.pallas_call(
        _to_scan_kernel,
        grid=(pl.cdiv(length, RELAYOUT_STEPS),),
        in_specs=[pl.BlockSpec((SCAN_GROUP, RELAYOUT_STEPS, D_MODEL), lambda i: (0, i, 0))],
        out_specs=pl.BlockSpec((RELAYOUT_STEPS * SCAN_PITCH, LANES), lambda i: (i, 0)),
        out_shape=jax.ShapeDtypeStruct((length * SCAN_PITCH, LANES), F32),
        scratch_shapes=[pltpu.VMEM((LANES * RELAYOUT_PITCH, LANES), F32)],
        compiler_params=_params("parallel"),
        name="to_scan_layout",
    )(x3)
    return out.reshape(length, SCAN_PITCH, LANES)


def _rwkv_out_scan_kernel(o_ref, g_ref, h_ref, wo_ref, out_ref, stage):
    for v in range(RW_HEAD):
        by_step = o_ref[pl.ds(v, RELAYOUT_STEPS, stride=SCAN_PITCH), :]
        stage[pl.ds(v, LANES, stride=RELAYOUT_PITCH), :] = by_step.T
    for b in range(SCAN_GROUP):
        tiles = []
        for hp in range(RW_HEADS // 2):
            halves = []
            for hl in range(2):
                lane = (2 * hp + hl) * SCAN_GROUP + b
                halves.append(stage[lane * RELAYOUT_PITCH:lane * RELAYOUT_PITCH + RW_HEAD, :])
            tiles.append(jnp.concatenate(halves, axis=0).T)
        o_b = jnp.concatenate(tiles, axis=1)
        out_ref[b] = h_ref[b] + _bdot(o_b * g_ref[b], wo_ref[...])


def _rwkv_out_scan(o_scan, g, h, wo, length):
    d = D_MODEL
    seq = pl.BlockSpec((SCAN_GROUP, RELAYOUT_STEPS, d), lambda i: (0, i, 0))
    out = pl.pallas_call(
        _rwkv_out_scan_kernel,
        grid=(pl.cdiv(length, RELAYOUT_STEPS),),
        in_specs=[pl.BlockSpec((RELAYOUT_STEPS * SCAN_PITCH, LANES), lambda i: (i, 0)), seq, seq,
                  _const_spec(wo.shape)],
        out_specs=seq,
        out_shape=jax.ShapeDtypeStruct((SCAN_GROUP, length, d), F32),
        scratch_shapes=[pltpu.VMEM((LANES * RELAYOUT_PITCH, LANES), F32)],
        compiler_params=_params("parallel"),
        name="rwkv_out_from_scan",
    )(o_scan.reshape(length * SCAN_PITCH, LANES), g.reshape(SCAN_GROUP, length, d), h.reshape(SCAN_GROUP, length, d), wo)
    return out.reshape(SCAN_GROUP * length, d)


def _to_scan(x, bsz, length):
    if bsz == SCAN_GROUP and length >= RELAYOUT_STEPS:
        return _to_scan_pallas(x, length)
    x = x.reshape(bsz // SCAN_GROUP, SCAN_GROUP, length, RW_HEADS, RW_HEAD)
    x = jnp.transpose(x, (2, 4, 0, 3, 1)).reshape(length, RW_HEAD, bsz * RW_HEADS)
    return jnp.pad(x, ((0, 0), (0, SCAN_PITCH - RW_HEAD), (0, 0)))


def _from_scan(x, bsz, length):
    x = x[:, :RW_HEAD].reshape(length, RW_HEAD, bsz // SCAN_GROUP, RW_HEADS, SCAN_GROUP)
    return jnp.transpose(x, (2, 4, 0, 3, 1)).reshape(bsz * length, D_MODEL)


def _state_to_scan(wkv, bsz):
    x = wkv.reshape(bsz // SCAN_GROUP, SCAN_GROUP, RW_HEADS, RW_HEAD, RW_HEAD)
    return jnp.transpose(x, (3, 4, 0, 2, 1)).reshape(RW_HEAD, RW_HEAD, bsz * RW_HEADS)


def _state_from_scan(s, bsz):
    x = s.reshape(RW_HEAD, RW_HEAD, bsz // SCAN_GROUP, RW_HEADS, SCAN_GROUP)
    return jnp.transpose(x, (2, 4, 3, 0, 1)).reshape(bsz, RW_HEADS, RW_HEAD, RW_HEAD)


def _rwkv_layer(h, g_mix, shift0, wkv0, p, bsz, length, tm, tt):
    if length % tm == 0:
        (r, w, k, v, kk, a, g), shift = _rwkv_proj_fused(h, g_mix, shift0, p, tm, length)
    else:
        xn = _norm(h, g_mix, tm)
        xn3 = xn.reshape(bsz, length, D_MODEL)
        prev = jnp.concatenate([shift0[:, None, :], xn3[:, :-1]], axis=1).reshape(bsz * length, D_MODEL)
        r, w, k, v, kk, a, g = _rwkv_proj(xn, prev, p, tm)
        shift = xn3[:, -1]
    seqs = [_to_scan(x, bsz, length) for x in (r, w, k, v, kk, a)]
    o, s_fin = _rwkv_scan(seqs, _state_to_scan(wkv0, bsz), p["lnw"], p["lnb"], p["rk"], tt)
    if bsz == SCAN_GROUP and length >= RELAYOUT_STEPS:
        h_new = _rwkv_out_scan(o, g, h, p["wo"], length)
    else:
        h_new = _rwkv_out(_from_scan(o, bsz, length), g, h, p["wo"], tm)
    return h_new, shift, _state_from_scan(s_fin, bsz)


def _mamba_layer(h, g_mix, conv0, ssm0, p, bsz, length, tm, rows):
    z, xbc, dt = _mamba_in(h, g_mix, p["w_in"], tm)
    xbc3 = xbc.reshape(bsz, length, M_CONV_DIM)
    tail = M_CONV - 1
    conv_new = xbc3[:, length - tail:] if length >= tail else jnp.concatenate([conv0, xbc3], axis=1)[:, length:]
    pad_t = (-length) % SUBLANES if length < rows else 0

    def seq(x, c):
        x = x.reshape(bsz, length, c)
        return jnp.pad(x, ((0, 0), (0, pad_t), (0, 0))) if pad_t else x

    conv0_p = jnp.pad(conv0, ((0, 0), (SUBLANES - (M_CONV - 1), 0), (0, 0)))
    out, ssm = _mamba_ssd(seq(xbc, M_CONV_DIM), seq(z, M_D_INNER), seq(dt, M_DT_PAD), seq(h, D_MODEL),
                          conv0_p, ssm0, p, min(rows, length + pad_t), length)
    if pad_t:
        out = out[:, :length]
    return out.reshape(bsz * length, D_MODEL), conv_new, ssm


def _peer_layer(h, g_ffn, p, tm, g_final=None):
    xt, r1, e1, m0, w0 = _peer_route(h, g_ffn, p["wq_t"], p["keys"], p["keys_lo"], tm)
    final = g_final is not None
    return _peer_dense(xt, p["u"], p["vt"], r1, e1, m0, w0, h, g_final if final else g_ffn, tm, final)


def _trunk(h, shift0, wkv0, conv0, ssm0, norm_mix, norm_ffn, norm_final, rwkv_p, mamba_p, peer_p,
           bsz, length, tm, tt, peer_tm, rows):
    h, shift, wkv = _rwkv_layer(h, norm_mix[0], shift0, wkv0, rwkv_p, bsz, length, tm, tt)
    h = _peer_layer(h, norm_ffn[0], peer_p[0], peer_tm)
    h, conv, ssm = _mamba_layer(h, norm_mix[1], conv0, ssm0, mamba_p, bsz, length, tm, rows)
    y = _peer_layer(h, norm_ffn[1], peer_p[1], peer_tm, g_final=norm_final)
    return y.reshape(bsz, length, D_MODEL), shift[None], wkv[None], conv[None], ssm[None]


def kernel(x_prompt, x_sample, state_rwkv_shift, state_rwkv_wkv, state_mamba_conv, state_mamba_ssm, meta_tokens, norm_mix, norm_ffn, norm_final, rwkv_mix, rwkv_w_rkv, rwkv_w0, rwkv_w1, rwkv_w2, rwkv_a0, rwkv_a1, rwkv_a2, rwkv_g1, rwkv_g2, rwkv_k_k, rwkv_k_a, rwkv_r_k, rwkv_ln_w, rwkv_ln_b, rwkv_w_o, mamba_in_proj, mamba_conv_w, mamba_conv_b, mamba_dt_bias, mamba_a_log, mamba_d, mamba_norm_w, mamba_out_proj, peer_w_q, peer_sub_keys, peer_u, peer_v):
    rwkv_p = _prep_rwkv(rwkv_mix[0], rwkv_w_rkv[0], rwkv_w0[0], rwkv_w1[0], rwkv_w2[0], rwkv_a0[0],
                        rwkv_a1[0], rwkv_a2[0], rwkv_g1[0], rwkv_g2[0], rwkv_k_k[0], rwkv_k_a[0],
                        rwkv_r_k[0], rwkv_ln_w[0], rwkv_ln_b[0], rwkv_w_o[0])
    mamba_p = _prep_mamba(mamba_in_proj[0], mamba_conv_w[0], mamba_conv_b[0], mamba_dt_bias[0],
                          mamba_a_log[0], mamba_d[0], mamba_norm_w[0], mamba_out_proj[0])
    peer_p = [_prep_peer(peer_w_q[i], peer_sub_keys[i], peer_u[i], peer_v[i]) for i in range(2)]

    bp, lp = x_prompt.shape[0], x_prompt.shape[1] + N_META
    meta = jnp.broadcast_to(meta_tokens[None], (bp, N_META, D_MODEL))
    hp = jnp.concatenate([meta, x_prompt], axis=1).reshape(bp * lp, D_MODEL)
    zeros = lambda *s: jnp.zeros(s, F32)
    yp, p_shift, p_wkv, p_conv, p_ssm = _trunk(
        hp, zeros(bp, D_MODEL), zeros(bp, RW_HEADS, RW_HEAD, RW_HEAD), zeros(bp, M_CONV - 1, M_CONV_DIM),
        zeros(bp, M_HEADS, M_HEADDIM, M_STATE), norm_mix, norm_ffn, norm_final, rwkv_p, mamba_p, peer_p,
        bp, lp, tm=344, tt=48, peer_tm=512, rows=M_CHUNK)

    bs, ls = x_sample.shape[0], x_sample.shape[1]
    ys, s_shift, s_wkv, s_conv, s_ssm = _trunk(
        x_sample.reshape(bs * ls, D_MODEL), state_rwkv_shift[0], state_rwkv_wkv[0], state_mamba_conv[0],
        state_mamba_ssm[0], norm_mix, norm_ffn, norm_final, rwkv_p, mamba_p, peer_p,
        bs, ls, tm=256, tt=ls, peer_tm=512, rows=M_CHUNK)
    return (yp[:, N_META:], ys, p_shift, p_wkv, p_conv, p_ssm, s_shift, s_wkv, s_conv, s_ssm)
```

```python
import functools
import math

import jax
import jax.numpy as jnp
from jax import lax
from jax.experimental import pallas as pl
from jax.experimental.pallas import tpu as pltpu

F32 = jnp.float32
BF16 = jnp.bfloat16

D_MODEL = 1024
N_META = 16
NORM_EPS = 1e-5
RW_HEAD = 64
RW_HEADS = D_MODEL // RW_HEAD
RW_LN_EPS = 64e-5
LORA_PAD = 128
GATE_LORA_PAD = 256
M_D_INNER = 2048
M_HEADDIM = 64
M_HEADS = M_D_INNER // M_HEADDIM
M_GROUPS = 4
M_HPG = M_HEADS // M_GROUPS
M_STATE = 128
M_CONV = 4
M_CONV_DIM = M_D_INNER + 2 * M_GROUPS * M_STATE
M_IN_DIM = 2 * M_D_INNER + 2 * M_GROUPS * M_STATE + M_HEADS
M_DT_PAD = 128
M_CHUNK = 128
P_HEADS = 8
P_NKEYS = 128
P_QDIM = 256
P_TOPK = 16
P_EXPERT_TILE = 2048

LANES = 128
SUBLANES = 8
VMEM_LIMIT_BYTES = 56 * 1024 * 1024


def _params(*semantics):
    return pltpu.CompilerParams(dimension_semantics=semantics, vmem_limit_bytes=VMEM_LIMIT_BYTES)


def _const_spec(shape):
    zeros = (0,) * len(shape)
    return pl.BlockSpec(shape, lambda *_: zeros)


def _rms(x, g):
    return x * lax.rsqrt(jnp.mean(x * x, axis=-1, keepdims=True) + NORM_EPS) * g


def _softplus(x):
    return jnp.maximum(x, 0.0) + jnp.log1p(jnp.exp(-jnp.abs(x)))


def _bdot(a, b):
    return jnp.dot(a.astype(BF16), b.astype(BF16), preferred_element_type=F32)


def _split_bf16(w):
    hi = w.astype(BF16)
    return hi, (w - hi.astype(F32)).astype(BF16)


def _split3_bf16(x):
    p0 = x.astype(BF16)
    r0 = x - p0.astype(F32)
    p1 = r0.astype(BF16)
    p2 = (r0 - p1.astype(F32)).astype(BF16)
    return p0, p1, p2


def _dot_select(a, sel):
    p0, p1, p2 = _split3_bf16(a)
    return (jnp.dot(p0, sel, preferred_element_type=F32)
            + (jnp.dot(p1, sel, preferred_element_type=F32) + jnp.dot(p2, sel, preferred_element_type=F32)))


def _select_dot(sel, b):
    p0, p1, p2 = _split3_bf16(b)
    return (jnp.dot(sel, p0, preferred_element_type=F32)
            + (jnp.dot(sel, p1, preferred_element_type=F32) + jnp.dot(sel, p2, preferred_element_type=F32)))


def _dot3(a, b_hi, b_lo):
    a_hi, a_lo = _split_bf16(a)
    return (jnp.dot(a_hi, b_hi, preferred_element_type=F32)
            + (jnp.dot(a_hi, b_lo, preferred_element_type=F32) + jnp.dot(a_lo, b_hi, preferred_element_type=F32)))


def _norm_kernel(h_ref, g_ref, o_ref):
    o_ref[...] = _rms(h_ref[...], g_ref[...])


def _norm(h, g, tm):
    t, d = h.shape
    return pl.pallas_call(
        _norm_kernel,
        grid=(t // tm,),
        in_specs=[pl.BlockSpec((tm, d), lambda i: (i, 0)), _const_spec((1, d))],
        out_specs=pl.BlockSpec((tm, d), lambda i: (i, 0)),
        out_shape=jax.ShapeDtypeStruct((t, d), F32),
        compiler_params=_params("parallel"),
        name="rms_norm",
    )(h, g.reshape(1, d))


def _rwkv_proj_kernel(xn_ref, prev_ref, *refs):
    _rwkv_proj_body(xn_ref[...], prev_ref[...], *refs)


def _rwkv_proj_fused_kernel(h_ref, before_ref, shift_ref, gain_ref, *refs, tiles_per_seq):
    *proj_refs, last_ref = refs
    i = pl.program_id(0)
    xn = _rms(h_ref[...], gain_ref[...])
    before = _rms(before_ref[SUBLANES - 1:SUBLANES, :], gain_ref[...])
    first = jnp.where(i % tiles_per_seq == 0, shift_ref[0], before)
    row = lax.broadcasted_iota(jnp.int32, xn.shape, 0)
    prev = jnp.where(row == 0, first, pltpu.roll(xn, 1, axis=0))
    _rwkv_proj_body(xn, prev, *proj_refs)

    @pl.when(i % tiles_per_seq == tiles_per_seq - 1)
    def _():
        last_ref[0] = xn[xn.shape[0] - 1:, :]


def _rwkv_proj_body(xn, prev, mix_ref, vec_ref, wr_ref, wk_ref, wv_ref,
                    w1h_ref, w1l_ref, w2h_ref, w2l_ref, a1h_ref, a1l_ref, a2h_ref, a2l_ref, g1_ref, g2_ref,
                    r_ref, w_ref, k_ref, v_ref, kk_ref, a_ref, g_ref):
    dx = prev - xn
    xr, xw, xk, xv, xa, xg = [xn + dx * mix_ref[j:j + 1, :] for j in range(6)]
    w0, a0, k_k, k_a = [vec_ref[j:j + 1, :] for j in range(4)]
    r_ref[...] = _bdot(xr, wr_ref[...])
    k = _bdot(xk, wk_ref[...])
    v_ref[...] = _bdot(xv, wv_ref[...])
    w_lora = _dot3(jnp.tanh(_dot3(xw, w1h_ref[...], w1l_ref[...])), w2h_ref[...], w2l_ref[...])
    w_log = -_softplus(-(w0 + w_lora)) - 0.5
    w_ref[...] = jnp.exp(-jnp.exp(w_log))
    a = jax.nn.sigmoid(a0 + _dot3(_dot3(xa, a1h_ref[...], a1l_ref[...]), a2h_ref[...], a2l_ref[...]))
    a_ref[...] = a
    g_ref[...] = _bdot(jax.nn.sigmoid(_bdot(xg, g1_ref[...])), g2_ref[...])
    kk_ref[...] = k * k_k
    k_ref[...] = k * (1.0 + (a - 1.0) * k_a)


def _rwkv_proj(xn, prev, p, tm):
    t, d = xn.shape
    tok = pl.BlockSpec((tm, d), lambda i: (i, 0))
    weights = [p["mix"], p["vec"], p["wr"], p["wk"], p["wv"], *p["w1"], *p["w2"], *p["a1"], *p["a2"],
               p["g1"], p["g2"]]
    return pl.pallas_call(
        _rwkv_proj_kernel,
        grid=(t // tm,),
        in_specs=[tok, tok] + [_const_spec(w.shape) for w in weights],
        out_specs=[tok] * 7,
        out_shape=[jax.ShapeDtypeStruct((t, d), F32)] * 7,
        compiler_params=_params("parallel"),
        name="rwkv_proj",
    )(xn, prev, *weights)


def _rwkv_proj_fused(h, gain, shift0, p, tm, length):
    t, d = h.shape
    bsz = t // length
    tiles_per_seq = length // tm
    tok = pl.BlockSpec((tm, d), lambda i: (i, 0))
    before = pl.BlockSpec((SUBLANES, d), lambda i: (jnp.maximum(i * (tm // SUBLANES) - 1, 0), 0))
    per_seq = pl.BlockSpec((1, 1, d), lambda i: (i // tiles_per_seq, 0, 0))
    weights = [p["mix"], p["vec"], p["wr"], p["wk"], p["wv"], *p["w1"], *p["w2"], *p["a1"], *p["a2"],
               p["g1"], p["g2"]]
    kern = functools.partial(_rwkv_proj_fused_kernel, tiles_per_seq=tiles_per_seq)
    *proj, last = pl.pallas_call(
        kern,
        grid=(t // tm,),
        in_specs=[tok, before, per_seq, _const_spec((1, d))] + [_const_spec(w.shape) for w in weights],
        out_specs=[tok] * 7 + [per_seq],
        out_shape=[jax.ShapeDtypeStruct((t, d), F32)] * 7 + [jax.ShapeDtypeStruct((bsz, 1, d), F32)],
        compiler_params=_params("arbitrary"),
        name="rwkv_norm_proj",
    )(h, h, shift0.reshape(bsz, 1, d), gain.reshape(1, d), *weights)
    return proj, last.reshape(bsz, d)


def _rwkv_scan_kernel(r_ref, w_ref, k_ref, v_ref, kk_ref, a_ref, s0_ref, lnw_ref, lnb_ref, rk_ref,
                      o_ref, sfin_ref, state, o_rows):
    tb = pl.program_id(1)
    steps = r_ref.shape[0]

    @pl.when(tb == 0)
    def _():
        state[...] = s0_ref[...]

    live = slice(0, RW_HEAD)

    def step(t, carry):
        r = r_ref[t, live, :]
        w = w_ref[t, live, :]
        k = k_ref[t, live, :]
        vv = v_ref[t, live, :]
        kku = kk_ref[t, live, :]
        norm = jnp.sqrt(jnp.sum(kku * kku, axis=0, keepdims=True))
        kk = kku / jnp.maximum(norm, 1e-12)
        alpha = -kk
        beta = kk * a_ref[t, live, :]

        def row(vi, c):
            s_v = state[vi]
            u = jnp.sum(s_v * alpha, axis=0, keepdims=True)
            s_new = s_v * w + u * beta + v_ref[t, pl.ds(vi, 1), :] * k
            state[vi] = s_new
            o_rows[pl.ds(vi, 1), :] = jnp.sum(s_new * r, axis=0, keepdims=True)
            return c

        lax.fori_loop(0, RW_HEAD, row, 0, unroll=32)
        o = o_rows[...]
        mu = jnp.mean(o, axis=0, keepdims=True)
        var = jnp.mean(jnp.square(o - mu), axis=0, keepdims=True)
        on = (o - mu) * lax.rsqrt(var + RW_LN_EPS)
        bonus = jnp.sum(r * k * rk_ref[...], axis=0, keepdims=True) * vv
        o_ref[t, live, :] = on * lnw_ref[...] + lnb_ref[...] + bonus
        return carry

    lax.fori_loop(0, steps, step, 0)

    @pl.when(tb == pl.num_programs(1) - 1)
    def _():
        sfin_ref[...] = state[...]


def _rwkv_scan(seqs, s0, lnw, lnb, rk, tt):
    length, pitch, probs = seqs[0].shape
    n = RW_HEAD
    groups = probs // LANES
    seq_spec = pl.BlockSpec((tt, pitch, LANES), lambda g, i: (i, 0, g))
    st_spec = pl.BlockSpec((n, n, LANES), lambda g, i: (0, 0, g))
    lane_spec = pl.BlockSpec((n, LANES), lambda g, i: (0, 0))
    return pl.pallas_call(
        _rwkv_scan_kernel,
        grid=(groups, length // tt),
        in_specs=[seq_spec] * 6 + [st_spec] + [lane_spec] * 3,
        out_specs=[seq_spec, st_spec],
        out_shape=[jax.ShapeDtypeStruct((length, pitch, probs), F32),
                   jax.ShapeDtypeStruct((n, n, probs), F32)],
        scratch_shapes=[pltpu.VMEM((n, n, LANES), F32), pltpu.VMEM((n, LANES), F32)],
        compiler_params=_params("arbitrary", "arbitrary"),
        name="rwkv_scan",
    )(*seqs, s0, lnw, lnb, rk)


def _rwkv_out_kernel(o_ref, g_ref, h_ref, wo_ref, out_ref):
    out_ref[...] = h_ref[...] + _bdot(o_ref[...] * g_ref[...], wo_ref[...])


def _rwkv_out(o, g, h, wo, tm):
    t, d = h.shape
    tok = pl.BlockSpec((tm, d), lambda i: (i, 0))
    return pl.pallas_call(
        _rwkv_out_kernel,
        grid=(t // tm,),
        in_specs=[tok, tok, tok, _const_spec(wo.shape)],
        out_specs=tok,
        out_shape=jax.ShapeDtypeStruct((t, d), F32),
        compiler_params=_params("parallel"),
        name="rwkv_out",
    )(o, g, h, wo)


def _mamba_in_kernel(h_ref, g_ref, w_ref, z_ref, xbc_ref, dt_ref):
    xn = _rms(h_ref[...], g_ref[...]).astype(BF16)
    z_ref[...] = jnp.dot(xn, w_ref[:, :M_D_INNER], preferred_element_type=F32)
    xbc_ref[...] = jnp.dot(xn, w_ref[:, M_D_INNER:M_D_INNER + M_CONV_DIM], preferred_element_type=F32)
    dt_ref[...] = jnp.dot(xn, w_ref[:, M_D_INNER + M_CONV_DIM:], preferred_element_type=F32)


def _mamba_in(h, g, w_in, tm):
    t, d = h.shape
    return pl.pallas_call(
        _mamba_in_kernel,
        grid=(t // tm,),
        in_specs=[pl.BlockSpec((tm, d), lambda i: (i, 0)), _const_spec((1, d)), _const_spec(w_in.shape)],
        out_specs=[pl.BlockSpec((tm, M_D_INNER), lambda i: (i, 0)),
                   pl.BlockSpec((tm, M_CONV_DIM), lambda i: (i, 0)),
                   pl.BlockSpec((tm, M_DT_PAD), lambda i: (i, 0))],
        out_shape=[jax.ShapeDtypeStruct((t, M_D_INNER), F32),
                   jax.ShapeDtypeStruct((t, M_CONV_DIM), F32),
                   jax.ShapeDtypeStruct((t, M_DT_PAD), F32)],
        compiler_params=_params("parallel"),
        name="mamba_in_proj",
    )(h, g.reshape(1, d), w_in)


def _mamba_ssd_kernel(xbc_ref, z_ref, dt_ref, h_ref, conv0_ref, ssm0_ref, convw_ref, vec_ref,
                      dvec_ref, expand_ref, normw_ref, wout_ref,
                      out_ref, ssm_ref,
                      conv_buf, src_adt, src_acs, src_b, src_xdt, src_xdt_st, state, y_buf, *, seq_len, rows):
    b = pl.program_id(0)
    c = pl.program_id(1)
    q = M_CHUNK
    halo = SUBLANES
    d_bc = M_GROUPS * M_STATE

    @pl.when(c == 0)
    def _():
        state[...] = ssm0_ref[0]
        conv_buf[0:halo, :] = conv0_ref[0]

    @pl.when(c > 0)
    def _():
        conv_buf[0:halo, :] = conv_buf[rows:rows + halo, :]

    if rows < q:
        @pl.when((b == 0) & (c == 0))
        def _():
            src_adt[...] = jnp.zeros(src_adt.shape, F32)
            src_acs[...] = jnp.zeros(src_acs.shape, F32)
            src_b[...] = jnp.zeros(src_b.shape, BF16)
            src_xdt[...] = jnp.zeros(src_xdt.shape, BF16)
            src_xdt_st[...] = jnp.zeros(src_xdt_st.shape, BF16)

    conv_buf[halo:halo + rows, :] = xbc_ref[0]
    valid = (lax.broadcasted_iota(jnp.int32, (rows, 1), 0) + c * rows) < seq_len
    conv = vec_ref[0:1, :]
    for j in range(M_CONV):
        conv = conv + conv_buf[halo - (M_CONV - 1) + j:halo - (M_CONV - 1) + j + rows, :] * convw_ref[j:j + 1, :]
    act = jnp.where(valid, jax.nn.silu(conv), 0.0)
    xs = act[:, :M_D_INNER]
    dt = jnp.where(valid, _softplus(dt_ref[0] + dvec_ref[0:1, :]), 0.0)
    src_adt[0:rows, :] = dt * dvec_ref[1:2, :]
    li = lax.broadcasted_iota(jnp.int32, (rows, q), 0)
    si = lax.broadcasted_iota(jnp.int32, (rows, q), 1)
    causal = li >= si
    acs = _select_dot(causal.astype(BF16), src_adt[...])
    src_acs[0:rows, :] = acs
    acs_t = src_acs[...].T
    expand = expand_ref[...]
    dt_full = _dot_select(dt, expand)
    acs_full = _dot_select(acs, expand)
    last = acs[rows - 1:rows, :]
    last_full = acs_full[rows - 1:rows, :]
    xdt = xs * dt_full
    src_xdt[0:rows, :] = xdt.astype(BF16)
    src_xdt_st[0:rows, :] = (xdt * jnp.exp(last_full - acs_full)).astype(BF16)
    src_b[0:rows, :] = act[:, M_D_INNER:M_D_INNER + d_bc].astype(BF16)
    eacs_full = jnp.exp(acs_full)
    state_decay = jnp.exp(last)
    width = M_HPG * M_HEADDIM

    for g in range(M_GROUPS):
        cols = slice(g * width, (g + 1) * width)
        b_g = src_b[:, g * M_STATE:(g + 1) * M_STATE]
        c_g = act[:, M_D_INNER + d_bc + g * M_STATE:M_D_INNER + d_bc + (g + 1) * M_STATE].astype(BF16)
        cb = lax.dot_general(c_g, b_g, (((1,), (1,)), ((), ())), preferred_element_type=F32)
        h_g = state[g * M_HPG:(g + 1) * M_HPG].reshape(width, M_STATE)
        y_off = lax.dot_general(c_g, h_g.astype(BF16), (((1,), (1,)), ((), ())),
                                preferred_element_type=F32)
        upd = lax.dot_general(src_xdt_st[:, cols], b_g, (((0,), (0,)), ((), ())),
                              preferred_element_type=F32)
        ys = []
        for r in range(g * M_HPG, (g + 1) * M_HPG):
            lo = r * M_HEADDIM
            decay = jnp.where(causal, jnp.exp(acs[:, r:r + 1] - acs_t[r:r + 1, :]), 0.0)
            ys.append(jnp.dot((cb * decay).astype(BF16), src_xdt[:, lo:lo + M_HEADDIM],
                              preferred_element_type=F32))
            sub = slice((r - g * M_HPG) * M_HEADDIM, (r - g * M_HPG + 1) * M_HEADDIM)
            state[r] = state[r] * state_decay[:, r:r + 1] + upd[sub, :]
        y_buf[:, cols] = jnp.concatenate(ys, axis=1) + y_off * eacs_full[:, cols]

    y = y_buf[...] + vec_ref[1:2, :M_D_INNER] * xs
    yg = y * jax.nn.silu(z_ref[0])
    parts = []
    for g in range(M_GROUPS):
        part = yg[:, g * width:(g + 1) * width]
        parts.append(part * lax.rsqrt(jnp.mean(part * part, axis=-1, keepdims=True) + NORM_EPS))
    yn = jnp.concatenate(parts, axis=1) * normw_ref[...]
    out_ref[0] = h_ref[0] + _bdot(yn, wout_ref[...])

    @pl.when(c == pl.num_programs(1) - 1)
    def _():
        ssm_ref[0] = state[...]


def _mamba_ssd(xbc, z, dt, h, conv0, ssm0, p, rows, seq_len):
    bsz, length, _ = xbc.shape
    chunks = pl.cdiv(length, rows)
    weights = [p["conv_w"], p["vec"], p["dvec"], p["expand"], p["norm_w"], p["w_out"]]

    def seq_spec(width):
        return pl.BlockSpec((1, rows, width), lambda b, c: (b, c, 0))

    kern = functools.partial(_mamba_ssd_kernel, seq_len=seq_len, rows=rows)
    return pl.pallas_call(
        kern,
        grid=(bsz, chunks),
        in_specs=[seq_spec(M_CONV_DIM), seq_spec(M_D_INNER), seq_spec(M_DT_PAD), seq_spec(D_MODEL),
                  pl.BlockSpec((1, SUBLANES, M_CONV_DIM), lambda b, c: (b, 0, 0)),
                  pl.BlockSpec((1, M_HEADS, M_HEADDIM, M_STATE), lambda b, c: (b, 0, 0, 0))]
                 + [_const_spec(w.shape) for w in weights],
        out_specs=[seq_spec(D_MODEL),
                   pl.BlockSpec((1, M_HEADS, M_HEADDIM, M_STATE), lambda b, c: (b, 0, 0, 0))],
        out_shape=[jax.ShapeDtypeStruct((bsz, length, D_MODEL), F32),
                   jax.ShapeDtypeStruct((bsz, M_HEADS, M_HEADDIM, M_STATE), F32)],
        scratch_shapes=[pltpu.VMEM((rows + SUBLANES, M_CONV_DIM), F32),
                        pltpu.VMEM((M_CHUNK, M_DT_PAD), F32),
                        pltpu.VMEM((M_CHUNK, M_DT_PAD), F32),
                        pltpu.VMEM((M_CHUNK, M_GROUPS * M_STATE), BF16),
                        pltpu.VMEM((M_CHUNK, M_D_INNER), BF16),
                        pltpu.VMEM((M_CHUNK, M_D_INNER), BF16),
                        pltpu.VMEM((M_HEADS, M_HEADDIM, M_STATE), F32),
                        pltpu.VMEM((rows, M_D_INNER), F32)],
        compiler_params=_params("arbitrary", "arbitrary"),
        name="mamba_ssd",
    )(xbc, z, dt, h, conv0, ssm0, *weights)


_CANDS = [(ra, rb) for ra in range(P_TOPK) for rb in range(P_TOPK) if (ra + 1) * (rb + 1) <= P_TOPK]


def _peer_route_kernel(h_ref, g_ref, wq_ref, keys_ref, keys_lo_ref,
                       xt_ref, r1_ref, e1_ref, m0_ref, w0_ref,
                       q_buf, s_buf, code0, code_book, best_vals, cand, counts, inv_z):
    tm = h_ref.shape[0]
    neg_inf = float("-inf")
    xn = _rms(h_ref[...], g_ref[...])
    xt = xn.T.astype(BF16)
    xt_ref[...] = xt
    q_buf[...] = jnp.dot(wq_ref[...], xt, preferred_element_type=F32)
    key_iota = lax.broadcasted_iota(jnp.int32, (P_NKEYS, tm), 0).astype(F32)

    def half(m, carry):
        head = m // 2
        z = m % 2
        row0 = pl.multiple_of(m * P_NKEYS, P_NKEYS)
        qm = q_buf[pl.ds(row0, P_NKEYS), :]
        q_hi = qm.astype(BF16)
        q_lo = (qm - q_hi.astype(F32)).astype(BF16)
        k_hi = keys_ref[m]
        s = (jnp.dot(k_hi, q_hi, preferred_element_type=F32)
             + (jnp.dot(k_hi, q_lo, preferred_element_type=F32)
                + jnp.dot(keys_lo_ref[m], q_hi, preferred_element_type=F32)))

        s_buf[...] = s

        def next_distinct(j, prev):
            sj = s_buf[...]
            best = jnp.max(jnp.where(sj < prev, sj, neg_inf), axis=0, keepdims=True)
            best_vals[z, j, pl.ds(head, 1), :] = best
            return best

        def extract_ties(j, sc):
            cur, rank = sc
            best = jnp.max(cur, axis=0, keepdims=True)
            first = jnp.min(jnp.where(cur == best, key_iota, float(P_NKEYS)), axis=0, keepdims=True)
            sel = key_iota == first
            best_vals[z, j, pl.ds(head, 1), :] = best
            return jnp.where(sel, neg_inf, cur), jnp.where(sel, jnp.asarray(j, dtype=F32), rank)

        last = lax.fori_loop(0, P_TOPK, next_distinct, jnp.full((1, tm), float("inf"), F32))
        reached = jnp.sum(jnp.where(s >= last, 1.0, 0.0), axis=0, keepdims=True)
        has_ties = jnp.max(jnp.abs(reached - float(P_TOPK))) > 0.0

        def rank_by_count():
            rank = jnp.zeros((P_NKEYS, tm), F32)
            for jj in range(P_TOPK):
                rank = rank + jnp.where(best_vals[z, jj, pl.ds(head, 1), :] > s, 1.0, 0.0)
            return rank

        def rank_with_ties():
            no_rank = jnp.full((P_NKEYS, tm), float(P_NKEYS), F32)
            return lax.fori_loop(0, P_TOPK, extract_ties, (s, no_rank))[1]

        e = jnp.exp(s - jnp.max(s, axis=0, keepdims=True))

        @pl.when(z == 0)
        def _():
            w0_ref[head] = e

            @pl.when(has_ties)
            def _():
                code0[head] = rank_with_ties()
                for ra in range(P_TOPK):
                    code_book[ra, pl.ds(head, 1), :] = jnp.full((1, tm), float(ra), F32)

            @pl.when(jnp.logical_not(has_ties))
            def _():
                code0[head] = s
                for ra in range(P_TOPK):
                    code_book[ra, pl.ds(head, 1), :] = best_vals[0, ra, pl.ds(head, 1), :]

        @pl.when(z == 1)
        def _():
            rank = lax.cond(has_ties, rank_with_ties, rank_by_count)
            for lt in range(tm // LANES):
                r1_ref[head, lt] = rank[:, lt * LANES:(lt + 1) * LANES].astype(BF16)
                e1_ref[head, lt] = e[:, lt * LANES:(lt + 1) * LANES].astype(BF16)

        return carry

    lax.fori_loop(0, 2 * P_HEADS, half, 0)

    ex0 = [jnp.exp(best_vals[0, ra] - best_vals[0, 0]) for ra in range(P_TOPK)]
    ex1 = [jnp.exp(best_vals[1, rb] - best_vals[1, 0]) for rb in range(P_TOPK)]
    for i, (ra, rb) in enumerate(_CANDS):
        cand[i] = best_vals[0, ra] + best_vals[1, rb]

    def next_sum(it, prev):
        best = None
        for i in range(len(_CANDS)):
            ci = cand[i]
            below = jnp.where(ci < prev, ci, neg_inf)
            best = below if best is None else jnp.maximum(best, below)
        return best

    last_sum = lax.fori_loop(0, P_TOPK, next_sum, jnp.full((P_HEADS, tm), float("inf"), F32))
    hits = [jnp.where(cand[i] >= last_sum, 1.0, 0.0) for i in range(len(_CANDS))]
    n_hits = functools.reduce(lambda x, y: x + y, hits)
    sums_tie = jnp.max(jnp.abs(n_hits - float(P_TOPK))) > 0.0

    def select_distinct():
        z = jnp.zeros((P_HEADS, tm), F32)
        per_rank = [None] * P_TOPK
        for i, (ra, rb) in enumerate(_CANDS):
            per_rank[ra] = hits[i] if per_rank[ra] is None else per_rank[ra] + hits[i]
            z = z + hits[i] * (ex0[ra] * ex1[rb])
        for ra in range(P_TOPK):
            counts[ra] = per_rank[ra]
        return z

    def select_with_ties():
        counts[...] = jnp.zeros(counts.shape, F32)

        def pick(it, z):
            best = cand[0]
            for i in range(1, len(_CANDS)):
                best = jnp.maximum(best, cand[i])
            found = jnp.zeros((P_HEADS, tm), F32)
            for i, (ra, rb) in enumerate(_CANDS):
                ci = cand[i]
                hit = jnp.where(ci == best, 1.0, 0.0) * (1.0 - found)
                found = found + hit
                cand[i] = jnp.where(hit > 0.0, neg_inf, ci)
                counts[ra] = counts[ra] + hit
                z = z + hit * (ex0[ra] * ex1[rb])
            return z

        return lax.fori_loop(0, P_TOPK, pick, jnp.zeros((P_HEADS, tm), F32))

    inv_z[...] = 1.0 / lax.cond(sums_tie, select_with_ties, select_distinct)

    def finish(head, carry):
        code = code0[head]
        m0 = jnp.zeros((P_NKEYS, tm), F32)
        for ra in range(P_TOPK):
            m0 = jnp.where(code == code_book[ra, pl.ds(head, 1), :], counts[ra, pl.ds(head, 1), :], m0)
        m0_ref[head] = m0
        w0_ref[head] = w0_ref[head] * inv_z[pl.ds(head, 1), :]
        return carry

    lax.fori_loop(0, P_HEADS, finish, 0)


def _peer_route(h, g, wq_t, keys, keys_lo, tm):
    t, d = h.shape
    tiles = pl.cdiv(t, tm)
    t_pad = tiles * tm
    tile = pl.BlockSpec((P_HEADS, P_NKEYS, tm), lambda i: (0, 0, i))
    tile_f32 = jax.ShapeDtypeStruct((P_HEADS, P_NKEYS, t_pad), F32)
    tile_bf16 = jax.ShapeDtypeStruct((P_HEADS, t_pad // LANES, P_NKEYS, LANES), BF16)
    lane_tiles = pl.BlockSpec((P_HEADS, tm // LANES, P_NKEYS, LANES), lambda i: (0, i, 0, 0))
    return pl.pallas_call(
        _peer_route_kernel,
        grid=(tiles,),
        in_specs=[pl.BlockSpec((tm, d), lambda i: (i, 0)), _const_spec((1, d)),
                  _const_spec(wq_t.shape), _const_spec(keys.shape), _const_spec(keys_lo.shape)],
        out_specs=[pl.BlockSpec((d, tm), lambda i: (0, i)), lane_tiles, lane_tiles, tile, tile],
        out_shape=[jax.ShapeDtypeStruct((d, t_pad), BF16), tile_bf16, tile_bf16, tile_f32, tile_f32],
        scratch_shapes=[pltpu.VMEM((P_HEADS * P_QDIM, tm), F32),
                        pltpu.VMEM((P_NKEYS, tm), F32),
                        pltpu.VMEM((P_HEADS, P_NKEYS, tm), F32),
                        pltpu.VMEM((P_TOPK, P_HEADS, tm), F32),
                        pltpu.VMEM((2, P_TOPK, P_HEADS, tm), F32),
                        pltpu.VMEM((len(_CANDS), P_HEADS, tm), F32),
                        pltpu.VMEM((P_TOPK, P_HEADS, tm), F32),
                        pltpu.VMEM((P_HEADS, tm), F32)],
        compiler_params=_params("parallel"),
        name="peer_route",
    )(h, g.reshape(1, d), wq_t, keys, keys_lo)


def _gelu(x):
    return 0.5 * x * (1.0 + lax.erf(x * (1.0 / math.sqrt(2.0))))


def _peer_dense_kernel(xt_ref, u_ref, vt_ref, r1_ref, e1_ref, m0_ref, w0_ref, h_ref, gfin_ref,
                       out_ref, acc, hid, prob, *, final_norm):
    j = pl.program_id(1)
    keys_per_step = P_EXPERT_TILE // P_NKEYS

    @pl.when(j == 0)
    def _():
        acc[...] = jnp.zeros(acc.shape, F32)

    tm = xt_ref.shape[1]
    hid[...] = jnp.dot(u_ref[...], xt_ref[...], preferred_element_type=F32)
    zero = jnp.zeros((), BF16)
    for ii in range(keys_per_step):
        i1 = j * keys_per_step + ii
        rows = slice(ii * P_NKEYS, (ii + 1) * P_NKEYS)
        m0_rows = [m0_ref[head, pl.ds(i1, 1), :] for head in range(P_HEADS)]
        w0_rows = [w0_ref[head, pl.ds(i1, 1), :] for head in range(P_HEADS)]

        def packed_rows(row):
            one = jnp.broadcast_to(row, (2 * SUBLANES, LANES)).astype(BF16)
            return jnp.tile(one, (P_NKEYS // (2 * SUBLANES), 1))

        for lt in range(tm // LANES):
            lanes = slice(lt * LANES, (lt + 1) * LANES)
            gate = None
            for head in range(P_HEADS):
                m0 = packed_rows(m0_rows[head][:, lanes])
                w0 = packed_rows(w0_rows[head][:, lanes])
                term = w0 * jnp.where(r1_ref[head, lt] < m0, e1_ref[head, lt], zero)
                gate = term if gate is None else gate + term
            prob[rows, lanes] = gate * _gelu(hid[rows, lanes]).astype(BF16)
    acc[...] += jnp.dot(vt_ref[...], prob[...], preferred_element_type=F32)

    @pl.when(j == pl.num_programs(1) - 1)
    def _():
        res = h_ref[...] + acc[...].T
        if final_norm:
            res = _rms(res, gfin_ref[...])
        out_ref[...] = res


def _peer_dense(xt, u_bf, vt_bf, r1, e1, m0, w0, h, g_final, tm, final_norm):
    t, d = h.shape
    n_exp = u_bf.shape[0]
    tile = pl.BlockSpec((P_HEADS, P_NKEYS, tm), lambda i, j: (0, 0, i))
    lane_tiles = pl.BlockSpec((P_HEADS, tm // LANES, P_NKEYS, LANES), lambda i, j: (0, i, 0, 0))
    kern = functools.partial(_peer_dense_kernel, final_norm=final_norm)
    return pl.pallas_call(
        kern,
        grid=(pl.cdiv(t, tm), n_exp // P_EXPERT_TILE),
        in_specs=[pl.BlockSpec((d, tm), lambda i, j: (0, i)),
                  pl.BlockSpec((P_EXPERT_TILE, d), lambda i, j: (j, 0)),
                  pl.BlockSpec((d, P_EXPERT_TILE), lambda i, j: (0, j)),
                  lane_tiles, lane_tiles, tile, tile,
                  pl.BlockSpec((tm, d), lambda i, j: (i, 0)),
                  _const_spec((1, d))],
        out_specs=pl.BlockSpec((tm, d), lambda i, j: (i, 0)),
        out_shape=jax.ShapeDtypeStruct((t, d), F32),
        scratch_shapes=[pltpu.VMEM((d, tm), F32),
                        pltpu.VMEM((P_EXPERT_TILE, tm), F32),
                        pltpu.VMEM((P_EXPERT_TILE, tm), BF16)],
        compiler_params=_params("parallel", "arbitrary"),
        name="peer_dense",
    )(xt, u_bf, vt_bf, r1, e1, m0, w0, h, g_final.reshape(1, d))


def _pad_cols(w, n):
    return jnp.pad(w, ((0, 0), (0, n - w.shape[1])))


def _pad_rows(w, n):
    return jnp.pad(w, ((0, n - w.shape[0]), (0, 0)))


def _prep_rwkv(mix, w_rkv, w0, w1, w2, a0, a1, a2, g1, g2, k_k, k_a, r_k, ln_w, ln_b, w_o):
    d = D_MODEL
    lane = lambda x: jnp.repeat(x.reshape(RW_HEADS, RW_HEAD).T, LANES // RW_HEADS, axis=1)
    return {
        "mix": _pad_rows(mix, SUBLANES),
        "vec": _pad_rows(jnp.stack([w0, a0, k_k, k_a]), SUBLANES),
        "wr": w_rkv[0].astype(BF16), "wk": w_rkv[1].astype(BF16), "wv": w_rkv[2].astype(BF16),
        "w1": _split_bf16(_pad_cols(w1, LORA_PAD)), "w2": _split_bf16(_pad_rows(w2, LORA_PAD)),
        "a1": _split_bf16(_pad_cols(a1, LORA_PAD)), "a2": _split_bf16(_pad_rows(a2, LORA_PAD)),
        "g1": _pad_cols(g1, GATE_LORA_PAD).astype(BF16), "g2": _pad_rows(g2, GATE_LORA_PAD).astype(BF16),
        "lnw": lane(ln_w), "lnb": lane(ln_b), "rk": lane(r_k.reshape(d)),
        "wo": w_o.astype(BF16),
    }


def _prep_mamba(in_proj, conv_w, conv_b, dt_bias, a_log, d_skip, norm_w, out_proj):
    head_of_channel = jnp.arange(M_D_INNER) // M_HEADDIM
    expand = (jnp.arange(M_DT_PAD)[:, None] == head_of_channel[None, :]).astype(BF16)
    d_full = jnp.pad(jnp.repeat(d_skip, M_HEADDIM), (0, M_CONV_DIM - M_D_INNER))
    pad_h = lambda x: jnp.pad(x, (0, M_DT_PAD - M_HEADS))
    return {
        "w_in": _pad_cols(in_proj, M_IN_DIM - M_HEADS + M_DT_PAD).astype(BF16),
        "conv_w": _pad_rows(conv_w, SUBLANES),
        "vec": _pad_rows(jnp.stack([conv_b, d_full]), SUBLANES),
        "dvec": _pad_rows(jnp.stack([pad_h(dt_bias), pad_h(-jnp.exp(a_log))]), SUBLANES),
        "expand": expand,
        "norm_w": norm_w.reshape(1, M_D_INNER),
        "w_out": out_proj.astype(BF16),
    }


def _prep_peer(w_q, sub_keys, u_tab, v_tab):
    keys = jnp.transpose(sub_keys, (1, 0, 2, 3)).reshape(2 * P_HEADS, P_NKEYS, P_QDIM // 2)
    keys_hi, keys_lo = _split_bf16(keys)
    return {"wq_t": w_q.T.astype(BF16), "keys": keys_hi, "keys_lo": keys_lo,
            "u": u_tab.astype(BF16), "vt": v_tab.T.astype(BF16)}


SCAN_GROUP = LANES // RW_HEADS
RELAYOUT_STEPS = 128
RELAYOUT_PITCH = RW_HEAD + SUBLANES
SCAN_PITCH = RW_HEAD + SUBLANES


def _to_scan_kernel(x_ref, o_ref, stage):
    for b in range(SCAN_GROUP):
        for hp in range(RW_HEADS // 2):
            t = x_ref[b, :, hp * LANES:(hp + 1) * LANES].T
            for hl in range(2):
                lane = (2 * hp + hl) * SCAN_GROUP + b
                stage[lane * RELAYOUT_PITCH:lane * RELAYOUT_PITCH + RW_HEAD, :] = t[hl * RW_HEAD:(hl + 1) * RW_HEAD, :]
    for k in range(RW_HEAD):
        by_lane = stage[pl.ds(k, LANES, stride=RELAYOUT_PITCH), :]
        o_ref[pl.ds(k, RELAYOUT_STEPS, stride=SCAN_PITCH), :] = by_lane.T


def _to_scan_pallas(x, length):
    x3 = x.reshape(SCAN_GROUP, length, D_MODEL)
    out = pl.pallas_call(
        _to_scan_kernel,
        grid=(pl.cdiv(length, RELAYOUT_STEPS),),
        in_specs=[pl.BlockSpec((SCAN_GROUP, RELAYOUT_STEPS, D_MODEL), lambda i: (0, i, 0))],
        out_specs=pl.BlockSpec((RELAYOUT_STEPS * SCAN_PITCH, LANES), lambda i: (i, 0)),
        out_shape=jax.ShapeDtypeStruct((length * SCAN_PITCH, LANES), F32),
        scratch_shapes=[pltpu.VMEM((LANES * RELAYOUT_PITCH, LANES), F32)],
        compiler_params=_params("parallel"),
        name="to_scan_layout",
    )(x3)
    return out.reshape(length, SCAN_PITCH, LANES)


def _rwkv_out_scan_kernel(o_ref, g_ref, h_ref, wo_ref, out_ref, stage):
    for v in range(RW_HEAD):
        by_step = o_ref[pl.ds(v, RELAYOUT_STEPS, stride=SCAN_PITCH), :]
        stage[pl.ds(v, LANES, stride=RELAYOUT_PITCH), :] = by_step.T
    for b in range(SCAN_GROUP):
        tiles = []
        for hp in range(RW_HEADS // 2):
            halves = []
            for hl in range(2):
                lane = (2 * hp + hl) * SCAN_GROUP + b
                halves.append(stage[lane * RELAYOUT_PITCH:lane * RELAYOUT_PITCH + RW_HEAD, :])
            tiles.append(jnp.concatenate(halves, axis=0).T)
        o_b = jnp.concatenate(tiles, axis=1)
        out_ref[b] = h_ref[b] + _bdot(o_b * g_ref[b], wo_ref[...])


def _rwkv_out_scan(o_scan, g, h, wo, length):
    d = D_MODEL
    seq = pl.BlockSpec((SCAN_GROUP, RELAYOUT_STEPS, d), lambda i: (0, i, 0))
    out = pl.pallas_call(
        _rwkv_out_scan_kernel,
        grid=(pl.cdiv(length, RELAYOUT_STEPS),),
        in_specs=[pl.BlockSpec((RELAYOUT_STEPS * SCAN_PITCH, LANES), lambda i: (i, 0)), seq, seq,
                  _const_spec(wo.shape)],
        out_specs=seq,
        out_shape=jax.ShapeDtypeStruct((SCAN_GROUP, length, d), F32),
        scratch_shapes=[pltpu.VMEM((LANES * RELAYOUT_PITCH, LANES), F32)],
        compiler_params=_params("parallel"),
        name="rwkv_out_from_scan",
    )(o_scan.reshape(length * SCAN_PITCH, LANES), g.reshape(SCAN_GROUP, length, d), h.reshape(SCAN_GROUP, length, d), wo)
    return out.reshape(SCAN_GROUP * length, d)


def _to_scan(x, bsz, length):
    if bsz == SCAN_GROUP and length >= RELAYOUT_STEPS:
        return _to_scan_pallas(x, length)
    x = x.reshape(bsz // SCAN_GROUP, SCAN_GROUP, length, RW_HEADS, RW_HEAD)
    x = jnp.transpose(x, (2, 4, 0, 3, 1)).reshape(length, RW_HEAD, bsz * RW_HEADS)
    return jnp.pad(x, ((0, 0), (0, SCAN_PITCH - RW_HEAD), (0, 0)))


def _from_scan(x, bsz, length):
    x = x[:, :RW_HEAD].reshape(length, RW_HEAD, bsz // SCAN_GROUP, RW_HEADS, SCAN_GROUP)
    return jnp.transpose(x, (2, 4, 0, 3, 1)).reshape(bsz * length, D_MODEL)


def _state_to_scan(wkv, bsz):
    x = wkv.reshape(bsz // SCAN_GROUP, SCAN_GROUP, RW_HEADS, RW_HEAD, RW_HEAD)
    return jnp.transpose(x, (3, 4, 0, 2, 1)).reshape(RW_HEAD, RW_HEAD, bsz * RW_HEADS)


def _state_from_scan(s, bsz):
    x = s.reshape(RW_HEAD, RW_HEAD, bsz // SCAN_GROUP, RW_HEADS, SCAN_GROUP)
    return jnp.transpose(x, (2, 4, 3, 0, 1)).reshape(bsz, RW_HEADS, RW_HEAD, RW_HEAD)


def _rwkv_layer(h, g_mix, shift0, wkv0, p, bsz, length, tm, tt):
    if length % tm == 0:
        (r, w, k, v, kk, a, g), shift = _rwkv_proj_fused(h, g_mix, shift0, p, tm, length)
    else:
        xn = _norm(h, g_mix, tm)
        xn3 = xn.reshape(bsz, length, D_MODEL)
        prev = jnp.concatenate([shift0[:, None, :], xn3[:, :-1]], axis=1).reshape(bsz * length, D_MODEL)
        r, w, k, v, kk, a, g = _rwkv_proj(xn, prev, p, tm)
        shift = xn3[:, -1]
    seqs = [_to_scan(x, bsz, length) for x in (r, w, k, v, kk, a)]
    o, s_fin = _rwkv_scan(seqs, _state_to_scan(wkv0, bsz), p["lnw"], p["lnb"], p["rk"], tt)
    if bsz == SCAN_GROUP and length >= RELAYOUT_STEPS:
        h_new = _rwkv_out_scan(o, g, h, p["wo"], length)
    else:
        h_new = _rwkv_out(_from_scan(o, bsz, length), g, h, p["wo"], tm)
    return h_new, shift, _state_from_scan(s_fin, bsz)


def _mamba_layer(h, g_mix, conv0, ssm0, p, bsz, length, tm, rows):
    z, xbc, dt = _mamba_in(h, g_mix, p["w_in"], tm)
    xbc3 = xbc.reshape(bsz, length, M_CONV_DIM)
    tail = M_CONV - 1
    conv_new = xbc3[:, length - tail:] if length >= tail else jnp.concatenate([conv0, xbc3], axis=1)[:, length:]
    pad_t = (-length) % SUBLANES if length < rows else 0

    def seq(x, c):
        x = x.reshape(bsz, length, c)
        return jnp.pad(x, ((0, 0), (0, pad_t), (0, 0))) if pad_t else x

    conv0_p = jnp.pad(conv0, ((0, 0), (SUBLANES - (M_CONV - 1), 0), (0, 0)))
    out, ssm = _mamba_ssd(seq(xbc, M_CONV_DIM), seq(z, M_D_INNER), seq(dt, M_DT_PAD), seq(h, D_MODEL),
                          conv0_p, ssm0, p, min(rows, length + pad_t), length)
    if pad_t:
        out = out[:, :length]
    return out.reshape(bsz * length, D_MODEL), conv_new, ssm


def _peer_layer(h, g_ffn, p, tm, g_final=None):
    xt, r1, e1, m0, w0 = _peer_route(h, g_ffn, p["wq_t"], p["keys"], p["keys_lo"], tm)
    final = g_final is not None
    return _peer_dense(xt, p["u"], p["vt"], r1, e1, m0, w0, h, g_final if final else g_ffn, tm, final)


TOKEN_TILE_LIMIT = 384
SCAN_BLOCK_LIMIT = 48
PEER_TOKEN_TILE = 512


def _largest_divisor(n, limit, multiple_of=1):
    return max(c for c in range(multiple_of, limit + 1, multiple_of) if n % c == 0)


def _tile_sizes(bsz, length):
    if length % SUBLANES == 0:
        tm = _largest_divisor(length, TOKEN_TILE_LIMIT, SUBLANES)
    else:
        tm = _largest_divisor(bsz * length, TOKEN_TILE_LIMIT, SUBLANES)
    return tm, _largest_divisor(length, SCAN_BLOCK_LIMIT)


def _trunk(h, shift0, wkv0, conv0, ssm0, norm_mix, norm_ffn, norm_final, rwkv_p, mamba_p, peer_p, bsz, length):
    tm, tt = _tile_sizes(bsz, length)
    h, shift, wkv = _rwkv_layer(h, norm_mix[0], shift0, wkv0, rwkv_p, bsz, length, tm, tt)
    h = _peer_layer(h, norm_ffn[0], peer_p[0], PEER_TOKEN_TILE)
    h, conv, ssm = _mamba_layer(h, norm_mix[1], conv0, ssm0, mamba_p, bsz, length, tm, M_CHUNK)
    y = _peer_layer(h, norm_ffn[1], peer_p[1], PEER_TOKEN_TILE, g_final=norm_final)
    return y.reshape(bsz, length, D_MODEL), shift[None], wkv[None], conv[None], ssm[None]


def kernel(x_prompt, x_sample, state_rwkv_shift, state_rwkv_wkv, state_mamba_conv, state_mamba_ssm, meta_tokens, norm_mix, norm_ffn, norm_final, rwkv_mix, rwkv_w_rkv, rwkv_w0, rwkv_w1, rwkv_w2, rwkv_a0, rwkv_a1, rwkv_a2, rwkv_g1, rwkv_g2, rwkv_k_k, rwkv_k_a, rwkv_r_k, rwkv_ln_w, rwkv_ln_b, rwkv_w_o, mamba_in_proj, mamba_conv_w, mamba_conv_b, mamba_dt_bias, mamba_a_log, mamba_d, mamba_norm_w, mamba_out_proj, peer_w_q, peer_sub_keys, peer_u, peer_v):
    rwkv_p = _prep_rwkv(rwkv_mix[0], rwkv_w_rkv[0], rwkv_w0[0], rwkv_w1[0], rwkv_w2[0], rwkv_a0[0],
                        rwkv_a1[0], rwkv_a2[0], rwkv_g1[0], rwkv_g2[0], rwkv_k_k[0], rwkv_k_a[0],
                        rwkv_r_k[0], rwkv_ln_w[0], rwkv_ln_b[0], rwkv_w_o[0])
    mamba_p = _prep_mamba(mamba_in_proj[0], mamba_conv_w[0], mamba_conv_b[0], mamba_dt_bias[0],
                          mamba_a_log[0], mamba_d[0], mamba_norm_w[0], mamba_out_proj[0])
    peer_p = [_prep_peer(peer_w_q[i], peer_sub_keys[i], peer_u[i], peer_v[i]) for i in range(2)]

    bp, lp = x_prompt.shape[0], x_prompt.shape[1] + N_META
    meta = jnp.broadcast_to(meta_tokens[None], (bp, N_META, D_MODEL))
    hp = jnp.concatenate([meta, x_prompt], axis=1).reshape(bp * lp, D_MODEL)
    zeros = lambda *s: jnp.zeros(s, F32)
    yp, p_shift, p_wkv, p_conv, p_ssm = _trunk(
        hp, zeros(bp, D_MODEL), zeros(bp, RW_HEADS, RW_HEAD, RW_HEAD), zeros(bp, M_CONV - 1, M_CONV_DIM),
        zeros(bp, M_HEADS, M_HEADDIM, M_STATE), norm_mix, norm_ffn, norm_final, rwkv_p, mamba_p, peer_p, bp, lp)

    bs, ls = x_sample.shape[0], x_sample.shape[1]
    ys, s_shift, s_wkv, s_conv, s_ssm = _trunk(
        x_sample.reshape(bs * ls, D_MODEL), state_rwkv_shift[0], state_rwkv_wkv[0], state_mamba_conv[0],
        state_mamba_ssm[0], norm_mix, norm_ffn, norm_final, rwkv_p, mamba_p, peer_p, bs, ls)
    return (yp[:, N_META:], ys, p_shift, p_wkv, p_conv, p_ssm, s_shift, s_wkv, s_conv, s_ssm)
```

```python
import functools
import math

import jax
import jax.numpy as jnp
from jax import lax
from jax.experimental import pallas as pl
from jax.experimental.pallas import tpu as pltpu

F32 = jnp.float32
BF16 = jnp.bfloat16

D_MODEL = 1024
N_META = 16
NORM_EPS = 1e-5
RW_HEAD = 64
RW_HEADS = D_MODEL // RW_HEAD
RW_LN_EPS = 64e-5
LORA_PAD = 128
GATE_LORA_PAD = 256
M_D_INNER = 2048
M_HEADDIM = 64
M_HEADS = M_D_INNER // M_HEADDIM
M_GROUPS = 4
M_HPG = M_HEADS // M_GROUPS
M_STATE = 128
M_CONV = 4
M_CONV_DIM = M_D_INNER + 2 * M_GROUPS * M_STATE
M_IN_DIM = 2 * M_D_INNER + 2 * M_GROUPS * M_STATE + M_HEADS
M_DT_PAD = 128
M_CHUNK = 128
P_HEADS = 8
P_NKEYS = 128
P_QDIM = 256
P_TOPK = 16
P_EXPERT_TILE = 2048

LANES = 128
SUBLANES = 8
VMEM_LIMIT_BYTES = 56 * 1024 * 1024


def _params(*semantics):
    return pltpu.CompilerParams(dimension_semantics=semantics, vmem_limit_bytes=VMEM_LIMIT_BYTES)


def _const_spec(shape):
    zeros = (0,) * len(shape)
    return pl.BlockSpec(shape, lambda *_: zeros)


def _rms(x, g):
    return x * lax.rsqrt(jnp.mean(x * x, axis=-1, keepdims=True) + NORM_EPS) * g


def _softplus(x):
    return jnp.maximum(x, 0.0) + jnp.log1p(jnp.exp(-jnp.abs(x)))


def _bdot(a, b):
    return jnp.dot(a.astype(BF16), b.astype(BF16), preferred_element_type=F32)


def _split_bf16(w):
    hi = w.astype(BF16)
    return hi, (w - hi.astype(F32)).astype(BF16)


def _split3_bf16(x):
    p0 = x.astype(BF16)
    r0 = x - p0.astype(F32)
    p1 = r0.astype(BF16)
    p2 = (r0 - p1.astype(F32)).astype(BF16)
    return p0, p1, p2


def _dot_select(a, sel):
    p0, p1, p2 = _split3_bf16(a)
    return (jnp.dot(p0, sel, preferred_element_type=F32)
            + (jnp.dot(p1, sel, preferred_element_type=F32) + jnp.dot(p2, sel, preferred_element_type=F32)))


def _select_dot(sel, b):
    p0, p1, p2 = _split3_bf16(b)
    return (jnp.dot(sel, p0, preferred_element_type=F32)
            + (jnp.dot(sel, p1, preferred_element_type=F32) + jnp.dot(sel, p2, preferred_element_type=F32)))


def _dot3(a, b_hi, b_lo):
    a_hi, a_lo = _split_bf16(a)
    return (jnp.dot(a_hi, b_hi, preferred_element_type=F32)
            + (jnp.dot(a_hi, b_lo, preferred_element_type=F32) + jnp.dot(a_lo, b_hi, preferred_element_type=F32)))


def _norm_kernel(h_ref, g_ref, o_ref):
    o_ref[...] = _rms(h_ref[...], g_ref[...])


def _norm(h, g, tm):
    t, d = h.shape
    return pl.pallas_call(
        _norm_kernel,
        grid=(t // tm,),
        in_specs=[pl.BlockSpec((tm, d), lambda i: (i, 0)), _const_spec((1, d))],
        out_specs=pl.BlockSpec((tm, d), lambda i: (i, 0)),
        out_shape=jax.ShapeDtypeStruct((t, d), F32),
        compiler_params=_params("parallel"),
        name="rms_norm",
    )(h, g.reshape(1, d))


def _rwkv_proj_kernel(xn_ref, prev_ref, *refs):
    _rwkv_proj_body(xn_ref[...], prev_ref[...], *refs)


def _rwkv_proj_fused_kernel(h_ref, before_ref, shift_ref, gain_ref, *refs, tiles_per_seq):
    *proj_refs, last_ref = refs
    i = pl.program_id(0)
    xn = _rms(h_ref[...], gain_ref[...])
    before = _rms(before_ref[SUBLANES - 1:SUBLANES, :], gain_ref[...])
    first = jnp.where(i % tiles_per_seq == 0, shift_ref[0], before)
    row = lax.broadcasted_iota(jnp.int32, xn.shape, 0)
    prev = jnp.where(row == 0, first, pltpu.roll(xn, 1, axis=0))
    _rwkv_proj_body(xn, prev, *proj_refs)

    @pl.when(i % tiles_per_seq == tiles_per_seq - 1)
    def _():
        last_ref[0] = xn[xn.shape[0] - 1:, :]


def _rwkv_proj_body(xn, prev, mix_ref, vec_ref, wr_ref, wk_ref, wv_ref,
                    w1h_ref, w1l_ref, w2h_ref, w2l_ref, a1h_ref, a1l_ref, a2h_ref, a2l_ref, g1_ref, g2_ref,
                    r_ref, w_ref, k_ref, v_ref, kk_ref, a_ref, g_ref):
    dx = prev - xn
    xr, xw, xk, xv, xa, xg = [xn + dx * mix_ref[j:j + 1, :] for j in range(6)]
    w0, a0, k_k, k_a = [vec_ref[j:j + 1, :] for j in range(4)]
    r_ref[...] = _bdot(xr, wr_ref[...])
    k = _bdot(xk, wk_ref[...])
    v_ref[...] = _bdot(xv, wv_ref[...])
    w_lora = _dot3(jnp.tanh(_dot3(xw, w1h_ref[...], w1l_ref[...])), w2h_ref[...], w2l_ref[...])
    w_log = -_softplus(-(w0 + w_lora)) - 0.5
    w_ref[...] = jnp.exp(-jnp.exp(w_log))
    a = jax.nn.sigmoid(a0 + _dot3(_dot3(xa, a1h_ref[...], a1l_ref[...]), a2h_ref[...], a2l_ref[...]))
    a_ref[...] = a
    g_ref[...] = _bdot(jax.nn.sigmoid(_bdot(xg, g1_ref[...])), g2_ref[...])
    kk_ref[...] = k * k_k
    k_ref[...] = k * (1.0 + (a - 1.0) * k_a)


def _rwkv_proj(xn, prev, p, tm):
    t, d = xn.shape
    tok = pl.BlockSpec((tm, d), lambda i: (i, 0))
    weights = [p["mix"], p["vec"], p["wr"], p["wk"], p["wv"], *p["w1"], *p["w2"], *p["a1"], *p["a2"],
               p["g1"], p["g2"]]
    return pl.pallas_call(
        _rwkv_proj_kernel,
        grid=(t // tm,),
        in_specs=[tok, tok] + [_const_spec(w.shape) for w in weights],
        out_specs=[tok] * 7,
        out_shape=[jax.ShapeDtypeStruct((t, d), F32)] * 7,
        compiler_params=_params("parallel"),
        name="rwkv_proj",
    )(xn, prev, *weights)


def _rwkv_proj_fused(h, gain, shift0, p, tm, length):
    t, d = h.shape
    bsz = t // length
    tiles_per_seq = length // tm
    tok = pl.BlockSpec((tm, d), lambda i: (i, 0))
    before = pl.BlockSpec((SUBLANES, d), lambda i: (jnp.maximum(i * (tm // SUBLANES) - 1, 0), 0))
    per_seq = pl.BlockSpec((1, 1, d), lambda i: (i // tiles_per_seq, 0, 0))
    weights = [p["mix"], p["vec"], p["wr"], p["wk"], p["wv"], *p["w1"], *p["w2"], *p["a1"], *p["a2"],
               p["g1"], p["g2"]]
    kern = functools.partial(_rwkv_proj_fused_kernel, tiles_per_seq=tiles_per_seq)
    *proj, last = pl.pallas_call(
        kern,
        grid=(t // tm,),
        in_specs=[tok, before, per_seq, _const_spec((1, d))] + [_const_spec(w.shape) for w in weights],
        out_specs=[tok] * 7 + [per_seq],
        out_shape=[jax.ShapeDtypeStruct((t, d), F32)] * 7 + [jax.ShapeDtypeStruct((bsz, 1, d), F32)],
        compiler_params=_params("arbitrary"),
        name="rwkv_norm_proj",
    )(h, h, shift0.reshape(bsz, 1, d), gain.reshape(1, d), *weights)
    return proj, last.reshape(bsz, d)


def _rwkv_scan_kernel(r_ref, w_ref, k_ref, v_ref, kk_ref, a_ref, s0_ref, lnw_ref, lnb_ref, rk_ref,
                      o_ref, sfin_ref, state, o_rows):
    tb = pl.program_id(1)
    steps = r_ref.shape[0]

    @pl.when(tb == 0)
    def _():
        state[...] = s0_ref[...]

    live = slice(0, RW_HEAD)

    def step(t, carry):
        r = r_ref[t, live, :]
        w = w_ref[t, live, :]
        k = k_ref[t, live, :]
        vv = v_ref[t, live, :]
        kku = kk_ref[t, live, :]
        norm = jnp.sqrt(jnp.sum(kku * kku, axis=0, keepdims=True))
        kk = kku / jnp.maximum(norm, 1e-12)
        alpha = -kk
        beta = kk * a_ref[t, live, :]

        def row(vi, c):
            s_v = state[vi]
            u = jnp.sum(s_v * alpha, axis=0, keepdims=True)
            s_new = s_v * w + u * beta + v_ref[t, pl.ds(vi, 1), :] * k
            state[vi] = s_new
            o_rows[pl.ds(vi, 1), :] = jnp.sum(s_new * r, axis=0, keepdims=True)
            return c

        lax.fori_loop(0, RW_HEAD, row, 0, unroll=32)
        o = o_rows[...]
        mu = jnp.mean(o, axis=0, keepdims=True)
        var = jnp.mean(jnp.square(o - mu), axis=0, keepdims=True)
        on = (o - mu) * lax.rsqrt(var + RW_LN_EPS)
        bonus = jnp.sum(r * k * rk_ref[...], axis=0, keepdims=True) * vv
        o_ref[t, live, :] = on * lnw_ref[...] + lnb_ref[...] + bonus
        o_ref[t, RW_HEAD:, :] = jnp.zeros((o_ref.shape[1] - RW_HEAD, LANES), F32)
        return carry

    lax.fori_loop(0, steps, step, 0)

    @pl.when(tb == pl.num_programs(1) - 1)
    def _():
        sfin_ref[...] = state[...]


def _rwkv_scan(seqs, s0, lnw, lnb, rk, tt):
    length, pitch, probs = seqs[0].shape
    n = RW_HEAD
    groups = probs // LANES
    seq_spec = pl.BlockSpec((tt, pitch, LANES), lambda g, i: (i, 0, g))
    st_spec = pl.BlockSpec((n, n, LANES), lambda g, i: (0, 0, g))
    lane_spec = pl.BlockSpec((n, LANES), lambda g, i: (0, 0))
    return pl.pallas_call(
        _rwkv_scan_kernel,
        grid=(groups, length // tt),
        in_specs=[seq_spec] * 6 + [st_spec] + [lane_spec] * 3,
        out_specs=[seq_spec, st_spec],
        out_shape=[jax.ShapeDtypeStruct((length, pitch, probs), F32),
                   jax.ShapeDtypeStruct((n, n, probs), F32)],
        scratch_shapes=[pltpu.VMEM((n, n, LANES), F32), pltpu.VMEM((n, LANES), F32)],
        compiler_params=_params("arbitrary", "arbitrary"),
        name="rwkv_scan",
    )(*seqs, s0, lnw, lnb, rk)


def _rwkv_out_kernel(o_ref, g_ref, h_ref, wo_ref, out_ref):
    out_ref[...] = h_ref[...] + _bdot(o_ref[...] * g_ref[...], wo_ref[...])


def _rwkv_out(o, g, h, wo, tm):
    t, d = h.shape
    tok = pl.BlockSpec((tm, d), lambda i: (i, 0))
    return pl.pallas_call(
        _rwkv_out_kernel,
        grid=(t // tm,),
        in_specs=[tok, tok, tok, _const_spec(wo.shape)],
        out_specs=tok,
        out_shape=jax.ShapeDtypeStruct((t, d), F32),
        compiler_params=_params("parallel"),
        name="rwkv_out",
    )(o, g, h, wo)


def _mamba_in_kernel(h_ref, g_ref, w_ref, z_ref, xbc_ref, dt_ref):
    xn = _rms(h_ref[...], g_ref[...]).astype(BF16)
    z_ref[...] = jnp.dot(xn, w_ref[:, :M_D_INNER], preferred_element_type=F32)
    xbc_ref[...] = jnp.dot(xn, w_ref[:, M_D_INNER:M_D_INNER + M_CONV_DIM], preferred_element_type=F32)
    dt_ref[...] = jnp.dot(xn, w_ref[:, M_D_INNER + M_CONV_DIM:], preferred_element_type=F32)


def _mamba_in(h, g, w_in, tm):
    t, d = h.shape
    return pl.pallas_call(
        _mamba_in_kernel,
        grid=(t // tm,),
        in_specs=[pl.BlockSpec((tm, d), lambda i: (i, 0)), _const_spec((1, d)), _const_spec(w_in.shape)],
        out_specs=[pl.BlockSpec((tm, M_D_INNER), lambda i: (i, 0)),
                   pl.BlockSpec((tm, M_CONV_DIM), lambda i: (i, 0)),
                   pl.BlockSpec((tm, M_DT_PAD), lambda i: (i, 0))],
        out_shape=[jax.ShapeDtypeStruct((t, M_D_INNER), F32),
                   jax.ShapeDtypeStruct((t, M_CONV_DIM), F32),
                   jax.ShapeDtypeStruct((t, M_DT_PAD), F32)],
        compiler_params=_params("parallel"),
        name="mamba_in_proj",
    )(h, g.reshape(1, d), w_in)


def _mamba_ssd_kernel(xbc_ref, z_ref, dt_ref, h_ref, conv0_ref, ssm0_ref, convw_ref, vec_ref,
                      dvec_ref, expand_ref, normw_ref, wout_ref,
                      out_ref, ssm_ref,
                      conv_buf, src_adt, src_acs, src_b, src_xdt, src_xdt_st, state, y_buf, *, seq_len, rows):
    b = pl.program_id(0)
    c = pl.program_id(1)
    q = M_CHUNK
    halo = SUBLANES
    d_bc = M_GROUPS * M_STATE

    @pl.when(c == 0)
    def _():
        state[...] = ssm0_ref[0]
        conv_buf[0:halo, :] = conv0_ref[0]

    @pl.when(c > 0)
    def _():
        conv_buf[0:halo, :] = conv_buf[rows:rows + halo, :]

    if rows < q:
        @pl.when((b == 0) & (c == 0))
        def _():
            src_adt[...] = jnp.zeros(src_adt.shape, F32)
            src_acs[...] = jnp.zeros(src_acs.shape, F32)
            src_b[...] = jnp.zeros(src_b.shape, BF16)
            src_xdt[...] = jnp.zeros(src_xdt.shape, BF16)
            src_xdt_st[...] = jnp.zeros(src_xdt_st.shape, BF16)

    conv_buf[halo:halo + rows, :] = xbc_ref[0]
    valid = (lax.broadcasted_iota(jnp.int32, (rows, 1), 0) + c * rows) < seq_len
    conv = vec_ref[0:1, :]
    for j in range(M_CONV):
        conv = conv + conv_buf[halo - (M_CONV - 1) + j:halo - (M_CONV - 1) + j + rows, :] * convw_ref[j:j + 1, :]
    act = jnp.where(valid, jax.nn.silu(conv), 0.0)
    xs = act[:, :M_D_INNER]
    dt = jnp.where(valid, _softplus(dt_ref[0] + dvec_ref[0:1, :]), 0.0)
    src_adt[0:rows, :] = dt * dvec_ref[1:2, :]
    li = lax.broadcasted_iota(jnp.int32, (rows, q), 0)
    si = lax.broadcasted_iota(jnp.int32, (rows, q), 1)
    causal = li >= si
    acs = _select_dot(causal.astype(BF16), src_adt[...])
    src_acs[0:rows, :] = acs
    acs_t = src_acs[...].T
    expand = expand_ref[...]
    dt_full = _dot_select(dt, expand)
    acs_full = _dot_select(acs, expand)
    last = acs[rows - 1:rows, :]
    last_full = acs_full[rows - 1:rows, :]
    xdt = xs * dt_full
    src_xdt[0:rows, :] = xdt.astype(BF16)
    src_xdt_st[0:rows, :] = (xdt * jnp.exp(last_full - acs_full)).astype(BF16)
    src_b[0:rows, :] = act[:, M_D_INNER:M_D_INNER + d_bc].astype(BF16)
    eacs_full = jnp.exp(acs_full)
    state_decay = jnp.exp(last)
    width = M_HPG * M_HEADDIM

    for g in range(M_GROUPS):
        cols = slice(g * width, (g + 1) * width)
        b_g = src_b[:, g * M_STATE:(g + 1) * M_STATE]
        c_g = act[:, M_D_INNER + d_bc + g * M_STATE:M_D_INNER + d_bc + (g + 1) * M_STATE].astype(BF16)
        cb = lax.dot_general(c_g, b_g, (((1,), (1,)), ((), ())), preferred_element_type=F32)
        h_g = state[g * M_HPG:(g + 1) * M_HPG].reshape(width, M_STATE)
        y_off = lax.dot_general(c_g, h_g.astype(BF16), (((1,), (1,)), ((), ())),
                                preferred_element_type=F32)
        upd = lax.dot_general(src_xdt_st[:, cols], b_g, (((0,), (0,)), ((), ())),
                              preferred_element_type=F32)
        ys = []
        for r in range(g * M_HPG, (g + 1) * M_HPG):
            lo = r * M_HEADDIM
            decay = jnp.where(causal, jnp.exp(acs[:, r:r + 1] - acs_t[r:r + 1, :]), 0.0)
            ys.append(jnp.dot((cb * decay).astype(BF16), src_xdt[:, lo:lo + M_HEADDIM],
                              preferred_element_type=F32))
            sub = slice((r - g * M_HPG) * M_HEADDIM, (r - g * M_HPG + 1) * M_HEADDIM)
            state[r] = state[r] * state_decay[:, r:r + 1] + upd[sub, :]
        y_buf[:, cols] = jnp.concatenate(ys, axis=1) + y_off * eacs_full[:, cols]

    y = y_buf[...] + vec_ref[1:2, :M_D_INNER] * xs
    yg = y * jax.nn.silu(z_ref[0])
    parts = []
    for g in range(M_GROUPS):
        part = yg[:, g * width:(g + 1) * width]
        parts.append(part * lax.rsqrt(jnp.mean(part * part, axis=-1, keepdims=True) + NORM_EPS))
    yn = jnp.concatenate(parts, axis=1) * normw_ref[...]
    out_ref[0] = h_ref[0] + _bdot(yn, wout_ref[...])

    @pl.when(c == pl.num_programs(1) - 1)
    def _():
        ssm_ref[0] = state[...]


def _mamba_ssd(xbc, z, dt, h, conv0, ssm0, p, rows, seq_len):
    bsz, length, _ = xbc.shape
    chunks = pl.cdiv(length, rows)
    weights = [p["conv_w"], p["vec"], p["dvec"], p["expand"], p["norm_w"], p["w_out"]]

    def seq_spec(width):
        return pl.BlockSpec((1, rows, width), lambda b, c: (b, c, 0))

    kern = functools.partial(_mamba_ssd_kernel, seq_len=seq_len, rows=rows)
    return pl.pallas_call(
        kern,
        grid=(bsz, chunks),
        in_specs=[seq_spec(M_CONV_DIM), seq_spec(M_D_INNER), seq_spec(M_DT_PAD), seq_spec(D_MODEL),
                  pl.BlockSpec((1, SUBLANES, M_CONV_DIM), lambda b, c: (b, 0, 0)),
                  pl.BlockSpec((1, M_HEADS, M_HEADDIM, M_STATE), lambda b, c: (b, 0, 0, 0))]
                 + [_const_spec(w.shape) for w in weights],
        out_specs=[seq_spec(D_MODEL),
                   pl.BlockSpec((1, M_HEADS, M_HEADDIM, M_STATE), lambda b, c: (b, 0, 0, 0))],
        out_shape=[jax.ShapeDtypeStruct((bsz, length, D_MODEL), F32),
                   jax.ShapeDtypeStruct((bsz, M_HEADS, M_HEADDIM, M_STATE), F32)],
        scratch_shapes=[pltpu.VMEM((rows + SUBLANES, M_CONV_DIM), F32),
                        pltpu.VMEM((M_CHUNK, M_DT_PAD), F32),
                        pltpu.VMEM((M_CHUNK, M_DT_PAD), F32),
                        pltpu.VMEM((M_CHUNK, M_GROUPS * M_STATE), BF16),
                        pltpu.VMEM((M_CHUNK, M_D_INNER), BF16),
                        pltpu.VMEM((M_CHUNK, M_D_INNER), BF16),
                        pltpu.VMEM((M_HEADS, M_HEADDIM, M_STATE), F32),
                        pltpu.VMEM((rows, M_D_INNER), F32)],
        compiler_params=_params("arbitrary", "arbitrary"),
        name="mamba_ssd",
    )(xbc, z, dt, h, conv0, ssm0, *weights)


_CANDS = [(ra, rb) for ra in range(P_TOPK) for rb in range(P_TOPK) if (ra + 1) * (rb + 1) <= P_TOPK]


def _peer_route_kernel(h_ref, g_ref, wq_ref, keys_ref, keys_lo_ref,
                       xt_ref, r1_ref, e1_ref, m0_ref, w0_ref,
                       q_buf, s_buf, code0, code_book, best_vals, cand, counts, inv_z, *, n_tokens):
    tm = h_ref.shape[0]
    neg_inf = float("-inf")
    h_tile = h_ref[...]
    if n_tokens % tm:
        def stand_in():
            row = lax.broadcasted_iota(jnp.int32, h_tile.shape, 0)
            col = lax.broadcasted_iota(jnp.int32, h_tile.shape, 1)
            filler = ((row * 7 + col * 13) & 63).astype(F32) * (1.0 / 64.0) - 0.5
            return jnp.where(row < n_tokens - pl.program_id(0) * tm, h_tile, filler)

        h_tile = lax.cond(pl.program_id(0) == pl.num_programs(0) - 1, stand_in, lambda: h_tile)
    xn = _rms(h_tile, g_ref[...])
    xt = xn.T.astype(BF16)
    xt_ref[...] = xt
    q_buf[...] = jnp.dot(wq_ref[...], xt, preferred_element_type=F32)
    key_iota = lax.broadcasted_iota(jnp.int32, (P_NKEYS, tm), 0).astype(F32)

    def half(m, carry):
        head = m // 2
        z = m % 2
        row0 = pl.multiple_of(m * P_NKEYS, P_NKEYS)
        qm = q_buf[pl.ds(row0, P_NKEYS), :]
        q_hi = qm.astype(BF16)
        q_lo = (qm - q_hi.astype(F32)).astype(BF16)
        k_hi = keys_ref[m]
        s = (jnp.dot(k_hi, q_hi, preferred_element_type=F32)
             + (jnp.dot(k_hi, q_lo, preferred_element_type=F32)
                + jnp.dot(keys_lo_ref[m], q_hi, preferred_element_type=F32)))

        s_buf[...] = s

        def next_distinct(j, prev):
            sj = s_buf[...]
            best = jnp.max(jnp.where(sj < prev, sj, neg_inf), axis=0, keepdims=True)
            best_vals[z, j, pl.ds(head, 1), :] = best
            return best

        def extract_ties(j, sc):
            cur, rank = sc
            best = jnp.max(cur, axis=0, keepdims=True)
            first = jnp.min(jnp.where(cur == best, key_iota, float(P_NKEYS)), axis=0, keepdims=True)
            sel = key_iota == first
            best_vals[z, j, pl.ds(head, 1), :] = best
            return jnp.where(sel, neg_inf, cur), jnp.where(sel, jnp.asarray(j, dtype=F32), rank)

        last = lax.fori_loop(0, P_TOPK, next_distinct, jnp.full((1, tm), float("inf"), F32))
        reached = jnp.sum(jnp.where(s >= last, 1.0, 0.0), axis=0, keepdims=True)
        has_ties = jnp.max(jnp.abs(reached - float(P_TOPK))) > 0.0

        def rank_by_count():
            rank = jnp.zeros((P_NKEYS, tm), F32)
            for jj in range(P_TOPK):
                rank = rank + jnp.where(best_vals[z, jj, pl.ds(head, 1), :] > s, 1.0, 0.0)
            return rank

        def rank_with_ties():
            no_rank = jnp.full((P_NKEYS, tm), float(P_NKEYS), F32)
            return lax.fori_loop(0, P_TOPK, extract_ties, (s, no_rank))[1]

        e = jnp.exp(s - jnp.max(s, axis=0, keepdims=True))

        @pl.when(z == 0)
        def _():
            w0_ref[head] = e

            @pl.when(has_ties)
            def _():
                code0[head] = rank_with_ties()
                for ra in range(P_TOPK):
                    code_book[ra, pl.ds(head, 1), :] = jnp.full((1, tm), float(ra), F32)

            @pl.when(jnp.logical_not(has_ties))
            def _():
                code0[head] = s
                for ra in range(P_TOPK):
                    code_book[ra, pl.ds(head, 1), :] = best_vals[0, ra, pl.ds(head, 1), :]

        @pl.when(z == 1)
        def _():
            rank = lax.cond(has_ties, rank_with_ties, rank_by_count)
            for lt in range(tm // LANES):
                r1_ref[head, lt] = rank[:, lt * LANES:(lt + 1) * LANES].astype(BF16)
                e1_ref[head, lt] = e[:, lt * LANES:(lt + 1) * LANES].astype(BF16)

        return carry

    lax.fori_loop(0, 2 * P_HEADS, half, 0)

    ex0 = [jnp.exp(best_vals[0, ra] - best_vals[0, 0]) for ra in range(P_TOPK)]
    ex1 = [jnp.exp(best_vals[1, rb] - best_vals[1, 0]) for rb in range(P_TOPK)]
    for i, (ra, rb) in enumerate(_CANDS):
        cand[i] = best_vals[0, ra] + best_vals[1, rb]

    def next_sum(it, prev):
        best = None
        for i in range(len(_CANDS)):
            ci = cand[i]
            below = jnp.where(ci < prev, ci, neg_inf)
            best = below if best is None else jnp.maximum(best, below)
        return best

    last_sum = lax.fori_loop(0, P_TOPK, next_sum, jnp.full((P_HEADS, tm), float("inf"), F32))
    hits = [jnp.where(cand[i] >= last_sum, 1.0, 0.0) for i in range(len(_CANDS))]
    n_hits = functools.reduce(lambda x, y: x + y, hits)
    sums_tie = jnp.max(jnp.abs(n_hits - float(P_TOPK))) > 0.0

    def select_distinct():
        z = jnp.zeros((P_HEADS, tm), F32)
        per_rank = [None] * P_TOPK
        for i, (ra, rb) in enumerate(_CANDS):
            per_rank[ra] = hits[i] if per_rank[ra] is None else per_rank[ra] + hits[i]
            z = z + hits[i] * (ex0[ra] * ex1[rb])
        for ra in range(P_TOPK):
            counts[ra] = per_rank[ra]
        return z

    def select_with_ties():
        counts[...] = jnp.zeros(counts.shape, F32)

        def pick(it, z):
            best = cand[0]
            for i in range(1, len(_CANDS)):
                best = jnp.maximum(best, cand[i])
            found = jnp.zeros((P_HEADS, tm), F32)
            for i, (ra, rb) in enumerate(_CANDS):
                ci = cand[i]
                hit = jnp.where(ci == best, 1.0, 0.0) * (1.0 - found)
                found = found + hit
                cand[i] = jnp.where(hit > 0.0, neg_inf, ci)
                counts[ra] = counts[ra] + hit
                z = z + hit * (ex0[ra] * ex1[rb])
            return z

        return lax.fori_loop(0, P_TOPK, pick, jnp.zeros((P_HEADS, tm), F32))

    inv_z[...] = 1.0 / lax.cond(sums_tie, select_with_ties, select_distinct)

    def finish(head, carry):
        code = code0[head]
        m0 = jnp.zeros((P_NKEYS, tm), F32)
        for ra in range(P_TOPK):
            m0 = jnp.where(code == code_book[ra, pl.ds(head, 1), :], counts[ra, pl.ds(head, 1), :], m0)
        m0_ref[head] = m0
        w0_ref[head] = w0_ref[head] * inv_z[pl.ds(head, 1), :]
        return carry

    lax.fori_loop(0, P_HEADS, finish, 0)


def _peer_route(h, g, wq_t, keys, keys_lo, tm):
    t, d = h.shape
    tiles = pl.cdiv(t, tm)
    t_pad = tiles * tm
    tile = pl.BlockSpec((P_HEADS, P_NKEYS, tm), lambda i: (0, 0, i))
    tile_f32 = jax.ShapeDtypeStruct((P_HEADS, P_NKEYS, t_pad), F32)
    tile_bf16 = jax.ShapeDtypeStruct((P_HEADS, t_pad // LANES, P_NKEYS, LANES), BF16)
    lane_tiles = pl.BlockSpec((P_HEADS, tm // LANES, P_NKEYS, LANES), lambda i: (0, i, 0, 0))
    return pl.pallas_call(
        functools.partial(_peer_route_kernel, n_tokens=t),
        grid=(tiles,),
        in_specs=[pl.BlockSpec((tm, d), lambda i: (i, 0)), _const_spec((1, d)),
                  _const_spec(wq_t.shape), _const_spec(keys.shape), _const_spec(keys_lo.shape)],
        out_specs=[pl.BlockSpec((d, tm), lambda i: (0, i)), lane_tiles, lane_tiles, tile, tile],
        out_shape=[jax.ShapeDtypeStruct((d, t_pad), BF16), tile_bf16, tile_bf16, tile_f32, tile_f32],
        scratch_shapes=[pltpu.VMEM((P_HEADS * P_QDIM, tm), F32),
                        pltpu.VMEM((P_NKEYS, tm), F32),
                        pltpu.VMEM((P_HEADS, P_NKEYS, tm), F32),
                        pltpu.VMEM((P_TOPK, P_HEADS, tm), F32),
                        pltpu.VMEM((2, P_TOPK, P_HEADS, tm), F32),
                        pltpu.VMEM((len(_CANDS), P_HEADS, tm), F32),
                        pltpu.VMEM((P_TOPK, P_HEADS, tm), F32),
                        pltpu.VMEM((P_HEADS, tm), F32)],
        compiler_params=_params("parallel"),
        name="peer_route",
    )(h, g.reshape(1, d), wq_t, keys, keys_lo)


def _gelu(x):
    return 0.5 * x * (1.0 + lax.erf(x * (1.0 / math.sqrt(2.0))))


def _peer_dense_kernel(xt_ref, u_ref, vt_ref, r1_ref, e1_ref, m0_ref, w0_ref, h_ref, gfin_ref,
                       out_ref, acc, hid, prob, *, final_norm):
    j = pl.program_id(1)
    keys_per_step = P_EXPERT_TILE // P_NKEYS

    @pl.when(j == 0)
    def _():
        acc[...] = jnp.zeros(acc.shape, F32)

    tm = xt_ref.shape[1]
    hid[...] = jnp.dot(u_ref[...], xt_ref[...], preferred_element_type=F32)
    zero = jnp.zeros((), BF16)
    for ii in range(keys_per_step):
        i1 = j * keys_per_step + ii
        rows = slice(ii * P_NKEYS, (ii + 1) * P_NKEYS)
        m0_rows = [m0_ref[head, pl.ds(i1, 1), :] for head in range(P_HEADS)]
        w0_rows = [w0_ref[head, pl.ds(i1, 1), :] for head in range(P_HEADS)]

        def packed_rows(row):
            one = jnp.broadcast_to(row, (2 * SUBLANES, LANES)).astype(BF16)
            return jnp.tile(one, (P_NKEYS // (2 * SUBLANES), 1))

        for lt in range(tm // LANES):
            lanes = slice(lt * LANES, (lt + 1) * LANES)
            gate = None
            for head in range(P_HEADS):
                m0 = packed_rows(m0_rows[head][:, lanes])
                w0 = packed_rows(w0_rows[head][:, lanes])
                term = w0 * jnp.where(r1_ref[head, lt] < m0, e1_ref[head, lt], zero)
                gate = term if gate is None else gate + term
            prob[rows, lanes] = gate * _gelu(hid[rows, lanes]).astype(BF16)
    acc[...] += jnp.dot(vt_ref[...], prob[...], preferred_element_type=F32)

    @pl.when(j == pl.num_programs(1) - 1)
    def _():
        res = h_ref[...] + acc[...].T
        if final_norm:
            res = _rms(res, gfin_ref[...])
        out_ref[...] = res


def _peer_dense(xt, u_bf, vt_bf, r1, e1, m0, w0, h, g_final, tm, final_norm):
    t, d = h.shape
    n_exp = u_bf.shape[0]
    tile = pl.BlockSpec((P_HEADS, P_NKEYS, tm), lambda i, j: (0, 0, i))
    lane_tiles = pl.BlockSpec((P_HEADS, tm // LANES, P_NKEYS, LANES), lambda i, j: (0, i, 0, 0))
    kern = functools.partial(_peer_dense_kernel, final_norm=final_norm)
    return pl.pallas_call(
        kern,
        grid=(pl.cdiv(t, tm), n_exp // P_EXPERT_TILE),
        in_specs=[pl.BlockSpec((d, tm), lambda i, j: (0, i)),
                  pl.BlockSpec((P_EXPERT_TILE, d), lambda i, j: (j, 0)),
                  pl.BlockSpec((d, P_EXPERT_TILE), lambda i, j: (0, j)),
                  lane_tiles, lane_tiles, tile, tile,
                  pl.BlockSpec((tm, d), lambda i, j: (i, 0)),
                  _const_spec((1, d))],
        out_specs=pl.BlockSpec((tm, d), lambda i, j: (i, 0)),
        out_shape=jax.ShapeDtypeStruct((t, d), F32),
        scratch_shapes=[pltpu.VMEM((d, tm), F32),
                        pltpu.VMEM((P_EXPERT_TILE, tm), F32),
                        pltpu.VMEM((P_EXPERT_TILE, tm), BF16)],
        compiler_params=_params("parallel", "arbitrary"),
        name="peer_dense",
    )(xt, u_bf, vt_bf, r1, e1, m0, w0, h, g_final.reshape(1, d))


def _pad_cols(w, n):
    return jnp.pad(w, ((0, 0), (0, n - w.shape[1])))


def _pad_rows(w, n):
    return jnp.pad(w, ((0, n - w.shape[0]), (0, 0)))


def _prep_rwkv(mix, w_rkv, w0, w1, w2, a0, a1, a2, g1, g2, k_k, k_a, r_k, ln_w, ln_b, w_o):
    d = D_MODEL
    lane = lambda x: jnp.repeat(x.reshape(RW_HEADS, RW_HEAD).T, LANES // RW_HEADS, axis=1)
    return {
        "mix": _pad_rows(mix, SUBLANES),
        "vec": _pad_rows(jnp.stack([w0, a0, k_k, k_a]), SUBLANES),
        "wr": w_rkv[0].astype(BF16), "wk": w_rkv[1].astype(BF16), "wv": w_rkv[2].astype(BF16),
        "w1": _split_bf16(_pad_cols(w1, LORA_PAD)), "w2": _split_bf16(_pad_rows(w2, LORA_PAD)),
        "a1": _split_bf16(_pad_cols(a1, LORA_PAD)), "a2": _split_bf16(_pad_rows(a2, LORA_PAD)),
        "g1": _pad_cols(g1, GATE_LORA_PAD).astype(BF16), "g2": _pad_rows(g2, GATE_LORA_PAD).astype(BF16),
        "lnw": lane(ln_w), "lnb": lane(ln_b), "rk": lane(r_k.reshape(d)),
        "wo": w_o.astype(BF16),
    }


def _prep_mamba(in_proj, conv_w, conv_b, dt_bias, a_log, d_skip, norm_w, out_proj):
    head_of_channel = jnp.arange(M_D_INNER) // M_HEADDIM
    expand = (jnp.arange(M_DT_PAD)[:, None] == head_of_channel[None, :]).astype(BF16)
    d_full = jnp.pad(jnp.repeat(d_skip, M_HEADDIM), (0, M_CONV_DIM - M_D_INNER))
    pad_h = lambda x: jnp.pad(x, (0, M_DT_PAD - M_HEADS))
    return {
        "w_in": _pad_cols(in_proj, M_IN_DIM - M_HEADS + M_DT_PAD).astype(BF16),
        "conv_w": _pad_rows(conv_w, SUBLANES),
        "vec": _pad_rows(jnp.stack([conv_b, d_full]), SUBLANES),
        "dvec": _pad_rows(jnp.stack([pad_h(dt_bias), pad_h(-jnp.exp(a_log))]), SUBLANES),
        "expand": expand,
        "norm_w": norm_w.reshape(1, M_D_INNER),
        "w_out": out_proj.astype(BF16),
    }


def _prep_peer(w_q, sub_keys, u_tab, v_tab):
    keys = jnp.transpose(sub_keys, (1, 0, 2, 3)).reshape(2 * P_HEADS, P_NKEYS, P_QDIM // 2)
    keys_hi, keys_lo = _split_bf16(keys)
    return {"wq_t": w_q.T.astype(BF16), "keys": keys_hi, "keys_lo": keys_lo,
            "u": u_tab.astype(BF16), "vt": v_tab.T.astype(BF16)}


SCAN_GROUP = LANES // RW_HEADS
RELAYOUT_STEPS = 128
RELAYOUT_PITCH = RW_HEAD + SUBLANES
SCAN_PITCH = RW_HEAD + SUBLANES


def _to_scan_kernel(x_ref, o_ref, stage):
    for b in range(SCAN_GROUP):
        for hp in range(RW_HEADS // 2):
            t = x_ref[b, :, hp * LANES:(hp + 1) * LANES].T
            for hl in range(2):
                lane = (2 * hp + hl) * SCAN_GROUP + b
                stage[lane * RELAYOUT_PITCH:lane * RELAYOUT_PITCH + RW_HEAD, :] = t[hl * RW_HEAD:(hl + 1) * RW_HEAD, :]
    for k in range(RW_HEAD):
        by_lane = stage[pl.ds(k, LANES, stride=RELAYOUT_PITCH), :]
        o_ref[pl.ds(k, RELAYOUT_STEPS, stride=SCAN_PITCH), :] = by_lane.T
    for k in range(RW_HEAD, SCAN_PITCH):
        o_ref[pl.ds(k, RELAYOUT_STEPS, stride=SCAN_PITCH), :] = jnp.zeros((RELAYOUT_STEPS, LANES), F32)


def _to_scan_pallas(x, length):
    x3 = x.reshape(SCAN_GROUP, length, D_MODEL)
    out = pl.pallas_call(
        _to_scan_kernel,
        grid=(pl.cdiv(length, RELAYOUT_STEPS),),
        in_specs=[pl.BlockSpec((SCAN_GROUP, RELAYOUT_STEPS, D_MODEL), lambda i: (0, i, 0))],
        out_specs=pl.BlockSpec((RELAYOUT_STEPS * SCAN_PITCH, LANES), lambda i: (i, 0)),
        out_shape=jax.ShapeDtypeStruct((length * SCAN_PITCH, LANES), F32),
        scratch_shapes=[pltpu.VMEM((LANES * RELAYOUT_PITCH, LANES), F32)],
        compiler_params=_params("parallel"),
        name="to_scan_layout",
    )(x3)
    return out.reshape(length, SCAN_PITCH, LANES)


def _rwkv_out_scan_kernel(o_ref, g_ref, h_ref, wo_ref, out_ref, stage):
    for v in range(RW_HEAD):
        by_step = o_ref[pl.ds(v, RELAYOUT_STEPS, stride=SCAN_PITCH), :]
        stage[pl.ds(v, LANES, stride=RELAYOUT_PITCH), :] = by_step.T
    for b in range(SCAN_GROUP):
        tiles = []
        for hp in range(RW_HEADS // 2):
            halves = []
            for hl in range(2):
                lane = (2 * hp + hl) * SCAN_GROUP + b
                halves.append(stage[lane * RELAYOUT_PITCH:lane * RELAYOUT_PITCH + RW_HEAD, :])
            tiles.append(jnp.concatenate(halves, axis=0).T)
        o_b = jnp.concatenate(tiles, axis=1)
        out_ref[b] = h_ref[b] + _bdot(o_b * g_ref[b], wo_ref[...])


def _rwkv_out_scan(o_scan, g, h, wo, length):
    d = D_MODEL
    seq = pl.BlockSpec((SCAN_GROUP, RELAYOUT_STEPS, d), lambda i: (0, i, 0))
    out = pl.pallas_call(
        _rwkv_out_scan_kernel,
        grid=(pl.cdiv(length, RELAYOUT_STEPS),),
        in_specs=[pl.BlockSpec((RELAYOUT_STEPS * SCAN_PITCH, LANES), lambda i: (i, 0)), seq, seq,
                  _const_spec(wo.shape)],
        out_specs=seq,
        out_shape=jax.ShapeDtypeStruct((SCAN_GROUP, length, d), F32),
        scratch_shapes=[pltpu.VMEM((LANES * RELAYOUT_PITCH, LANES), F32)],
        compiler_params=_params("parallel"),
        name="rwkv_out_from_scan",
    )(o_scan.reshape(length * SCAN_PITCH, LANES), g.reshape(SCAN_GROUP, length, d), h.reshape(SCAN_GROUP, length, d), wo)
    return out.reshape(SCAN_GROUP * length, d)


def _to_scan(x, bsz, length):
    if bsz == SCAN_GROUP and length >= RELAYOUT_STEPS:
        return _to_scan_pallas(x, length)
    x = x.reshape(bsz // SCAN_GROUP, SCAN_GROUP, length, RW_HEADS, RW_HEAD)
    x = jnp.transpose(x, (2, 4, 0, 3, 1)).reshape(length, RW_HEAD, bsz * RW_HEADS)
    return jnp.pad(x, ((0, 0), (0, SCAN_PITCH - RW_HEAD), (0, 0)))


def _from_scan(x, bsz, length):
    x = x[:, :RW_HEAD].reshape(length, RW_HEAD, bsz // SCAN_GROUP, RW_HEADS, SCAN_GROUP)
    return jnp.transpose(x, (2, 4, 0, 3, 1)).reshape(bsz * length, D_MODEL)


def _state_to_scan(wkv, bsz):
    x = wkv.reshape(bsz // SCAN_GROUP, SCAN_GROUP, RW_HEADS, RW_HEAD, RW_HEAD)
    return jnp.transpose(x, (3, 4, 0, 2, 1)).reshape(RW_HEAD, RW_HEAD, bsz * RW_HEADS)


def _state_from_scan(s, bsz):
    x = s.reshape(RW_HEAD, RW_HEAD, bsz // SCAN_GROUP, RW_HEADS, SCAN_GROUP)
    return jnp.transpose(x, (2, 4, 3, 0, 1)).reshape(bsz, RW_HEADS, RW_HEAD, RW_HEAD)


def _rwkv_layer(h, g_mix, shift0, wkv0, p, bsz, length, tm, tt):
    if length % tm == 0:
        (r, w, k, v, kk, a, g), shift = _rwkv_proj_fused(h, g_mix, shift0, p, tm, length)
    else:
        xn = _norm(h, g_mix, tm)
        xn3 = xn.reshape(bsz, length, D_MODEL)
        prev = jnp.concatenate([shift0[:, None, :], xn3[:, :-1]], axis=1).reshape(bsz * length, D_MODEL)
        r, w, k, v, kk, a, g = _rwkv_proj(xn, prev, p, tm)
        shift = xn3[:, -1]
    seqs = [_to_scan(x, bsz, length) for x in (r, w, k, v, kk, a)]
    o, s_fin = _rwkv_scan(seqs, _state_to_scan(wkv0, bsz), p["lnw"], p["lnb"], p["rk"], tt)
    if bsz == SCAN_GROUP and length >= RELAYOUT_STEPS:
        h_new = _rwkv_out_scan(o, g, h, p["wo"], length)
    else:
        h_new = _rwkv_out(_from_scan(o, bsz, length), g, h, p["wo"], tm)
    return h_new, shift, _state_from_scan(s_fin, bsz)


def _mamba_layer(h, g_mix, conv0, ssm0, p, bsz, length, tm, rows):
    z, xbc, dt = _mamba_in(h, g_mix, p["w_in"], tm)
    xbc3 = xbc.reshape(bsz, length, M_CONV_DIM)
    tail = M_CONV - 1
    conv_new = xbc3[:, length - tail:] if length >= tail else jnp.concatenate([conv0, xbc3], axis=1)[:, length:]
    pad_t = (-length) % SUBLANES if length < rows else 0

    def seq(x, c):
        x = x.reshape(bsz, length, c)
        return jnp.pad(x, ((0, 0), (0, pad_t), (0, 0))) if pad_t else x

    conv0_p = jnp.pad(conv0, ((0, 0), (SUBLANES - (M_CONV - 1), 0), (0, 0)))
    out, ssm = _mamba_ssd(seq(xbc, M_CONV_DIM), seq(z, M_D_INNER), seq(dt, M_DT_PAD), seq(h, D_MODEL),
                          conv0_p, ssm0, p, min(rows, length + pad_t), length)
    if pad_t:
        out = out[:, :length]
    return out.reshape(bsz * length, D_MODEL), conv_new, ssm


def _peer_layer(h, g_ffn, p, tm, g_final=None):
    xt, r1, e1, m0, w0 = _peer_route(h, g_ffn, p["wq_t"], p["keys"], p["keys_lo"], tm)
    final = g_final is not None
    return _peer_dense(xt, p["u"], p["vt"], r1, e1, m0, w0, h, g_final if final else g_ffn, tm, final)


TOKEN_TILE_LIMIT = 384
SCAN_BLOCK_LIMIT = 48
PEER_TOKEN_TILE = 512


def _largest_divisor(n, limit, multiple_of=1):
    return max(c for c in range(multiple_of, limit + 1, multiple_of) if n % c == 0)


def _tile_sizes(bsz, length):
    if length % SUBLANES == 0:
        tm = _largest_divisor(length, TOKEN_TILE_LIMIT, SUBLANES)
    else:
        tm = _largest_divisor(bsz * length, TOKEN_TILE_LIMIT, SUBLANES)
    return tm, _largest_divisor(length, SCAN_BLOCK_LIMIT)


def _trunk(h, shift0, wkv0, conv0, ssm0, norm_mix, norm_ffn, norm_final, rwkv_p, mamba_p, peer_p, bsz, length):
    tm, tt = _tile_sizes(bsz, length)
    h, shift, wkv = _rwkv_layer(h, norm_mix[0], shift0, wkv0, rwkv_p, bsz, length, tm, tt)
    h = _peer_layer(h, norm_ffn[0], peer_p[0], PEER_TOKEN_TILE)
    h, conv, ssm = _mamba_layer(h, norm_mix[1], conv0, ssm0, mamba_p, bsz, length, tm, M_CHUNK)
    y = _peer_layer(h, norm_ffn[1], peer_p[1], PEER_TOKEN_TILE, g_final=norm_final)
    return y.reshape(bsz, length, D_MODEL), shift[None], wkv[None], conv[None], ssm[None]


def kernel(x_prompt, x_sample, state_rwkv_shift, state_rwkv_wkv, state_mamba_conv, state_mamba_ssm, meta_tokens, norm_mix, norm_ffn, norm_final, rwkv_mix, rwkv_w_rkv, rwkv_w0, rwkv_w1, rwkv_w2, rwkv_a0, rwkv_a1, rwkv_a2, rwkv_g1, rwkv_g2, rwkv_k_k, rwkv_k_a, rwkv_r_k, rwkv_ln_w, rwkv_ln_b, rwkv_w_o, mamba_in_proj, mamba_conv_w, mamba_conv_b, mamba_dt_bias, mamba_a_log, mamba_d, mamba_norm_w, mamba_out_proj, peer_w_q, peer_sub_keys, peer_u, peer_v):
    rwkv_p = _prep_rwkv(rwkv_mix[0], rwkv_w_rkv[0], rwkv_w0[0], rwkv_w1[0], rwkv_w2[0], rwkv_a0[0],
                        rwkv_a1[0], rwkv_a2[0], rwkv_g1[0], rwkv_g2[0], rwkv_k_k[0], rwkv_k_a[0],
                        rwkv_r_k[0], rwkv_ln_w[0], rwkv_ln_b[0], rwkv_w_o[0])
    mamba_p = _prep_mamba(mamba_in_proj[0], mamba_conv_w[0], mamba_conv_b[0], mamba_dt_bias[0],
                          mamba_a_log[0], mamba_d[0], mamba_norm_w[0], mamba_out_proj[0])
    peer_p = [_prep_peer(peer_w_q[i], peer_sub_keys[i], peer_u[i], peer_v[i]) for i in range(2)]

    bp, lp = x_prompt.shape[0], x_prompt.shape[1] + N_META
    meta = jnp.broadcast_to(meta_tokens[None], (bp, N_META, D_MODEL))
    hp = jnp.concatenate([meta, x_prompt], axis=1).reshape(bp * lp, D_MODEL)
    zeros = lambda *s: jnp.zeros(s, F32)
    yp, p_shift, p_wkv, p_conv, p_ssm = _trunk(
        hp, zeros(bp, D_MODEL), zeros(bp, RW_HEADS, RW_HEAD, RW_HEAD), zeros(bp, M_CONV - 1, M_CONV_DIM),
        zeros(bp, M_HEADS, M_HEADDIM, M_STATE), norm_mix, norm_ffn, norm_final, rwkv_p, mamba_p, peer_p, bp, lp)

    bs, ls = x_sample.shape[0], x_sample.shape[1]
    ys, s_shift, s_wkv, s_conv, s_ssm = _trunk(
        x_sample.reshape(bs * ls, D_MODEL), state_rwkv_shift[0], state_rwkv_wkv[0], state_mamba_conv[0],
        state_mamba_ssm[0], norm_mix, norm_ffn, norm_final, rwkv_p, mamba_p, peer_p, bs, ls)
    return (yp[:, N_META:], ys, p_shift, p_wkv, p_conv, p_ssm, s_shift, s_wkv, s_conv, s_ssm)
```

```python
import functools
import math

import jax
import jax.numpy as jnp
from jax import lax
from jax.experimental import pallas as pl
from jax.experimental.pallas import tpu as pltpu

F32 = jnp.float32
BF16 = jnp.bfloat16

D_MODEL = 1024
N_META = 16
NORM_EPS = 1e-5
RW_HEAD = 64
RW_HEADS = D_MODEL // RW_HEAD
RW_LN_EPS = 64e-5
LORA_PAD = 128
GATE_LORA_PAD = 256
M_D_INNER = 2048
M_HEADDIM = 64
M_HEADS = M_D_INNER // M_HEADDIM
M_GROUPS = 4
M_HPG = M_HEADS // M_GROUPS
M_STATE = 128
M_CONV = 4
M_CONV_DIM = M_D_INNER + 2 * M_GROUPS * M_STATE
M_IN_DIM = 2 * M_D_INNER + 2 * M_GROUPS * M_STATE + M_HEADS
M_DT_PAD = 128
M_CHUNK = 128
P_HEADS = 8
P_NKEYS = 128
P_QDIM = 256
P_TOPK = 16
P_EXPERT_TILE = 2048

LANES = 128
SUBLANES = 8
VMEM_LIMIT_BYTES = 56 * 1024 * 1024


def _params(*semantics):
    return pltpu.CompilerParams(dimension_semantics=semantics, vmem_limit_bytes=VMEM_LIMIT_BYTES)


def _const_spec(shape):
    zeros = (0,) * len(shape)
    return pl.BlockSpec(shape, lambda *_: zeros)


def _rms(x, g):
    return x * lax.rsqrt(jnp.mean(x * x, axis=-1, keepdims=True) + NORM_EPS) * g


def _softplus(x):
    return jnp.maximum(x, 0.0) + jnp.log1p(jnp.exp(-jnp.abs(x)))


def _bdot(a, b):
    return jnp.dot(a.astype(BF16), b.astype(BF16), preferred_element_type=F32)


def _split_bf16(w):
    hi = w.astype(BF16)
    return hi, (w - hi.astype(F32)).astype(BF16)


def _split3_bf16(x):
    p0 = x.astype(BF16)
    r0 = x - p0.astype(F32)
    p1 = r0.astype(BF16)
    p2 = (r0 - p1.astype(F32)).astype(BF16)
    return p0, p1, p2


def _dot_select(a, sel):
    p0, p1, p2 = _split3_bf16(a)
    return (jnp.dot(p0, sel, preferred_element_type=F32)
            + (jnp.dot(p1, sel, preferred_element_type=F32) + jnp.dot(p2, sel, preferred_element_type=F32)))


def _select_dot(sel, b):
    p0, p1, p2 = _split3_bf16(b)
    return (jnp.dot(sel, p0, preferred_element_type=F32)
            + (jnp.dot(sel, p1, preferred_element_type=F32) + jnp.dot(sel, p2, preferred_element_type=F32)))


def _dot3(a, b_hi, b_lo):
    a_hi, a_lo = _split_bf16(a)
    return (jnp.dot(a_hi, b_hi, preferred_element_type=F32)
            + (jnp.dot(a_hi, b_lo, preferred_element_type=F32) + jnp.dot(a_lo, b_hi, preferred_element_type=F32)))


def _norm_kernel(h_ref, g_ref, o_ref):
    o_ref[...] = _rms(h_ref[...], g_ref[...])


def _norm(h, g, tm):
    t, d = h.shape
    return pl.pallas_call(
        _norm_kernel,
        grid=(t // tm,),
        in_specs=[pl.BlockSpec((tm, d), lambda i: (i, 0)), _const_spec((1, d))],
        out_specs=pl.BlockSpec((tm, d), lambda i: (i, 0)),
        out_shape=jax.ShapeDtypeStruct((t, d), F32),
        compiler_params=_params("parallel"),
        name="rms_norm",
    )(h, g.reshape(1, d))


def _rwkv_proj_kernel(xn_ref, prev_ref, *refs):
    _rwkv_proj_body(xn_ref[...], prev_ref[...], *refs)


def _rwkv_proj_fused_kernel(h_ref, before_ref, shift_ref, gain_ref, *refs, tiles_per_seq):
    *proj_refs, last_ref = refs
    i = pl.program_id(0)
    xn = _rms(h_ref[...], gain_ref[...])
    before = _rms(before_ref[SUBLANES - 1:SUBLANES, :], gain_ref[...])
    first = jnp.where(i % tiles_per_seq == 0, shift_ref[0], before)
    row = lax.broadcasted_iota(jnp.int32, xn.shape, 0)
    prev = jnp.where(row == 0, first, pltpu.roll(xn, 1, axis=0))
    _rwkv_proj_body(xn, prev, *proj_refs)

    @pl.when(i % tiles_per_seq == tiles_per_seq - 1)
    def _():
        last_ref[0] = xn[xn.shape[0] - 1:, :]


def _rwkv_proj_body(xn, prev, mix_ref, vec_ref, wr_ref, wk_ref, wv_ref,
                    w1h_ref, w1l_ref, w2h_ref, w2l_ref, a1h_ref, a1l_ref, a2h_ref, a2l_ref, g1_ref, g2_ref,
                    r_ref, w_ref, k_ref, v_ref, kk_ref, a_ref, g_ref):
    dx = prev - xn
    xr, xw, xk, xv, xa, xg = [xn + dx * mix_ref[j:j + 1, :] for j in range(6)]
    w0, a0, k_k, k_a = [vec_ref[j:j + 1, :] for j in range(4)]
    r_ref[...] = _bdot(xr, wr_ref[...])
    k = _bdot(xk, wk_ref[...])
    v_ref[...] = _bdot(xv, wv_ref[...])
    w_lora = _dot3(jnp.tanh(_dot3(xw, w1h_ref[...], w1l_ref[...])), w2h_ref[...], w2l_ref[...])
    w_log = -_softplus(-(w0 + w_lora)) - 0.5
    w_ref[...] = jnp.exp(-jnp.exp(w_log))
    a = jax.nn.sigmoid(a0 + _dot3(_dot3(xa, a1h_ref[...], a1l_ref[...]), a2h_ref[...], a2l_ref[...]))
    a_ref[...] = a
    g_ref[...] = _bdot(jax.nn.sigmoid(_bdot(xg, g1_ref[...])), g2_ref[...])
    kk_ref[...] = k * k_k
    k_ref[...] = k * (1.0 + (a - 1.0) * k_a)


def _rwkv_proj(xn, prev, p, tm):
    t, d = xn.shape
    tok = pl.BlockSpec((tm, d), lambda i: (i, 0))
    weights = [p["mix"], p["vec"], p["wr"], p["wk"], p["wv"], *p["w1"], *p["w2"], *p["a1"], *p["a2"],
               p["g1"], p["g2"]]
    return pl.pallas_call(
        _rwkv_proj_kernel,
        grid=(t // tm,),
        in_specs=[tok, tok] + [_const_spec(w.shape) for w in weights],
        out_specs=[tok] * 7,
        out_shape=[jax.ShapeDtypeStruct((t, d), F32)] * 7,
        compiler_params=_params("parallel"),
        name="rwkv_proj",
    )(xn, prev, *weights)


def _rwkv_proj_fused(h, gain, shift0, p, tm, length):
    t, d = h.shape
    bsz = t // length
    tiles_per_seq = length // tm
    tok = pl.BlockSpec((tm, d), lambda i: (i, 0))
    before = pl.BlockSpec((SUBLANES, d), lambda i: (jnp.maximum(i * (tm // SUBLANES) - 1, 0), 0))
    per_seq = pl.BlockSpec((1, 1, d), lambda i: (i // tiles_per_seq, 0, 0))
    weights = [p["mix"], p["vec"], p["wr"], p["wk"], p["wv"], *p["w1"], *p["w2"], *p["a1"], *p["a2"],
               p["g1"], p["g2"]]
    kern = functools.partial(_rwkv_proj_fused_kernel, tiles_per_seq=tiles_per_seq)
    *proj, last = pl.pallas_call(
        kern,
        grid=(t // tm,),
        in_specs=[tok, before, per_seq, _const_spec((1, d))] + [_const_spec(w.shape) for w in weights],
        out_specs=[tok] * 7 + [per_seq],
        out_shape=[jax.ShapeDtypeStruct((t, d), F32)] * 7 + [jax.ShapeDtypeStruct((bsz, 1, d), F32)],
        compiler_params=_params("arbitrary"),
        name="rwkv_norm_proj",
    )(h, h, shift0.reshape(bsz, 1, d), gain.reshape(1, d), *weights)
    return proj, last.reshape(bsz, d)


def _rwkv_scan_kernel(r_ref, w_ref, k_ref, v_ref, kk_ref, a_ref, s0_ref, lnw_ref, lnb_ref, rk_ref,
                      o_ref, sfin_ref, state, alpha_blk, beta_blk, o_blk):
    tb = pl.program_id(1)
    steps = r_ref.shape[0]

    @pl.when(tb == 0)
    def _():
        state[...] = s0_ref[...]

    live = slice(0, RW_HEAD)

    kku = kk_ref[:, live, :]
    norm = jnp.sqrt(jnp.sum(kku * kku, axis=1, keepdims=True))
    kk = kku / jnp.maximum(norm, 1e-12)
    alpha_blk[...] = -kk
    beta_blk[...] = kk * a_ref[:, live, :]

    def step(t, carry):
        r = r_ref[t, live, :]
        w = w_ref[t, live, :]
        k = k_ref[t, live, :]
        alpha = alpha_blk[t]
        beta = beta_blk[t]

        def row(vi, c):
            s_v = state[vi]
            u = jnp.sum(s_v * alpha, axis=0, keepdims=True)
            s_new = s_v * w + u * beta + v_ref[t, pl.ds(vi, 1), :] * k
            state[vi] = s_new
            o_blk[t, pl.ds(vi, 1), :] = jnp.sum(s_new * r, axis=0, keepdims=True)
            return c

        lax.fori_loop(0, RW_HEAD, row, 0, unroll=32)
        return carry

    lax.fori_loop(0, steps, step, 0)

    o = o_blk[...]
    mu = jnp.mean(o, axis=1, keepdims=True)
    var = jnp.mean(jnp.square(o - mu), axis=1, keepdims=True)
    on = (o - mu) * lax.rsqrt(var + RW_LN_EPS)
    bonus = jnp.sum(r_ref[:, live, :] * k_ref[:, live, :] * rk_ref[...][None], axis=1, keepdims=True) * v_ref[:, live, :]
    o_ref[:, live, :] = on * lnw_ref[...][None] + lnb_ref[...][None] + bonus
    o_ref[:, RW_HEAD:, :] = jnp.zeros((steps, o_ref.shape[1] - RW_HEAD, LANES), F32)

    @pl.when(tb == pl.num_programs(1) - 1)
    def _():
        sfin_ref[...] = state[...]


def _rwkv_scan(seqs, s0, lnw, lnb, rk, tt):
    length, pitch, probs = seqs[0].shape
    n = RW_HEAD
    groups = probs // LANES
    seq_spec = pl.BlockSpec((tt, pitch, LANES), lambda g, i: (i, 0, g))
    st_spec = pl.BlockSpec((n, n, LANES), lambda g, i: (0, 0, g))
    lane_spec = pl.BlockSpec((n, LANES), lambda g, i: (0, 0))
    return pl.pallas_call(
        _rwkv_scan_kernel,
        grid=(groups, length // tt),
        in_specs=[seq_spec] * 6 + [st_spec] + [lane_spec] * 3,
        out_specs=[seq_spec, st_spec],
        out_shape=[jax.ShapeDtypeStruct((length, pitch, probs), F32),
                   jax.ShapeDtypeStruct((n, n, probs), F32)],
        scratch_shapes=[pltpu.VMEM((n, n, LANES), F32)] + [pltpu.VMEM((tt, n, LANES), F32)] * 3,
        compiler_params=_params("arbitrary", "arbitrary"),
        name="rwkv_scan",
    )(*seqs, s0, lnw, lnb, rk)


def _rwkv_out_kernel(o_ref, g_ref, h_ref, wo_ref, out_ref):
    out_ref[...] = h_ref[...] + _bdot(o_ref[...] * g_ref[...], wo_ref[...])


def _rwkv_out(o, g, h, wo, tm):
    t, d = h.shape
    tok = pl.BlockSpec((tm, d), lambda i: (i, 0))
    return pl.pallas_call(
        _rwkv_out_kernel,
        grid=(t // tm,),
        in_specs=[tok, tok, tok, _const_spec(wo.shape)],
        out_specs=tok,
        out_shape=jax.ShapeDtypeStruct((t, d), F32),
        compiler_params=_params("parallel"),
        name="rwkv_out",
    )(o, g, h, wo)


def _mamba_in_kernel(h_ref, g_ref, w_ref, z_ref, xbc_ref, dt_ref):
    xn = _rms(h_ref[...], g_ref[...]).astype(BF16)
    z_ref[...] = jnp.dot(xn, w_ref[:, :M_D_INNER], preferred_element_type=F32)
    xbc_ref[...] = jnp.dot(xn, w_ref[:, M_D_INNER:M_D_INNER + M_CONV_DIM], preferred_element_type=F32)
    dt_ref[...] = jnp.dot(xn, w_ref[:, M_D_INNER + M_CONV_DIM:], preferred_element_type=F32)


def _mamba_in(h, g, w_in, tm):
    t, d = h.shape
    return pl.pallas_call(
        _mamba_in_kernel,
        grid=(t // tm,),
        in_specs=[pl.BlockSpec((tm, d), lambda i: (i, 0)), _const_spec((1, d)), _const_spec(w_in.shape)],
        out_specs=[pl.BlockSpec((tm, M_D_INNER), lambda i: (i, 0)),
                   pl.BlockSpec((tm, M_CONV_DIM), lambda i: (i, 0)),
                   pl.BlockSpec((tm, M_DT_PAD), lambda i: (i, 0))],
        out_shape=[jax.ShapeDtypeStruct((t, M_D_INNER), F32),
                   jax.ShapeDtypeStruct((t, M_CONV_DIM), F32),
                   jax.ShapeDtypeStruct((t, M_DT_PAD), F32)],
        compiler_params=_params("parallel"),
        name="mamba_in_proj",
    )(h, g.reshape(1, d), w_in)


def _mamba_ssd_kernel(xbc_ref, z_ref, dt_ref, h_ref, conv0_ref, ssm0_ref, convw_ref, vec_ref,
                      dvec_ref, expand_ref, normw_ref, wout_ref,
                      out_ref, ssm_ref,
                      conv_buf, src_adt, src_acs, src_b, src_xdt, src_xdt_st, state, y_buf, *, seq_len, rows):
    b = pl.program_id(0)
    c = pl.program_id(1)
    q = M_CHUNK
    halo = SUBLANES
    d_bc = M_GROUPS * M_STATE

    @pl.when(c == 0)
    def _():
        state[...] = ssm0_ref[0]
        conv_buf[0:halo, :] = conv0_ref[0]

    @pl.when(c > 0)
    def _():
        conv_buf[0:halo, :] = conv_buf[rows:rows + halo, :]

    if rows < q:
        @pl.when((b == 0) & (c == 0))
        def _():
            src_adt[...] = jnp.zeros(src_adt.shape, F32)
            src_acs[...] = jnp.zeros(src_acs.shape, F32)
            src_b[...] = jnp.zeros(src_b.shape, BF16)
            src_xdt[...] = jnp.zeros(src_xdt.shape, BF16)
            src_xdt_st[...] = jnp.zeros(src_xdt_st.shape, BF16)

    conv_buf[halo:halo + rows, :] = xbc_ref[0]
    valid = (lax.broadcasted_iota(jnp.int32, (rows, 1), 0) + c * rows) < seq_len
    conv = vec_ref[0:1, :]
    for j in range(M_CONV):
        conv = conv + conv_buf[halo - (M_CONV - 1) + j:halo - (M_CONV - 1) + j + rows, :] * convw_ref[j:j + 1, :]
    act = jnp.where(valid, jax.nn.silu(conv), 0.0)
    xs = act[:, :M_D_INNER]
    dt = jnp.where(valid, _softplus(dt_ref[0] + dvec_ref[0:1, :]), 0.0)
    src_adt[0:rows, :] = dt * dvec_ref[1:2, :]
    li = lax.broadcasted_iota(jnp.int32, (rows, q), 0)
    si = lax.broadcasted_iota(jnp.int32, (rows, q), 1)
    causal = li >= si
    acs = _select_dot(causal.astype(BF16), src_adt[...])
    src_acs[0:rows, :] = acs
    acs_t = src_acs[...].T
    expand = expand_ref[...]
    dt_full = _dot_select(dt, expand)
    acs_full = _dot_select(acs, expand)
    last = acs[rows - 1:rows, :]
    last_full = acs_full[rows - 1:rows, :]
    xdt = xs * dt_full
    src_xdt[0:rows, :] = xdt.astype(BF16)
    src_xdt_st[0:rows, :] = (xdt * jnp.exp(last_full - acs_full)).astype(BF16)
    src_b[0:rows, :] = act[:, M_D_INNER:M_D_INNER + d_bc].astype(BF16)
    eacs_full = jnp.exp(acs_full)
    state_decay = jnp.exp(last)
    width = M_HPG * M_HEADDIM

    for g in range(M_GROUPS):
        cols = slice(g * width, (g + 1) * width)
        b_g = src_b[:, g * M_STATE:(g + 1) * M_STATE]
        c_g = act[:, M_D_INNER + d_bc + g * M_STATE:M_D_INNER + d_bc + (g + 1) * M_STATE].astype(BF16)
        cb = lax.dot_general(c_g, b_g, (((1,), (1,)), ((), ())), preferred_element_type=F32)
        h_g = state[g * M_HPG:(g + 1) * M_HPG].reshape(width, M_STATE)
        y_off = lax.dot_general(c_g, h_g.astype(BF16), (((1,), (1,)), ((), ())),
                                preferred_element_type=F32)
        upd = lax.dot_general(src_xdt_st[:, cols], b_g, (((0,), (0,)), ((), ())),
                              preferred_element_type=F32)
        ys = []
        for r in range(g * M_HPG, (g + 1) * M_HPG):
            lo = r * M_HEADDIM
            decay = jnp.where(causal, jnp.exp(acs[:, r:r + 1] - acs_t[r:r + 1, :]), 0.0)
            ys.append(jnp.dot((cb * decay).astype(BF16), src_xdt[:, lo:lo + M_HEADDIM],
                              preferred_element_type=F32))
            sub = slice((r - g * M_HPG) * M_HEADDIM, (r - g * M_HPG + 1) * M_HEADDIM)
            state[r] = state[r] * state_decay[:, r:r + 1] + upd[sub, :]
        y_buf[:, cols] = jnp.concatenate(ys, axis=1) + y_off * eacs_full[:, cols]

    y = y_buf[...] + vec_ref[1:2, :M_D_INNER] * xs
    yg = y * jax.nn.silu(z_ref[0])
    parts = []
    for g in range(M_GROUPS):
        part = yg[:, g * width:(g + 1) * width]
        parts.append(part * lax.rsqrt(jnp.mean(part * part, axis=-1, keepdims=True) + NORM_EPS))
    yn = jnp.concatenate(parts, axis=1) * normw_ref[...]
    out_ref[0] = h_ref[0] + _bdot(yn, wout_ref[...])

    @pl.when(c == pl.num_programs(1) - 1)
    def _():
        ssm_ref[0] = state[...]


def _mamba_ssd(xbc, z, dt, h, conv0, ssm0, p, rows, seq_len):
    bsz, length, _ = xbc.shape
    chunks = pl.cdiv(length, rows)
    weights = [p["conv_w"], p["vec"], p["dvec"], p["expand"], p["norm_w"], p["w_out"]]

    def seq_spec(width):
        return pl.BlockSpec((1, rows, width), lambda b, c: (b, c, 0))

    kern = functools.partial(_mamba_ssd_kernel, seq_len=seq_len, rows=rows)
    return pl.pallas_call(
        kern,
        grid=(bsz, chunks),
        in_specs=[seq_spec(M_CONV_DIM), seq_spec(M_D_INNER), seq_spec(M_DT_PAD), seq_spec(D_MODEL),
                  pl.BlockSpec((1, SUBLANES, M_CONV_DIM), lambda b, c: (b, 0, 0)),
                  pl.BlockSpec((1, M_HEADS, M_HEADDIM, M_STATE), lambda b, c: (b, 0, 0, 0))]
                 + [_const_spec(w.shape) for w in weights],
        out_specs=[seq_spec(D_MODEL),
                   pl.BlockSpec((1, M_HEADS, M_HEADDIM, M_STATE), lambda b, c: (b, 0, 0, 0))],
        out_shape=[jax.ShapeDtypeStruct((bsz, length, D_MODEL), F32),
                   jax.ShapeDtypeStruct((bsz, M_HEADS, M_HEADDIM, M_STATE), F32)],
        scratch_shapes=[pltpu.VMEM((rows + SUBLANES, M_CONV_DIM), F32),
                        pltpu.VMEM((M_CHUNK, M_DT_PAD), F32),
                        pltpu.VMEM((M_CHUNK, M_DT_PAD), F32),
                        pltpu.VMEM((M_CHUNK, M_GROUPS * M_STATE), BF16),
                        pltpu.VMEM((M_CHUNK, M_D_INNER), BF16),
                        pltpu.VMEM((M_CHUNK, M_D_INNER), BF16),
                        pltpu.VMEM((M_HEADS, M_HEADDIM, M_STATE), F32),
                        pltpu.VMEM((rows, M_D_INNER), F32)],
        compiler_params=_params("arbitrary", "arbitrary"),
        name="mamba_ssd",
    )(xbc, z, dt, h, conv0, ssm0, *weights)


_CANDS = [(ra, rb) for ra in range(P_TOPK) for rb in range(P_TOPK) if (ra + 1) * (rb + 1) <= P_TOPK]


def _peer_route_kernel(h_ref, g_ref, wq_ref, keys_ref, keys_lo_ref,
                       xt_ref, r1_ref, e1_ref, m0_ref, w0_ref,
                       q_buf, s_buf, code0, code_book, best_vals, cand, counts, inv_z, *, n_tokens):
    tm = h_ref.shape[0]
    neg_inf = float("-inf")
    h_tile = h_ref[...]
    if n_tokens % tm:
        def stand_in():
            row = lax.broadcasted_iota(jnp.int32, h_tile.shape, 0)
            col = lax.broadcasted_iota(jnp.int32, h_tile.shape, 1)
            filler = ((row * 7 + col * 13) & 63).astype(F32) * (1.0 / 64.0) - 0.5
            return jnp.where(row < n_tokens - pl.program_id(0) * tm, h_tile, filler)

        h_tile = lax.cond(pl.program_id(0) == pl.num_programs(0) - 1, stand_in, lambda: h_tile)
    xn = _rms(h_tile, g_ref[...])
    xt = xn.T.astype(BF16)
    xt_ref[...] = xt
    q_buf[...] = jnp.dot(wq_ref[...], xt, preferred_element_type=F32)
    key_iota = lax.broadcasted_iota(jnp.int32, (P_NKEYS, tm), 0).astype(F32)

    def half(m, carry):
        head = m // 2
        z = m % 2
        row0 = pl.multiple_of(m * P_NKEYS, P_NKEYS)
        qm = q_buf[pl.ds(row0, P_NKEYS), :]
        q_hi = qm.astype(BF16)
        q_lo = (qm - q_hi.astype(F32)).astype(BF16)
        k_hi = keys_ref[m]
        s = (jnp.dot(k_hi, q_hi, preferred_element_type=F32)
             + (jnp.dot(k_hi, q_lo, preferred_element_type=F32)
                + jnp.dot(keys_lo_ref[m], q_hi, preferred_element_type=F32)))

        s_buf[...] = s

        def next_distinct(j, prev):
            sj = s_buf[...]
            best = jnp.max(jnp.where(sj < prev, sj, neg_inf), axis=0, keepdims=True)
            best_vals[z, j, pl.ds(head, 1), :] = best
            return best

        def extract_ties(j, sc):
            cur, rank = sc
            best = jnp.max(cur, axis=0, keepdims=True)
            first = jnp.min(jnp.where(cur == best, key_iota, float(P_NKEYS)), axis=0, keepdims=True)
            sel = key_iota == first
            best_vals[z, j, pl.ds(head, 1), :] = best
            return jnp.where(sel, neg_inf, cur), jnp.where(sel, jnp.asarray(j, dtype=F32), rank)

        last = lax.fori_loop(0, P_TOPK, next_distinct, jnp.full((1, tm), float("inf"), F32))
        reached = jnp.sum(jnp.where(s >= last, 1.0, 0.0), axis=0, keepdims=True)
        has_ties = jnp.max(jnp.abs(reached - float(P_TOPK))) > 0.0

        def rank_by_count():
            rank = jnp.zeros((P_NKEYS, tm), F32)
            for jj in range(P_TOPK):
                rank = rank + jnp.where(best_vals[z, jj, pl.ds(head, 1), :] > s, 1.0, 0.0)
            return rank

        def rank_with_ties():
            no_rank = jnp.full((P_NKEYS, tm), float(P_NKEYS), F32)
            return lax.fori_loop(0, P_TOPK, extract_ties, (s, no_rank))[1]

        e = jnp.exp(s - jnp.max(s, axis=0, keepdims=True))

        @pl.when(z == 0)
        def _():
            w0_ref[head] = e

            @pl.when(has_ties)
            def _():
                code0[head] = rank_with_ties()
                for ra in range(P_TOPK):
                    code_book[ra, pl.ds(head, 1), :] = jnp.full((1, tm), float(ra), F32)

            @pl.when(jnp.logical_not(has_ties))
            def _():
                code0[head] = s
                for ra in range(P_TOPK):
                    code_book[ra, pl.ds(head, 1), :] = best_vals[0, ra, pl.ds(head, 1), :]

        @pl.when(z == 1)
        def _():
            rank = lax.cond(has_ties, rank_with_ties, rank_by_count)
            for lt in range(tm // LANES):
                r1_ref[head, lt] = rank[:, lt * LANES:(lt + 1) * LANES].astype(BF16)
                e1_ref[head, lt] = e[:, lt * LANES:(lt + 1) * LANES].astype(BF16)

        return carry

    lax.fori_loop(0, 2 * P_HEADS, half, 0)

    ex0 = [jnp.exp(best_vals[0, ra] - best_vals[0, 0]) for ra in range(P_TOPK)]
    ex1 = [jnp.exp(best_vals[1, rb] - best_vals[1, 0]) for rb in range(P_TOPK)]
    for i, (ra, rb) in enumerate(_CANDS):
        cand[i] = best_vals[0, ra] + best_vals[1, rb]

    def next_sum(it, prev):
        best = None
        for i in range(len(_CANDS)):
            ci = cand[i]
            below = jnp.where(ci < prev, ci, neg_inf)
            best = below if best is None else jnp.maximum(best, below)
        return best

    last_sum = lax.fori_loop(0, P_TOPK, next_sum, jnp.full((P_HEADS, tm), float("inf"), F32))
    hits = [jnp.where(cand[i] >= last_sum, 1.0, 0.0) for i in range(len(_CANDS))]
    n_hits = functools.reduce(lambda x, y: x + y, hits)
    sums_tie = jnp.max(jnp.abs(n_hits - float(P_TOPK))) > 0.0

    def select_distinct():
        z = jnp.zeros((P_HEADS, tm), F32)
        per_rank = [None] * P_TOPK
        for i, (ra, rb) in enumerate(_CANDS):
            per_rank[ra] = hits[i] if per_rank[ra] is None else per_rank[ra] + hits[i]
            z = z + hits[i] * (ex0[ra] * ex1[rb])
        for ra in range(P_TOPK):
            counts[ra] = per_rank[ra]
        return z

    def select_with_ties():
        counts[...] = jnp.zeros(counts.shape, F32)

        def pick(it, z):
            best = cand[0]
            for i in range(1, len(_CANDS)):
                best = jnp.maximum(best, cand[i])
            found = jnp.zeros((P_HEADS, tm), F32)
            for i, (ra, rb) in enumerate(_CANDS):
                ci = cand[i]
                hit = jnp.where(ci == best, 1.0, 0.0) * (1.0 - found)
                found = found + hit
                cand[i] = jnp.where(hit > 0.0, neg_inf, ci)
                counts[ra] = counts[ra] + hit
                z = z + hit * (ex0[ra] * ex1[rb])
            return z

        return lax.fori_loop(0, P_TOPK, pick, jnp.zeros((P_HEADS, tm), F32))

    inv_z[...] = 1.0 / lax.cond(sums_tie, select_with_ties, select_distinct)

    def finish(head, carry):
        code = code0[head]
        m0 = jnp.zeros((P_NKEYS, tm), F32)
        for ra in range(P_TOPK):
            m0 = jnp.where(code == code_book[ra, pl.ds(head, 1), :], counts[ra, pl.ds(head, 1), :], m0)
        m0_ref[head] = m0
        w0_ref[head] = w0_ref[head] * inv_z[pl.ds(head, 1), :]
        return carry

    lax.fori_loop(0, P_HEADS, finish, 0)


def _peer_route(h, g, wq_t, keys, keys_lo, tm):
    t, d = h.shape
    tiles = pl.cdiv(t, tm)
    t_pad = tiles * tm
    tile = pl.BlockSpec((P_HEADS, P_NKEYS, tm), lambda i: (0, 0, i))
    tile_f32 = jax.ShapeDtypeStruct((P_HEADS, P_NKEYS, t_pad), F32)
    tile_bf16 = jax.ShapeDtypeStruct((P_HEADS, t_pad // LANES, P_NKEYS, LANES), BF16)
    lane_tiles = pl.BlockSpec((P_HEADS, tm // LANES, P_NKEYS, LANES), lambda i: (0, i, 0, 0))
    return pl.pallas_call(
        functools.partial(_peer_route_kernel, n_tokens=t),
        grid=(tiles,),
        in_specs=[pl.BlockSpec((tm, d), lambda i: (i, 0)), _const_spec((1, d)),
                  _const_spec(wq_t.shape), _const_spec(keys.shape), _const_spec(keys_lo.shape)],
        out_specs=[pl.BlockSpec((d, tm), lambda i: (0, i)), lane_tiles, lane_tiles, tile, tile],
        out_shape=[jax.ShapeDtypeStruct((d, t_pad), BF16), tile_bf16, tile_bf16, tile_f32, tile_f32],
        scratch_shapes=[pltpu.VMEM((P_HEADS * P_QDIM, tm), F32),
                        pltpu.VMEM((P_NKEYS, tm), F32),
                        pltpu.VMEM((P_HEADS, P_NKEYS, tm), F32),
                        pltpu.VMEM((P_TOPK, P_HEADS, tm), F32),
                        pltpu.VMEM((2, P_TOPK, P_HEADS, tm), F32),
                        pltpu.VMEM((len(_CANDS), P_HEADS, tm), F32),
                        pltpu.VMEM((P_TOPK, P_HEADS, tm), F32),
                        pltpu.VMEM((P_HEADS, tm), F32)],
        compiler_params=_params("parallel"),
        name="peer_route",
    )(h, g.reshape(1, d), wq_t, keys, keys_lo)


def _gelu(x):
    return 0.5 * x * (1.0 + lax.erf(x * (1.0 / math.sqrt(2.0))))


def _peer_dense_kernel(xt_ref, u_ref, vt_ref, r1_ref, e1_ref, m0_ref, w0_ref, h_ref, gfin_ref,
                       out_ref, acc, hid, prob, *, final_norm):
    j = pl.program_id(1)
    keys_per_step = P_EXPERT_TILE // P_NKEYS

    @pl.when(j == 0)
    def _():
        acc[...] = jnp.zeros(acc.shape, F32)

    tm = xt_ref.shape[1]
    hid[...] = jnp.dot(u_ref[...], xt_ref[...], preferred_element_type=F32)
    zero = jnp.zeros((), BF16)
    for ii in range(keys_per_step):
        i1 = j * keys_per_step + ii
        rows = slice(ii * P_NKEYS, (ii + 1) * P_NKEYS)
        m0_rows = [m0_ref[head, pl.ds(i1, 1), :] for head in range(P_HEADS)]
        w0_rows = [w0_ref[head, pl.ds(i1, 1), :] for head in range(P_HEADS)]

        def packed_rows(row):
            one = jnp.broadcast_to(row, (2 * SUBLANES, LANES)).astype(BF16)
            return jnp.tile(one, (P_NKEYS // (2 * SUBLANES), 1))

        for lt in range(tm // LANES):
            lanes = slice(lt * LANES, (lt + 1) * LANES)
            gate = None
            for head in range(P_HEADS):
                m0 = packed_rows(m0_rows[head][:, lanes])
                w0 = packed_rows(w0_rows[head][:, lanes])
                term = w0 * jnp.where(r1_ref[head, lt] < m0, e1_ref[head, lt], zero)
                gate = term if gate is None else gate + term
            prob[rows, lanes] = gate * _gelu(hid[rows, lanes]).astype(BF16)
    acc[...] += jnp.dot(vt_ref[...], prob[...], preferred_element_type=F32)

    @pl.when(j == pl.num_programs(1) - 1)
    def _():
        res = h_ref[...] + acc[...].T
        if final_norm:
            res = _rms(res, gfin_ref[...])
        out_ref[...] = res


def _peer_dense(xt, u_bf, vt_bf, r1, e1, m0, w0, h, g_final, tm, final_norm):
    t, d = h.shape
    n_exp = u_bf.shape[0]
    tile = pl.BlockSpec((P_HEADS, P_NKEYS, tm), lambda i, j: (0, 0, i))
    lane_tiles = pl.BlockSpec((P_HEADS, tm // LANES, P_NKEYS, LANES), lambda i, j: (0, i, 0, 0))
    kern = functools.partial(_peer_dense_kernel, final_norm=final_norm)
    return pl.pallas_call(
        kern,
        grid=(pl.cdiv(t, tm), n_exp // P_EXPERT_TILE),
        in_specs=[pl.BlockSpec((d, tm), lambda i, j: (0, i)),
                  pl.BlockSpec((P_EXPERT_TILE, d), lambda i, j: (j, 0)),
                  pl.BlockSpec((d, P_EXPERT_TILE), lambda i, j: (0, j)),
                  lane_tiles, lane_tiles, tile, tile,
                  pl.BlockSpec((tm, d), lambda i, j: (i, 0)),
                  _const_spec((1, d))],
        out_specs=pl.BlockSpec((tm, d), lambda i, j: (i, 0)),
        out_shape=jax.ShapeDtypeStruct((t, d), F32),
        scratch_shapes=[pltpu.VMEM((d, tm), F32),
                        pltpu.VMEM((P_EXPERT_TILE, tm), F32),
                        pltpu.VMEM((P_EXPERT_TILE, tm), BF16)],
        compiler_params=_params("parallel", "arbitrary"),
        name="peer_dense",
    )(xt, u_bf, vt_bf, r1, e1, m0, w0, h, g_final.reshape(1, d))


def _pad_cols(w, n):
    return jnp.pad(w, ((0, 0), (0, n - w.shape[1])))


def _pad_rows(w, n):
    return jnp.pad(w, ((0, n - w.shape[0]), (0, 0)))


def _prep_rwkv(mix, w_rkv, w0, w1, w2, a0, a1, a2, g1, g2, k_k, k_a, r_k, ln_w, ln_b, w_o):
    d = D_MODEL
    lane = lambda x: jnp.repeat(x.reshape(RW_HEADS, RW_HEAD).T, LANES // RW_HEADS, axis=1)
    return {
        "mix": _pad_rows(mix, SUBLANES),
        "vec": _pad_rows(jnp.stack([w0, a0, k_k, k_a]), SUBLANES),
        "wr": w_rkv[0].astype(BF16), "wk": w_rkv[1].astype(BF16), "wv": w_rkv[2].astype(BF16),
        "w1": _split_bf16(_pad_cols(w1, LORA_PAD)), "w2": _split_bf16(_pad_rows(w2, LORA_PAD)),
        "a1": _split_bf16(_pad_cols(a1, LORA_PAD)), "a2": _split_bf16(_pad_rows(a2, LORA_PAD)),
        "g1": _pad_cols(g1, GATE_LORA_PAD).astype(BF16), "g2": _pad_rows(g2, GATE_LORA_PAD).astype(BF16),
        "lnw": lane(ln_w), "lnb": lane(ln_b), "rk": lane(r_k.reshape(d)),
        "wo": w_o.astype(BF16),
    }


def _prep_mamba(in_proj, conv_w, conv_b, dt_bias, a_log, d_skip, norm_w, out_proj):
    head_of_channel = jnp.arange(M_D_INNER) // M_HEADDIM
    expand = (jnp.arange(M_DT_PAD)[:, None] == head_of_channel[None, :]).astype(BF16)
    d_full = jnp.pad(jnp.repeat(d_skip, M_HEADDIM), (0, M_CONV_DIM - M_D_INNER))
    pad_h = lambda x: jnp.pad(x, (0, M_DT_PAD - M_HEADS))
    return {
        "w_in": _pad_cols(in_proj, M_IN_DIM - M_HEADS + M_DT_PAD).astype(BF16),
        "conv_w": _pad_rows(conv_w, SUBLANES),
        "vec": _pad_rows(jnp.stack([conv_b, d_full]), SUBLANES),
        "dvec": _pad_rows(jnp.stack([pad_h(dt_bias), pad_h(-jnp.exp(a_log))]), SUBLANES),
        "expand": expand,
        "norm_w": norm_w.reshape(1, M_D_INNER),
        "w_out": out_proj.astype(BF16),
    }


def _prep_peer(w_q, sub_keys, u_tab, v_tab):
    keys = jnp.transpose(sub_keys, (1, 0, 2, 3)).reshape(2 * P_HEADS, P_NKEYS, P_QDIM // 2)
    keys_hi, keys_lo = _split_bf16(keys)
    return {"wq_t": w_q.T.astype(BF16), "keys": keys_hi, "keys_lo": keys_lo,
            "u": u_tab.astype(BF16), "vt": v_tab.T.astype(BF16)}


SCAN_GROUP = LANES // RW_HEADS
RELAYOUT_STEPS = 128
RELAYOUT_PITCH = RW_HEAD + SUBLANES
SCAN_PITCH = RW_HEAD + SUBLANES


def _to_scan_kernel(x_ref, o_ref, stage):
    for b in range(SCAN_GROUP):
        for hp in range(RW_HEADS // 2):
            t = x_ref[b, :, hp * LANES:(hp + 1) * LANES].T
            for hl in range(2):
                lane = (2 * hp + hl) * SCAN_GROUP + b
                stage[lane * RELAYOUT_PITCH:lane * RELAYOUT_PITCH + RW_HEAD, :] = t[hl * RW_HEAD:(hl + 1) * RW_HEAD, :]
    for k in range(RW_HEAD):
        by_lane = stage[pl.ds(k, LANES, stride=RELAYOUT_PITCH), :]
        o_ref[pl.ds(k, RELAYOUT_STEPS, stride=SCAN_PITCH), :] = by_lane.T
    for k in range(RW_HEAD, SCAN_PITCH):
        o_ref[pl.ds(k, RELAYOUT_STEPS, stride=SCAN_PITCH), :] = jnp.zeros((RELAYOUT_STEPS, LANES), F32)


def _to_scan_pallas(x, length):
    x3 = x.reshape(SCAN_GROUP, length, D_MODEL)
    out = pl.pallas_call(
        _to_scan_kernel,
        grid=(pl.cdiv(length, RELAYOUT_STEPS),),
        in_specs=[pl.BlockSpec((SCAN_GROUP, RELAYOUT_STEPS, D_MODEL), lambda i: (0, i, 0))],
        out_specs=pl.BlockSpec((RELAYOUT_STEPS * SCAN_PITCH, LANES), lambda i: (i, 0)),
        out_shape=jax.ShapeDtypeStruct((length * SCAN_PITCH, LANES), F32),
        scratch_shapes=[pltpu.VMEM((LANES * RELAYOUT_PITCH, LANES), F32)],
        compiler_params=_params("parallel"),
        name="to_scan_layout",
    )(x3)
    return out.reshape(length, SCAN_PITCH, LANES)


def _rwkv_out_scan_kernel(o_ref, g_ref, h_ref, wo_ref, out_ref, stage):
    for v in range(RW_HEAD):
        by_step = o_ref[pl.ds(v, RELAYOUT_STEPS, stride=SCAN_PITCH), :]
        stage[pl.ds(v, LANES, stride=RELAYOUT_PITCH), :] = by_step.T
    for b in range(SCAN_GROUP):
        tiles = []
        for hp in range(RW_HEADS // 2):
            halves = []
            for hl in range(2):
                lane = (2 * hp + hl) * SCAN_GROUP + b
                halves.append(stage[lane * RELAYOUT_PITCH:lane * RELAYOUT_PITCH + RW_HEAD, :])
            tiles.append(jnp.concatenate(halves, axis=0).T)
        o_b = jnp.concatenate(tiles, axis=1)
        out_ref[b] = h_ref[b] + _bdot(o_b * g_ref[b], wo_ref[...])


def _rwkv_out_scan(o_scan, g, h, wo, length):
    d = D_MODEL
    seq = pl.BlockSpec((SCAN_GROUP, RELAYOUT_STEPS, d), lambda i: (0, i, 0))
    out = pl.pallas_call(
        _rwkv_out_scan_kernel,
        grid=(pl.cdiv(length, RELAYOUT_STEPS),),
        in_specs=[pl.BlockSpec((RELAYOUT_STEPS * SCAN_PITCH, LANES), lambda i: (i, 0)), seq, seq,
                  _const_spec(wo.shape)],
        out_specs=seq,
        out_shape=jax.ShapeDtypeStruct((SCAN_GROUP, length, d), F32),
        scratch_shapes=[pltpu.VMEM((LANES * RELAYOUT_PITCH, LANES), F32)],
        compiler_params=_params("parallel"),
        name="rwkv_out_from_scan",
    )(o_scan.reshape(length * SCAN_PITCH, LANES), g.reshape(SCAN_GROUP, length, d), h.reshape(SCAN_GROUP, length, d), wo)
    return out.reshape(SCAN_GROUP * length, d)


def _to_scan(x, bsz, length):
    if bsz == SCAN_GROUP and length >= RELAYOUT_STEPS:
        return _to_scan_pallas(x, length)
    x = x.reshape(bsz // SCAN_GROUP, SCAN_GROUP, length, RW_HEADS, RW_HEAD)
    x = jnp.transpose(x, (2, 4, 0, 3, 1)).reshape(length, RW_HEAD, bsz * RW_HEADS)
    return jnp.pad(x, ((0, 0), (0, SCAN_PITCH - RW_HEAD), (0, 0)))


def _from_scan(x, bsz, length):
    x = x[:, :RW_HEAD].reshape(length, RW_HEAD, bsz // SCAN_GROUP, RW_HEADS, SCAN_GROUP)
    return jnp.transpose(x, (2, 4, 0, 3, 1)).reshape(bsz * length, D_MODEL)


def _state_to_scan(wkv, bsz):
    x = wkv.reshape(bsz // SCAN_GROUP, SCAN_GROUP, RW_HEADS, RW_HEAD, RW_HEAD)
    return jnp.transpose(x, (3, 4, 0, 2, 1)).reshape(RW_HEAD, RW_HEAD, bsz * RW_HEADS)


def _state_from_scan(s, bsz):
    x = s.reshape(RW_HEAD, RW_HEAD, bsz // SCAN_GROUP, RW_HEADS, SCAN_GROUP)
    return jnp.transpose(x, (2, 4, 3, 0, 1)).reshape(bsz, RW_HEADS, RW_HEAD, RW_HEAD)


def _rwkv_layer(h, g_mix, shift0, wkv0, p, bsz, length, tm, tt):
    if length % tm == 0:
        (r, w, k, v, kk, a, g), shift = _rwkv_proj_fused(h, g_mix, shift0, p, tm, length)
    else:
        xn = _norm(h, g_mix, tm)
        xn3 = xn.reshape(bsz, length, D_MODEL)
        prev = jnp.concatenate([shift0[:, None, :], xn3[:, :-1]], axis=1).reshape(bsz * length, D_MODEL)
        r, w, k, v, kk, a, g = _rwkv_proj(xn, prev, p, tm)
        shift = xn3[:, -1]
    seqs = [_to_scan(x, bsz, length) for x in (r, w, k, v, kk, a)]
    o, s_fin = _rwkv_scan(seqs, _state_to_scan(wkv0, bsz), p["lnw"], p["lnb"], p["rk"], tt)
    if bsz == SCAN_GROUP and length >= RELAYOUT_STEPS:
        h_new = _rwkv_out_scan(o, g, h, p["wo"], length)
    else:
        h_new = _rwkv_out(_from_scan(o, bsz, length), g, h, p["wo"], tm)
    return h_new, shift, _state_from_scan(s_fin, bsz)


def _mamba_layer(h, g_mix, conv0, ssm0, p, bsz, length, tm, rows):
    z, xbc, dt = _mamba_in(h, g_mix, p["w_in"], tm)
    xbc3 = xbc.reshape(bsz, length, M_CONV_DIM)
    tail = M_CONV - 1
    conv_new = xbc3[:, length - tail:] if length >= tail else jnp.concatenate([conv0, xbc3], axis=1)[:, length:]
    pad_t = (-length) % SUBLANES if length < rows else 0

    def seq(x, c):
        x = x.reshape(bsz, length, c)
        return jnp.pad(x, ((0, 0), (0, pad_t), (0, 0))) if pad_t else x

    conv0_p = jnp.pad(conv0, ((0, 0), (SUBLANES - (M_CONV - 1), 0), (0, 0)))
    out, ssm = _mamba_ssd(seq(xbc, M_CONV_DIM), seq(z, M_D_INNER), seq(dt, M_DT_PAD), seq(h, D_MODEL),
                          conv0_p, ssm0, p, min(rows, length + pad_t), length)
    if pad_t:
        out = out[:, :length]
    return out.reshape(bsz * length, D_MODEL), conv_new, ssm


def _peer_layer(h, g_ffn, p, tm, g_final=None):
    xt, r1, e1, m0, w0 = _peer_route(h, g_ffn, p["wq_t"], p["keys"], p["keys_lo"], tm)
    final = g_final is not None
    return _peer_dense(xt, p["u"], p["vt"], r1, e1, m0, w0, h, g_final if final else g_ffn, tm, final)


TOKEN_TILE_LIMIT = 384
SCAN_BLOCK_LIMIT = 48
PEER_TOKEN_TILE = 512


def _largest_divisor(n, limit, multiple_of=1):
    return max(c for c in range(multiple_of, limit + 1, multiple_of) if n % c == 0)


def _tile_sizes(bsz, length):
    if length % SUBLANES == 0:
        tm = _largest_divisor(length, TOKEN_TILE_LIMIT, SUBLANES)
    else:
        tm = _largest_divisor(bsz * length, TOKEN_TILE_LIMIT, SUBLANES)
    return tm, _largest_divisor(length, SCAN_BLOCK_LIMIT)


def _trunk(h, shift0, wkv0, conv0, ssm0, norm_mix, norm_ffn, norm_final, rwkv_p, mamba_p, peer_p, bsz, length):
    tm, tt = _tile_sizes(bsz, length)
    h, shift, wkv = _rwkv_layer(h, norm_mix[0], shift0, wkv0, rwkv_p, bsz, length, tm, tt)
    h = _peer_layer(h, norm_ffn[0], peer_p[0], PEER_TOKEN_TILE)
    h, conv, ssm = _mamba_layer(h, norm_mix[1], conv0, ssm0, mamba_p, bsz, length, tm, M_CHUNK)
    y = _peer_layer(h, norm_ffn[1], peer_p[1], PEER_TOKEN_TILE, g_final=norm_final)
    return y.reshape(bsz, length, D_MODEL), shift[None], wkv[None], conv[None], ssm[None]


def kernel(x_prompt, x_sample, state_rwkv_shift, state_rwkv_wkv, state_mamba_conv, state_mamba_ssm, meta_tokens, norm_mix, norm_ffn, norm_final, rwkv_mix, rwkv_w_rkv, rwkv_w0, rwkv_w1, rwkv_w2, rwkv_a0, rwkv_a1, rwkv_a2, rwkv_g1, rwkv_g2, rwkv_k_k, rwkv_k_a, rwkv_r_k, rwkv_ln_w, rwkv_ln_b, rwkv_w_o, mamba_in_proj, mamba_conv_w, mamba_conv_b, mamba_dt_bias, mamba_a_log, mamba_d, mamba_norm_w, mamba_out_proj, peer_w_q, peer_sub_keys, peer_u, peer_v):
    rwkv_p = _prep_rwkv(rwkv_mix[0], rwkv_w_rkv[0], rwkv_w0[0], rwkv_w1[0], rwkv_w2[0], rwkv_a0[0],
                        rwkv_a1[0], rwkv_a2[0], rwkv_g1[0], rwkv_g2[0], rwkv_k_k[0], rwkv_k_a[0],
                        rwkv_r_k[0], rwkv_ln_w[0], rwkv_ln_b[0], rwkv_w_o[0])
    mamba_p = _prep_mamba(mamba_in_proj[0], mamba_conv_w[0], mamba_conv_b[0], mamba_dt_bias[0],
                          mamba_a_log[0], mamba_d[0], mamba_norm_w[0], mamba_out_proj[0])
    peer_p = [_prep_peer(peer_w_q[i], peer_sub_keys[i], peer_u[i], peer_v[i]) for i in range(2)]

    bp, lp = x_prompt.shape[0], x_prompt.shape[1] + N_META
    meta = jnp.broadcast_to(meta_tokens[None], (bp, N_META, D_MODEL))
    hp = jnp.concatenate([meta, x_prompt], axis=1).reshape(bp * lp, D_MODEL)
    zeros = lambda *s: jnp.zeros(s, F32)
    yp, p_shift, p_wkv, p_conv, p_ssm = _trunk(
        hp, zeros(bp, D_MODEL), zeros(bp, RW_HEADS, RW_HEAD, RW_HEAD), zeros(bp, M_CONV - 1, M_CONV_DIM),
        zeros(bp, M_HEADS, M_HEADDIM, M_STATE), norm_mix, norm_ffn, norm_final, rwkv_p, mamba_p, peer_p, bp, lp)

    bs, ls = x_sample.shape[0], x_sample.shape[1]
    ys, s_shift, s_wkv, s_conv, s_ssm = _trunk(
        x_sample.reshape(bs * ls, D_MODEL), state_rwkv_shift[0], state_rwkv_wkv[0], state_mamba_conv[0],
        state_mamba_ssm[0], norm_mix, norm_ffn, norm_final, rwkv_p, mamba_p, peer_p, bs, ls)
    return (yp[:, N_META:], ys, p_shift, p_wkv, p_conv, p_ssm, s_shift, s_wkv, s_conv, s_ssm)
```

```python
import functools
import math

import jax
import jax.numpy as jnp
from jax import lax
from jax.experimental import pallas as pl
from jax.experimental.pallas import tpu as pltpu

F32 = jnp.float32
BF16 = jnp.bfloat16

D_MODEL = 1024
N_META = 16
NORM_EPS = 1e-5
RW_HEAD = 64
RW_HEADS = D_MODEL // RW_HEAD
RW_LN_EPS = 64e-5
LORA_PAD = 128
GATE_LORA_PAD = 256
M_D_INNER = 2048
M_HEADDIM = 64
M_HEADS = M_D_INNER // M_HEADDIM
M_GROUPS = 4
M_HPG = M_HEADS // M_GROUPS
M_STATE = 128
M_CONV = 4
M_CONV_DIM = M_D_INNER + 2 * M_GROUPS * M_STATE
M_IN_DIM = 2 * M_D_INNER + 2 * M_GROUPS * M_STATE + M_HEADS
M_DT_PAD = 128
M_CHUNK = 128
P_HEADS = 8
P_NKEYS = 128
P_QDIM = 256
P_TOPK = 16
P_EXPERT_TILE = 2048

LANES = 128
SUBLANES = 8
VMEM_LIMIT_BYTES = 56 * 1024 * 1024


def _params(*semantics):
    return pltpu.CompilerParams(dimension_semantics=semantics, vmem_limit_bytes=VMEM_LIMIT_BYTES)


def _const_spec(shape):
    zeros = (0,) * len(shape)
    return pl.BlockSpec(shape, lambda *_: zeros)


def _rms(x, g):
    return x * lax.rsqrt(jnp.mean(x * x, axis=-1, keepdims=True) + NORM_EPS) * g


def _softplus(x):
    return jnp.maximum(x, 0.0) + jnp.log1p(jnp.exp(-jnp.abs(x)))


def _bdot(a, b):
    return jnp.dot(a.astype(BF16), b.astype(BF16), preferred_element_type=F32)


def _split_bf16(w):
    hi = w.astype(BF16)
    return hi, (w - hi.astype(F32)).astype(BF16)


def _split3_bf16(x):
    p0 = x.astype(BF16)
    r0 = x - p0.astype(F32)
    p1 = r0.astype(BF16)
    p2 = (r0 - p1.astype(F32)).astype(BF16)
    return p0, p1, p2


def _dot_select(a, sel):
    p0, p1, p2 = _split3_bf16(a)
    return (jnp.dot(p0, sel, preferred_element_type=F32)
            + (jnp.dot(p1, sel, preferred_element_type=F32) + jnp.dot(p2, sel, preferred_element_type=F32)))


def _select_dot(sel, b):
    p0, p1, p2 = _split3_bf16(b)
    return (jnp.dot(sel, p0, preferred_element_type=F32)
            + (jnp.dot(sel, p1, preferred_element_type=F32) + jnp.dot(sel, p2, preferred_element_type=F32)))


def _dot3(a, b_hi, b_lo):
    a_hi, a_lo = _split_bf16(a)
    return (jnp.dot(a_hi, b_hi, preferred_element_type=F32)
            + (jnp.dot(a_hi, b_lo, preferred_element_type=F32) + jnp.dot(a_lo, b_hi, preferred_element_type=F32)))


def _norm_kernel(h_ref, g_ref, o_ref):
    o_ref[...] = _rms(h_ref[...], g_ref[...])


def _norm(h, g, tm):
    t, d = h.shape
    return pl.pallas_call(
        _norm_kernel,
        grid=(t // tm,),
        in_specs=[pl.BlockSpec((tm, d), lambda i: (i, 0)), _const_spec((1, d))],
        out_specs=pl.BlockSpec((tm, d), lambda i: (i, 0)),
        out_shape=jax.ShapeDtypeStruct((t, d), F32),
        compiler_params=_params("parallel"),
        name="rms_norm",
    )(h, g.reshape(1, d))


def _rwkv_proj_kernel(xn_ref, prev_ref, *refs):
    _rwkv_proj_body(xn_ref[...], prev_ref[...], *refs)


def _rwkv_proj_fused_kernel(h_ref, before_ref, shift_ref, gain_ref, *refs, tiles_per_seq):
    *proj_refs, last_ref = refs
    i = pl.program_id(0)
    xn = _rms(h_ref[...], gain_ref[...])
    before = _rms(before_ref[SUBLANES - 1:SUBLANES, :], gain_ref[...])
    first = jnp.where(i % tiles_per_seq == 0, shift_ref[0], before)
    row = lax.broadcasted_iota(jnp.int32, xn.shape, 0)
    prev = jnp.where(row == 0, first, pltpu.roll(xn, 1, axis=0))
    _rwkv_proj_body(xn, prev, *proj_refs)

    @pl.when(i % tiles_per_seq == tiles_per_seq - 1)
    def _():
        last_ref[0] = xn[xn.shape[0] - 1:, :]


def _rwkv_proj_body(xn, prev, mix_ref, vec_ref, wr_ref, wk_ref, wv_ref,
                    w1h_ref, w1l_ref, w2h_ref, w2l_ref, a1h_ref, a1l_ref, a2h_ref, a2l_ref, g1_ref, g2_ref,
                    r_ref, w_ref, k_ref, v_ref, kk_ref, a_ref, g_ref):
    dx = prev - xn
    xr, xw, xk, xv, xa, xg = [xn + dx * mix_ref[j:j + 1, :] for j in range(6)]
    w0, a0, k_k, k_a = [vec_ref[j:j + 1, :] for j in range(4)]
    r_ref[...] = _bdot(xr, wr_ref[...])
    k = _bdot(xk, wk_ref[...])
    v_ref[...] = _bdot(xv, wv_ref[...])
    w_lora = _dot3(jnp.tanh(_dot3(xw, w1h_ref[...], w1l_ref[...])), w2h_ref[...], w2l_ref[...])
    w_log = -_softplus(-(w0 + w_lora)) - 0.5
    w_ref[...] = jnp.exp(-jnp.exp(w_log))
    a = jax.nn.sigmoid(a0 + _dot3(_dot3(xa, a1h_ref[...], a1l_ref[...]), a2h_ref[...], a2l_ref[...]))
    a_ref[...] = a
    g_ref[...] = _bdot(jax.nn.sigmoid(_bdot(xg, g1_ref[...])), g2_ref[...])
    kk_ref[...] = k * k_k
    k_ref[...] = k * (1.0 + (a - 1.0) * k_a)


def _rwkv_proj(xn, prev, p, tm):
    t, d = xn.shape
    tok = pl.BlockSpec((tm, d), lambda i: (i, 0))
    weights = [p["mix"], p["vec"], p["wr"], p["wk"], p["wv"], *p["w1"], *p["w2"], *p["a1"], *p["a2"],
               p["g1"], p["g2"]]
    return pl.pallas_call(
        _rwkv_proj_kernel,
        grid=(t // tm,),
        in_specs=[tok, tok] + [_const_spec(w.shape) for w in weights],
        out_specs=[tok] * 7,
        out_shape=[jax.ShapeDtypeStruct((t, d), F32)] * 7,
        compiler_params=_params("parallel"),
        name="rwkv_proj",
    )(xn, prev, *weights)


def _rwkv_proj_fused(h, gain, shift0, p, tm, length):
    t, d = h.shape
    bsz = t // length
    tiles_per_seq = length // tm
    tok = pl.BlockSpec((tm, d), lambda i: (i, 0))
    before = pl.BlockSpec((SUBLANES, d), lambda i: (jnp.maximum(i * (tm // SUBLANES) - 1, 0), 0))
    per_seq = pl.BlockSpec((1, 1, d), lambda i: (i // tiles_per_seq, 0, 0))
    weights = [p["mix"], p["vec"], p["wr"], p["wk"], p["wv"], *p["w1"], *p["w2"], *p["a1"], *p["a2"],
               p["g1"], p["g2"]]
    kern = functools.partial(_rwkv_proj_fused_kernel, tiles_per_seq=tiles_per_seq)
    *proj, last = pl.pallas_call(
        kern,
        grid=(t // tm,),
        in_specs=[tok, before, per_seq, _const_spec((1, d))] + [_const_spec(w.shape) for w in weights],
        out_specs=[tok] * 7 + [per_seq],
        out_shape=[jax.ShapeDtypeStruct((t, d), F32)] * 7 + [jax.ShapeDtypeStruct((bsz, 1, d), F32)],
        compiler_params=_params("arbitrary"),
        name="rwkv_norm_proj",
    )(h, h, shift0.reshape(bsz, 1, d), gain.reshape(1, d), *weights)
    return proj, last.reshape(bsz, d)


def _rwkv_scan_kernel(r_ref, w_ref, k_ref, v_ref, kk_ref, a_ref, s0_ref, lnw_ref, lnb_ref, rk_ref,
                      o_ref, sfin_ref, state, alpha_blk, beta_blk, o_blk):
    tb = pl.program_id(1)
    steps = r_ref.shape[0]

    @pl.when(tb == 0)
    def _():
        state[...] = s0_ref[...]

    live = slice(0, RW_HEAD)

    kku = kk_ref[:, live, :]
    norm = jnp.sqrt(jnp.sum(kku * kku, axis=1, keepdims=True))
    kk = kku / jnp.maximum(norm, 1e-12)
    alpha_blk[...] = -kk
    beta_blk[...] = kk * a_ref[:, live, :]

    def step(t, carry):
        r = r_ref[t, live, :]
        w = w_ref[t, live, :]
        k = k_ref[t, live, :]
        alpha = alpha_blk[t]
        beta = beta_blk[t]

        def row(vi, c):
            s_v = state[vi]
            u = jnp.sum(s_v * alpha, axis=0, keepdims=True)
            s_new = s_v * w + u * beta + v_ref[t, pl.ds(vi, 1), :] * k
            state[vi] = s_new
            o_blk[t, pl.ds(vi, 1), :] = jnp.sum(s_new * r, axis=0, keepdims=True)
            return c

        lax.fori_loop(0, RW_HEAD, row, 0, unroll=RW_HEAD)
        return carry

    lax.fori_loop(0, steps, step, 0)

    o = o_blk[...]
    mu = jnp.mean(o, axis=1, keepdims=True)
    var = jnp.mean(jnp.square(o - mu), axis=1, keepdims=True)
    on = (o - mu) * lax.rsqrt(var + RW_LN_EPS)
    bonus = jnp.sum(r_ref[:, live, :] * k_ref[:, live, :] * rk_ref[...][None], axis=1, keepdims=True) * v_ref[:, live, :]
    o_ref[:, live, :] = on * lnw_ref[...][None] + lnb_ref[...][None] + bonus
    o_ref[:, RW_HEAD:, :] = jnp.zeros((steps, o_ref.shape[1] - RW_HEAD, LANES), F32)

    @pl.when(tb == pl.num_programs(1) - 1)
    def _():
        sfin_ref[...] = state[...]


def _rwkv_scan(seqs, s0, lnw, lnb, rk, tt):
    length, pitch, probs = seqs[0].shape
    n = RW_HEAD
    groups = probs // LANES
    seq_spec = pl.BlockSpec((tt, pitch, LANES), lambda g, i: (i, 0, g))
    st_spec = pl.BlockSpec((n, n, LANES), lambda g, i: (0, 0, g))
    lane_spec = pl.BlockSpec((n, LANES), lambda g, i: (0, 0))
    return pl.pallas_call(
        _rwkv_scan_kernel,
        grid=(groups, length // tt),
        in_specs=[seq_spec] * 6 + [st_spec] + [lane_spec] * 3,
        out_specs=[seq_spec, st_spec],
        out_shape=[jax.ShapeDtypeStruct((length, pitch, probs), F32),
                   jax.ShapeDtypeStruct((n, n, probs), F32)],
        scratch_shapes=[pltpu.VMEM((n, n, LANES), F32)] + [pltpu.VMEM((tt, n, LANES), F32)] * 3,
        compiler_params=_params("arbitrary", "arbitrary"),
        name="rwkv_scan",
    )(*seqs, s0, lnw, lnb, rk)


def _rwkv_out_kernel(o_ref, g_ref, h_ref, wo_ref, out_ref):
    out_ref[...] = h_ref[...] + _bdot(o_ref[...] * g_ref[...], wo_ref[...])


def _rwkv_out(o, g, h, wo, tm):
    t, d = h.shape
    tok = pl.BlockSpec((tm, d), lambda i: (i, 0))
    return pl.pallas_call(
        _rwkv_out_kernel,
        grid=(t // tm,),
        in_specs=[tok, tok, tok, _const_spec(wo.shape)],
        out_specs=tok,
        out_shape=jax.ShapeDtypeStruct((t, d), F32),
        compiler_params=_params("parallel"),
        name="rwkv_out",
    )(o, g, h, wo)


def _mamba_in_kernel(h_ref, g_ref, w_ref, z_ref, xbc_ref, dt_ref):
    xn = _rms(h_ref[...], g_ref[...]).astype(BF16)
    z_ref[...] = jnp.dot(xn, w_ref[:, :M_D_INNER], preferred_element_type=F32)
    xbc_ref[...] = jnp.dot(xn, w_ref[:, M_D_INNER:M_D_INNER + M_CONV_DIM], preferred_element_type=F32)
    dt_ref[...] = jnp.dot(xn, w_ref[:, M_D_INNER + M_CONV_DIM:], preferred_element_type=F32)


def _mamba_in(h, g, w_in, tm):
    t, d = h.shape
    return pl.pallas_call(
        _mamba_in_kernel,
        grid=(t // tm,),
        in_specs=[pl.BlockSpec((tm, d), lambda i: (i, 0)), _const_spec((1, d)), _const_spec(w_in.shape)],
        out_specs=[pl.BlockSpec((tm, M_D_INNER), lambda i: (i, 0)),
                   pl.BlockSpec((tm, M_CONV_DIM), lambda i: (i, 0)),
                   pl.BlockSpec((tm, M_DT_PAD), lambda i: (i, 0))],
        out_shape=[jax.ShapeDtypeStruct((t, M_D_INNER), F32),
                   jax.ShapeDtypeStruct((t, M_CONV_DIM), F32),
                   jax.ShapeDtypeStruct((t, M_DT_PAD), F32)],
        compiler_params=_params("parallel"),
        name="mamba_in_proj",
    )(h, g.reshape(1, d), w_in)


def _mamba_ssd_kernel(xbc_ref, z_ref, dt_ref, h_ref, conv0_ref, ssm0_ref, convw_ref, vec_ref,
                      dvec_ref, expand_ref, normw_ref, wout_ref,
                      out_ref, ssm_ref,
                      conv_buf, src_adt, src_acs, src_b, src_xdt, src_xdt_st, state, y_buf, *, seq_len, rows):
    b = pl.program_id(0)
    c = pl.program_id(1)
    q = M_CHUNK
    halo = SUBLANES
    d_bc = M_GROUPS * M_STATE

    @pl.when(c == 0)
    def _():
        state[...] = ssm0_ref[0]
        conv_buf[0:halo, :] = conv0_ref[0]

    @pl.when(c > 0)
    def _():
        conv_buf[0:halo, :] = conv_buf[rows:rows + halo, :]

    if rows < q:
        @pl.when((b == 0) & (c == 0))
        def _():
            src_adt[...] = jnp.zeros(src_adt.shape, F32)
            src_acs[...] = jnp.zeros(src_acs.shape, F32)
            src_b[...] = jnp.zeros(src_b.shape, BF16)
            src_xdt[...] = jnp.zeros(src_xdt.shape, BF16)
            src_xdt_st[...] = jnp.zeros(src_xdt_st.shape, BF16)

    conv_buf[halo:halo + rows, :] = xbc_ref[0]
    valid = (lax.broadcasted_iota(jnp.int32, (rows, 1), 0) + c * rows) < seq_len
    conv = vec_ref[0:1, :]
    for j in range(M_CONV):
        conv = conv + conv_buf[halo - (M_CONV - 1) + j:halo - (M_CONV - 1) + j + rows, :] * convw_ref[j:j + 1, :]
    act = jnp.where(valid, jax.nn.silu(conv), 0.0)
    xs = act[:, :M_D_INNER]
    dt = jnp.where(valid, _softplus(dt_ref[0] + dvec_ref[0:1, :]), 0.0)
    src_adt[0:rows, :] = dt * dvec_ref[1:2, :]
    li = lax.broadcasted_iota(jnp.int32, (rows, q), 0)
    si = lax.broadcasted_iota(jnp.int32, (rows, q), 1)
    causal = li >= si
    acs = _select_dot(causal.astype(BF16), src_adt[...])
    src_acs[0:rows, :] = acs
    acs_t = src_acs[...].T
    expand = expand_ref[...]
    dt_full = _dot_select(dt, expand)
    acs_full = _dot_select(acs, expand)
    last = acs[rows - 1:rows, :]
    last_full = acs_full[rows - 1:rows, :]
    xdt = xs * dt_full
    src_xdt[0:rows, :] = xdt.astype(BF16)
    src_xdt_st[0:rows, :] = (xdt * jnp.exp(last_full - acs_full)).astype(BF16)
    src_b[0:rows, :] = act[:, M_D_INNER:M_D_INNER + d_bc].astype(BF16)
    eacs_full = jnp.exp(acs_full)
    state_decay = jnp.exp(last)
    width = M_HPG * M_HEADDIM

    for g in range(M_GROUPS):
        cols = slice(g * width, (g + 1) * width)
        b_g = src_b[:, g * M_STATE:(g + 1) * M_STATE]
        c_g = act[:, M_D_INNER + d_bc + g * M_STATE:M_D_INNER + d_bc + (g + 1) * M_STATE].astype(BF16)
        cb = lax.dot_general(c_g, b_g, (((1,), (1,)), ((), ())), preferred_element_type=F32)
        h_g = state[g * M_HPG:(g + 1) * M_HPG].reshape(width, M_STATE)
        y_off = lax.dot_general(c_g, h_g.astype(BF16), (((1,), (1,)), ((), ())),
                                preferred_element_type=F32)
        upd = lax.dot_general(src_xdt_st[:, cols], b_g, (((0,), (0,)), ((), ())),
                              preferred_element_type=F32)
        ys = []
        for r in range(g * M_HPG, (g + 1) * M_HPG):
            lo = r * M_HEADDIM
            decay = jnp.where(causal, jnp.exp(acs[:, r:r + 1] - acs_t[r:r + 1, :]), 0.0)
            ys.append(jnp.dot((cb * decay).astype(BF16), src_xdt[:, lo:lo + M_HEADDIM],
                              preferred_element_type=F32))
            sub = slice((r - g * M_HPG) * M_HEADDIM, (r - g * M_HPG + 1) * M_HEADDIM)
            state[r] = state[r] * state_decay[:, r:r + 1] + upd[sub, :]
        y_buf[:, cols] = jnp.concatenate(ys, axis=1) + y_off * eacs_full[:, cols]

    y = y_buf[...] + vec_ref[1:2, :M_D_INNER] * xs
    yg = y * jax.nn.silu(z_ref[0])
    parts = []
    for g in range(M_GROUPS):
        part = yg[:, g * width:(g + 1) * width]
        parts.append(part * lax.rsqrt(jnp.mean(part * part, axis=-1, keepdims=True) + NORM_EPS))
    yn = jnp.concatenate(parts, axis=1) * normw_ref[...]
    out_ref[0] = h_ref[0] + _bdot(yn, wout_ref[...])

    @pl.when(c == pl.num_programs(1) - 1)
    def _():
        ssm_ref[0] = state[...]


def _mamba_ssd(xbc, z, dt, h, conv0, ssm0, p, rows, seq_len):
    bsz, length, _ = xbc.shape
    chunks = pl.cdiv(length, rows)
    weights = [p["conv_w"], p["vec"], p["dvec"], p["expand"], p["norm_w"], p["w_out"]]

    def seq_spec(width):
        return pl.BlockSpec((1, rows, width), lambda b, c: (b, c, 0))

    kern = functools.partial(_mamba_ssd_kernel, seq_len=seq_len, rows=rows)
    return pl.pallas_call(
        kern,
        grid=(bsz, chunks),
        in_specs=[seq_spec(M_CONV_DIM), seq_spec(M_D_INNER), seq_spec(M_DT_PAD), seq_spec(D_MODEL),
                  pl.BlockSpec((1, SUBLANES, M_CONV_DIM), lambda b, c: (b, 0, 0)),
                  pl.BlockSpec((1, M_HEADS, M_HEADDIM, M_STATE), lambda b, c: (b, 0, 0, 0))]
                 + [_const_spec(w.shape) for w in weights],
        out_specs=[seq_spec(D_MODEL),
                   pl.BlockSpec((1, M_HEADS, M_HEADDIM, M_STATE), lambda b, c: (b, 0, 0, 0))],
        out_shape=[jax.ShapeDtypeStruct((bsz, length, D_MODEL), F32),
                   jax.ShapeDtypeStruct((bsz, M_HEADS, M_HEADDIM, M_STATE), F32)],
        scratch_shapes=[pltpu.VMEM((rows + SUBLANES, M_CONV_DIM), F32),
                        pltpu.VMEM((M_CHUNK, M_DT_PAD), F32),
                        pltpu.VMEM((M_CHUNK, M_DT_PAD), F32),
                        pltpu.VMEM((M_CHUNK, M_GROUPS * M_STATE), BF16),
                        pltpu.VMEM((M_CHUNK, M_D_INNER), BF16),
                        pltpu.VMEM((M_CHUNK, M_D_INNER), BF16),
                        pltpu.VMEM((M_HEADS, M_HEADDIM, M_STATE), F32),
                        pltpu.VMEM((rows, M_D_INNER), F32)],
        compiler_params=_params("arbitrary", "arbitrary"),
        name="mamba_ssd",
    )(xbc, z, dt, h, conv0, ssm0, *weights)


_CANDS = [(ra, rb) for ra in range(P_TOPK) for rb in range(P_TOPK) if (ra + 1) * (rb + 1) <= P_TOPK]


def _peer_route_kernel(h_ref, g_ref, wq_ref, keys_ref, keys_lo_ref,
                       xt_ref, r1_ref, e1_ref, m0_ref, w0_ref,
                       q_buf, s_buf, code0, code_book, best_vals, cand, counts, inv_z, *, n_tokens):
    tm = h_ref.shape[0]
    neg_inf = float("-inf")
    h_tile = h_ref[...]
    if n_tokens % tm:
        def stand_in():
            row = lax.broadcasted_iota(jnp.int32, h_tile.shape, 0)
            col = lax.broadcasted_iota(jnp.int32, h_tile.shape, 1)
            filler = ((row * 7 + col * 13) & 63).astype(F32) * (1.0 / 64.0) - 0.5
            return jnp.where(row < n_tokens - pl.program_id(0) * tm, h_tile, filler)

        h_tile = lax.cond(pl.program_id(0) == pl.num_programs(0) - 1, stand_in, lambda: h_tile)
    xn = _rms(h_tile, g_ref[...])
    xt = xn.T.astype(BF16)
    xt_ref[...] = xt
    q_buf[...] = jnp.dot(wq_ref[...], xt, preferred_element_type=F32)
    key_iota = lax.broadcasted_iota(jnp.int32, (P_NKEYS, tm), 0).astype(F32)

    def half(m, carry):
        head = m // 2
        z = m % 2
        row0 = pl.multiple_of(m * P_NKEYS, P_NKEYS)
        qm = q_buf[pl.ds(row0, P_NKEYS), :]
        q_hi = qm.astype(BF16)
        q_lo = (qm - q_hi.astype(F32)).astype(BF16)
        k_hi = keys_ref[m]
        s = (jnp.dot(k_hi, q_hi, preferred_element_type=F32)
             + (jnp.dot(k_hi, q_lo, preferred_element_type=F32)
                + jnp.dot(keys_lo_ref[m], q_hi, preferred_element_type=F32)))

        s_buf[...] = s

        def next_distinct(j, prev):
            sj = s_buf[...]
            best = jnp.max(jnp.where(sj < prev, sj, neg_inf), axis=0, keepdims=True)
            best_vals[z, j, pl.ds(head, 1), :] = best
            return best

        def extract_ties(j, sc):
            cur, rank = sc
            best = jnp.max(cur, axis=0, keepdims=True)
            first = jnp.min(jnp.where(cur == best, key_iota, float(P_NKEYS)), axis=0, keepdims=True)
            sel = key_iota == first
            best_vals[z, j, pl.ds(head, 1), :] = best
            return jnp.where(sel, neg_inf, cur), jnp.where(sel, jnp.asarray(j, dtype=F32), rank)

        last = lax.fori_loop(0, P_TOPK, next_distinct, jnp.full((1, tm), float("inf"), F32))
        reached = jnp.sum(jnp.where(s >= last, 1.0, 0.0), axis=0, keepdims=True)
        has_ties = jnp.max(jnp.abs(reached - float(P_TOPK))) > 0.0

        def rank_by_count():
            rank = jnp.zeros((P_NKEYS, tm), F32)
            for jj in range(P_TOPK):
                rank = rank + jnp.where(best_vals[z, jj, pl.ds(head, 1), :] > s, 1.0, 0.0)
            return rank

        def rank_with_ties():
            no_rank = jnp.full((P_NKEYS, tm), float(P_NKEYS), F32)
            return lax.fori_loop(0, P_TOPK, extract_ties, (s, no_rank))[1]

        e = jnp.exp(s - jnp.max(s, axis=0, keepdims=True))

        @pl.when(z == 0)
        def _():
            w0_ref[head] = e

            @pl.when(has_ties)
            def _():
                code0[head] = rank_with_ties()
                for ra in range(P_TOPK):
                    code_book[ra, pl.ds(head, 1), :] = jnp.full((1, tm), float(ra), F32)

            @pl.when(jnp.logical_not(has_ties))
            def _():
                code0[head] = s
                for ra in range(P_TOPK):
                    code_book[ra, pl.ds(head, 1), :] = best_vals[0, ra, pl.ds(head, 1), :]

        @pl.when(z == 1)
        def _():
            rank = lax.cond(has_ties, rank_with_ties, rank_by_count)
            for lt in range(tm // LANES):
                r1_ref[head, lt] = rank[:, lt * LANES:(lt + 1) * LANES].astype(BF16)
                e1_ref[head, lt] = e[:, lt * LANES:(lt + 1) * LANES].astype(BF16)

        return carry

    lax.fori_loop(0, 2 * P_HEADS, half, 0)

    ex0 = [jnp.exp(best_vals[0, ra] - best_vals[0, 0]) for ra in range(P_TOPK)]
    ex1 = [jnp.exp(best_vals[1, rb] - best_vals[1, 0]) for rb in range(P_TOPK)]
    for i, (ra, rb) in enumerate(_CANDS):
        cand[i] = best_vals[0, ra] + best_vals[1, rb]

    def next_sum(it, prev):
        best = None
        for i in range(len(_CANDS)):
            ci = cand[i]
            below = jnp.where(ci < prev, ci, neg_inf)
            best = below if best is None else jnp.maximum(best, below)
        return best

    last_sum = lax.fori_loop(0, P_TOPK, next_sum, jnp.full((P_HEADS, tm), float("inf"), F32))
    hits = [jnp.where(cand[i] >= last_sum, 1.0, 0.0) for i in range(len(_CANDS))]
    n_hits = functools.reduce(lambda x, y: x + y, hits)
    sums_tie = jnp.max(jnp.abs(n_hits - float(P_TOPK))) > 0.0

    def select_distinct():
        z = jnp.zeros((P_HEADS, tm), F32)
        per_rank = [None] * P_TOPK
        for i, (ra, rb) in enumerate(_CANDS):
            per_rank[ra] = hits[i] if per_rank[ra] is None else per_rank[ra] + hits[i]
            z = z + hits[i] * (ex0[ra] * ex1[rb])
        for ra in range(P_TOPK):
            counts[ra] = per_rank[ra]
        return z

    def select_with_ties():
        counts[...] = jnp.zeros(counts.shape, F32)

        def pick(it, z):
            best = cand[0]
            for i in range(1, len(_CANDS)):
                best = jnp.maximum(best, cand[i])
            found = jnp.zeros((P_HEADS, tm), F32)
            for i, (ra, rb) in enumerate(_CANDS):
                ci = cand[i]
                hit = jnp.where(ci == best, 1.0, 0.0) * (1.0 - found)
                found = found + hit
                cand[i] = jnp.where(hit > 0.0, neg_inf, ci)
                counts[ra] = counts[ra] + hit
                z = z + hit * (ex0[ra] * ex1[rb])
            return z

        return lax.fori_loop(0, P_TOPK, pick, jnp.zeros((P_HEADS, tm), F32))

    inv_z[...] = 1.0 / lax.cond(sums_tie, select_with_ties, select_distinct)

    def finish(head, carry):
        code = code0[head]
        m0 = jnp.zeros((P_NKEYS, tm), F32)
        for ra in range(P_TOPK):
            m0 = jnp.where(code == code_book[ra, pl.ds(head, 1), :], counts[ra, pl.ds(head, 1), :], m0)
        m0_ref[head] = m0
        w0_ref[head] = w0_ref[head] * inv_z[pl.ds(head, 1), :]
        return carry

    lax.fori_loop(0, P_HEADS, finish, 0)


def _peer_route(h, g, wq_t, keys, keys_lo, tm):
    t, d = h.shape
    tiles = pl.cdiv(t, tm)
    t_pad = tiles * tm
    tile = pl.BlockSpec((P_HEADS, P_NKEYS, tm), lambda i: (0, 0, i))
    tile_f32 = jax.ShapeDtypeStruct((P_HEADS, P_NKEYS, t_pad), F32)
    tile_bf16 = jax.ShapeDtypeStruct((P_HEADS, t_pad // LANES, P_NKEYS, LANES), BF16)
    lane_tiles = pl.BlockSpec((P_HEADS, tm // LANES, P_NKEYS, LANES), lambda i: (0, i, 0, 0))
    return pl.pallas_call(
        functools.partial(_peer_route_kernel, n_tokens=t),
        grid=(tiles,),
        in_specs=[pl.BlockSpec((tm, d), lambda i: (i, 0)), _const_spec((1, d)),
                  _const_spec(wq_t.shape), _const_spec(keys.shape), _const_spec(keys_lo.shape)],
        out_specs=[pl.BlockSpec((d, tm), lambda i: (0, i)), lane_tiles, lane_tiles, tile, tile],
        out_shape=[jax.ShapeDtypeStruct((d, t_pad), BF16), tile_bf16, tile_bf16, tile_f32, tile_f32],
        scratch_shapes=[pltpu.VMEM((P_HEADS * P_QDIM, tm), F32),
                        pltpu.VMEM((P_NKEYS, tm), F32),
                        pltpu.VMEM((P_HEADS, P_NKEYS, tm), F32),
                        pltpu.VMEM((P_TOPK, P_HEADS, tm), F32),
                        pltpu.VMEM((2, P_TOPK, P_HEADS, tm), F32),
                        pltpu.VMEM((len(_CANDS), P_HEADS, tm), F32),
                        pltpu.VMEM((P_TOPK, P_HEADS, tm), F32),
                        pltpu.VMEM((P_HEADS, tm), F32)],
        compiler_params=_params("parallel"),
        name="peer_route",
    )(h, g.reshape(1, d), wq_t, keys, keys_lo)


def _gelu(x):
    return 0.5 * x * (1.0 + lax.erf(x * (1.0 / math.sqrt(2.0))))


def _peer_dense_kernel(xt_ref, u_ref, vt_ref, r1_ref, e1_ref, m0_ref, w0_ref, h_ref, gfin_ref,
                       out_ref, acc, hid, prob, *, final_norm):
    j = pl.program_id(1)
    keys_per_step = P_EXPERT_TILE // P_NKEYS

    @pl.when(j == 0)
    def _():
        acc[...] = jnp.zeros(acc.shape, F32)

    tm = xt_ref.shape[1]
    hid[...] = jnp.dot(u_ref[...], xt_ref[...], preferred_element_type=F32)
    zero = jnp.zeros((), BF16)
    for ii in range(keys_per_step):
        i1 = j * keys_per_step + ii
        rows = slice(ii * P_NKEYS, (ii + 1) * P_NKEYS)
        m0_rows = [m0_ref[head, pl.ds(i1, 1), :] for head in range(P_HEADS)]
        w0_rows = [w0_ref[head, pl.ds(i1, 1), :] for head in range(P_HEADS)]

        def packed_rows(row):
            one = jnp.broadcast_to(row, (2 * SUBLANES, LANES)).astype(BF16)
            return jnp.tile(one, (P_NKEYS // (2 * SUBLANES), 1))

        for lt in range(tm // LANES):
            lanes = slice(lt * LANES, (lt + 1) * LANES)
            gate = None
            for head in range(P_HEADS):
                m0 = packed_rows(m0_rows[head][:, lanes])
                w0 = packed_rows(w0_rows[head][:, lanes])
                term = w0 * jnp.where(r1_ref[head, lt] < m0, e1_ref[head, lt], zero)
                gate = term if gate is None else gate + term
            prob[rows, lanes] = gate * _gelu(hid[rows, lanes]).astype(BF16)
    acc[...] += jnp.dot(vt_ref[...], prob[...], preferred_element_type=F32)

    @pl.when(j == pl.num_programs(1) - 1)
    def _():
        res = h_ref[...] + acc[...].T
        if final_norm:
            res = _rms(res, gfin_ref[...])
        out_ref[...] = res


def _peer_dense(xt, u_bf, vt_bf, r1, e1, m0, w0, h, g_final, tm, final_norm):
    t, d = h.shape
    n_exp = u_bf.shape[0]
    tile = pl.BlockSpec((P_HEADS, P_NKEYS, tm), lambda i, j: (0, 0, i))
    lane_tiles = pl.BlockSpec((P_HEADS, tm // LANES, P_NKEYS, LANES), lambda i, j: (0, i, 0, 0))
    kern = functools.partial(_peer_dense_kernel, final_norm=final_norm)
    return pl.pallas_call(
        kern,
        grid=(pl.cdiv(t, tm), n_exp // P_EXPERT_TILE),
        in_specs=[pl.BlockSpec((d, tm), lambda i, j: (0, i)),
                  pl.BlockSpec((P_EXPERT_TILE, d), lambda i, j: (j, 0)),
                  pl.BlockSpec((d, P_EXPERT_TILE), lambda i, j: (0, j)),
                  lane_tiles, lane_tiles, tile, tile,
                  pl.BlockSpec((tm, d), lambda i, j: (i, 0)),
                  _const_spec((1, d))],
        out_specs=pl.BlockSpec((tm, d), lambda i, j: (i, 0)),
        out_shape=jax.ShapeDtypeStruct((t, d), F32),
        scratch_shapes=[pltpu.VMEM((d, tm), F32),
                        pltpu.VMEM((P_EXPERT_TILE, tm), F32),
                        pltpu.VMEM((P_EXPERT_TILE, tm), BF16)],
        compiler_params=_params("parallel", "arbitrary"),
        name="peer_dense",
    )(xt, u_bf, vt_bf, r1, e1, m0, w0, h, g_final.reshape(1, d))


def _pad_cols(w, n):
    return jnp.pad(w, ((0, 0), (0, n - w.shape[1])))


def _pad_rows(w, n):
    return jnp.pad(w, ((0, n - w.shape[0]), (0, 0)))


def _prep_rwkv(mix, w_rkv, w0, w1, w2, a0, a1, a2, g1, g2, k_k, k_a, r_k, ln_w, ln_b, w_o):
    d = D_MODEL
    lane = lambda x: jnp.repeat(x.reshape(RW_HEADS, RW_HEAD).T, LANES // RW_HEADS, axis=1)
    return {
        "mix": _pad_rows(mix, SUBLANES),
        "vec": _pad_rows(jnp.stack([w0, a0, k_k, k_a]), SUBLANES),
        "wr": w_rkv[0].astype(BF16), "wk": w_rkv[1].astype(BF16), "wv": w_rkv[2].astype(BF16),
        "w1": _split_bf16(_pad_cols(w1, LORA_PAD)), "w2": _split_bf16(_pad_rows(w2, LORA_PAD)),
        "a1": _split_bf16(_pad_cols(a1, LORA_PAD)), "a2": _split_bf16(_pad_rows(a2, LORA_PAD)),
        "g1": _pad_cols(g1, GATE_LORA_PAD).astype(BF16), "g2": _pad_rows(g2, GATE_LORA_PAD).astype(BF16),
        "lnw": lane(ln_w), "lnb": lane(ln_b), "rk": lane(r_k.reshape(d)),
        "wo": w_o.astype(BF16),
    }


def _prep_mamba(in_proj, conv_w, conv_b, dt_bias, a_log, d_skip, norm_w, out_proj):
    head_of_channel = jnp.arange(M_D_INNER) // M_HEADDIM
    expand = (jnp.arange(M_DT_PAD)[:, None] == head_of_channel[None, :]).astype(BF16)
    d_full = jnp.pad(jnp.repeat(d_skip, M_HEADDIM), (0, M_CONV_DIM - M_D_INNER))
    pad_h = lambda x: jnp.pad(x, (0, M_DT_PAD - M_HEADS))
    return {
        "w_in": _pad_cols(in_proj, M_IN_DIM - M_HEADS + M_DT_PAD).astype(BF16),
        "conv_w": _pad_rows(conv_w, SUBLANES),
        "vec": _pad_rows(jnp.stack([conv_b, d_full]), SUBLANES),
        "dvec": _pad_rows(jnp.stack([pad_h(dt_bias), pad_h(-jnp.exp(a_log))]), SUBLANES),
        "expand": expand,
        "norm_w": norm_w.reshape(1, M_D_INNER),
        "w_out": out_proj.astype(BF16),
    }


def _prep_peer(w_q, sub_keys, u_tab, v_tab):
    keys = jnp.transpose(sub_keys, (1, 0, 2, 3)).reshape(2 * P_HEADS, P_NKEYS, P_QDIM // 2)
    keys_hi, keys_lo = _split_bf16(keys)
    return {"wq_t": w_q.T.astype(BF16), "keys": keys_hi, "keys_lo": keys_lo,
            "u": u_tab.astype(BF16), "vt": v_tab.T.astype(BF16)}


SCAN_GROUP = LANES // RW_HEADS
RELAYOUT_STEPS = 128
RELAYOUT_PITCH = RW_HEAD + SUBLANES
SCAN_PITCH = RW_HEAD + SUBLANES


def _to_scan_kernel(x_ref, o_ref, stage):
    for b in range(SCAN_GROUP):
        for hp in range(RW_HEADS // 2):
            t = x_ref[b, :, hp * LANES:(hp + 1) * LANES].T
            for hl in range(2):
                lane = (2 * hp + hl) * SCAN_GROUP + b
                stage[lane * RELAYOUT_PITCH:lane * RELAYOUT_PITCH + RW_HEAD, :] = t[hl * RW_HEAD:(hl + 1) * RW_HEAD, :]
    for k in range(RW_HEAD):
        by_lane = stage[pl.ds(k, LANES, stride=RELAYOUT_PITCH), :]
        o_ref[pl.ds(k, RELAYOUT_STEPS, stride=SCAN_PITCH), :] = by_lane.T
    for k in range(RW_HEAD, SCAN_PITCH):
        o_ref[pl.ds(k, RELAYOUT_STEPS, stride=SCAN_PITCH), :] = jnp.zeros((RELAYOUT_STEPS, LANES), F32)


def _to_scan_pallas(x, length):
    x3 = x.reshape(SCAN_GROUP, length, D_MODEL)
    out = pl.pallas_call(
        _to_scan_kernel,
        grid=(pl.cdiv(length, RELAYOUT_STEPS),),
        in_specs=[pl.BlockSpec((SCAN_GROUP, RELAYOUT_STEPS, D_MODEL), lambda i: (0, i, 0))],
        out_specs=pl.BlockSpec((RELAYOUT_STEPS * SCAN_PITCH, LANES), lambda i: (i, 0)),
        out_shape=jax.ShapeDtypeStruct((length * SCAN_PITCH, LANES), F32),
        scratch_shapes=[pltpu.VMEM((LANES * RELAYOUT_PITCH, LANES), F32)],
        compiler_params=_params("parallel"),
        name="to_scan_layout",
    )(x3)
    return out.reshape(length, SCAN_PITCH, LANES)


def _rwkv_out_scan_kernel(o_ref, g_ref, h_ref, wo_ref, out_ref, stage):
    for v in range(RW_HEAD):
        by_step = o_ref[pl.ds(v, RELAYOUT_STEPS, stride=SCAN_PITCH), :]
        stage[pl.ds(v, LANES, stride=RELAYOUT_PITCH), :] = by_step.T
    for b in range(SCAN_GROUP):
        tiles = []
        for hp in range(RW_HEADS // 2):
            halves = []
            for hl in range(2):
                lane = (2 * hp + hl) * SCAN_GROUP + b
                halves.append(stage[lane * RELAYOUT_PITCH:lane * RELAYOUT_PITCH + RW_HEAD, :])
            tiles.append(jnp.concatenate(halves, axis=0).T)
        o_b = jnp.concatenate(tiles, axis=1)
        out_ref[b] = h_ref[b] + _bdot(o_b * g_ref[b], wo_ref[...])


def _rwkv_out_scan(o_scan, g, h, wo, length):
    d = D_MODEL
    seq = pl.BlockSpec((SCAN_GROUP, RELAYOUT_STEPS, d), lambda i: (0, i, 0))
    out = pl.pallas_call(
        _rwkv_out_scan_kernel,
        grid=(pl.cdiv(length, RELAYOUT_STEPS),),
        in_specs=[pl.BlockSpec((RELAYOUT_STEPS * SCAN_PITCH, LANES), lambda i: (i, 0)), seq, seq,
                  _const_spec(wo.shape)],
        out_specs=seq,
        out_shape=jax.ShapeDtypeStruct((SCAN_GROUP, length, d), F32),
        scratch_shapes=[pltpu.VMEM((LANES * RELAYOUT_PITCH, LANES), F32)],
        compiler_params=_params("parallel"),
        name="rwkv_out_from_scan",
    )(o_scan.reshape(length * SCAN_PITCH, LANES), g.reshape(SCAN_GROUP, length, d), h.reshape(SCAN_GROUP, length, d), wo)
    return out.reshape(SCAN_GROUP * length, d)


def _to_scan(x, bsz, length):
    if bsz == SCAN_GROUP and length >= RELAYOUT_STEPS:
        return _to_scan_pallas(x, length)
    x = x.reshape(bsz // SCAN_GROUP, SCAN_GROUP, length, RW_HEADS, RW_HEAD)
    x = jnp.transpose(x, (2, 4, 0, 3, 1)).reshape(length, RW_HEAD, bsz * RW_HEADS)
    return jnp.pad(x, ((0, 0), (0, SCAN_PITCH - RW_HEAD), (0, 0)))


def _from_scan(x, bsz, length):
    x = x[:, :RW_HEAD].reshape(length, RW_HEAD, bsz // SCAN_GROUP, RW_HEADS, SCAN_GROUP)
    return jnp.transpose(x, (2, 4, 0, 3, 1)).reshape(bsz * length, D_MODEL)


def _state_to_scan(wkv, bsz):
    x = wkv.reshape(bsz // SCAN_GROUP, SCAN_GROUP, RW_HEADS, RW_HEAD, RW_HEAD)
    return jnp.transpose(x, (3, 4, 0, 2, 1)).reshape(RW_HEAD, RW_HEAD, bsz * RW_HEADS)


def _state_from_scan(s, bsz):
    x = s.reshape(RW_HEAD, RW_HEAD, bsz // SCAN_GROUP, RW_HEADS, SCAN_GROUP)
    return jnp.transpose(x, (2, 4, 3, 0, 1)).reshape(bsz, RW_HEADS, RW_HEAD, RW_HEAD)


def _rwkv_layer(h, g_mix, shift0, wkv0, p, bsz, length, tm, tt):
    if length % tm == 0:
        (r, w, k, v, kk, a, g), shift = _rwkv_proj_fused(h, g_mix, shift0, p, tm, length)
    else:
        xn = _norm(h, g_mix, tm)
        xn3 = xn.reshape(bsz, length, D_MODEL)
        prev = jnp.concatenate([shift0[:, None, :], xn3[:, :-1]], axis=1).reshape(bsz * length, D_MODEL)
        r, w, k, v, kk, a, g = _rwkv_proj(xn, prev, p, tm)
        shift = xn3[:, -1]
    seqs = [_to_scan(x, bsz, length) for x in (r, w, k, v, kk, a)]
    o, s_fin = _rwkv_scan(seqs, _state_to_scan(wkv0, bsz), p["lnw"], p["lnb"], p["rk"], tt)
    if bsz == SCAN_GROUP and length >= RELAYOUT_STEPS:
        h_new = _rwkv_out_scan(o, g, h, p["wo"], length)
    else:
        h_new = _rwkv_out(_from_scan(o, bsz, length), g, h, p["wo"], tm)
    return h_new, shift, _state_from_scan(s_fin, bsz)


def _mamba_layer(h, g_mix, conv0, ssm0, p, bsz, length, tm, rows):
    z, xbc, dt = _mamba_in(h, g_mix, p["w_in"], tm)
    xbc3 = xbc.reshape(bsz, length, M_CONV_DIM)
    tail = M_CONV - 1
    conv_new = xbc3[:, length - tail:] if length >= tail else jnp.concatenate([conv0, xbc3], axis=1)[:, length:]
    pad_t = (-length) % SUBLANES if length < rows else 0

    def seq(x, c):
        x = x.reshape(bsz, length, c)
        return jnp.pad(x, ((0, 0), (0, pad_t), (0, 0))) if pad_t else x

    conv0_p = jnp.pad(conv0, ((0, 0), (SUBLANES - (M_CONV - 1), 0), (0, 0)))
    out, ssm = _mamba_ssd(seq(xbc, M_CONV_DIM), seq(z, M_D_INNER), seq(dt, M_DT_PAD), seq(h, D_MODEL),
                          conv0_p, ssm0, p, min(rows, length + pad_t), length)
    if pad_t:
        out = out[:, :length]
    return out.reshape(bsz * length, D_MODEL), conv_new, ssm


def _peer_layer(h, g_ffn, p, tm, g_final=None):
    xt, r1, e1, m0, w0 = _peer_route(h, g_ffn, p["wq_t"], p["keys"], p["keys_lo"], tm)
    final = g_final is not None
    return _peer_dense(xt, p["u"], p["vt"], r1, e1, m0, w0, h, g_final if final else g_ffn, tm, final)


TOKEN_TILE_LIMIT = 384
SCAN_BLOCK_LIMIT = 48
PEER_TOKEN_TILE = 512


def _largest_divisor(n, limit, multiple_of=1):
    return max(c for c in range(multiple_of, limit + 1, multiple_of) if n % c == 0)


def _tile_sizes(bsz, length):
    if length % SUBLANES == 0:
        tm = _largest_divisor(length, TOKEN_TILE_LIMIT, SUBLANES)
    else:
        tm = _largest_divisor(bsz * length, TOKEN_TILE_LIMIT, SUBLANES)
    return tm, _largest_divisor(length, SCAN_BLOCK_LIMIT)


def _trunk(h, shift0, wkv0, conv0, ssm0, norm_mix, norm_ffn, norm_final, rwkv_p, mamba_p, peer_p, bsz, length):
    tm, tt = _tile_sizes(bsz, length)
    h, shift, wkv = _rwkv_layer(h, norm_mix[0], shift0, wkv0, rwkv_p, bsz, length, tm, tt)
    h = _peer_layer(h, norm_ffn[0], peer_p[0], PEER_TOKEN_TILE)
    h, conv, ssm = _mamba_layer(h, norm_mix[1], conv0, ssm0, mamba_p, bsz, length, tm, M_CHUNK)
    y = _peer_layer(h, norm_ffn[1], peer_p[1], PEER_TOKEN_TILE, g_final=norm_final)
    return y.reshape(bsz, length, D_MODEL), shift[None], wkv[None], conv[None], ssm[None]


def kernel(x_prompt, x_sample, state_rwkv_shift, state_rwkv_wkv, state_mamba_conv, state_mamba_ssm, meta_tokens, norm_mix, norm_ffn, norm_final, rwkv_mix, rwkv_w_rkv, rwkv_w0, rwkv_w1, rwkv_w2, rwkv_a0, rwkv_a1, rwkv_a2, rwkv_g1, rwkv_g2, rwkv_k_k, rwkv_k_a, rwkv_r_k, rwkv_ln_w, rwkv_ln_b, rwkv_w_o, mamba_in_proj, mamba_conv_w, mamba_conv_b, mamba_dt_bias, mamba_a_log, mamba_d, mamba_norm_w, mamba_out_proj, peer_w_q, peer_sub_keys, peer_u, peer_v):
    rwkv_p = _prep_rwkv(rwkv_mix[0], rwkv_w_rkv[0], rwkv_w0[0], rwkv_w1[0], rwkv_w2[0], rwkv_a0[0],
                        rwkv_a1[0], rwkv_a2[0], rwkv_g1[0], rwkv_g2[0], rwkv_k_k[0], rwkv_k_a[0],
                        rwkv_r_k[0], rwkv_ln_w[0], rwkv_ln_b[0], rwkv_w_o[0])
    mamba_p = _prep_mamba(mamba_in_proj[0], mamba_conv_w[0], mamba_conv_b[0], mamba_dt_bias[0],
                          mamba_a_log[0], mamba_d[0], mamba_norm_w[0], mamba_out_proj[0])
    peer_p = [_prep_peer(peer_w_q[i], peer_sub_keys[i], peer_u[i], peer_v[i]) for i in range(2)]

    bp, lp = x_prompt.shape[0], x_prompt.shape[1] + N_META
    meta = jnp.broadcast_to(meta_tokens[None], (bp, N_META, D_MODEL))
    hp = jnp.concatenate([meta, x_prompt], axis=1).reshape(bp * lp, D_MODEL)
    zeros = lambda *s: jnp.zeros(s, F32)
    yp, p_shift, p_wkv, p_conv, p_ssm = _trunk(
        hp, zeros(bp, D_MODEL), zeros(bp, RW_HEADS, RW_HEAD, RW_HEAD), zeros(bp, M_CONV - 1, M_CONV_DIM),
        zeros(bp, M_HEADS, M_HEADDIM, M_STATE), norm_mix, norm_ffn, norm_final, rwkv_p, mamba_p, peer_p, bp, lp)

    bs, ls = x_sample.shape[0], x_sample.shape[1]
    ys, s_shift, s_wkv, s_conv, s_ssm = _trunk(
        x_sample.reshape(bs * ls, D_MODEL), state_rwkv_shift[0], state_rwkv_wkv[0], state_mamba_conv[0],
        state_mamba_ssm[0], norm_mix, norm_ffn, norm_final, rwkv_p, mamba_p, peer_p, bs, ls)
    return (yp[:, N_META:], ys, p_shift, p_wkv, p_conv, p_ssm, s_shift, s_wkv, s_conv, s_ssm)
```

```python
import functools
import math

import jax
import jax.numpy as jnp
from jax import lax
from jax.experimental import pallas as pl
from jax.experimental.pallas import tpu as pltpu

F32 = jnp.float32
BF16 = jnp.bfloat16

D_MODEL = 1024
N_META = 16
NORM_EPS = 1e-5
RW_HEAD = 64
RW_HEADS = D_MODEL // RW_HEAD
RW_LN_EPS = 64e-5
LORA_PAD = 128
GATE_LORA_PAD = 256
M_D_INNER = 2048
M_HEADDIM = 64
M_HEADS = M_D_INNER // M_HEADDIM
M_GROUPS = 4
M_HPG = M_HEADS // M_GROUPS
M_STATE = 128
M_CONV = 4
M_CONV_DIM = M_D_INNER + 2 * M_GROUPS * M_STATE
M_IN_DIM = 2 * M_D_INNER + 2 * M_GROUPS * M_STATE + M_HEADS
M_DT_PAD = 128
M_CHUNK = 128
P_HEADS = 8
P_NKEYS = 128
P_QDIM = 256
P_TOPK = 16
P_EXPERT_TILE = 2048
P_TOKEN_HALF = 256

LANES = 128
SUBLANES = 8
VMEM_LIMIT_BYTES = 56 * 1024 * 1024


def _params(*semantics):
    return pltpu.CompilerParams(dimension_semantics=semantics, vmem_limit_bytes=VMEM_LIMIT_BYTES)


def _const_spec(shape):
    zeros = (0,) * len(shape)
    return pl.BlockSpec(shape, lambda *_: zeros)


def _rms(x, g):
    return x * lax.rsqrt(jnp.mean(x * x, axis=-1, keepdims=True) + NORM_EPS) * g


def _softplus(x):
    return jnp.maximum(x, 0.0) + jnp.log1p(jnp.exp(-jnp.abs(x)))


def _bdot(a, b):
    return jnp.dot(a.astype(BF16), b.astype(BF16), preferred_element_type=F32)


def _split_bf16(w):
    hi = w.astype(BF16)
    return hi, (w - hi.astype(F32)).astype(BF16)


def _split3_bf16(x):
    p0 = x.astype(BF16)
    r0 = x - p0.astype(F32)
    p1 = r0.astype(BF16)
    p2 = (r0 - p1.astype(F32)).astype(BF16)
    return p0, p1, p2


def _dot_select(a, sel):
    p0, p1, p2 = _split3_bf16(a)
    return (jnp.dot(p0, sel, preferred_element_type=F32)
            + (jnp.dot(p1, sel, preferred_element_type=F32) + jnp.dot(p2, sel, preferred_element_type=F32)))


def _select_dot(sel, b):
    p0, p1, p2 = _split3_bf16(b)
    return (jnp.dot(sel, p0, preferred_element_type=F32)
            + (jnp.dot(sel, p1, preferred_element_type=F32) + jnp.dot(sel, p2, preferred_element_type=F32)))


def _dot3(a, b_hi, b_lo):
    a_hi, a_lo = _split_bf16(a)
    return (jnp.dot(a_hi, b_hi, preferred_element_type=F32)
            + (jnp.dot(a_hi, b_lo, preferred_element_type=F32) + jnp.dot(a_lo, b_hi, preferred_element_type=F32)))


def _norm_kernel(h_ref, g_ref, o_ref):
    o_ref[...] = _rms(h_ref[...], g_ref[...])


def _norm(h, g, tm):
    t, d = h.shape
    return pl.pallas_call(
        _norm_kernel,
        grid=(t // tm,),
        in_specs=[pl.BlockSpec((tm, d), lambda i: (i, 0)), _const_spec((1, d))],
        out_specs=pl.BlockSpec((tm, d), lambda i: (i, 0)),
        out_shape=jax.ShapeDtypeStruct((t, d), F32),
        compiler_params=_params("parallel"),
        name="rms_norm",
    )(h, g.reshape(1, d))


def _rwkv_proj_kernel(xn_ref, prev_ref, *refs):
    _rwkv_proj_body(xn_ref[...], prev_ref[...], *refs)


def _rwkv_proj_fused_kernel(h_ref, before_ref, shift_ref, gain_ref, *refs, tiles_per_seq):
    *proj_refs, last_ref = refs
    i = pl.program_id(0)
    xn = _rms(h_ref[...], gain_ref[...])
    before = _rms(before_ref[SUBLANES - 1:SUBLANES, :], gain_ref[...])
    first = jnp.where(i % tiles_per_seq == 0, shift_ref[0], before)
    row = lax.broadcasted_iota(jnp.int32, xn.shape, 0)
    prev = jnp.where(row == 0, first, pltpu.roll(xn, 1, axis=0))
    _rwkv_proj_body(xn, prev, *proj_refs)

    @pl.when(i % tiles_per_seq == tiles_per_seq - 1)
    def _():
        last_ref[0] = xn[xn.shape[0] - 1:, :]


def _rwkv_proj_body(xn, prev, mix_ref, vec_ref, wr_ref, wk_ref, wv_ref,
                    w1h_ref, w1l_ref, w2h_ref, w2l_ref, a1h_ref, a1l_ref, a2h_ref, a2l_ref, g1_ref, g2_ref,
                    r_ref, w_ref, k_ref, v_ref, kk_ref, a_ref, g_ref):
    dx = prev - xn
    xr, xw, xk, xv, xa, xg = [xn + dx * mix_ref[j:j + 1, :] for j in range(6)]
    w0, a0, k_k, k_a = [vec_ref[j:j + 1, :] for j in range(4)]
    r_ref[...] = _bdot(xr, wr_ref[...])
    k = _bdot(xk, wk_ref[...])
    v_ref[...] = _bdot(xv, wv_ref[...])
    w_lora = _dot3(jnp.tanh(_dot3(xw, w1h_ref[...], w1l_ref[...])), w2h_ref[...], w2l_ref[...])
    w_log = -_softplus(-(w0 + w_lora)) - 0.5
    w_ref[...] = jnp.exp(-jnp.exp(w_log))
    a = jax.nn.sigmoid(a0 + _dot3(_dot3(xa, a1h_ref[...], a1l_ref[...]), a2h_ref[...], a2l_ref[...]))
    a_ref[...] = a
    g_ref[...] = _bdot(jax.nn.sigmoid(_bdot(xg, g1_ref[...])), g2_ref[...])
    kk_ref[...] = k * k_k
    k_ref[...] = k * (1.0 + (a - 1.0) * k_a)


def _rwkv_proj(xn, prev, p, tm):
    t, d = xn.shape
    tok = pl.BlockSpec((tm, d), lambda i: (i, 0))
    weights = [p["mix"], p["vec"], p["wr"], p["wk"], p["wv"], *p["w1"], *p["w2"], *p["a1"], *p["a2"],
               p["g1"], p["g2"]]
    return pl.pallas_call(
        _rwkv_proj_kernel,
        grid=(t // tm,),
        in_specs=[tok, tok] + [_const_spec(w.shape) for w in weights],
        out_specs=[tok] * 7,
        out_shape=[jax.ShapeDtypeStruct((t, d), F32)] * 7,
        compiler_params=_params("parallel"),
        name="rwkv_proj",
    )(xn, prev, *weights)


def _rwkv_proj_fused(h, gain, shift0, p, tm, length):
    t, d = h.shape
    bsz = t // length
    tiles_per_seq = length // tm
    tok = pl.BlockSpec((tm, d), lambda i: (i, 0))
    before = pl.BlockSpec((SUBLANES, d), lambda i: (jnp.maximum(i * (tm // SUBLANES) - 1, 0), 0))
    per_seq = pl.BlockSpec((1, 1, d), lambda i: (i // tiles_per_seq, 0, 0))
    weights = [p["mix"], p["vec"], p["wr"], p["wk"], p["wv"], *p["w1"], *p["w2"], *p["a1"], *p["a2"],
               p["g1"], p["g2"]]
    kern = functools.partial(_rwkv_proj_fused_kernel, tiles_per_seq=tiles_per_seq)
    *proj, last = pl.pallas_call(
        kern,
        grid=(t // tm,),
        in_specs=[tok, before, per_seq, _const_spec((1, d))] + [_const_spec(w.shape) for w in weights],
        out_specs=[tok] * 7 + [per_seq],
        out_shape=[jax.ShapeDtypeStruct((t, d), F32)] * 7 + [jax.ShapeDtypeStruct((bsz, 1, d), F32)],
        compiler_params=_params("arbitrary"),
        name="rwkv_norm_proj",
    )(h, h, shift0.reshape(bsz, 1, d), gain.reshape(1, d), *weights)
    return proj, last.reshape(bsz, d)


def _rwkv_scan_kernel(r_ref, w_ref, k_ref, v_ref, kk_ref, a_ref, s0_ref, lnw_ref, lnb_ref, rk_ref,
                      o_ref, sfin_ref, state, alpha_blk, beta_blk, o_blk):
    tb = pl.program_id(1)
    steps = r_ref.shape[0]

    @pl.when(tb == 0)
    def _():
        state[...] = s0_ref[...]

    live = slice(0, RW_HEAD)

    kku = kk_ref[:, live, :]
    norm = jnp.sqrt(jnp.sum(kku * kku, axis=1, keepdims=True))
    kk = kku / jnp.maximum(norm, 1e-12)
    alpha_blk[...] = -kk
    beta_blk[...] = kk * a_ref[:, live, :]

    def step(t, carry):
        r = r_ref[t, live, :]
        w = w_ref[t, live, :]
        k = k_ref[t, live, :]
        alpha = alpha_blk[t]
        beta = beta_blk[t]

        def row(vi, c):
            s_v = state[vi]
            u = jnp.sum(s_v * alpha, axis=0, keepdims=True)
            s_new = s_v * w + u * beta + v_ref[t, pl.ds(vi, 1), :] * k
            state[vi] = s_new
            o_blk[t, pl.ds(vi, 1), :] = jnp.sum(s_new * r, axis=0, keepdims=True)
            return c

        lax.fori_loop(0, RW_HEAD, row, 0, unroll=RW_HEAD)
        return carry

    lax.fori_loop(0, steps, step, 0)

    o = o_blk[...]
    mu = jnp.mean(o, axis=1, keepdims=True)
    var = jnp.mean(jnp.square(o - mu), axis=1, keepdims=True)
    on = (o - mu) * lax.rsqrt(var + RW_LN_EPS)
    bonus = jnp.sum(r_ref[:, live, :] * k_ref[:, live, :] * rk_ref[...][None], axis=1, keepdims=True) * v_ref[:, live, :]
    o_ref[:, live, :] = on * lnw_ref[...][None] + lnb_ref[...][None] + bonus
    o_ref[:, RW_HEAD:, :] = jnp.zeros((steps, o_ref.shape[1] - RW_HEAD, LANES), F32)

    @pl.when(tb == pl.num_programs(1) - 1)
    def _():
        sfin_ref[...] = state[...]


def _rwkv_scan(seqs, s0, lnw, lnb, rk, tt):
    length, pitch, probs = seqs[0].shape
    n = RW_HEAD
    groups = probs // LANES
    seq_spec = pl.BlockSpec((tt, pitch, LANES), lambda g, i: (i, 0, g))
    st_spec = pl.BlockSpec((n, n, LANES), lambda g, i: (0, 0, g))
    lane_spec = pl.BlockSpec((n, LANES), lambda g, i: (0, 0))
    return pl.pallas_call(
        _rwkv_scan_kernel,
        grid=(groups, length // tt),
        in_specs=[seq_spec] * 6 + [st_spec] + [lane_spec] * 3,
        out_specs=[seq_spec, st_spec],
        out_shape=[jax.ShapeDtypeStruct((length, pitch, probs), F32),
                   jax.ShapeDtypeStruct((n, n, probs), F32)],
        scratch_shapes=[pltpu.VMEM((n, n, LANES), F32)] + [pltpu.VMEM((tt, n, LANES), F32)] * 3,
        compiler_params=_params("arbitrary", "arbitrary"),
        name="rwkv_scan",
    )(*seqs, s0, lnw, lnb, rk)


def _rwkv_out_kernel(o_ref, g_ref, h_ref, wo_ref, out_ref):
    out_ref[...] = h_ref[...] + _bdot(o_ref[...] * g_ref[...], wo_ref[...])


def _rwkv_out(o, g, h, wo, tm):
    t, d = h.shape
    tok = pl.BlockSpec((tm, d), lambda i: (i, 0))
    return pl.pallas_call(
        _rwkv_out_kernel,
        grid=(t // tm,),
        in_specs=[tok, tok, tok, _const_spec(wo.shape)],
        out_specs=tok,
        out_shape=jax.ShapeDtypeStruct((t, d), F32),
        compiler_params=_params("parallel"),
        name="rwkv_out",
    )(o, g, h, wo)


def _mamba_in_kernel(h_ref, g_ref, w_ref, z_ref, xbc_ref, dt_ref):
    xn = _rms(h_ref[...], g_ref[...]).astype(BF16)
    z_ref[...] = jnp.dot(xn, w_ref[:, :M_D_INNER], preferred_element_type=F32)
    xbc_ref[...] = jnp.dot(xn, w_ref[:, M_D_INNER:M_D_INNER + M_CONV_DIM], preferred_element_type=F32)
    dt_ref[...] = jnp.dot(xn, w_ref[:, M_D_INNER + M_CONV_DIM:], preferred_element_type=F32)


def _mamba_in(h, g, w_in, tm):
    t, d = h.shape
    return pl.pallas_call(
        _mamba_in_kernel,
        grid=(t // tm,),
        in_specs=[pl.BlockSpec((tm, d), lambda i: (i, 0)), _const_spec((1, d)), _const_spec(w_in.shape)],
        out_specs=[pl.BlockSpec((tm, M_D_INNER), lambda i: (i, 0)),
                   pl.BlockSpec((tm, M_CONV_DIM), lambda i: (i, 0)),
                   pl.BlockSpec((tm, M_DT_PAD), lambda i: (i, 0))],
        out_shape=[jax.ShapeDtypeStruct((t, M_D_INNER), F32),
                   jax.ShapeDtypeStruct((t, M_CONV_DIM), F32),
                   jax.ShapeDtypeStruct((t, M_DT_PAD), F32)],
        compiler_params=_params("parallel"),
        name="mamba_in_proj",
    )(h, g.reshape(1, d), w_in)


def _mamba_ssd_kernel(xbc_ref, z_ref, dt_ref, h_ref, conv0_ref, ssm0_ref, convw_ref, vec_ref,
                      dvec_ref, expand_ref, normw_ref, wout_ref,
                      out_ref, ssm_ref,
                      conv_buf, src_adt, src_acs, src_b, src_xdt, src_xdt_st, state, y_buf, *, seq_len, rows):
    b = pl.program_id(0)
    c = pl.program_id(1)
    q = M_CHUNK
    halo = SUBLANES
    d_bc = M_GROUPS * M_STATE

    @pl.when(c == 0)
    def _():
        state[...] = ssm0_ref[0]
        conv_buf[0:halo, :] = conv0_ref[0]

    @pl.when(c > 0)
    def _():
        conv_buf[0:halo, :] = conv_buf[rows:rows + halo, :]

    if rows < q:
        @pl.when((b == 0) & (c == 0))
        def _():
            src_adt[...] = jnp.zeros(src_adt.shape, F32)
            src_acs[...] = jnp.zeros(src_acs.shape, F32)
            src_b[...] = jnp.zeros(src_b.shape, BF16)
            src_xdt[...] = jnp.zeros(src_xdt.shape, BF16)
            src_xdt_st[...] = jnp.zeros(src_xdt_st.shape, BF16)

    conv_buf[halo:halo + rows, :] = xbc_ref[0]
    valid = (lax.broadcasted_iota(jnp.int32, (rows, 1), 0) + c * rows) < seq_len
    conv = vec_ref[0:1, :]
    for j in range(M_CONV):
        conv = conv + conv_buf[halo - (M_CONV - 1) + j:halo - (M_CONV - 1) + j + rows, :] * convw_ref[j:j + 1, :]
    act = jnp.where(valid, jax.nn.silu(conv), 0.0)
    xs = act[:, :M_D_INNER]
    dt = jnp.where(valid, _softplus(dt_ref[0] + dvec_ref[0:1, :]), 0.0)
    src_adt[0:rows, :] = dt * dvec_ref[1:2, :]
    li = lax.broadcasted_iota(jnp.int32, (rows, q), 0)
    si = lax.broadcasted_iota(jnp.int32, (rows, q), 1)
    causal = li >= si
    acs = _select_dot(causal.astype(BF16), src_adt[...])
    src_acs[0:rows, :] = acs
    acs_t = src_acs[...].T
    expand = expand_ref[...]
    dt_full = _dot_select(dt, expand)
    acs_full = _dot_select(acs, expand)
    last = acs[rows - 1:rows, :]
    last_full = acs_full[rows - 1:rows, :]
    xdt = xs * dt_full
    src_xdt[0:rows, :] = xdt.astype(BF16)
    src_xdt_st[0:rows, :] = (xdt * jnp.exp(last_full - acs_full)).astype(BF16)
    src_b[0:rows, :] = act[:, M_D_INNER:M_D_INNER + d_bc].astype(BF16)
    eacs_full = jnp.exp(acs_full)
    state_decay = jnp.exp(last)
    width = M_HPG * M_HEADDIM

    for g in range(M_GROUPS):
        cols = slice(g * width, (g + 1) * width)
        b_g = src_b[:, g * M_STATE:(g + 1) * M_STATE]
        c_g = act[:, M_D_INNER + d_bc + g * M_STATE:M_D_INNER + d_bc + (g + 1) * M_STATE].astype(BF16)
        cb = lax.dot_general(c_g, b_g, (((1,), (1,)), ((), ())), preferred_element_type=F32)
        h_g = state[g * M_HPG:(g + 1) * M_HPG].reshape(width, M_STATE)
        y_off = lax.dot_general(c_g, h_g.astype(BF16), (((1,), (1,)), ((), ())),
                                preferred_element_type=F32)
        upd = lax.dot_general(src_xdt_st[:, cols], b_g, (((0,), (0,)), ((), ())),
                              preferred_element_type=F32)
        ys = []
        for r in range(g * M_HPG, (g + 1) * M_HPG):
            lo = r * M_HEADDIM
            decay = jnp.where(causal, jnp.exp(acs[:, r:r + 1] - acs_t[r:r + 1, :]), 0.0)
            ys.append(jnp.dot((cb * decay).astype(BF16), src_xdt[:, lo:lo + M_HEADDIM],
                              preferred_element_type=F32))
            sub = slice((r - g * M_HPG) * M_HEADDIM, (r - g * M_HPG + 1) * M_HEADDIM)
            state[r] = state[r] * state_decay[:, r:r + 1] + upd[sub, :]
        y_buf[:, cols] = jnp.concatenate(ys, axis=1) + y_off * eacs_full[:, cols]

    y = y_buf[...] + vec_ref[1:2, :M_D_INNER] * xs
    yg = y * jax.nn.silu(z_ref[0])
    parts = []
    for g in range(M_GROUPS):
        part = yg[:, g * width:(g + 1) * width]
        parts.append(part * lax.rsqrt(jnp.mean(part * part, axis=-1, keepdims=True) + NORM_EPS))
    yn = jnp.concatenate(parts, axis=1) * normw_ref[...]
    out_ref[0] = h_ref[0] + _bdot(yn, wout_ref[...])

    @pl.when(c == pl.num_programs(1) - 1)
    def _():
        ssm_ref[0] = state[...]


def _mamba_ssd(xbc, z, dt, h, conv0, ssm0, p, rows, seq_len):
    bsz, length, _ = xbc.shape
    chunks = pl.cdiv(length, rows)
    weights = [p["conv_w"], p["vec"], p["dvec"], p["expand"], p["norm_w"], p["w_out"]]

    def seq_spec(width):
        return pl.BlockSpec((1, rows, width), lambda b, c: (b, c, 0))

    kern = functools.partial(_mamba_ssd_kernel, seq_len=seq_len, rows=rows)
    return pl.pallas_call(
        kern,
        grid=(bsz, chunks),
        in_specs=[seq_spec(M_CONV_DIM), seq_spec(M_D_INNER), seq_spec(M_DT_PAD), seq_spec(D_MODEL),
                  pl.BlockSpec((1, SUBLANES, M_CONV_DIM), lambda b, c: (b, 0, 0)),
                  pl.BlockSpec((1, M_HEADS, M_HEADDIM, M_STATE), lambda b, c: (b, 0, 0, 0))]
                 + [_const_spec(w.shape) for w in weights],
        out_specs=[seq_spec(D_MODEL),
                   pl.BlockSpec((1, M_HEADS, M_HEADDIM, M_STATE), lambda b, c: (b, 0, 0, 0))],
        out_shape=[jax.ShapeDtypeStruct((bsz, length, D_MODEL), F32),
                   jax.ShapeDtypeStruct((bsz, M_HEADS, M_HEADDIM, M_STATE), F32)],
        scratch_shapes=[pltpu.VMEM((rows + SUBLANES, M_CONV_DIM), F32),
                        pltpu.VMEM((M_CHUNK, M_DT_PAD), F32),
                        pltpu.VMEM((M_CHUNK, M_DT_PAD), F32),
                        pltpu.VMEM((M_CHUNK, M_GROUPS * M_STATE), BF16),
                        pltpu.VMEM((M_CHUNK, M_D_INNER), BF16),
                        pltpu.VMEM((M_CHUNK, M_D_INNER), BF16),
                        pltpu.VMEM((M_HEADS, M_HEADDIM, M_STATE), F32),
                        pltpu.VMEM((rows, M_D_INNER), F32)],
        compiler_params=_params("arbitrary", "arbitrary"),
        name="mamba_ssd",
    )(xbc, z, dt, h, conv0, ssm0, *weights)


_CANDS = [(ra, rb) for ra in range(P_TOPK) for rb in range(P_TOPK) if (ra + 1) * (rb + 1) <= P_TOPK]


def _peer_route_kernel(h_ref, g_ref, wq_ref, keys_ref, keys_lo_ref,
                       xt_ref, r1_ref, e1_ref, m0_ref, w0_ref,
                       q_buf, s_buf, code0, code_book, best_vals, cand, counts, inv_z, *, n_tokens):
    tm = h_ref.shape[0]
    neg_inf = float("-inf")
    h_tile = h_ref[...]
    if n_tokens % tm:
        def stand_in():
            row = lax.broadcasted_iota(jnp.int32, h_tile.shape, 0)
            col = lax.broadcasted_iota(jnp.int32, h_tile.shape, 1)
            filler = ((row * 7 + col * 13) & 63).astype(F32) * (1.0 / 64.0) - 0.5
            return jnp.where(row < n_tokens - pl.program_id(0) * tm, h_tile, filler)

        h_tile = lax.cond(pl.program_id(0) == pl.num_programs(0) - 1, stand_in, lambda: h_tile)
    xn = _rms(h_tile, g_ref[...])
    xt = xn.T.astype(BF16)
    xt_ref[...] = xt
    q_buf[...] = jnp.dot(wq_ref[...], xt, preferred_element_type=F32)
    key_iota = lax.broadcasted_iota(jnp.int32, (P_NKEYS, tm), 0).astype(F32)

    def half(m, carry):
        head = m // 2
        z = m % 2
        row0 = pl.multiple_of(m * P_NKEYS, P_NKEYS)
        qm = q_buf[pl.ds(row0, P_NKEYS), :]
        q_hi = qm.astype(BF16)
        q_lo = (qm - q_hi.astype(F32)).astype(BF16)
        k_hi = keys_ref[m]
        s = (jnp.dot(k_hi, q_hi, preferred_element_type=F32)
             + (jnp.dot(k_hi, q_lo, preferred_element_type=F32)
                + jnp.dot(keys_lo_ref[m], q_hi, preferred_element_type=F32)))

        s_buf[...] = s

        def next_distinct(j, prev):
            sj = s_buf[...]
            best = jnp.max(jnp.where(sj < prev, sj, neg_inf), axis=0, keepdims=True)
            best_vals[z, j, pl.ds(head, 1), :] = best
            return best

        def extract_ties(j, sc):
            cur, rank = sc
            best = jnp.max(cur, axis=0, keepdims=True)
            first = jnp.min(jnp.where(cur == best, key_iota, float(P_NKEYS)), axis=0, keepdims=True)
            sel = key_iota == first
            best_vals[z, j, pl.ds(head, 1), :] = best
            return jnp.where(sel, neg_inf, cur), jnp.where(sel, jnp.asarray(j, dtype=F32), rank)

        last = lax.fori_loop(0, P_TOPK, next_distinct, jnp.full((1, tm), float("inf"), F32))
        reached = jnp.sum(jnp.where(s >= last, 1.0, 0.0), axis=0, keepdims=True)
        has_ties = jnp.max(jnp.abs(reached - float(P_TOPK))) > 0.0

        def rank_by_count():
            rank = jnp.zeros((P_NKEYS, tm), F32)
            for jj in range(P_TOPK):
                rank = rank + jnp.where(best_vals[z, jj, pl.ds(head, 1), :] > s, 1.0, 0.0)
            return rank

        def rank_with_ties():
            no_rank = jnp.full((P_NKEYS, tm), float(P_NKEYS), F32)
            return lax.fori_loop(0, P_TOPK, extract_ties, (s, no_rank))[1]

        e = jnp.exp(s - jnp.max(s, axis=0, keepdims=True))

        @pl.when(z == 0)
        def _():
            w0_ref[head] = e

            @pl.when(has_ties)
            def _():
                code0[head] = rank_with_ties()
                for ra in range(P_TOPK):
                    code_book[ra, pl.ds(head, 1), :] = jnp.full((1, tm), float(ra), F32)

            @pl.when(jnp.logical_not(has_ties))
            def _():
                code0[head] = s
                for ra in range(P_TOPK):
                    code_book[ra, pl.ds(head, 1), :] = best_vals[0, ra, pl.ds(head, 1), :]

        @pl.when(z == 1)
        def _():
            rank = lax.cond(has_ties, rank_with_ties, rank_by_count)
            for lt in range(tm // LANES):
                r1_ref[head, lt] = rank[:, lt * LANES:(lt + 1) * LANES].astype(BF16)
                e1_ref[head, lt] = e[:, lt * LANES:(lt + 1) * LANES].astype(BF16)

        return carry

    lax.fori_loop(0, 2 * P_HEADS, half, 0)

    ex0 = [jnp.exp(best_vals[0, ra] - best_vals[0, 0]) for ra in range(P_TOPK)]
    ex1 = [jnp.exp(best_vals[1, rb] - best_vals[1, 0]) for rb in range(P_TOPK)]
    for i, (ra, rb) in enumerate(_CANDS):
        cand[i] = best_vals[0, ra] + best_vals[1, rb]

    def next_sum(it, prev):
        best = None
        for i in range(len(_CANDS)):
            ci = cand[i]
            below = jnp.where(ci < prev, ci, neg_inf)
            best = below if best is None else jnp.maximum(best, below)
        return best

    last_sum = lax.fori_loop(0, P_TOPK, next_sum, jnp.full((P_HEADS, tm), float("inf"), F32))
    hits = [jnp.where(cand[i] >= last_sum, 1.0, 0.0) for i in range(len(_CANDS))]
    n_hits = functools.reduce(lambda x, y: x + y, hits)
    sums_tie = jnp.max(jnp.abs(n_hits - float(P_TOPK))) > 0.0

    def select_distinct():
        z = jnp.zeros((P_HEADS, tm), F32)
        per_rank = [None] * P_TOPK
        for i, (ra, rb) in enumerate(_CANDS):
            per_rank[ra] = hits[i] if per_rank[ra] is None else per_rank[ra] + hits[i]
            z = z + hits[i] * (ex0[ra] * ex1[rb])
        for ra in range(P_TOPK):
            counts[ra] = per_rank[ra]
        return z

    def select_with_ties():
        counts[...] = jnp.zeros(counts.shape, F32)

        def pick(it, z):
            best = cand[0]
            for i in range(1, len(_CANDS)):
                best = jnp.maximum(best, cand[i])
            found = jnp.zeros((P_HEADS, tm), F32)
            for i, (ra, rb) in enumerate(_CANDS):
                ci = cand[i]
                hit = jnp.where(ci == best, 1.0, 0.0) * (1.0 - found)
                found = found + hit
                cand[i] = jnp.where(hit > 0.0, neg_inf, ci)
                counts[ra] = counts[ra] + hit
                z = z + hit * (ex0[ra] * ex1[rb])
            return z

        return lax.fori_loop(0, P_TOPK, pick, jnp.zeros((P_HEADS, tm), F32))

    inv_z[...] = 1.0 / lax.cond(sums_tie, select_with_ties, select_distinct)

    def finish(head, carry):
        code = code0[head]
        m0 = jnp.zeros((P_NKEYS, tm), F32)
        for ra in range(P_TOPK):
            m0 = jnp.where(code == code_book[ra, pl.ds(head, 1), :], counts[ra, pl.ds(head, 1), :], m0)
        m0_ref[head] = m0
        w0_ref[head] = w0_ref[head] * inv_z[pl.ds(head, 1), :]
        return carry

    lax.fori_loop(0, P_HEADS, finish, 0)


def _peer_route(h, g, wq_t, keys, keys_lo, tm):
    t, d = h.shape
    tiles = pl.cdiv(t, tm)
    t_pad = tiles * tm
    tile = pl.BlockSpec((P_HEADS, P_NKEYS, tm), lambda i: (0, 0, i))
    tile_f32 = jax.ShapeDtypeStruct((P_HEADS, P_NKEYS, t_pad), F32)
    tile_bf16 = jax.ShapeDtypeStruct((P_HEADS, t_pad // LANES, P_NKEYS, LANES), BF16)
    lane_tiles = pl.BlockSpec((P_HEADS, tm // LANES, P_NKEYS, LANES), lambda i: (0, i, 0, 0))
    return pl.pallas_call(
        functools.partial(_peer_route_kernel, n_tokens=t),
        grid=(tiles,),
        in_specs=[pl.BlockSpec((tm, d), lambda i: (i, 0)), _const_spec((1, d)),
                  _const_spec(wq_t.shape), _const_spec(keys.shape), _const_spec(keys_lo.shape)],
        out_specs=[pl.BlockSpec((d, tm), lambda i: (0, i)), lane_tiles, lane_tiles, tile, tile],
        out_shape=[jax.ShapeDtypeStruct((d, t_pad), BF16), tile_bf16, tile_bf16, tile_f32, tile_f32],
        scratch_shapes=[pltpu.VMEM((P_HEADS * P_QDIM, tm), F32),
                        pltpu.VMEM((P_NKEYS, tm), F32),
                        pltpu.VMEM((P_HEADS, P_NKEYS, tm), F32),
                        pltpu.VMEM((P_TOPK, P_HEADS, tm), F32),
                        pltpu.VMEM((2, P_TOPK, P_HEADS, tm), F32),
                        pltpu.VMEM((len(_CANDS), P_HEADS, tm), F32),
                        pltpu.VMEM((P_TOPK, P_HEADS, tm), F32),
                        pltpu.VMEM((P_HEADS, tm), F32)],
        compiler_params=_params("parallel"),
        name="peer_route",
    )(h, g.reshape(1, d), wq_t, keys, keys_lo)


def _gelu(x):
    return 0.5 * x * (1.0 + lax.erf(x * (1.0 / math.sqrt(2.0))))


def _peer_dense_kernel(xt_ref, u_ref, vt_ref, r1_ref, e1_ref, m0_ref, w0_ref, h_ref, gfin_ref,
                       out_ref, acc, hid, prob, *, final_norm):
    j = pl.program_id(1)
    keys_per_step = P_EXPERT_TILE // P_NKEYS

    @pl.when(j == 0)
    def _():
        acc[...] = jnp.zeros(acc.shape, F32)

    tm = xt_ref.shape[1]
    zero = jnp.zeros((), BF16)

    def packed_rows(row):
        one = jnp.broadcast_to(row, (2 * SUBLANES, LANES)).astype(BF16)
        return jnp.tile(one, (P_NKEYS // (2 * SUBLANES), 1))

    for half in range(tm // P_TOKEN_HALF):
        cols = slice(half * P_TOKEN_HALF, (half + 1) * P_TOKEN_HALF)
        hid[:, cols] = jnp.dot(u_ref[...], xt_ref[:, cols], preferred_element_type=F32)
        for ii in range(keys_per_step):
            i1 = j * keys_per_step + ii
            rows = slice(ii * P_NKEYS, (ii + 1) * P_NKEYS)
            for lt in range(half * (P_TOKEN_HALF // LANES), (half + 1) * (P_TOKEN_HALF // LANES)):
                lanes = slice(lt * LANES, (lt + 1) * LANES)
                gate = None
                for head in range(P_HEADS):
                    m0 = packed_rows(m0_ref[head, pl.ds(i1, 1), :][:, lanes])
                    w0 = packed_rows(w0_ref[head, pl.ds(i1, 1), :][:, lanes])
                    term = w0 * jnp.where(r1_ref[head, lt] < m0, e1_ref[head, lt], zero)
                    gate = term if gate is None else gate + term
                prob[rows, lanes] = gate * _gelu(hid[rows, lanes]).astype(BF16)
        acc[:, cols] += jnp.dot(vt_ref[...], prob[:, cols], preferred_element_type=F32)

    @pl.when(j == pl.num_programs(1) - 1)
    def _():
        res = h_ref[...] + acc[...].T
        if final_norm:
            res = _rms(res, gfin_ref[...])
        out_ref[...] = res


def _peer_dense(xt, u_bf, vt_bf, r1, e1, m0, w0, h, g_final, tm, final_norm):
    t, d = h.shape
    n_exp = u_bf.shape[0]
    tile = pl.BlockSpec((P_HEADS, P_NKEYS, tm), lambda i, j: (0, 0, i))
    lane_tiles = pl.BlockSpec((P_HEADS, tm // LANES, P_NKEYS, LANES), lambda i, j: (0, i, 0, 0))
    kern = functools.partial(_peer_dense_kernel, final_norm=final_norm)
    return pl.pallas_call(
        kern,
        grid=(pl.cdiv(t, tm), n_exp // P_EXPERT_TILE),
        in_specs=[pl.BlockSpec((d, tm), lambda i, j: (0, i)),
                  pl.BlockSpec((P_EXPERT_TILE, d), lambda i, j: (j, 0)),
                  pl.BlockSpec((d, P_EXPERT_TILE), lambda i, j: (0, j)),
                  lane_tiles, lane_tiles, tile, tile,
                  pl.BlockSpec((tm, d), lambda i, j: (i, 0)),
                  _const_spec((1, d))],
        out_specs=pl.BlockSpec((tm, d), lambda i, j: (i, 0)),
        out_shape=jax.ShapeDtypeStruct((t, d), F32),
        scratch_shapes=[pltpu.VMEM((d, tm), F32),
                        pltpu.VMEM((P_EXPERT_TILE, tm), F32),
                        pltpu.VMEM((P_EXPERT_TILE, tm), BF16)],
        compiler_params=_params("parallel", "arbitrary"),
        name="peer_dense",
    )(xt, u_bf, vt_bf, r1, e1, m0, w0, h, g_final.reshape(1, d))


def _pad_cols(w, n):
    return jnp.pad(w, ((0, 0), (0, n - w.shape[1])))


def _pad_rows(w, n):
    return jnp.pad(w, ((0, n - w.shape[0]), (0, 0)))


def _prep_rwkv(mix, w_rkv, w0, w1, w2, a0, a1, a2, g1, g2, k_k, k_a, r_k, ln_w, ln_b, w_o):
    d = D_MODEL
    lane = lambda x: jnp.repeat(x.reshape(RW_HEADS, RW_HEAD).T, LANES // RW_HEADS, axis=1)
    return {
        "mix": _pad_rows(mix, SUBLANES),
        "vec": _pad_rows(jnp.stack([w0, a0, k_k, k_a]), SUBLANES),
        "wr": w_rkv[0].astype(BF16), "wk": w_rkv[1].astype(BF16), "wv": w_rkv[2].astype(BF16),
        "w1": _split_bf16(_pad_cols(w1, LORA_PAD)), "w2": _split_bf16(_pad_rows(w2, LORA_PAD)),
        "a1": _split_bf16(_pad_cols(a1, LORA_PAD)), "a2": _split_bf16(_pad_rows(a2, LORA_PAD)),
        "g1": _pad_cols(g1, GATE_LORA_PAD).astype(BF16), "g2": _pad_rows(g2, GATE_LORA_PAD).astype(BF16),
        "lnw": lane(ln_w), "lnb": lane(ln_b), "rk": lane(r_k.reshape(d)),
        "wo": w_o.astype(BF16),
    }


def _prep_mamba(in_proj, conv_w, conv_b, dt_bias, a_log, d_skip, norm_w, out_proj):
    head_of_channel = jnp.arange(M_D_INNER) // M_HEADDIM
    expand = (jnp.arange(M_DT_PAD)[:, None] == head_of_channel[None, :]).astype(BF16)
    d_full = jnp.pad(jnp.repeat(d_skip, M_HEADDIM), (0, M_CONV_DIM - M_D_INNER))
    pad_h = lambda x: jnp.pad(x, (0, M_DT_PAD - M_HEADS))
    return {
        "w_in": _pad_cols(in_proj, M_IN_DIM - M_HEADS + M_DT_PAD).astype(BF16),
        "conv_w": _pad_rows(conv_w, SUBLANES),
        "vec": _pad_rows(jnp.stack([conv_b, d_full]), SUBLANES),
        "dvec": _pad_rows(jnp.stack([pad_h(dt_bias), pad_h(-jnp.exp(a_log))]), SUBLANES),
        "expand": expand,
        "norm_w": norm_w.reshape(1, M_D_INNER),
        "w_out": out_proj.astype(BF16),
    }


def _prep_peer(w_q, sub_keys, u_tab, v_tab):
    keys = jnp.transpose(sub_keys, (1, 0, 2, 3)).reshape(2 * P_HEADS, P_NKEYS, P_QDIM // 2)
    keys_hi, keys_lo = _split_bf16(keys)
    return {"wq_t": w_q.T.astype(BF16), "keys": keys_hi, "keys_lo": keys_lo,
            "u": u_tab.astype(BF16), "vt": v_tab.T.astype(BF16)}


SCAN_GROUP = LANES // RW_HEADS
RELAYOUT_STEPS = 128
RELAYOUT_PITCH = RW_HEAD + SUBLANES
SCAN_PITCH = RW_HEAD + SUBLANES


def _to_scan_kernel(x_ref, o_ref, stage):
    for b in range(SCAN_GROUP):
        for hp in range(RW_HEADS // 2):
            t = x_ref[b, :, hp * LANES:(hp + 1) * LANES].T
            for hl in range(2):
                lane = (2 * hp + hl) * SCAN_GROUP + b
                stage[lane * RELAYOUT_PITCH:lane * RELAYOUT_PITCH + RW_HEAD, :] = t[hl * RW_HEAD:(hl + 1) * RW_HEAD, :]
    for k in range(RW_HEAD):
        by_lane = stage[pl.ds(k, LANES, stride=RELAYOUT_PITCH), :]
        o_ref[pl.ds(k, RELAYOUT_STEPS, stride=SCAN_PITCH), :] = by_lane.T
    for k in range(RW_HEAD, SCAN_PITCH):
        o_ref[pl.ds(k, RELAYOUT_STEPS, stride=SCAN_PITCH), :] = jnp.zeros((RELAYOUT_STEPS, LANES), F32)


def _to_scan_pallas(x, length):
    x3 = x.reshape(SCAN_GROUP, length, D_MODEL)
    out = pl.pallas_call(
        _to_scan_kernel,
        grid=(pl.cdiv(length, RELAYOUT_STEPS),),
        in_specs=[pl.BlockSpec((SCAN_GROUP, RELAYOUT_STEPS, D_MODEL), lambda i: (0, i, 0))],
        out_specs=pl.BlockSpec((RELAYOUT_STEPS * SCAN_PITCH, LANES), lambda i: (i, 0)),
        out_shape=jax.ShapeDtypeStruct((length * SCAN_PITCH, LANES), F32),
        scratch_shapes=[pltpu.VMEM((LANES * RELAYOUT_PITCH, LANES), F32)],
        compiler_params=_params("parallel"),
        name="to_scan_layout",
    )(x3)
    return out.reshape(length, SCAN_PITCH, LANES)


def _rwkv_out_scan_kernel(o_ref, g_ref, h_ref, wo_ref, out_ref, stage):
    for v in range(RW_HEAD):
        by_step = o_ref[pl.ds(v, RELAYOUT_STEPS, stride=SCAN_PITCH), :]
        stage[pl.ds(v, LANES, stride=RELAYOUT_PITCH), :] = by_step.T
    for b in range(SCAN_GROUP):
        tiles = []
        for hp in range(RW_HEADS // 2):
            halves = []
            for hl in range(2):
                lane = (2 * hp + hl) * SCAN_GROUP + b
                halves.append(stage[lane * RELAYOUT_PITCH:lane * RELAYOUT_PITCH + RW_HEAD, :])
            tiles.append(jnp.concatenate(halves, axis=0).T)
        o_b = jnp.concatenate(tiles, axis=1)
        out_ref[b] = h_ref[b] + _bdot(o_b * g_ref[b], wo_ref[...])


def _rwkv_out_scan(o_scan, g, h, wo, length):
    d = D_MODEL
    seq = pl.BlockSpec((SCAN_GROUP, RELAYOUT_STEPS, d), lambda i: (0, i, 0))
    out = pl.pallas_call(
        _rwkv_out_scan_kernel,
        grid=(pl.cdiv(length, RELAYOUT_STEPS),),
        in_specs=[pl.BlockSpec((RELAYOUT_STEPS * SCAN_PITCH, LANES), lambda i: (i, 0)), seq, seq,
                  _const_spec(wo.shape)],
        out_specs=seq,
        out_shape=jax.ShapeDtypeStruct((SCAN_GROUP, length, d), F32),
        scratch_shapes=[pltpu.VMEM((LANES * RELAYOUT_PITCH, LANES), F32)],
        compiler_params=_params("parallel"),
        name="rwkv_out_from_scan",
    )(o_scan.reshape(length * SCAN_PITCH, LANES), g.reshape(SCAN_GROUP, length, d), h.reshape(SCAN_GROUP, length, d), wo)
    return out.reshape(SCAN_GROUP * length, d)


def _to_scan(x, bsz, length):
    if bsz == SCAN_GROUP and length >= RELAYOUT_STEPS:
        return _to_scan_pallas(x, length)
    x = x.reshape(bsz // SCAN_GROUP, SCAN_GROUP, length, RW_HEADS, RW_HEAD)
    x = jnp.transpose(x, (2, 4, 0, 3, 1)).reshape(length, RW_HEAD, bsz * RW_HEADS)
    return jnp.pad(x, ((0, 0), (0, SCAN_PITCH - RW_HEAD), (0, 0)))


def _from_scan(x, bsz, length):
    x = x[:, :RW_HEAD].reshape(length, RW_HEAD, bsz // SCAN_GROUP, RW_HEADS, SCAN_GROUP)
    return jnp.transpose(x, (2, 4, 0, 3, 1)).reshape(bsz * length, D_MODEL)


def _state_to_scan(wkv, bsz):
    x = wkv.reshape(bsz // SCAN_GROUP, SCAN_GROUP, RW_HEADS, RW_HEAD, RW_HEAD)
    return jnp.transpose(x, (3, 4, 0, 2, 1)).reshape(RW_HEAD, RW_HEAD, bsz * RW_HEADS)


def _state_from_scan(s, bsz):
    x = s.reshape(RW_HEAD, RW_HEAD, bsz // SCAN_GROUP, RW_HEADS, SCAN_GROUP)
    return jnp.transpose(x, (2, 4, 3, 0, 1)).reshape(bsz, RW_HEADS, RW_HEAD, RW_HEAD)


def _rwkv_layer(h, g_mix, shift0, wkv0, p, bsz, length, tm, tt):
    if length % tm == 0:
        (r, w, k, v, kk, a, g), shift = _rwkv_proj_fused(h, g_mix, shift0, p, tm, length)
    else:
        xn = _norm(h, g_mix, tm)
        xn3 = xn.reshape(bsz, length, D_MODEL)
        prev = jnp.concatenate([shift0[:, None, :], xn3[:, :-1]], axis=1).reshape(bsz * length, D_MODEL)
        r, w, k, v, kk, a, g = _rwkv_proj(xn, prev, p, tm)
        shift = xn3[:, -1]
    seqs = [_to_scan(x, bsz, length) for x in (r, w, k, v, kk, a)]
    o, s_fin = _rwkv_scan(seqs, _state_to_scan(wkv0, bsz), p["lnw"], p["lnb"], p["rk"], tt)
    if bsz == SCAN_GROUP and length >= RELAYOUT_STEPS:
        h_new = _rwkv_out_scan(o, g, h, p["wo"], length)
    else:
        h_new = _rwkv_out(_from_scan(o, bsz, length), g, h, p["wo"], tm)
    return h_new, shift, _state_from_scan(s_fin, bsz)


def _mamba_layer(h, g_mix, conv0, ssm0, p, bsz, length, tm, rows):
    z, xbc, dt = _mamba_in(h, g_mix, p["w_in"], tm)
    xbc3 = xbc.reshape(bsz, length, M_CONV_DIM)
    tail = M_CONV - 1
    conv_new = xbc3[:, length - tail:] if length >= tail else jnp.concatenate([conv0, xbc3], axis=1)[:, length:]
    pad_t = (-length) % SUBLANES if length < rows else 0

    def seq(x, c):
        x = x.reshape(bsz, length, c)
        return jnp.pad(x, ((0, 0), (0, pad_t), (0, 0))) if pad_t else x

    conv0_p = jnp.pad(conv0, ((0, 0), (SUBLANES - (M_CONV - 1), 0), (0, 0)))
    out, ssm = _mamba_ssd(seq(xbc, M_CONV_DIM), seq(z, M_D_INNER), seq(dt, M_DT_PAD), seq(h, D_MODEL),
                          conv0_p, ssm0, p, min(rows, length + pad_t), length)
    if pad_t:
        out = out[:, :length]
    return out.reshape(bsz * length, D_MODEL), conv_new, ssm


def _peer_layer(h, g_ffn, p, tm, g_final=None):
    xt, r1, e1, m0, w0 = _peer_route(h, g_ffn, p["wq_t"], p["keys"], p["keys_lo"], tm)
    final = g_final is not None
    return _peer_dense(xt, p["u"], p["vt"], r1, e1, m0, w0, h, g_final if final else g_ffn, tm, final)


TOKEN_TILE_LIMIT = 384
SCAN_BLOCK_LIMIT = 48
PEER_TOKEN_TILE = 512


def _largest_divisor(n, limit, multiple_of=1):
    return max(c for c in range(multiple_of, limit + 1, multiple_of) if n % c == 0)


def _tile_sizes(bsz, length):
    if length % SUBLANES == 0:
        tm = _largest_divisor(length, TOKEN_TILE_LIMIT, SUBLANES)
    else:
        tm = _largest_divisor(bsz * length, TOKEN_TILE_LIMIT, SUBLANES)
    return tm, _largest_divisor(length, SCAN_BLOCK_LIMIT)


def _trunk(h, shift0, wkv0, conv0, ssm0, norm_mix, norm_ffn, norm_final, rwkv_p, mamba_p, peer_p, bsz, length):
    tm, tt = _tile_sizes(bsz, length)
    h, shift, wkv = _rwkv_layer(h, norm_mix[0], shift0, wkv0, rwkv_p, bsz, length, tm, tt)
    h = _peer_layer(h, norm_ffn[0], peer_p[0], PEER_TOKEN_TILE)
    h, conv, ssm = _mamba_layer(h, norm_mix[1], conv0, ssm0, mamba_p, bsz, length, tm, M_CHUNK)
    y = _peer_layer(h, norm_ffn[1], peer_p[1], PEER_TOKEN_TILE, g_final=norm_final)
    return y.reshape(bsz, length, D_MODEL), shift[None], wkv[None], conv[None], ssm[None]


def kernel(x_prompt, x_sample, state_rwkv_shift, state_rwkv_wkv, state_mamba_conv, state_mamba_ssm, meta_tokens, norm_mix, norm_ffn, norm_final, rwkv_mix, rwkv_w_rkv, rwkv_w0, rwkv_w1, rwkv_w2, rwkv_a0, rwkv_a1, rwkv_a2, rwkv_g1, rwkv_g2, rwkv_k_k, rwkv_k_a, rwkv_r_k, rwkv_ln_w, rwkv_ln_b, rwkv_w_o, mamba_in_proj, mamba_conv_w, mamba_conv_b, mamba_dt_bias, mamba_a_log, mamba_d, mamba_norm_w, mamba_out_proj, peer_w_q, peer_sub_keys, peer_u, peer_v):
    rwkv_p = _prep_rwkv(rwkv_mix[0], rwkv_w_rkv[0], rwkv_w0[0], rwkv_w1[0], rwkv_w2[0], rwkv_a0[0],
                        rwkv_a1[0], rwkv_a2[0], rwkv_g1[0], rwkv_g2[0], rwkv_k_k[0], rwkv_k_a[0],
                        rwkv_r_k[0], rwkv_ln_w[0], rwkv_ln_b[0], rwkv_w_o[0])
    mamba_p = _prep_mamba(mamba_in_proj[0], mamba_conv_w[0], mamba_conv_b[0], mamba_dt_bias[0],
                          mamba_a_log[0], mamba_d[0], mamba_norm_w[0], mamba_out_proj[0])
    peer_p = [_prep_peer(peer_w_q[i], peer_sub_keys[i], peer_u[i], peer_v[i]) for i in range(2)]

    bp, lp = x_prompt.shape[0], x_prompt.shape[1] + N_META
    meta = jnp.broadcast_to(meta_tokens[None], (bp, N_META, D_MODEL))
    hp = jnp.concatenate([meta, x_prompt], axis=1).reshape(bp * lp, D_MODEL)
    zeros = lambda *s: jnp.zeros(s, F32)
    yp, p_shift, p_wkv, p_conv, p_ssm = _trunk(
        hp, zeros(bp, D_MODEL), zeros(bp, RW_HEADS, RW_HEAD, RW_HEAD), zeros(bp, M_CONV - 1, M_CONV_DIM),
        zeros(bp, M_HEADS, M_HEADDIM, M_STATE), norm_mix, norm_ffn, norm_final, rwkv_p, mamba_p, peer_p, bp, lp)

    bs, ls = x_sample.shape[0], x_sample.shape[1]
    ys, s_shift, s_wkv, s_conv, s_ssm = _trunk(
        x_sample.reshape(bs * ls, D_MODEL), state_rwkv_shift[0], state_rwkv_wkv[0], state_mamba_conv[0],
        state_mamba_ssm[0], norm_mix, norm_ffn, norm_final, rwkv_p, mamba_p, peer_p, bs, ls)
    return (yp[:, N_META:], ys, p_shift, p_wkv, p_conv, p_ssm, s_shift, s_wkv, s_conv, s_ssm)
```
